```python
import jax, jax.numpy as jnp
from jax import lax
import numpy as np

D_MODEL = 2048
BATCH = 4
SEQ = 4096
DEPTH = 4

N_META = 16
NORM_EPS = 1e-6
ROPE_BASE = 10000.0

RW_HEADS = 16
RW_HEAD_DIM = 64
RW_WIDTH = RW_HEADS * RW_HEAD_DIM
RW_DECAY_LORA = 96
RW_A_LORA = 96
RW_GATE_LORA = 256
RW_COLS = 3 * RW_WIDTH + RW_DECAY_LORA + RW_A_LORA + RW_GATE_LORA
RW_GN_EPS = RW_HEAD_DIM * 1e-5

RET_HEADS = 8
RET_HEAD_DIM = 128
RET_WIDTH = RET_HEADS * RET_HEAD_DIM
RET_CHUNK = 128
RET_COLS = 4 * RET_WIDTH

MLA_HEADS = 8
MLA_NOPE = 128
MLA_ROPE = 64
MLA_V = 128
MLA_Q_RANK = 512
MLA_KV_RANK = 256
MLA_WIDTH = MLA_HEADS * MLA_V
MLA_COLS = MLA_Q_RANK + MLA_KV_RANK + MLA_ROPE
ATTN_BLOCK = 128

N_BRANCH = 3
GATE_COLS = N_BRANCH * D_MODEL
IN_COLS = RW_COLS + RET_COLS + MLA_COLS + GATE_COLS

FFN_HIDDEN = -(-8 * D_MODEL // (3 * 256)) * 256

kernel_name = 'hybrid_rwkv7_retention_mla_trunk'

F32 = jnp.float32


def split_last(x, sizes):
    idx, acc = [], 0
    for s in sizes[:-1]:
        acc += s
        idx.append(acc)
    return jnp.split(x, idx, axis=-1)


def rms_norm(x, g, eps=NORM_EPS):
    xf = x.astype(F32)
    y = xf * lax.rsqrt(jnp.mean(xf * xf, axis=-1, keepdims=True) + eps)
    return (y * g.astype(F32)).astype(x.dtype)


def rope_tables(positions, dim):
    inv = ROPE_BASE ** (-jnp.arange(0, dim, 2, dtype=F32) / dim)
    ang = positions.astype(F32)[:, None] * inv[None, :]
    return jnp.cos(ang), jnp.sin(ang)


def apply_rope(x, cos, sin):
    xf = x.astype(F32)
    half = x.shape[-1] // 2
    x1, x2 = xf[..., :half], xf[..., half:]
    return jnp.concatenate([x1 * cos - x2 * sin, x2 * cos + x1 * sin], axis=-1).astype(x.dtype)


def rwkv7_mix(z, mu, w0, w_up, a0, a_up, g_up, k_k, k_a, r_k, ln_w, ln_b):
    B, L, _ = z.shape
    H, N = RW_HEADS, RW_HEAD_DIM
    z_prev = jnp.pad(z, ((0, 0), (1, 0), (0, 0)))[:, :-1]
    z = z + (z_prev - z) * mu
    r, k, v, wd, ad, gd = split_last(z, [RW_WIDTH, RW_WIDTH, RW_WIDTH, RW_DECAY_LORA, RW_A_LORA, RW_GATE_LORA])
    w_log = -jax.nn.softplus(-(w0 + jnp.tanh(wd) @ w_up).astype(F32)) - 0.5
    decay = jnp.exp(-jnp.exp(w_log))
    a = jax.nn.sigmoid(a0 + ad @ a_up)
    g = jax.nn.sigmoid(gd) @ g_up
    kk = (k * k_k).astype(F32).reshape(B, L, H, N)
    kk = kk / jnp.maximum(jnp.sqrt(jnp.sum(kk * kk, axis=-1, keepdims=True)), 1e-12)
    k = k * (1.0 + (a - 1.0) * k_a)

    def heads_t(t):
        return jnp.moveaxis(t.astype(F32).reshape(B, L, H, N), 1, 0)

    xs = (heads_t(r), heads_t(decay), heads_t(k), heads_t(v), jnp.moveaxis(kk, 1, 0), heads_t(a))

    def step(S, inp):
        r_t, w_t, k_t, v_t, kk_t, a_t = inp
        s_kk = jnp.einsum('bhvk,bhk->bhv', S, kk_t)
        S = (S * w_t[:, :, None, :] - s_kk[..., None] * (kk_t * a_t)[:, :, None, :]
             + v_t[..., None] * k_t[:, :, None, :])
        return S, jnp.einsum('bhvk,bhk->bhv', S, r_t)

    S0 = jnp.zeros((B, H, N, N), F32)
    _, y = lax.scan(step, S0, xs)
    y = jnp.moveaxis(y, 0, 1)
    mean = jnp.mean(y, axis=-1, keepdims=True)
    var = jnp.mean(jnp.square(y - mean), axis=-1, keepdims=True)
    y = (y - mean) * lax.rsqrt(var + RW_GN_EPS)
    y = y * ln_w.astype(F32).reshape(H, N) + ln_b.astype(F32).reshape(H, N)
    rh = r.astype(F32).reshape(B, L, H, N)
    kh = k.astype(F32).reshape(B, L, H, N)
    vh = v.astype(F32).reshape(B, L, H, N)
    bonus = jnp.sum(rh * kh * r_k.astype(F32).reshape(H, N), axis=-1, keepdims=True) * vh
    y = (y + bonus).reshape(B, L, RW_WIDTH).astype(z.dtype)
    return y * g


def _ret_chunk(R, q, k, v, log_gamma):
    C = q.shape[1]
    idx = jnp.arange(C, dtype=F32)
    diff = idx[:, None] - idx[None, :]
    causal = diff >= 0
    dmat = jnp.where(causal[None], jnp.exp(log_gamma[:, None, None] * jnp.maximum(diff, 0.0)[None]), 0.0)
    s = jnp.einsum('bihd,bjhd->bhij', q, k) * dmat[None]
    o = jnp.einsum('bhij,bjhe->bihe', s, v)
    cross_decay = jnp.exp(log_gamma[None, :] * (idx[:, None] + 1.0))
    o = o + jnp.einsum('bihd,bhde->bihe', q, R) * cross_decay[None, :, :, None]
    k_decay = jnp.exp(log_gamma[None, :] * (C - 1.0 - idx)[:, None])
    R = (R * jnp.exp(log_gamma * C)[None, :, None, None]
         + jnp.einsum('bjhd,bjhe->bhde', k * k_decay[None, :, :, None], v))
    return R, o


def retention_mix(z, cos, sin):
    B, L, _ = z.shape
    H, d = RET_HEADS, RET_HEAD_DIM
    q, k, v, g = split_last(z, [RET_WIDTH] * 4)
    q = apply_rope(q.reshape(B, L, H, d), cos, sin).astype(F32)
    k = apply_rope(k.reshape(B, L, H, d), cos, sin).astype(F32) * (d ** -0.5)
    v = v.reshape(B, L, H, d).astype(F32)
    log_gamma = jnp.log1p(-jnp.exp2(-5.0 - jnp.arange(H, dtype=F32)))
    R0 = jnp.zeros((B, H, d, d), F32)
    R, o_meta = _ret_chunk(R0, q[:, :N_META], k[:, :N_META], v[:, :N_META], log_gamma)
    n_chunks = (L - N_META) // RET_CHUNK

    def chunks(t):
        return jnp.moveaxis(t[:, N_META:].reshape(B, n_chunks, RET_CHUNK, H, d), 1, 0)

    _, o_real = lax.scan(lambda c, xs: _ret_chunk(c, xs[0], xs[1], xs[2], log_gamma), R,
                         (chunks(q), chunks(k), chunks(v)))
    o_real = jnp.moveaxis(o_real, 0, 1).reshape(B, L - N_META, H, d)
    o = jnp.concatenate([o_meta, o_real], axis=1)
    o = o * lax.rsqrt(jnp.mean(o * o, axis=-1, keepdims=True) + NORM_EPS)
    o = o.reshape(B, L, RET_WIDTH).astype(z.dtype)
    return jax.nn.silu(g) * o


def _attend(qb, k, v, q_pos, scale):
    s = jnp.einsum('bqhd,bkhd->bhqk', qb, k).astype(F32) * scale
    mask = jnp.arange(k.shape[1])[None, :] <= q_pos[:, None]
    s = jnp.where(mask[None, None], s, -jnp.inf)
    p = jax.nn.softmax(s, axis=-1).astype(v.dtype)
    return jnp.einsum('bhqk,bkhd->bqhd', p, v)


def mla_mix(qd, kvd, krd, norm_q, norm_kv, w_uq, w_ukv, cos, sin):
    B, L, _ = qd.shape
    H = MLA_HEADS
    q = (rms_norm(qd, norm_q) @ w_uq).reshape(B, L, H, MLA_NOPE + MLA_ROPE)
    q = jnp.concatenate([q[..., :MLA_NOPE], apply_rope(q[..., MLA_NOPE:], cos[:, None], sin[:, None])], axis=-1)
    kv = (rms_norm(kvd, norm_kv) @ w_ukv).reshape(B, L, H, MLA_NOPE + MLA_V)
    k_nope, v = kv[..., :MLA_NOPE], kv[..., MLA_NOPE:]
    k_rope = apply_rope(krd, cos, sin)
    k = jnp.concatenate([k_nope, jnp.broadcast_to(k_rope[:, :, None, :], (B, L, H, MLA_ROPE))], axis=-1)
    scale = (MLA_NOPE + MLA_ROPE) ** -0.5
    o_meta = _attend(q[:, :N_META], k[:, :N_META], v[:, :N_META], jnp.arange(N_META), scale)
    n_blocks = (L - N_META) // ATTN_BLOCK
    q_blocks = jnp.moveaxis(q[:, N_META:].reshape(B, n_blocks, ATTN_BLOCK, H, MLA_NOPE + MLA_ROPE), 1, 0)
    o_real = lax.map(lambda a: _attend(a[0], k, v, N_META + a[1] * ATTN_BLOCK + jnp.arange(ATTN_BLOCK), scale),
                     (q_blocks, jnp.arange(n_blocks)))
    o_real = jnp.moveaxis(o_real, 0, 1).reshape(B, L - N_META, H, MLA_V)
    return jnp.concatenate([o_meta, o_real], axis=1).reshape(B, L, MLA_WIDTH)


def hybrid_layer(h, cos_ret, sin_ret, cos_mla, sin_mla, norm_mix, w_in, rw_mu, rw_w0, rw_w_up, rw_a0,
                 rw_a_up, rw_g_up, rw_k_k, rw_k_a, rw_r_k, rw_ln_w, rw_ln_b, mla_norm_q, mla_norm_kv,
                 mla_w_uq, mla_w_ukv, w_br_rwkv, w_br_ret, w_br_mla, w_out, norm_ffn, w_gate_up, w_down):
    u = rms_norm(h, norm_mix)
    p = u @ w_in
    z_rw, z_ret, qd, kvd, krd, gates = split_last(p, [RW_COLS, RET_COLS, MLA_Q_RANK, MLA_KV_RANK, MLA_ROPE, GATE_COLS])
    y_a = rwkv7_mix(z_rw, rw_mu, rw_w0, rw_w_up, rw_a0, rw_a_up, rw_g_up, rw_k_k, rw_k_a, rw_r_k, rw_ln_w, rw_ln_b)
    y_b = retention_mix(z_ret, cos_ret, sin_ret)
    y_c = mla_mix(qd, kvd, krd, mla_norm_q, mla_norm_kv, mla_w_uq, mla_w_ukv, cos_mla, sin_mla)
    g_a, g_b, g_c = jnp.split(jax.nn.sigmoid(gates), N_BRANCH, axis=-1)
    merged = g_a * (y_a @ w_br_rwkv) + g_b * (y_b @ w_br_ret) + g_c * (y_c @ w_br_mla)
    h = h + merged @ w_out
    u = rms_norm(h, norm_ffn)
    hg, hu = jnp.split(u @ w_gate_up, 2, axis=-1)
    return h + (jax.nn.silu(hg) * hu) @ w_down


def setup_inputs(seed: int = 0) -> dict:
    key = jax.random.key(seed)
    ks = iter(jax.random.split(key, 40))

    def nrm(shape, scale):
        return jax.random.normal(next(ks), shape, F32) * scale

    def gain(shape):
        return 1.0 + nrm(shape, 0.02)

    L_ = DEPTH
    return {
        'x': nrm((BATCH, SEQ, D_MODEL), 1.0),
        'meta_tokens': nrm((N_META, D_MODEL), 1.0),
        'norm_mix': gain((L_, D_MODEL)),
        'w_in': nrm((L_, D_MODEL, IN_COLS), D_MODEL ** -0.5),
        'rw_mu': jax.random.uniform(next(ks), (L_, RW_COLS), F32),
        'rw_w0': jax.random.uniform(next(ks), (L_, RW_WIDTH), F32, minval=-6.0, maxval=1.0),
        'rw_w_up': nrm((L_, RW_DECAY_LORA, RW_WIDTH), RW_DECAY_LORA ** -0.5),
        'rw_a0': nrm((L_, RW_WIDTH), 0.1),
        'rw_a_up': nrm((L_, RW_A_LORA, RW_WIDTH), RW_A_LORA ** -0.5),
        'rw_g_up': nrm((L_, RW_GATE_LORA, RW_WIDTH), RW_GATE_LORA ** -0.5),
        'rw_k_k': 0.85 + nrm((L_, RW_WIDTH), 0.02),
        'rw_k_a': 1.0 + nrm((L_, RW_WIDTH), 0.02),
        'rw_r_k': nrm((L_, RW_WIDTH), 0.1),
        'rw_ln_w': gain((L_, RW_WIDTH)),
        'rw_ln_b': nrm((L_, RW_WIDTH), 0.02),
        'mla_norm_q': gain((L_, MLA_Q_RANK)),
        'mla_norm_kv': gain((L_, MLA_KV_RANK)),
        'mla_w_uq': nrm((L_, MLA_Q_RANK, MLA_HEADS * (MLA_NOPE + MLA_ROPE)), MLA_Q_RANK ** -0.5),
        'mla_w_ukv': nrm((L_, MLA_KV_RANK, MLA_HEADS * (MLA_NOPE + MLA_V)), MLA_KV_RANK ** -0.5),
        'w_br_rwkv': nrm((L_, RW_WIDTH, D_MODEL), RW_WIDTH ** -0.5),
        'w_br_ret': nrm((L_, RET_WIDTH, D_MODEL), RET_WIDTH ** -0.5),
        'w_br_mla': nrm((L_, MLA_WIDTH, D_MODEL), MLA_WIDTH ** -0.5),
        'w_out': nrm((L_, D_MODEL, D_MODEL), D_MODEL ** -0.5),
        'norm_ffn': gain((L_, D_MODEL)),
        'w_gate_up': nrm((L_, D_MODEL, 2 * FFN_HIDDEN), D_MODEL ** -0.5),
        'w_down': nrm((L_, FFN_HIDDEN, D_MODEL), FFN_HIDDEN ** -0.5),
        'final_norm': gain((D_MODEL,)),
    }


def reference(x, meta_tokens, norm_mix, w_in, rw_mu, rw_w0, rw_w_up, rw_a0, rw_a_up, rw_g_up, rw_k_k,
              rw_k_a, rw_r_k, rw_ln_w, rw_ln_b, mla_norm_q, mla_norm_kv, mla_w_uq, mla_w_ukv, w_br_rwkv,
              w_br_ret, w_br_mla, w_out, norm_ffn, w_gate_up, w_down, final_norm):
    B = x.shape[0]
    meta = jnp.broadcast_to(meta_tokens[None].astype(x.dtype), (B, N_META, x.shape[-1]))
    h = jnp.concatenate([meta, x], axis=1)
    pos = jnp.arange(h.shape[1])
    cos_ret, sin_ret = rope_tables(pos, RET_HEAD_DIM)
    cos_ret, sin_ret = cos_ret[:, None, :], sin_ret[:, None, :]
    cos_mla, sin_mla = rope_tables(pos, MLA_ROPE)
    for l in range(DEPTH):
        h = hybrid_layer(h, cos_ret, sin_ret, cos_mla, sin_mla, norm_mix[l], w_in[l], rw_mu[l], rw_w0[l],
                         rw_w_up[l], rw_a0[l], rw_a_up[l], rw_g_up[l], rw_k_k[l], rw_k_a[l], rw_r_k[l],
                         rw_ln_w[l], rw_ln_b[l], mla_norm_q[l], mla_norm_kv[l], mla_w_uq[l], mla_w_ukv[l],
                         w_br_rwkv[l], w_br_ret[l], w_br_mla[l], w_out[l], norm_ffn[l], w_gate_up[l], w_down[l])
    return rms_norm(h[:, N_META:], final_norm)
```

```python
import functools
import math

import jax
import jax.numpy as jnp
from jax import lax
from jax.experimental import pallas as pl
from jax.experimental.pallas import tpu as pltpu

F32 = jnp.float32
BF16 = jnp.bfloat16
HIGHEST = lax.Precision.HIGHEST

D_MODEL = 2048
N_META = 16
NORM_EPS = 1e-6
ROPE_BASE = 10000.0

RW_HEADS = 16
RW_HEAD_DIM = 64
RW_WIDTH = RW_HEADS * RW_HEAD_DIM
RW_DECAY_LORA = 96
RW_A_LORA = 96
RW_GATE_LORA = 256
RW_GN_EPS = RW_HEAD_DIM * 1e-5
RW_CHUNK = 64
RW_PAIRS = RW_WIDTH // 128

RET_HEADS = 8
RET_HEAD_DIM = 128
RET_WIDTH = RET_HEADS * RET_HEAD_DIM
RET_CHUNK = 128

MLA_HEADS = 8
MLA_NOPE = 128
MLA_ROPE = 64
MLA_V = 128
MLA_Q_RANK = 512
MLA_KV_RANK = 256
MLA_WIDTH = MLA_HEADS * MLA_V
MLA_QK_PAD = 256

FFN_HIDDEN = -(-8 * D_MODEL // (3 * 256)) * 256

LANE = 128
SEQ_ALIGN = 128

OFF_RET = 0
OFF_GATE = OFF_RET + 4 * RET_WIDTH
OFF_RW = OFF_GATE + 3 * D_MODEL
OFF_RW_WD = OFF_RW + 3 * RW_WIDTH
OFF_RW_AD = OFF_RW_WD + LANE
OFF_RW_GD = OFF_RW_AD + LANE
OFF_MLA_Q = OFF_RW_GD + RW_GATE_LORA
OFF_MLA_KV = OFF_MLA_Q + MLA_Q_RANK
OFF_MLA_KR = OFF_MLA_KV + MLA_KV_RANK
P_COLS_USED = OFF_MLA_KR + LANE
P_TILE_N = 512
P_COLS = -(-P_COLS_USED // P_TILE_N) * P_TILE_N

VMEM_LIMIT = 48 * 1024 * 1024


def _params(*sem):
    return pltpu.CompilerParams(dimension_semantics=sem, vmem_limit_bytes=VMEM_LIMIT)


def _sigmoid(x):
    return 1.0 / (1.0 + jnp.exp(-x))


def _dot(a, b):
    return jnp.dot(a, b, preferred_element_type=F32)


def _dot_nt(a, b):
    return lax.dot_general(a, b, (((1,), (1,)), ((), ())), preferred_element_type=F32)


def _dot_tn(a, b):
    return lax.dot_general(a, b, (((0,), (0,)), ((), ())), preferred_element_type=F32)


def _rmsnorm_body(x_ref, g_ref, o_ref):
    x = x_ref[...]
    y = x * lax.rsqrt(jnp.mean(x * x, axis=-1, keepdims=True) + NORM_EPS)
    o_ref[...] = (y * g_ref[...]).astype(o_ref.dtype)


def rmsnorm(x, g, tm, out_dtype):
    m, d = x.shape
    return pl.pallas_call(
        _rmsnorm_body,
        grid=(m // tm,),
        in_specs=[pl.BlockSpec((tm, d), lambda i: (i, 0)), pl.BlockSpec((1, d), lambda i: (0, 0))],
        out_specs=pl.BlockSpec((tm, d), lambda i: (i, 0)),
        out_shape=jax.ShapeDtypeStruct((m, d), out_dtype),
        compiler_params=_params("parallel"),
        name="rmsnorm",
    )(x, g.reshape(1, d))


def _matmul_body(x_ref, w_ref, o_ref):
    o_ref[...] = _dot(x_ref[...], w_ref[...]).astype(o_ref.dtype)


def matmul(x, w, tm, tn, out_dtype):
    m, k = x.shape
    n = w.shape[1]
    return pl.pallas_call(
        _matmul_body,
        grid=(m // tm, n // tn),
        in_specs=[pl.BlockSpec((tm, k), lambda i, j: (i, 0)), pl.BlockSpec((k, tn), lambda i, j: (0, j))],
        out_specs=pl.BlockSpec((tm, tn), lambda i, j: (i, j)),
        out_shape=jax.ShapeDtypeStruct((m, n), out_dtype),
        compiler_params=_params("parallel", "parallel"),
        name="in_proj",
    )(x, w)


def _rwkv_body(r_ref, k_ref, v_ref, wd_ref, ad_ref, gd_ref,
               mur_ref, muk_ref, muv_ref, muwd_ref, muad_ref, mugd_ref,
               w0_ref, a0_ref, kk_ref, ka_ref, rk_ref, lnw_ref, lnb_ref,
               wup_ref, aup_ref, gup_ref,
               o_ref,
               s_ref, pr_ref, pk_ref, pv_ref, pwd_ref, pad_ref, pgd_ref):
    c = pl.program_id(2)
    C = RW_CHUNK
    HD = RW_HEAD_DIM

    @pl.when(c == 0)
    def _():
        s_ref[...] = jnp.zeros_like(s_ref)
        pr_ref[...] = jnp.zeros_like(pr_ref)
        pk_ref[...] = jnp.zeros_like(pk_ref)
        pv_ref[...] = jnp.zeros_like(pv_ref)
        pwd_ref[...] = jnp.zeros_like(pwd_ref)
        pad_ref[...] = jnp.zeros_like(pad_ref)
        pgd_ref[...] = jnp.zeros_like(pgd_ref)

    def shift(x_ref, prev_ref, mu_ref):
        z = x_ref[...]
        first = lax.broadcasted_iota(jnp.int32, z.shape, 0) == 0
        zs = jnp.where(first, prev_ref[0:1, :], pltpu.roll(z, 1, 0))
        prev_ref[0:1, :] = z[C - 1:C, :]
        return z + (zs - z) * mu_ref[...]

    r = shift(r_ref, pr_ref, mur_ref)
    k = shift(k_ref, pk_ref, muk_ref)
    v = shift(v_ref, pv_ref, muv_ref)
    wd = shift(wd_ref, pwd_ref, muwd_ref)
    ad = shift(ad_ref, pad_ref, muad_ref)
    gd = shift(gd_ref, pgd_ref, mugd_ref)

    wl = w0_ref[...] + _dot(jnp.tanh(wd).astype(BF16), wup_ref[...])
    x = -wl
    softplus = jnp.maximum(x, 0.0) + jnp.log1p(jnp.exp(-jnp.abs(x)))
    lw = -jnp.exp(-softplus - 0.5)
    a = _sigmoid(a0_ref[...] + _dot(ad.astype(BF16), aup_ref[...]))
    g = _dot(_sigmoid(gd).astype(BF16), gup_ref[...])

    lane_sq = lax.broadcasted_iota(jnp.int32, (LANE, LANE), 1)
    row_sq = lax.broadcasted_iota(jnp.int32, (LANE, LANE), 0)
    same_head = (lane_sq < HD) == (row_sq < HD)
    head_ones = same_head.astype(F32)

    def head_sum(t):
        return jnp.dot(t, head_ones, precision=HIGHEST, preferred_element_type=F32)

    kkr = k * kk_ref[...]
    kkn = kkr / jnp.maximum(jnp.sqrt(head_sum(kkr * kkr)), 1e-12)
    kmod = k * (1.0 + (a - 1.0) * ka_ref[...])
    beta = a * kkn

    rc = lax.broadcasted_iota(jnp.int32, (C, C), 0)
    cc = lax.broadcasted_iota(jnp.int32, (C, C), 1)
    tril_incl = (cc <= rc).astype(F32)
    lcum = jnp.dot(tril_incl, lw, precision=HIGHEST, preferred_element_type=F32)
    lend = lcum[C - 1:C, :]
    e_pos = jnp.exp(lcum)
    e_neg = jnp.exp(-lcum)
    e_prev = jnp.exp(lcum - lw)
    e_end = jnp.exp(lend - lcum)

    rh = r * e_pos
    kh = kkn * e_prev
    kb = kmod * e_neg
    bb = beta * e_neg
    kbe = kmod * e_end
    bbe = beta * e_end

    kr_f = jnp.concatenate([kh, rh], axis=0)
    bk = jnp.concatenate([bb, kb], axis=0).astype(BF16)
    vb = v.astype(BF16)
    s0 = s_ref[...]
    p_all = _dot_nt(kr_f.astype(BF16), s0.astype(BF16))

    lane_tall = lax.broadcasted_iota(jnp.int32, (2 * C, LANE), 1)
    lane_c = lax.broadcasted_iota(jnp.int32, (C, LANE), 1)
    row_c = lax.broadcasted_iota(jnp.int32, (C, LANE), 0)
    head0_c = lane_c < HD
    eye = (cc == rc).astype(F32)
    vv = jnp.concatenate([vb, vb], axis=0)

    gs, us = [], []
    for j in range(2):
        in_head = (lane_tall < HD) if j == 0 else (lane_tall >= HD)
        gj = _dot_nt(jnp.where(in_head, kr_f, 0.0).astype(BF16), bk)
        gs.append(gj)
        n = jnp.where(cc < rc, -gj[:C, :C], 0.0)
        t = eye + n
        pw = n
        for _ in range(5):
            pwb = pw.astype(BF16)
            pw = _dot(pwb, pwb)
            t = t + _dot(t.astype(BF16), pw.astype(BF16))
        m1 = jnp.where((lane_c >= C) & (lane_c - C < row_c), gj[:C, :], 0.0)
        q = p_all[:C, :] + _dot(m1.astype(BF16), vv)
        us.append(_dot(t.astype(BF16), q.astype(BF16)))
    u = jnp.where(head0_c, us[0], us[1])

    uv = jnp.concatenate([u, v], axis=0).astype(BF16)
    ys = []
    for j in range(2):
        gb = gs[j][C:, :]
        m2 = jnp.where(lane_c < C,
                       jnp.where(lane_c <= row_c, -gb, 0.0),
                       jnp.where(lane_c - C <= row_c, gb, 0.0))
        ys.append(_dot(m2.astype(BF16), uv))
    y = p_all[C:, :] + jnp.where(head0_c, ys[0], ys[1])

    vu = jnp.concatenate([v, u], axis=0).astype(BF16)
    kbe_all = jnp.concatenate([kbe, -bbe], axis=0).astype(BF16)
    ds = _dot_tn(vu, kbe_all)
    s_ref[...] = s0 * jnp.exp(lend) + jnp.where(same_head, ds, 0.0)

    inv_n = 1.0 / HD
    mean = head_sum(y) * inv_n
    d = y - mean
    var = head_sum(d * d) * inv_n
    yn = d * lax.rsqrt(var + RW_GN_EPS) * lnw_ref[...] + lnb_ref[...]
    bonus = head_sum(r * kmod * rk_ref[...]) * v
    o_ref[...] = ((yn + bonus) * g).astype(o_ref.dtype)


def rwkv_mix(p, batch, mu, w0, a0, k_k, k_a, r_k, ln_w, ln_b, w_up, a_up, g_up):
    tp = p.shape[0]
    lp = tp // batch
    C = RW_CHUNK
    nchunk = lp // C

    def rows(b, q, c):
        return b * nchunk + c

    def col_spec(width, base):
        return pl.BlockSpec((C, width), lambda b, q, c: (rows(b, q, c), base // width + q))

    def shared_spec(width, base):
        return pl.BlockSpec((C, width), lambda b, q, c: (rows(b, q, c), base // width))

    def vec_pair(base):
        return pl.BlockSpec((1, LANE), lambda b, q, c: (0, base // LANE + q))

    def vec_fixed(width, base):
        return pl.BlockSpec((1, width), lambda b, q, c: (0, base // width))

    mu_off = 3 * RW_WIDTH
    in_specs = [
        col_spec(LANE, OFF_RW), col_spec(LANE, OFF_RW + RW_WIDTH), col_spec(LANE, OFF_RW + 2 * RW_WIDTH),
        shared_spec(LANE, OFF_RW_WD), shared_spec(LANE, OFF_RW_AD), shared_spec(RW_GATE_LORA, OFF_RW_GD),
        vec_pair(0), vec_pair(RW_WIDTH), vec_pair(2 * RW_WIDTH),
        vec_fixed(LANE, mu_off), vec_fixed(LANE, mu_off + LANE), vec_fixed(RW_GATE_LORA, mu_off + 2 * LANE),
    ] + [vec_pair(0)] * 7 + [
        pl.BlockSpec((LANE, LANE), lambda b, q, c: (0, q)),
        pl.BlockSpec((LANE, LANE), lambda b, q, c: (0, q)),
        pl.BlockSpec((RW_GATE_LORA, LANE), lambda b, q, c: (0, q)),
    ]
    row = lambda t: t.reshape(1, -1)
    return pl.pallas_call(
        _rwkv_body,
        grid=(batch, RW_PAIRS, nchunk),
        in_specs=in_specs,
        out_specs=pl.BlockSpec((C, LANE), lambda b, q, c: (rows(b, q, c), q)),
        out_shape=jax.ShapeDtypeStruct((tp, RW_WIDTH), BF16),
        scratch_shapes=[pltpu.VMEM((LANE, LANE), F32)] + [pltpu.VMEM((8, LANE), F32)] * 5
        + [pltpu.VMEM((8, RW_GATE_LORA), F32)],
        compiler_params=_params("parallel", "parallel", "arbitrary"),
        name="rwkv7_mix",
    )(p, p, p, p, p, p, mu, mu, mu, mu, mu, mu,
      row(w0), row(a0), row(k_k), row(k_a), row(r_k), row(ln_w), row(ln_b), w_up, a_up, g_up)


def _ret_body(q_ref, k_ref, v_ref, g_ref, cos_ref, sin_ref, o_ref, state_ref):
    c = pl.program_id(1)
    C = RET_CHUNK
    d = RET_HEAD_DIM

    @pl.when(c == 0)
    def _():
        state_ref[...] = jnp.zeros_like(state_ref)

    row = lax.broadcasted_iota(jnp.int32, (C, C), 0).astype(F32)
    col = lax.broadcasted_iota(jnp.int32, (C, C), 1).astype(F32)
    diff = row - col
    causal = diff >= 0
    cos = cos_ref[...]
    sin = sin_ref[...]
    for h in range(RET_HEADS):
        lg = math.log1p(-(2.0 ** (-5.0 - h)))
        sl = slice(d * h, d * (h + 1))
        q = q_ref[:, sl]
        k = k_ref[:, sl]
        q = q * cos + pltpu.roll(q, d // 2, 1) * sin
        k = (k * cos + pltpu.roll(k, d // 2, 1) * sin) * (d ** -0.5)
        g = g_ref[:, sl]
        qb = q.astype(BF16)
        kb = k.astype(BF16)
        vb = v_ref[:, sl].astype(BF16)
        dmat = jnp.where(causal, jnp.exp(lg * jnp.maximum(diff, 0.0)), 0.0)
        s = _dot_nt(qb, kb) * dmat
        state = state_ref[h]
        o = _dot(s.astype(BF16), vb) + _dot(qb, state.astype(BF16)) * jnp.exp(lg * (row + 1.0))
        kd = (k * jnp.exp(lg * (C - 1.0 - row))).astype(BF16)
        state_ref[h] = state * math.exp(lg * C) + _dot_tn(kd, vb)
        o = o * lax.rsqrt(jnp.mean(o * o, axis=-1, keepdims=True) + NORM_EPS)
        o_ref[:, sl] = (g * _sigmoid(g) * o).astype(o_ref.dtype)


def retention_mix(p, batch, cos, sin):
    tp = p.shape[0]
    lp = tp // batch
    C = RET_CHUNK
    nchunk = lp // C
    base = OFF_RET // RET_WIDTH

    def pspec(j):
        return pl.BlockSpec((C, RET_WIDTH), lambda b, c: (b * nchunk + c, base + j))

    tab = pl.BlockSpec((C, RET_HEAD_DIM), lambda b, c: (c, 0))
    return pl.pallas_call(
        _ret_body,
        grid=(batch, nchunk),
        in_specs=[pspec(0), pspec(1), pspec(2), pspec(3), tab, tab],
        out_specs=pl.BlockSpec((C, RET_WIDTH), lambda b, c: (b * nchunk + c, 0)),
        out_shape=jax.ShapeDtypeStruct((tp, RET_WIDTH), BF16),
        scratch_shapes=[pltpu.VMEM((RET_HEADS, RET_HEAD_DIM, RET_HEAD_DIM), F32)],
        compiler_params=_params("parallel", "arbitrary"),
        name="retention_mix",
    )(p, p, p, p, cos, sin)


def _mla_proj_body(qd_ref, kvd_ref, krd_ref, nq_ref, nkv_ref, wuq_ref, wukv_ref, cos_ref, sin_ref,
                   q_out, k_out, v_out):
    cos = cos_ref[...]
    sin = sin_ref[...]

    def rope(x):
        return x * cos + (pltpu.roll(x, MLA_ROPE // 2, 1) + pltpu.roll(x, LANE - MLA_ROPE // 2, 1)) * sin

    def norm(x, g):
        return x * lax.rsqrt(jnp.mean(x * x, axis=-1, keepdims=True) + NORM_EPS) * g

    scale = (MLA_NOPE + MLA_ROPE) ** -0.5
    q = _dot(norm(qd_ref[...], nq_ref[...]).astype(BF16), wuq_ref[...]) * scale
    kv = _dot(norm(kvd_ref[...], nkv_ref[...]).astype(BF16), wukv_ref[...])
    kr = rope(krd_ref[...]).astype(k_out.dtype)
    for h in range(MLA_HEADS):
        lo = MLA_QK_PAD * h
        q_out[:, lo:lo + LANE] = q[:, lo:lo + LANE].astype(q_out.dtype)
        q_out[:, lo + LANE:lo + 2 * LANE] = rope(q[:, lo + LANE:lo + 2 * LANE]).astype(q_out.dtype)
        k_out[:, lo:lo + LANE] = kv[:, MLA_NOPE * h:MLA_NOPE * (h + 1)].astype(k_out.dtype)
        k_out[:, lo + LANE:lo + 2 * LANE] = kr
    v_out[...] = kv[:, MLA_HEADS * MLA_NOPE:].astype(v_out.dtype)


def mla_proj(p, batch, norm_q, norm_kv, w_uq, w_ukv, cos, sin, tm):
    tp = p.shape[0]
    lp = tp // batch
    per_seq = lp // tm
    qk_w = MLA_HEADS * MLA_QK_PAD
    const = lambda i: (0, 0)
    return pl.pallas_call(
        _mla_proj_body,
        grid=(tp // tm,),
        in_specs=[
            pl.BlockSpec((tm, MLA_Q_RANK), lambda i: (i, OFF_MLA_Q // MLA_Q_RANK)),
            pl.BlockSpec((tm, MLA_KV_RANK), lambda i: (i, OFF_MLA_KV // MLA_KV_RANK)),
            pl.BlockSpec((tm, LANE), lambda i: (i, OFF_MLA_KR // LANE)),
            pl.BlockSpec((1, MLA_Q_RANK), const),
            pl.BlockSpec((1, MLA_KV_RANK), const),
            pl.BlockSpec((MLA_Q_RANK, qk_w), const),
            pl.BlockSpec((MLA_KV_RANK, MLA_HEADS * (MLA_NOPE + MLA_V)), const),
            pl.BlockSpec((tm, LANE), lambda i: (i % per_seq, 0)),
            pl.BlockSpec((tm, LANE), lambda i: (i % per_seq, 0)),
        ],
        out_specs=[
            pl.BlockSpec((tm, qk_w), lambda i: (i, 0)),
            pl.BlockSpec((tm, qk_w), lambda i: (i, 0)),
            pl.BlockSpec((tm, MLA_WIDTH), lambda i: (i, 0)),
        ],
        out_shape=[
            jax.ShapeDtypeStruct((tp, qk_w), BF16),
            jax.ShapeDtypeStruct((tp, qk_w), BF16),
            jax.ShapeDtypeStruct((tp, MLA_WIDTH), BF16),
        ],
        compiler_params=_params("parallel"),
        name="mla_proj",
    )(p, p, p, norm_q.reshape(1, -1), norm_kv.reshape(1, -1), w_uq, w_ukv, cos, sin)


def _attn_body(q_ref, k_ref, v_ref, o_ref, *, tq):
    i = pl.program_id(2)
    q = q_ref[...]
    row = lax.broadcasted_iota(jnp.int32, (tq, tq), 0)
    col = lax.broadcasted_iota(jnp.int32, (tq, tq), 1)

    def step(j, carry, diagonal):
        m, l, acc = carry
        off = pl.multiple_of(j * tq, tq)
        kblk = k_ref[pl.ds(off, tq), :]
        vblk = v_ref[pl.ds(off, tq), :]
        s = _dot_nt(q, kblk)
        if diagonal:
            s = jnp.where(col <= row, s, -jnp.inf)
        m_new = jnp.maximum(m, jnp.max(s, axis=-1, keepdims=True))
        alpha = jnp.exp(m - m_new)
        pmat = jnp.exp(s - m_new)
        l = alpha * l + jnp.sum(pmat, axis=-1, keepdims=True)
        acc = alpha * acc + _dot(pmat.astype(BF16), vblk)
        return m_new, l, acc

    init = (jnp.full((tq, 1), -1e30, F32), jnp.zeros((tq, 1), F32), jnp.zeros((tq, MLA_V), F32))
    carry = lax.fori_loop(0, i, lambda j, cr: step(j, cr, False), init)
    _, l, acc = step(i, carry, True)
    o_ref[...] = (acc / l).astype(o_ref.dtype)


def mla_attention(q, k, v, batch, tq):
    tp = q.shape[0]
    lp = tp // batch
    nq = lp // tq
    return pl.pallas_call(
        functools.partial(_attn_body, tq=tq),
        grid=(batch, MLA_HEADS, nq),
        in_specs=[
            pl.BlockSpec((tq, MLA_QK_PAD), lambda b, h, i: (b * nq + i, h)),
            pl.BlockSpec((lp, MLA_QK_PAD), lambda b, h, i: (b, h)),
            pl.BlockSpec((lp, MLA_V), lambda b, h, i: (b, h)),
        ],
        out_specs=pl.BlockSpec((tq, MLA_V), lambda b, h, i: (b * nq + i, h)),
        out_shape=jax.ShapeDtypeStruct((tp, MLA_WIDTH), BF16),
        compiler_params=_params("parallel", "parallel", "arbitrary"),
        name="mla_attention",
    )(q, k, v)


def _merge_body(ya_ref, yb_ref, yc_ref, wa_ref, wb_ref, wc_ref, ga_ref, gb_ref, gc_ref, o_ref):
    def branch(y_ref, w_ref, g_ref):
        return _sigmoid(g_ref[...]) * _dot(y_ref[...], w_ref[...])

    o_ref[...] = (branch(ya_ref, wa_ref, ga_ref) + branch(yb_ref, wb_ref, gb_ref)
                  + branch(yc_ref, wc_ref, gc_ref)).astype(o_ref.dtype)


def merge_branches(ya, yb, yc, wa, wb, wc, p, tm, tn):
    tp = ya.shape[0]

    def yspec(width):
        return pl.BlockSpec((tm, width), lambda i, j: (i, 0))

    def wspec(width):
        return pl.BlockSpec((width, tn), lambda i, j: (0, j))

    def gspec(branch):
        base = (OFF_GATE + branch * D_MODEL) // tn
        return pl.BlockSpec((tm, tn), lambda i, j: (i, base + j))

    return pl.pallas_call(
        _merge_body,
        grid=(tp // tm, D_MODEL // tn),
        in_specs=[yspec(RW_WIDTH), yspec(RET_WIDTH), yspec(MLA_WIDTH),
                  wspec(RW_WIDTH), wspec(RET_WIDTH), wspec(MLA_WIDTH),
                  gspec(0), gspec(1), gspec(2)],
        out_specs=pl.BlockSpec((tm, tn), lambda i, j: (i, j)),
        out_shape=jax.ShapeDtypeStruct((tp, D_MODEL), BF16),
        compiler_params=_params("parallel", "parallel"),
        name="merge_branches",
    )(ya, yb, yc, wa, wb, wc, p, p, p)


def _resid_body(x_ref, w_ref, h_ref, nw_ref, hn_ref, u_ref, acc_ref, *, nk):
    kstep = pl.program_id(1)

    @pl.when(kstep == 0)
    def _():
        acc_ref[...] = jnp.zeros_like(acc_ref)

    acc_ref[...] += _dot(x_ref[...], w_ref[...])

    @pl.when(kstep == nk - 1)
    def _():
        hn = h_ref[...] + acc_ref[...]
        hn_ref[...] = hn
        y = hn * lax.rsqrt(jnp.mean(hn * hn, axis=-1, keepdims=True) + NORM_EPS)
        u_ref[...] = (y * nw_ref[...]).astype(u_ref.dtype)


def resid_matmul_norm(x, w, h, norm_w, tm, tk, u_dtype):
    m, kdim = x.shape
    n = w.shape[1]
    nk = kdim // tk
    return pl.pallas_call(
        functools.partial(_resid_body, nk=nk),
        grid=(m // tm, nk),
        in_specs=[
            pl.BlockSpec((tm, tk), lambda i, k: (i, k)),
            pl.BlockSpec((tk, n), lambda i, k: (k, 0)),
            pl.BlockSpec((tm, n), lambda i, k: (i, 0)),
            pl.BlockSpec((1, n), lambda i, k: (0, 0)),
        ],
        out_specs=[pl.BlockSpec((tm, n), lambda i, k: (i, 0)), pl.BlockSpec((tm, n), lambda i, k: (i, 0))],
        out_shape=[jax.ShapeDtypeStruct((m, n), F32), jax.ShapeDtypeStruct((m, n), u_dtype)],
        scratch_shapes=[pltpu.VMEM((tm, n), F32)],
        compiler_params=_params("parallel", "arbitrary"),
        name="resid_matmul_norm",
    )(x, w, h, norm_w.reshape(1, n))


def _ffn_up_body(x_ref, wg_ref, wu_ref, o_ref):
    x = x_ref[...]
    hg = _dot(x, wg_ref[...])
    hu = _dot(x, wu_ref[...])
    o_ref[...] = (hg * _sigmoid(hg) * hu).astype(o_ref.dtype)


def ffn_up(x, w_gate_up, tm, tn):
    m, kdim = x.shape
    hidden = w_gate_up.shape[1] // 2
    nj = hidden // tn
    return pl.pallas_call(
        _ffn_up_body,
        grid=(m // tm, nj),
        in_specs=[
            pl.BlockSpec((tm, kdim), lambda i, j: (i, 0)),
            pl.BlockSpec((kdim, tn), lambda i, j: (0, j)),
            pl.BlockSpec((kdim, tn), lambda i, j: (0, nj + j)),
        ],
        out_specs=pl.BlockSpec((tm, tn), lambda i, j: (i, j)),
        out_shape=jax.ShapeDtypeStruct((m, hidden), BF16),
        compiler_params=_params("parallel", "parallel"),
        name="ffn_up",
    )(x, w_gate_up, w_gate_up)


def _pad_cols(w, width):
    return jnp.pad(w, [(0, 0)] * (w.ndim - 1) + [(0, width - w.shape[-1])])


def _pack_w_in(w_in):
    rw_cols = 3 * RW_WIDTH + RW_DECAY_LORA + RW_A_LORA + RW_GATE_LORA
    ret_cols = 4 * RET_WIDTH
    o = 0
    rw = w_in[..., o:o + rw_cols]
    o += rw_cols
    ret = w_in[..., o:o + ret_cols]
    o += ret_cols
    qd = w_in[..., o:o + MLA_Q_RANK]
    o += MLA_Q_RANK
    kvd = w_in[..., o:o + MLA_KV_RANK]
    o += MLA_KV_RANK
    krd = w_in[..., o:o + MLA_ROPE]
    o += MLA_ROPE
    gates = w_in[..., o:]
    rkv = rw[..., :3 * RW_WIDTH]
    wd = rw[..., 3 * RW_WIDTH:3 * RW_WIDTH + RW_DECAY_LORA]
    ad = rw[..., 3 * RW_WIDTH + RW_DECAY_LORA:3 * RW_WIDTH + RW_DECAY_LORA + RW_A_LORA]
    gd = rw[..., 3 * RW_WIDTH + RW_DECAY_LORA + RW_A_LORA:]
    parts = [ret, gates, rkv, _pad_cols(wd, LANE), _pad_cols(ad, LANE), gd, qd, kvd, _pad_cols(krd, LANE)]
    packed = jnp.concatenate(parts, axis=-1)
    return _pad_cols(packed, P_COLS).astype(BF16)


def _pack_mu(mu):
    rkv = mu[..., :3 * RW_WIDTH]
    wd = mu[..., 3 * RW_WIDTH:3 * RW_WIDTH + RW_DECAY_LORA]
    ad = mu[..., 3 * RW_WIDTH + RW_DECAY_LORA:3 * RW_WIDTH + RW_DECAY_LORA + RW_A_LORA]
    gd = mu[..., 3 * RW_WIDTH + RW_DECAY_LORA + RW_A_LORA:]
    return jnp.concatenate([rkv, _pad_cols(wd, LANE), _pad_cols(ad, LANE), gd], axis=-1)


def _pad_rows(w, rows):
    return jnp.pad(w, [(0, 0)] * (w.ndim - 2) + [(0, rows - w.shape[-2]), (0, 0)])


def _pack_w_uq(w):
    nl, rank, _ = w.shape
    w = w.reshape(nl, rank, MLA_HEADS, MLA_NOPE + MLA_ROPE)
    w = jnp.pad(w, ((0, 0), (0, 0), (0, 0), (0, MLA_QK_PAD - MLA_NOPE - MLA_ROPE)))
    return w.reshape(nl, rank, MLA_HEADS * MLA_QK_PAD).astype(BF16)


def _pack_w_ukv(w):
    nl, rank, _ = w.shape
    w = w.reshape(nl, rank, MLA_HEADS, 2, MLA_NOPE)
    w = jnp.swapaxes(w, 2, 3)
    return w.reshape(nl, rank, 2 * MLA_HEADS * MLA_NOPE).astype(BF16)


def _rope_tables(lp):
    pos = jnp.arange(lp, dtype=F32)

    def tables(dim):
        inv = ROPE_BASE ** (-jnp.arange(0, dim, 2, dtype=F32) / dim)
        ang = pos[:, None] * inv[None, :]
        return jnp.cos(ang), jnp.sin(ang)

    c, s = tables(RET_HEAD_DIM)
    ret = (jnp.concatenate([c, c], axis=1), jnp.concatenate([-s, s], axis=1))
    c, s = tables(MLA_ROPE)
    z = jnp.zeros((lp, LANE - MLA_ROPE), F32)
    mla = (jnp.concatenate([c, c, z], axis=1), jnp.concatenate([-s, s, z], axis=1))
    return ret, mla


def kernel(x, meta_tokens, norm_mix, w_in, rw_mu, rw_w0, rw_w_up, rw_a0, rw_a_up, rw_g_up, rw_k_k, rw_k_a, rw_r_k, rw_ln_w, rw_ln_b, mla_norm_q, mla_norm_kv, mla_w_uq, mla_w_ukv, w_br_rwkv, w_br_ret, w_br_mla, w_out, norm_ffn, w_gate_up, w_down, final_norm):
    batch, seq, d = x.shape
    depth = w_in.shape[0]
    lp = -(-(N_META + seq) // SEQ_ALIGN) * SEQ_ALIGN
    tp = batch * lp

    meta = jnp.broadcast_to(meta_tokens[None].astype(x.dtype), (batch, N_META, d))
    pad = jnp.zeros((batch, lp - N_META - seq, d), x.dtype)
    h = jnp.concatenate([meta, x, pad], axis=1).reshape(tp, d)

    wp = _pack_w_in(w_in)
    mu = _pack_mu(rw_mu)
    w_up = _pad_rows(rw_w_up, LANE).astype(BF16)
    a_up = _pad_rows(rw_a_up, LANE).astype(BF16)
    g_up = rw_g_up.astype(BF16)
    wuq = _pack_w_uq(mla_w_uq)
    wukv = _pack_w_ukv(mla_w_ukv)
    wa = w_br_rwkv.astype(BF16)
    wb = w_br_ret.astype(BF16)
    wc = w_br_mla.astype(BF16)
    wo = w_out.astype(BF16)
    wgu = w_gate_up.astype(BF16)
    wdn = w_down.astype(BF16)
    (cos_ret, sin_ret), (cos_mla, sin_mla) = _rope_tables(lp)

    tm = 768 if tp % 768 == 0 else SEQ_ALIGN
    tm_seq = 384 if lp % 384 == 0 else SEQ_ALIGN
    tk_down = 512 if FFN_HIDDEN % 512 == 0 else FFN_HIDDEN

    u = rmsnorm(h, norm_mix[0], tm, BF16)
    for l in range(depth):
        p = matmul(u, wp[l], tm, P_TILE_N, F32)
        ya = rwkv_mix(p, batch, mu[l:l + 1], rw_w0[l], rw_a0[l], rw_k_k[l], rw_k_a[l], rw_r_k[l],
                      rw_ln_w[l], rw_ln_b[l], w_up[l], a_up[l], g_up[l])
        yb = retention_mix(p, batch, cos_ret, sin_ret)
        q, k, v = mla_proj(p, batch, mla_norm_q[l], mla_norm_kv[l], wuq[l], wukv[l], cos_mla, sin_mla, tm_seq)
        yc = mla_attention(q, k, v, batch, tm_seq)
        merged = merge_branches(ya, yb, yc, wa[l], wb[l], wc[l], p, tm, 512)
        h, u2 = resid_matmul_norm(merged, wo[l], h, norm_ffn[l], tm_seq, d, BF16)
        act = ffn_up(u2, wgu[l], tm, 512)
        last = l == depth - 1
        h, u = resid_matmul_norm(act, wdn[l], h, final_norm if last else norm_mix[l + 1], tm_seq, tk_down,
                                 F32 if last else BF16)
    return u.reshape(batch, lp, d)[:, N_META:N_META + seq]
```

```python
import functools
import math

import jax
import jax.numpy as jnp
from jax import lax
from jax.experimental import pallas as pl
from jax.experimental.pallas import tpu as pltpu

F32 = jnp.float32
BF16 = jnp.bfloat16

D_MODEL = 2048
N_META = 16
NORM_EPS = 1e-6
ROPE_BASE = 10000.0

RW_HEADS = 16
RW_HEAD_DIM = 64
RW_WIDTH = RW_HEADS * RW_HEAD_DIM
RW_DECAY_LORA = 96
RW_A_LORA = 96
RW_GATE_LORA = 256
RW_GN_EPS = RW_HEAD_DIM * 1e-5
RW_CHUNK = 64
RW_PAIRS = RW_WIDTH // 128

RET_HEADS = 8
RET_HEAD_DIM = 128
RET_WIDTH = RET_HEADS * RET_HEAD_DIM
RET_CHUNK = 128

MLA_HEADS = 8
MLA_NOPE = 128
MLA_ROPE = 64
MLA_V = 128
MLA_Q_RANK = 512
MLA_KV_RANK = 256
MLA_WIDTH = MLA_HEADS * MLA_V
MLA_QK_PAD = 256

FFN_HIDDEN = -(-8 * D_MODEL // (3 * 256)) * 256

LANE = 128
SEQ_ALIGN = 128

OFF_RET = 0
OFF_GATE = OFF_RET + 4 * RET_WIDTH
OFF_RW = OFF_GATE + 3 * D_MODEL
OFF_RW_WD = OFF_RW + 3 * RW_WIDTH
OFF_RW_AD = OFF_RW_WD + LANE
OFF_RW_GD = OFF_RW_AD + LANE
OFF_MLA_Q = OFF_RW_GD + RW_GATE_LORA
OFF_MLA_KV = OFF_MLA_Q + MLA_Q_RANK
OFF_MLA_KR = OFF_MLA_KV + MLA_KV_RANK
P_COLS_USED = OFF_MLA_KR + LANE
P_TILE_N = 512
P_COLS = -(-P_COLS_USED // P_TILE_N) * P_TILE_N

VMEM_LIMIT = 48 * 1024 * 1024


def _params(*sem):
    return pltpu.CompilerParams(dimension_semantics=sem, vmem_limit_bytes=VMEM_LIMIT)


def _sigmoid(x):
    return 1.0 / (1.0 + jnp.exp(-x))


def _dot(a, b):
    return jnp.dot(a, b, preferred_element_type=F32)


def _dot_nt(a, b):
    return lax.dot_general(a, b, (((1,), (1,)), ((), ())), preferred_element_type=F32)


def _dot_tn(a, b):
    return lax.dot_general(a, b, (((0,), (0,)), ((), ())), preferred_element_type=F32)


def _rmsnorm_body(x_ref, g_ref, o_ref):
    x = x_ref[...]
    y = x * lax.rsqrt(jnp.mean(x * x, axis=-1, keepdims=True) + NORM_EPS)
    o_ref[...] = (y * g_ref[...]).astype(o_ref.dtype)


def rmsnorm(x, g, tm, out_dtype):
    m, d = x.shape
    return pl.pallas_call(
        _rmsnorm_body,
        grid=(m // tm,),
        in_specs=[pl.BlockSpec((tm, d), lambda i: (i, 0)), pl.BlockSpec((1, d), lambda i: (0, 0))],
        out_specs=pl.BlockSpec((tm, d), lambda i: (i, 0)),
        out_shape=jax.ShapeDtypeStruct((m, d), out_dtype),
        compiler_params=_params("parallel"),
        name="rmsnorm",
    )(x, g.reshape(1, d))


def _matmul_body(x_ref, w_ref, o_ref):
    o_ref[...] = _dot(x_ref[...], w_ref[...]).astype(o_ref.dtype)


def matmul(x, w, tm, tn, out_dtype):
    m, k = x.shape
    n = w.shape[1]
    return pl.pallas_call(
        _matmul_body,
        grid=(m // tm, n // tn),
        in_specs=[pl.BlockSpec((tm, k), lambda i, j: (i, 0)), pl.BlockSpec((k, tn), lambda i, j: (0, j))],
        out_specs=pl.BlockSpec((tm, tn), lambda i, j: (i, j)),
        out_shape=jax.ShapeDtypeStruct((m, n), out_dtype),
        compiler_params=_params("parallel", "parallel"),
        name="in_proj",
    )(x, w)


def _split_dot(a, b_bf16):
    hi = a.astype(BF16)
    lo = (a - hi.astype(F32)).astype(BF16)
    return _dot(hi, b_bf16) + _dot(lo, b_bf16)


def _rwkv_body(r_ref, k_ref, v_ref, wd_ref, ad_ref, gd_ref, mu_ref,
               w0_ref, a0_ref, kk_ref, ka_ref, rk_ref, lnw_ref, lnb_ref,
               wup_ref, aup_ref, gup_ref,
               o_ref,
               s_ref, pr_ref, pk_ref, pv_ref, pwd_ref, pad_ref, pgd_ref):
    c = pl.program_id(1)
    C = RW_CHUNK
    HD = RW_HEAD_DIM

    @pl.when(c == 0)
    def _():
        s_ref[...] = jnp.zeros_like(s_ref)
        pr_ref[...] = jnp.zeros_like(pr_ref)
        pk_ref[...] = jnp.zeros_like(pk_ref)
        pv_ref[...] = jnp.zeros_like(pv_ref)
        pwd_ref[...] = jnp.zeros_like(pwd_ref)
        pad_ref[...] = jnp.zeros_like(pad_ref)
        pgd_ref[...] = jnp.zeros_like(pgd_ref)

    def shift(x_ref, prev_ref, mu, sl):
        z = x_ref[:, sl]
        first = lax.broadcasted_iota(jnp.int32, z.shape, 0) == 0
        zs = jnp.where(first, prev_ref[0:1, sl], pltpu.roll(z, 1, 0))
        prev_ref[0:1, sl] = z[C - 1:C, :]
        return z + (zs - z) * mu

    mu_lora = 3 * RW_WIDTH
    full = slice(None)
    wd = shift(wd_ref, pwd_ref, mu_ref[:, mu_lora:mu_lora + LANE], full)
    ad = shift(ad_ref, pad_ref, mu_ref[:, mu_lora + LANE:mu_lora + 2 * LANE], full)
    gd = shift(gd_ref, pgd_ref, mu_ref[:, mu_lora + 2 * LANE:], full)
    tanh_wd = jnp.tanh(wd).astype(BF16)
    ad_b = ad.astype(BF16)
    sig_gd = _sigmoid(gd).astype(BF16)

    lane_sq = lax.broadcasted_iota(jnp.int32, (LANE, LANE), 1)
    row_sq = lax.broadcasted_iota(jnp.int32, (LANE, LANE), 0)
    same_head = (lane_sq < HD) == (row_sq < HD)
    head_ones = same_head.astype(BF16)
    eye = (lane_sq == row_sq).astype(F32)
    rc = lax.broadcasted_iota(jnp.int32, (C, C), 0)
    cc = lax.broadcasted_iota(jnp.int32, (C, C), 1)
    tril_incl = (cc <= rc).astype(BF16)
    lane_tall = lax.broadcasted_iota(jnp.int32, (2 * C, LANE), 1)
    lane_c = lax.broadcasted_iota(jnp.int32, (C, LANE), 1)
    row_c = lax.broadcasted_iota(jnp.int32, (C, LANE), 0)
    head0_c = lane_c < HD
    strict_lo = lane_c < row_c
    strict_hi = (lane_c >= C) & (lane_c - C < row_c)
    incl_lo = lane_c <= row_c
    incl_hi = (lane_c >= C) & (lane_c - C <= row_c)
    inv_n = 1.0 / HD

    def head_sum(t):
        return _split_dot(t, head_ones)

    P = range(RW_PAIRS)
    sls = [slice(LANE * i, LANE * (i + 1)) for i in P]

    def each(fn, *lists):
        return [fn(*args) for args in zip(*lists)]

    r = [shift(r_ref, pr_ref, mu_ref[:, sl], sl) for sl in sls]
    k = [shift(k_ref, pk_ref, mu_ref[:, RW_WIDTH + sl.start:RW_WIDTH + sl.stop], sl) for sl in sls]
    v = [shift(v_ref, pv_ref, mu_ref[:, 2 * RW_WIDTH + sl.start:2 * RW_WIDTH + sl.stop], sl) for sl in sls]

    def log_decay(sl):
        x = -(w0_ref[:, sl] + _dot(tanh_wd, wup_ref[:, sl]))
        softplus = jnp.maximum(x, 0.0) + jnp.log1p(jnp.exp(-jnp.abs(x)))
        return -jnp.exp(-softplus - 0.5)

    lw = each(log_decay, sls)
    a = each(lambda sl: _sigmoid(a0_ref[:, sl] + _dot(ad_b, aup_ref[:, sl])), sls)
    g = each(lambda sl: _dot(sig_gd, gup_ref[:, sl]), sls)

    kkr = each(lambda ki, sl: ki * kk_ref[:, sl], k, sls)
    ksq = each(lambda t: head_sum(t * t), kkr)
    kkn = each(lambda t, ss: t / jnp.maximum(jnp.sqrt(ss), 1e-12), kkr, ksq)
    kmod = each(lambda ki, ai, sl: ki * (1.0 + (ai - 1.0) * ka_ref[:, sl]), k, a, sls)
    beta = each(lambda ai, t: ai * t, a, kkn)

    def running_sum(lwi):
        hi = lwi.astype(BF16)
        return _dot(tril_incl, hi) + _dot(tril_incl, (lwi - hi.astype(F32)).astype(BF16))

    lcum = each(running_sum, lw)
    lend = each(lambda t: t[C - 1:C, :], lcum)
    rh = each(lambda ri, lc: ri * jnp.exp(lc), r, lcum)
    kh = each(lambda t, lc, lwi: t * jnp.exp(lc - lwi), kkn, lcum, lw)
    e_neg = each(lambda lc: jnp.exp(-lc), lcum)
    e_end = each(lambda le, lc: jnp.exp(le - lc), lend, lcum)
    kb = each(lambda t, e: t * e, kmod, e_neg)
    bb = each(lambda t, e: t * e, beta, e_neg)
    kbe = each(lambda t, e: t * e, kmod, e_end)
    bbe = each(lambda t, e: t * e, beta, e_end)

    kr_f = each(lambda x1, x2: jnp.concatenate([x1, x2], axis=0), kh, rh)
    bk = each(lambda x1, x2: jnp.concatenate([x1, x2], axis=0).astype(BF16), bb, kb)
    vb = each(lambda t: t.astype(BF16), v)
    s0 = [s_ref[i] for i in P]
    p_all = each(lambda x1, si: _dot_nt(x1.astype(BF16), si.astype(BF16)), kr_f, s0)

    def gram(krf, bki):
        kr2 = jnp.concatenate([jnp.where(lane_tall < HD, krf, 0.0), jnp.where(lane_tall >= HD, krf, 0.0)], axis=0)
        return _dot_nt(kr2.astype(BF16), bki)

    g_all = each(gram, kr_f, bk)

    n_bd = each(lambda ga: jnp.concatenate([jnp.where(strict_lo, -ga[0:C], 0.0),
                                            jnp.where(strict_hi, -pltpu.roll(ga[2 * C:3 * C], C, 1), 0.0)], axis=0),
                g_all)
    t = each(lambda n: eye + n, n_bd)
    pw = n_bd
    for _ in range(5):
        pw = each(lambda x1: _dot(x1.astype(BF16), x1.astype(BF16)), pw)
        t = each(lambda ti, pi: ti + _dot(ti.astype(BF16), pi.astype(BF16)), t, pw)

    def solve_rhs(ga, vbi, pa):
        m1s = jnp.concatenate([jnp.where(strict_hi, ga[0:C], 0.0), jnp.where(strict_hi, ga[2 * C:3 * C], 0.0)], axis=0)
        vv = jnp.concatenate([vbi, vbi], axis=0)
        pk2 = jnp.concatenate([pa[:C], pa[:C]], axis=0)
        return jnp.where(same_head, _dot(m1s.astype(BF16), vv) + pk2, 0.0)

    q_s = each(solve_rhs, g_all, vb, p_all)
    u_s = each(lambda ti, qi: _dot(ti.astype(BF16), qi.astype(BF16)), t, q_s)
    u = each(lambda us: us[:C] + us[C:], u_s)

    def m2(gb):
        return jnp.where(incl_lo, -gb, jnp.where(incl_hi, gb, 0.0))

    def outputs(ga, ui, vi, pa):
        uv = jnp.concatenate([ui, vi], axis=0).astype(BF16)
        y_s = _dot(jnp.concatenate([m2(ga[C:2 * C]), m2(ga[3 * C:4 * C])], axis=0).astype(BF16), uv)
        return pa[C:] + jnp.where(head0_c, y_s[:C], y_s[C:])

    y = each(outputs, g_all, u, v, p_all)

    def new_state(vi, ui, kbei, bbei, si, le):
        vu = jnp.concatenate([vi, ui], axis=0).astype(BF16)
        kbe_all = jnp.concatenate([kbei, -bbei], axis=0).astype(BF16)
        ds = _dot_tn(vu, kbe_all)
        return si * jnp.exp(le) + jnp.where(same_head, ds, 0.0)

    s_new = each(new_state, v, u, kbe, bbe, s0, lend)
    for i in P:
        s_ref[i] = s_new[i]

    mean = each(lambda yi: head_sum(yi) * inv_n, y)
    d = each(lambda yi, mi: yi - mi, y, mean)
    var = each(lambda di: head_sum(di * di) * inv_n, d)
    bsum = each(lambda ri, ki, sl: head_sum(ri * ki * rk_ref[:, sl]), r, kmod, sls)
    for i in P:
        sl = sls[i]
        yn = d[i] * lax.rsqrt(var[i] + RW_GN_EPS) * lnw_ref[:, sl] + lnb_ref[:, sl]
        o_ref[:, sl] = ((yn + bsum[i] * v[i]) * g[i]).astype(o_ref.dtype)


def rwkv_mix(p, batch, mu, w0, a0, k_k, k_a, r_k, ln_w, ln_b, w_up, a_up, g_up):
    tp = p.shape[0]
    lp = tp // batch
    C = RW_CHUNK
    nchunk = lp // C

    def pspec(width, base):
        return pl.BlockSpec((C, width), lambda b, c: (b * nchunk + c, base // width))

    def const(shape):
        return pl.BlockSpec(shape, lambda b, c: (0, 0))

    in_specs = [
        pspec(RW_WIDTH, OFF_RW), pspec(RW_WIDTH, OFF_RW + RW_WIDTH), pspec(RW_WIDTH, OFF_RW + 2 * RW_WIDTH),
        pspec(LANE, OFF_RW_WD), pspec(LANE, OFF_RW_AD), pspec(RW_GATE_LORA, OFF_RW_GD),
        const(mu.shape),
    ] + [const((1, RW_WIDTH))] * 7 + [const(w_up.shape), const(a_up.shape), const(g_up.shape)]
    row = lambda t: t.reshape(1, -1)
    return pl.pallas_call(
        _rwkv_body,
        grid=(batch, nchunk),
        in_specs=in_specs,
        out_specs=pl.BlockSpec((C, RW_WIDTH), lambda b, c: (b * nchunk + c, 0)),
        out_shape=jax.ShapeDtypeStruct((tp, RW_WIDTH), BF16),
        scratch_shapes=[pltpu.VMEM((RW_PAIRS, LANE, LANE), F32)] + [pltpu.VMEM((8, RW_WIDTH), F32)] * 3
        + [pltpu.VMEM((8, LANE), F32)] * 2 + [pltpu.VMEM((8, RW_GATE_LORA), F32)],
        compiler_params=_params("parallel", "arbitrary"),
        name="rwkv7_mix",
    )(p, p, p, p, p, p, mu, row(w0), row(a0), row(k_k), row(k_a), row(r_k), row(ln_w), row(ln_b),
      w_up, a_up, g_up)


def _ret_body(q_ref, k_ref, v_ref, g_ref, cos_ref, sin_ref, o_ref, state_ref):
    c = pl.program_id(1)
    C = RET_CHUNK
    d = RET_HEAD_DIM

    @pl.when(c == 0)
    def _():
        state_ref[...] = jnp.zeros_like(state_ref)

    row = lax.broadcasted_iota(jnp.int32, (C, C), 0).astype(F32)
    col = lax.broadcasted_iota(jnp.int32, (C, C), 1).astype(F32)
    diff = row - col
    causal = diff >= 0
    cos = cos_ref[...]
    sin = sin_ref[...]
    for h in range(RET_HEADS):
        lg = math.log1p(-(2.0 ** (-5.0 - h)))
        sl = slice(d * h, d * (h + 1))
        q = q_ref[:, sl]
        k = k_ref[:, sl]
        q = q * cos + pltpu.roll(q, d // 2, 1) * sin
        k = (k * cos + pltpu.roll(k, d // 2, 1) * sin) * (d ** -0.5)
        g = g_ref[:, sl]
        qb = q.astype(BF16)
        kb = k.astype(BF16)
        vb = v_ref[:, sl].astype(BF16)
        dmat = jnp.where(causal, jnp.exp(lg * jnp.maximum(diff, 0.0)), 0.0)
        s = _dot_nt(qb, kb) * dmat
        state = state_ref[h]
        o = _dot(s.astype(BF16), vb) + _dot(qb, state.astype(BF16)) * jnp.exp(lg * (row + 1.0))
        kd = (k * jnp.exp(lg * (C - 1.0 - row))).astype(BF16)
        state_ref[h] = state * math.exp(lg * C) + _dot_tn(kd, vb)
        o = o * lax.rsqrt(jnp.mean(o * o, axis=-1, keepdims=True) + NORM_EPS)
        o_ref[:, sl] = (g * _sigmoid(g) * o).astype(o_ref.dtype)


def retention_mix(p, batch, cos, sin):
    tp = p.shape[0]
    lp = tp // batch
    C = RET_CHUNK
    nchunk = lp // C
    base = OFF_RET // RET_WIDTH

    def pspec(j):
        return pl.BlockSpec((C, RET_WIDTH), lambda b, c: (b * nchunk + c, base + j))

    tab = pl.BlockSpec((C, RET_HEAD_DIM), lambda b, c: (c, 0))
    return pl.pallas_call(
        _ret_body,
        grid=(batch, nchunk),
        in_specs=[pspec(0), pspec(1), pspec(2), pspec(3), tab, tab],
        out_specs=pl.BlockSpec((C, RET_WIDTH), lambda b, c: (b * nchunk + c, 0)),
        out_shape=jax.ShapeDtypeStruct((tp, RET_WIDTH), BF16),
        scratch_shapes=[pltpu.VMEM((RET_HEADS, RET_HEAD_DIM, RET_HEAD_DIM), F32)],
        compiler_params=_params("parallel", "arbitrary"),
        name="retention_mix",
    )(p, p, p, p, cos, sin)


def _mla_proj_body(qd_ref, kvd_ref, krd_ref, nq_ref, nkv_ref, wuq_ref, wukv_ref, cos_ref, sin_ref,
                   q_out, k_out, v_out):
    cos = cos_ref[...]
    sin = sin_ref[...]

    def rope(x):
        return x * cos + (pltpu.roll(x, MLA_ROPE // 2, 1) + pltpu.roll(x, LANE - MLA_ROPE // 2, 1)) * sin

    def norm(x, g):
        return x * lax.rsqrt(jnp.mean(x * x, axis=-1, keepdims=True) + NORM_EPS) * g

    scale = (MLA_NOPE + MLA_ROPE) ** -0.5
    q = _dot(norm(qd_ref[...], nq_ref[...]).astype(BF16), wuq_ref[...]) * scale
    kv = _dot(norm(kvd_ref[...], nkv_ref[...]).astype(BF16), wukv_ref[...])
    kr = rope(krd_ref[...]).astype(k_out.dtype)
    for h in range(MLA_HEADS):
        lo = MLA_QK_PAD * h
        q_out[:, lo:lo + LANE] = q[:, lo:lo + LANE].astype(q_out.dtype)
        q_out[:, lo + LANE:lo + 2 * LANE] = rope(q[:, lo + LANE:lo + 2 * LANE]).astype(q_out.dtype)
        k_out[:, lo:lo + LANE] = kv[:, MLA_NOPE * h:MLA_NOPE * (h + 1)].astype(k_out.dtype)
        k_out[:, lo + LANE:lo + 2 * LANE] = kr
    v_out[...] = kv[:, MLA_HEADS * MLA_NOPE:].astype(v_out.dtype)


def mla_proj(p, batch, norm_q, norm_kv, w_uq, w_ukv, cos, sin, tm):
    tp = p.shape[0]
    lp = tp // batch
    per_seq = lp // tm
    qk_w = MLA_HEADS * MLA_QK_PAD
    const = lambda i: (0, 0)
    return pl.pallas_call(
        _mla_proj_body,
        grid=(tp // tm,),
        in_specs=[
            pl.BlockSpec((tm, MLA_Q_RANK), lambda i: (i, OFF_MLA_Q // MLA_Q_RANK)),
            pl.BlockSpec((tm, MLA_KV_RANK), lambda i: (i, OFF_MLA_KV // MLA_KV_RANK)),
            pl.BlockSpec((tm, LANE), lambda i: (i, OFF_MLA_KR // LANE)),
            pl.BlockSpec((1, MLA_Q_RANK), const),
            pl.BlockSpec((1, MLA_KV_RANK), const),
            pl.BlockSpec((MLA_Q_RANK, qk_w), const),
            pl.BlockSpec((MLA_KV_RANK, MLA_HEADS * (MLA_NOPE + MLA_V)), const),
            pl.BlockSpec((tm, LANE), lambda i: (i % per_seq, 0)),
            pl.BlockSpec((tm, LANE), lambda i: (i % per_seq, 0)),
        ],
        out_specs=[
            pl.BlockSpec((tm, qk_w), lambda i: (i, 0)),
            pl.BlockSpec((tm, qk_w), lambda i: (i, 0)),
            pl.BlockSpec((tm, MLA_WIDTH), lambda i: (i, 0)),
        ],
        out_shape=[
            jax.ShapeDtypeStruct((tp, qk_w), BF16),
            jax.ShapeDtypeStruct((tp, qk_w), BF16),
            jax.ShapeDtypeStruct((tp, MLA_WIDTH), BF16),
        ],
        compiler_params=_params("parallel"),
        name="mla_proj",
    )(p, p, p, norm_q.reshape(1, -1), norm_kv.reshape(1, -1), w_uq, w_ukv, cos, sin)


def _attn_body(q_ref, k_ref, v_ref, o_ref, *, tq):
    i = pl.program_id(2)
    q = q_ref[...]
    row = lax.broadcasted_iota(jnp.int32, (tq, tq), 0)
    col = lax.broadcasted_iota(jnp.int32, (tq, tq), 1)

    def step(j, carry, diagonal):
        m, l, acc = carry
        off = pl.multiple_of(j * tq, tq)
        kblk = k_ref[pl.ds(off, tq), :]
        vblk = v_ref[pl.ds(off, tq), :]
        s = _dot_nt(q, kblk)
        if diagonal:
            s = jnp.where(col <= row, s, -jnp.inf)
        m_new = jnp.maximum(m, jnp.max(s, axis=-1, keepdims=True))
        alpha = jnp.exp(m - m_new)
        pmat = jnp.exp(s - m_new)
        l = alpha * l + jnp.sum(pmat, axis=-1, keepdims=True)
        acc = alpha * acc + _dot(pmat.astype(BF16), vblk)
        return m_new, l, acc

    init = (jnp.full((tq, 1), -1e30, F32), jnp.zeros((tq, 1), F32), jnp.zeros((tq, MLA_V), F32))
    carry = lax.fori_loop(0, i, lambda j, cr: step(j, cr, False), init)
    _, l, acc = step(i, carry, True)
    o_ref[...] = (acc / l).astype(o_ref.dtype)


def mla_attention(q, k, v, batch, tq):
    tp = q.shape[0]
    lp = tp // batch
    nq = lp // tq
    return pl.pallas_call(
        functools.partial(_attn_body, tq=tq),
        grid=(batch, MLA_HEADS, nq),
        in_specs=[
            pl.BlockSpec((tq, MLA_QK_PAD), lambda b, h, i: (b * nq + i, h)),
            pl.BlockSpec((lp, MLA_QK_PAD), lambda b, h, i: (b, h)),
            pl.BlockSpec((lp, MLA_V), lambda b, h, i: (b, h)),
        ],
        out_specs=pl.BlockSpec((tq, MLA_V), lambda b, h, i: (b * nq + i, h)),
        out_shape=jax.ShapeDtypeStruct((tp, MLA_WIDTH), BF16),
        compiler_params=_params("parallel", "parallel", "arbitrary"),
        name="mla_attention",
    )(q, k, v)


def _merge_body(ya_ref, yb_ref, yc_ref, wa_ref, wb_ref, wc_ref, ga_ref, gb_ref, gc_ref, o_ref):
    def branch(y_ref, w_ref, g_ref):
        return _sigmoid(g_ref[...]) * _dot(y_ref[...], w_ref[...])

    o_ref[...] = (branch(ya_ref, wa_ref, ga_ref) + branch(yb_ref, wb_ref, gb_ref)
                  + branch(yc_ref, wc_ref, gc_ref)).astype(o_ref.dtype)


def merge_branches(ya, yb, yc, wa, wb, wc, p, tm, tn):
    tp = ya.shape[0]

    def yspec(width):
        return pl.BlockSpec((tm, width), lambda i, j: (i, 0))

    def wspec(width):
        return pl.BlockSpec((width, tn), lambda i, j: (0, j))

    def gspec(branch):
        base = (OFF_GATE + branch * D_MODEL) // tn
        return pl.BlockSpec((tm, tn), lambda i, j: (i, base + j))

    return pl.pallas_call(
        _merge_body,
        grid=(tp // tm, D_MODEL // tn),
        in_specs=[yspec(RW_WIDTH), yspec(RET_WIDTH), yspec(MLA_WIDTH),
                  wspec(RW_WIDTH), wspec(RET_WIDTH), wspec(MLA_WIDTH),
                  gspec(0), gspec(1), gspec(2)],
        out_specs=pl.BlockSpec((tm, tn), lambda i, j: (i, j)),
        out_shape=jax.ShapeDtypeStruct((tp, D_MODEL), BF16),
        compiler_params=_params("parallel", "parallel"),
        name="merge_branches",
    )(ya, yb, yc, wa, wb, wc, p, p, p)


def _resid_body(x_ref, w_ref, h_ref, nw_ref, hn_ref, u_ref, acc_ref, *, nk):
    kstep = pl.program_id(1)

    @pl.when(kstep == 0)
    def _():
        acc_ref[...] = jnp.zeros_like(acc_ref)

    acc_ref[...] += _dot(x_ref[...], w_ref[...])

    @pl.when(kstep == nk - 1)
    def _():
        hn = h_ref[...] + acc_ref[...]
        hn_ref[...] = hn
        y = hn * lax.rsqrt(jnp.mean(hn * hn, axis=-1, keepdims=True) + NORM_EPS)
        u_ref[...] = (y * nw_ref[...]).astype(u_ref.dtype)


def resid_matmul_norm(x, w, h, norm_w, tm, tk, u_dtype):
    m, kdim = x.shape
    n = w.shape[1]
    nk = kdim // tk
    return pl.pallas_call(
        functools.partial(_resid_body, nk=nk),
        grid=(m // tm, nk),
        in_specs=[
            pl.BlockSpec((tm, tk), lambda i, k: (i, k)),
            pl.BlockSpec((tk, n), lambda i, k: (k, 0)),
            pl.BlockSpec((tm, n), lambda i, k: (i, 0)),
            pl.BlockSpec((1, n), lambda i, k: (0, 0)),
        ],
        out_specs=[pl.BlockSpec((tm, n), lambda i, k: (i, 0)), pl.BlockSpec((tm, n), lambda i, k: (i, 0))],
        out_shape=[jax.ShapeDtypeStruct((m, n), F32), jax.ShapeDtypeStruct((m, n), u_dtype)],
        scratch_shapes=[pltpu.VMEM((tm, n), F32)],
        compiler_params=_params("parallel", "arbitrary"),
        name="resid_matmul_norm",
    )(x, w, h, norm_w.reshape(1, n))


def _ffn_up_body(x_ref, wg_ref, wu_ref, o_ref):
    x = x_ref[...]
    hg = _dot(x, wg_ref[...])
    hu = _dot(x, wu_ref[...])
    o_ref[...] = (hg * _sigmoid(hg) * hu).astype(o_ref.dtype)


def ffn_up(x, w_gate_up, tm, tn):
    m, kdim = x.shape
    hidden = w_gate_up.shape[1] // 2
    nj = hidden // tn
    return pl.pallas_call(
        _ffn_up_body,
        grid=(m // tm, nj),
        in_specs=[
            pl.BlockSpec((tm, kdim), lambda i, j: (i, 0)),
            pl.BlockSpec((kdim, tn), lambda i, j: (0, j)),
            pl.BlockSpec((kdim, tn), lambda i, j: (0, nj + j)),
        ],
        out_specs=pl.BlockSpec((tm, tn), lambda i, j: (i, j)),
        out_shape=jax.ShapeDtypeStruct((m, hidden), BF16),
        compiler_params=_params("parallel", "parallel"),
        name="ffn_up",
    )(x, w_gate_up, w_gate_up)


def _pad_cols(w, width):
    return jnp.pad(w, [(0, 0)] * (w.ndim - 1) + [(0, width - w.shape[-1])])


def _pack_w_in(w_in):
    rw_cols = 3 * RW_WIDTH + RW_DECAY_LORA + RW_A_LORA + RW_GATE_LORA
    ret_cols = 4 * RET_WIDTH
    o = 0
    rw = w_in[..., o:o + rw_cols]
    o += rw_cols
    ret = w_in[..., o:o + ret_cols]
    o += ret_cols
    qd = w_in[..., o:o + MLA_Q_RANK]
    o += MLA_Q_RANK
    kvd = w_in[..., o:o + MLA_KV_RANK]
    o += MLA_KV_RANK
    krd = w_in[..., o:o + MLA_ROPE]
    o += MLA_ROPE
    gates = w_in[..., o:]
    rkv = rw[..., :3 * RW_WIDTH]
    wd = rw[..., 3 * RW_WIDTH:3 * RW_WIDTH + RW_DECAY_LORA]
    ad = rw[..., 3 * RW_WIDTH + RW_DECAY_LORA:3 * RW_WIDTH + RW_DECAY_LORA + RW_A_LORA]
    gd = rw[..., 3 * RW_WIDTH + RW_DECAY_LORA + RW_A_LORA:]
    parts = [ret, gates, rkv, _pad_cols(wd, LANE), _pad_cols(ad, LANE), gd, qd, kvd, _pad_cols(krd, LANE)]
    packed = jnp.concatenate(parts, axis=-1)
    return _pad_cols(packed, P_COLS).astype(BF16)


def _pack_mu(mu):
    rkv = mu[..., :3 * RW_WIDTH]
    wd = mu[..., 3 * RW_WIDTH:3 * RW_WIDTH + RW_DECAY_LORA]
    ad = mu[..., 3 * RW_WIDTH + RW_DECAY_LORA:3 * RW_WIDTH + RW_DECAY_LORA + RW_A_LORA]
    gd = mu[..., 3 * RW_WIDTH + RW_DECAY_LORA + RW_A_LORA:]
    return jnp.concatenate([rkv, _pad_cols(wd, LANE), _pad_cols(ad, LANE), gd], axis=-1)


def _pad_rows(w, rows):
    return jnp.pad(w, [(0, 0)] * (w.ndim - 2) + [(0, rows - w.shape[-2]), (0, 0)])


def _pack_w_uq(w):
    nl, rank, _ = w.shape
    w = w.reshape(nl, rank, MLA_HEADS, MLA_NOPE + MLA_ROPE)
    w = jnp.pad(w, ((0, 0), (0, 0), (0, 0), (0, MLA_QK_PAD - MLA_NOPE - MLA_ROPE)))
    return w.reshape(nl, rank, MLA_HEADS * MLA_QK_PAD).astype(BF16)


def _pack_w_ukv(w):
    nl, rank, _ = w.shape
    w = w.reshape(nl, rank, MLA_HEADS, 2, MLA_NOPE)
    w = jnp.swapaxes(w, 2, 3)
    return w.reshape(nl, rank, 2 * MLA_HEADS * MLA_NOPE).astype(BF16)


def _rope_tables(lp):
    pos = jnp.arange(lp, dtype=F32)

    def tables(dim):
        inv = ROPE_BASE ** (-jnp.arange(0, dim, 2, dtype=F32) / dim)
        ang = pos[:, None] * inv[None, :]
        return jnp.cos(ang), jnp.sin(ang)

    c, s = tables(RET_HEAD_DIM)
    ret = (jnp.concatenate([c, c], axis=1), jnp.concatenate([-s, s], axis=1))
    c, s = tables(MLA_ROPE)
    z = jnp.zeros((lp, LANE - MLA_ROPE), F32)
    mla = (jnp.concatenate([c, c, z], axis=1), jnp.concatenate([-s, s, z], axis=1))
    return ret, mla


def kernel(x, meta_tokens, norm_mix, w_in, rw_mu, rw_w0, rw_w_up, rw_a0, rw_a_up, rw_g_up, rw_k_k, rw_k_a, rw_r_k, rw_ln_w, rw_ln_b, mla_norm_q, mla_norm_kv, mla_w_uq, mla_w_ukv, w_br_rwkv, w_br_ret, w_br_mla, w_out, norm_ffn, w_gate_up, w_down, final_norm):
    batch, seq, d = x.shape
    depth = w_in.shape[0]
    lp = -(-(N_META + seq) // SEQ_ALIGN) * SEQ_ALIGN
    tp = batch * lp

    meta = jnp.broadcast_to(meta_tokens[None].astype(x.dtype), (batch, N_META, d))
    pad = jnp.zeros((batch, lp - N_META - seq, d), x.dtype)
    h = jnp.concatenate([meta, x, pad], axis=1).reshape(tp, d)

    wp = _pack_w_in(w_in)
    mu = _pack_mu(rw_mu)
    w_up = _pad_rows(rw_w_up, LANE).astype(BF16)
    a_up = _pad_rows(rw_a_up, LANE).astype(BF16)
    g_up = rw_g_up.astype(BF16)
    wuq = _pack_w_uq(mla_w_uq)
    wukv = _pack_w_ukv(mla_w_ukv)
    wa = w_br_rwkv.astype(BF16)
    wb = w_br_ret.astype(BF16)
    wc = w_br_mla.astype(BF16)
    wo = w_out.astype(BF16)
    wgu = w_gate_up.astype(BF16)
    wdn = w_down.astype(BF16)
    (cos_ret, sin_ret), (cos_mla, sin_mla) = _rope_tables(lp)

    tm = 768 if tp % 768 == 0 else SEQ_ALIGN
    tm_seq = 384 if lp % 384 == 0 else SEQ_ALIGN
    tk_down = 512 if FFN_HIDDEN % 512 == 0 else FFN_HIDDEN

    u = rmsnorm(h, norm_mix[0], tm, BF16)
    for l in range(depth):
        p = matmul(u, wp[l], tm, P_TILE_N, F32)
        ya = rwkv_mix(p, batch, mu[l:l + 1], rw_w0[l], rw_a0[l], rw_k_k[l], rw_k_a[l], rw_r_k[l],
                      rw_ln_w[l], rw_ln_b[l], w_up[l], a_up[l], g_up[l])
        yb = retention_mix(p, batch, cos_ret, sin_ret)
        q, k, v = mla_proj(p, batch, mla_norm_q[l], mla_norm_kv[l], wuq[l], wukv[l], cos_mla, sin_mla, tm_seq)
        yc = mla_attention(q, k, v, batch, tm_seq)
        merged = merge_branches(ya, yb, yc, wa[l], wb[l], wc[l], p, tm, 512)
        h, u2 = resid_matmul_norm(merged, wo[l], h, norm_ffn[l], tm_seq, d, BF16)
        act = ffn_up(u2, wgu[l], tm, 512)
        last = l == depth - 1
        h, u = resid_matmul_norm(act, wdn[l], h, final_norm if last else norm_mix[l + 1], tm_seq, tk_down,
                                 F32 if last else BF16)
    return u.reshape(batch, lp, d)[:, N_META:N_META + seq]
```

```python
import functools
import math

import jax
import jax.numpy as jnp
from jax import lax
from jax.experimental import pallas as pl
from jax.experimental.pallas import tpu as pltpu

F32 = jnp.float32
BF16 = jnp.bfloat16

D_MODEL = 2048
N_META = 16
NORM_EPS = 1e-6
ROPE_BASE = 10000.0

RW_HEADS = 16
RW_HEAD_DIM = 64
RW_WIDTH = RW_HEADS * RW_HEAD_DIM
RW_DECAY_LORA = 96
RW_A_LORA = 96
RW_GATE_LORA = 256
RW_GN_EPS = RW_HEAD_DIM * 1e-5
RW_CHUNK = 64
RW_PAIRS = RW_WIDTH // 128

RET_HEADS = 8
RET_HEAD_DIM = 128
RET_WIDTH = RET_HEADS * RET_HEAD_DIM
RET_CHUNK = 128

MLA_HEADS = 8
MLA_NOPE = 128
MLA_ROPE = 64
MLA_V = 128
MLA_Q_RANK = 512
MLA_KV_RANK = 256
MLA_WIDTH = MLA_HEADS * MLA_V
MLA_QK_PAD = 256
ATTN_HEADS_PER_STEP = 2

FFN_HIDDEN = -(-8 * D_MODEL // (3 * 256)) * 256

LANE = 128
SEQ_ALIGN = 128

OFF_RET = 0
OFF_GATE = OFF_RET + 4 * RET_WIDTH
OFF_RW = OFF_GATE + 3 * D_MODEL
OFF_RW_WD = OFF_RW + 3 * RW_WIDTH
OFF_RW_AD = OFF_RW_WD + LANE
OFF_RW_GD = OFF_RW_AD + LANE
OFF_MLA_Q = OFF_RW_GD + RW_GATE_LORA
OFF_MLA_KV = OFF_MLA_Q + MLA_Q_RANK
OFF_MLA_KR = OFF_MLA_KV + MLA_KV_RANK
P_COLS_USED = OFF_MLA_KR + LANE
P_TILE_N = 512
P_COLS = -(-P_COLS_USED // P_TILE_N) * P_TILE_N

VMEM_LIMIT = 48 * 1024 * 1024


def _params(*sem):
    return pltpu.CompilerParams(dimension_semantics=sem, vmem_limit_bytes=VMEM_LIMIT)


def _sigmoid(x):
    return 1.0 / (1.0 + jnp.exp(-x))


def _dot(a, b):
    return jnp.dot(a, b, preferred_element_type=F32)


def _dot_nt(a, b):
    return lax.dot_general(a, b, (((1,), (1,)), ((), ())), preferred_element_type=F32)


def _dot_tn(a, b):
    return lax.dot_general(a, b, (((0,), (0,)), ((), ())), preferred_element_type=F32)


def _rmsnorm_body(x_ref, g_ref, o_ref):
    x = x_ref[...]
    y = x * lax.rsqrt(jnp.mean(x * x, axis=-1, keepdims=True) + NORM_EPS)
    o_ref[...] = (y * g_ref[...]).astype(o_ref.dtype)


def rmsnorm(x, g, tm, out_dtype):
    m, d = x.shape
    return pl.pallas_call(
        _rmsnorm_body,
        grid=(m // tm,),
        in_specs=[pl.BlockSpec((tm, d), lambda i: (i, 0)), pl.BlockSpec((1, d), lambda i: (0, 0))],
        out_specs=pl.BlockSpec((tm, d), lambda i: (i, 0)),
        out_shape=jax.ShapeDtypeStruct((m, d), out_dtype),
        compiler_params=_params("parallel"),
        name="rmsnorm",
    )(x, g.reshape(1, d))


def _matmul_body(x_ref, w_ref, o_ref):
    o_ref[...] = _dot(x_ref[...], w_ref[...]).astype(o_ref.dtype)


def matmul(x, w, tm, tn, out_dtype):
    m, k = x.shape
    n = w.shape[1]
    return pl.pallas_call(
        _matmul_body,
        grid=(m // tm, n // tn),
        in_specs=[pl.BlockSpec((tm, k), lambda i, j: (i, 0)), pl.BlockSpec((k, tn), lambda i, j: (0, j))],
        out_specs=pl.BlockSpec((tm, tn), lambda i, j: (i, j)),
        out_shape=jax.ShapeDtypeStruct((m, n), out_dtype),
        compiler_params=_params("parallel", "parallel"),
        name="in_proj",
    )(x, w)


def _split_dot(a, b_bf16):
    hi = a.astype(BF16)
    lo = (a - hi.astype(F32)).astype(BF16)
    return _dot(hi, b_bf16) + _dot(lo, b_bf16)


def _rwkv_body(r_ref, k_ref, v_ref, wd_ref, ad_ref, gd_ref, mu_ref,
               w0_ref, a0_ref, kk_ref, ka_ref, rk_ref, lnw_ref, lnb_ref,
               wup_ref, aup_ref, gup_ref,
               o_ref,
               s_ref, pr_ref, pk_ref, pv_ref, pwd_ref, pad_ref, pgd_ref):
    c = pl.program_id(1)
    C = RW_CHUNK
    HD = RW_HEAD_DIM

    @pl.when(c == 0)
    def _():
        s_ref[...] = jnp.zeros_like(s_ref)
        pr_ref[...] = jnp.zeros_like(pr_ref)
        pk_ref[...] = jnp.zeros_like(pk_ref)
        pv_ref[...] = jnp.zeros_like(pv_ref)
        pwd_ref[...] = jnp.zeros_like(pwd_ref)
        pad_ref[...] = jnp.zeros_like(pad_ref)
        pgd_ref[...] = jnp.zeros_like(pgd_ref)

    def shift(x_ref, prev_ref, mu, sl):
        z = x_ref[:, sl]
        first = lax.broadcasted_iota(jnp.int32, z.shape, 0) == 0
        zs = jnp.where(first, prev_ref[0:1, sl], pltpu.roll(z, 1, 0))
        prev_ref[0:1, sl] = z[C - 1:C, :]
        return z + (zs - z) * mu

    mu_lora = 3 * RW_WIDTH
    full = slice(None)
    wd = shift(wd_ref, pwd_ref, mu_ref[:, mu_lora:mu_lora + LANE], full)
    ad = shift(ad_ref, pad_ref, mu_ref[:, mu_lora + LANE:mu_lora + 2 * LANE], full)
    gd = shift(gd_ref, pgd_ref, mu_ref[:, mu_lora + 2 * LANE:], full)
    tanh_wd = jnp.tanh(wd).astype(BF16)
    ad_b = ad.astype(BF16)
    sig_gd = _sigmoid(gd).astype(BF16)

    lane_sq = lax.broadcasted_iota(jnp.int32, (LANE, LANE), 1)
    row_sq = lax.broadcasted_iota(jnp.int32, (LANE, LANE), 0)
    same_head = (lane_sq < HD) == (row_sq < HD)
    head_ones = same_head.astype(BF16)
    eye = (lane_sq == row_sq).astype(F32)
    rc = lax.broadcasted_iota(jnp.int32, (C, C), 0)
    cc = lax.broadcasted_iota(jnp.int32, (C, C), 1)
    tril_incl = (cc <= rc).astype(BF16)
    lane_tall = lax.broadcasted_iota(jnp.int32, (2 * C, LANE), 1)
    lane_c = lax.broadcasted_iota(jnp.int32, (C, LANE), 1)
    row_c = lax.broadcasted_iota(jnp.int32, (C, LANE), 0)
    head0_c = lane_c < HD
    strict_lo = lane_c < row_c
    strict_hi = (lane_c >= C) & (lane_c - C < row_c)
    incl_lo = lane_c <= row_c
    incl_hi = (lane_c >= C) & (lane_c - C <= row_c)
    inv_n = 1.0 / HD

    def head_sum(t):
        return _split_dot(t, head_ones)

    P = range(RW_PAIRS)
    sls = [slice(LANE * i, LANE * (i + 1)) for i in P]

    def each(fn, *lists):
        return [fn(*args) for args in zip(*lists)]

    r = [shift(r_ref, pr_ref, mu_ref[:, sl], sl) for sl in sls]
    k = [shift(k_ref, pk_ref, mu_ref[:, RW_WIDTH + sl.start:RW_WIDTH + sl.stop], sl) for sl in sls]
    v = [shift(v_ref, pv_ref, mu_ref[:, 2 * RW_WIDTH + sl.start:2 * RW_WIDTH + sl.stop], sl) for sl in sls]

    def log_decay(sl):
        x = -(w0_ref[:, sl] + _dot(tanh_wd, wup_ref[:, sl]))
        softplus = jnp.maximum(x, 0.0) + jnp.log1p(jnp.exp(-jnp.abs(x)))
        return -jnp.exp(-softplus - 0.5)

    lw = each(log_decay, sls)
    a = each(lambda sl: _sigmoid(a0_ref[:, sl] + _dot(ad_b, aup_ref[:, sl])), sls)
    g = each(lambda sl: _dot(sig_gd, gup_ref[:, sl]), sls)

    kkr = each(lambda ki, sl: ki * kk_ref[:, sl], k, sls)
    ksq = each(lambda t: head_sum(t * t), kkr)
    kkn = each(lambda t, ss: t / jnp.maximum(jnp.sqrt(ss), 1e-12), kkr, ksq)
    kmod = each(lambda ki, ai, sl: ki * (1.0 + (ai - 1.0) * ka_ref[:, sl]), k, a, sls)
    beta = each(lambda ai, t: ai * t, a, kkn)

    def running_sum(lwi):
        hi = lwi.astype(BF16)
        return _dot(tril_incl, hi) + _dot(tril_incl, (lwi - hi.astype(F32)).astype(BF16))

    lcum = each(running_sum, lw)
    lend = each(lambda t: t[C - 1:C, :], lcum)
    rh = each(lambda ri, lc: ri * jnp.exp(lc), r, lcum)
    kh = each(lambda t, lc, lwi: t * jnp.exp(lc - lwi), kkn, lcum, lw)
    e_neg = each(lambda lc: jnp.exp(-lc), lcum)
    e_end = each(lambda le, lc: jnp.exp(le - lc), lend, lcum)
    kb = each(lambda t, e: t * e, kmod, e_neg)
    bb = each(lambda t, e: t * e, beta, e_neg)
    kbe = each(lambda t, e: t * e, kmod, e_end)
    bbe = each(lambda t, e: t * e, beta, e_end)

    kr_f = each(lambda x1, x2: jnp.concatenate([x1, x2], axis=0), kh, rh)
    bk = each(lambda x1, x2: jnp.concatenate([x1, x2], axis=0).astype(BF16), bb, kb)
    vb = each(lambda t: t.astype(BF16), v)
    s0 = [s_ref[i] for i in P]
    p_all = each(lambda x1, si: _dot_nt(x1.astype(BF16), si.astype(BF16)), kr_f, s0)

    def gram(krf, bki):
        kr2 = jnp.concatenate([jnp.where(lane_tall < HD, krf, 0.0), jnp.where(lane_tall >= HD, krf, 0.0)], axis=0)
        return _dot_nt(kr2.astype(BF16), bki)

    g_all = each(gram, kr_f, bk)

    n_bd = each(lambda ga: jnp.concatenate([jnp.where(strict_lo, -ga[0:C], 0.0),
                                            jnp.where(strict_hi, -pltpu.roll(ga[2 * C:3 * C], C, 1), 0.0)], axis=0),
                g_all)
    t = each(lambda n: eye + n, n_bd)
    pw = n_bd
    for _ in range(5):
        pw = each(lambda x1: _dot(x1.astype(BF16), x1.astype(BF16)), pw)
        t = each(lambda ti, pi: ti + _dot(ti.astype(BF16), pi.astype(BF16)), t, pw)

    def solve_rhs(ga, vbi, pa):
        m1s = jnp.concatenate([jnp.where(strict_hi, ga[0:C], 0.0), jnp.where(strict_hi, ga[2 * C:3 * C], 0.0)], axis=0)
        vv = jnp.concatenate([vbi, vbi], axis=0)
        pk2 = jnp.concatenate([pa[:C], pa[:C]], axis=0)
        return jnp.where(same_head, _dot(m1s.astype(BF16), vv) + pk2, 0.0)

    q_s = each(solve_rhs, g_all, vb, p_all)
    u_s = each(lambda ti, qi: _dot(ti.astype(BF16), qi.astype(BF16)), t, q_s)
    u = each(lambda us: us[:C] + us[C:], u_s)

    def m2(gb):
        return jnp.where(incl_lo, -gb, jnp.where(incl_hi, gb, 0.0))

    def outputs(ga, ui, vi, pa):
        uv = jnp.concatenate([ui, vi], axis=0).astype(BF16)
        y_s = _dot(jnp.concatenate([m2(ga[C:2 * C]), m2(ga[3 * C:4 * C])], axis=0).astype(BF16), uv)
        return pa[C:] + jnp.where(head0_c, y_s[:C], y_s[C:])

    y = each(outputs, g_all, u, v, p_all)

    def new_state(vi, ui, kbei, bbei, si, le):
        vu = jnp.concatenate([vi, ui], axis=0).astype(BF16)
        kbe_all = jnp.concatenate([kbei, -bbei], axis=0).astype(BF16)
        ds = _dot_tn(vu, kbe_all)
        return si * jnp.exp(le) + jnp.where(same_head, ds, 0.0)

    s_new = each(new_state, v, u, kbe, bbe, s0, lend)
    for i in P:
        s_ref[i] = s_new[i]

    mean = each(lambda yi: head_sum(yi) * inv_n, y)
    d = each(lambda yi, mi: yi - mi, y, mean)
    var = each(lambda di: head_sum(di * di) * inv_n, d)
    bsum = each(lambda ri, ki, sl: head_sum(ri * ki * rk_ref[:, sl]), r, kmod, sls)
    for i in P:
        sl = sls[i]
        yn = d[i] * lax.rsqrt(var[i] + RW_GN_EPS) * lnw_ref[:, sl] + lnb_ref[:, sl]
        o_ref[:, sl] = ((yn + bsum[i] * v[i]) * g[i]).astype(o_ref.dtype)


def rwkv_mix(p, batch, mu, w0, a0, k_k, k_a, r_k, ln_w, ln_b, w_up, a_up, g_up):
    tp = p.shape[0]
    lp = tp // batch
    C = RW_CHUNK
    nchunk = lp // C

    def pspec(width, base):
        return pl.BlockSpec((C, width), lambda b, c: (b * nchunk + c, base // width))

    def const(shape):
        return pl.BlockSpec(shape, lambda b, c: (0, 0))

    in_specs = [
        pspec(RW_WIDTH, OFF_RW), pspec(RW_WIDTH, OFF_RW + RW_WIDTH), pspec(RW_WIDTH, OFF_RW + 2 * RW_WIDTH),
        pspec(LANE, OFF_RW_WD), pspec(LANE, OFF_RW_AD), pspec(RW_GATE_LORA, OFF_RW_GD),
        const(mu.shape),
    ] + [const((1, RW_WIDTH))] * 7 + [const(w_up.shape), const(a_up.shape), const(g_up.shape)]
    row = lambda t: t.reshape(1, -1)
    return pl.pallas_call(
        _rwkv_body,
        grid=(batch, nchunk),
        in_specs=in_specs,
        out_specs=pl.BlockSpec((C, RW_WIDTH), lambda b, c: (b * nchunk + c, 0)),
        out_shape=jax.ShapeDtypeStruct((tp, RW_WIDTH), BF16),
        scratch_shapes=[pltpu.VMEM((RW_PAIRS, LANE, LANE), F32)] + [pltpu.VMEM((8, RW_WIDTH), F32)] * 3
        + [pltpu.VMEM((8, LANE), F32)] * 2 + [pltpu.VMEM((8, RW_GATE_LORA), F32)],
        compiler_params=_params("parallel", "arbitrary"),
        name="rwkv7_mix",
    )(p, p, p, p, p, p, mu, row(w0), row(a0), row(k_k), row(k_a), row(r_k), row(ln_w), row(ln_b),
      w_up, a_up, g_up)


def _ret_body(q_ref, k_ref, v_ref, g_ref, cos_ref, sin_ref, o_ref, state_ref):
    c = pl.program_id(1)
    C = RET_CHUNK
    d = RET_HEAD_DIM

    @pl.when(c == 0)
    def _():
        state_ref[...] = jnp.zeros_like(state_ref)

    row = lax.broadcasted_iota(jnp.int32, (C, C), 0).astype(F32)
    col = lax.broadcasted_iota(jnp.int32, (C, C), 1).astype(F32)
    diff = row - col
    causal = diff >= 0
    cos = cos_ref[...]
    sin = sin_ref[...]
    for h in range(RET_HEADS):
        lg = math.log1p(-(2.0 ** (-5.0 - h)))
        sl = slice(d * h, d * (h + 1))
        q = q_ref[:, sl]
        k = k_ref[:, sl]
        q = q * cos + pltpu.roll(q, d // 2, 1) * sin
        k = (k * cos + pltpu.roll(k, d // 2, 1) * sin) * (d ** -0.5)
        g = g_ref[:, sl]
        qb = q.astype(BF16)
        kb = k.astype(BF16)
        vb = v_ref[:, sl].astype(BF16)
        dmat = jnp.where(causal, jnp.exp(lg * jnp.maximum(diff, 0.0)), 0.0)
        s = _dot_nt(qb, kb) * dmat
        state = state_ref[h]
        o = _dot(s.astype(BF16), vb) + _dot(qb, state.astype(BF16)) * jnp.exp(lg * (row + 1.0))
        kd = (k * jnp.exp(lg * (C - 1.0 - row))).astype(BF16)
        state_ref[h] = state * math.exp(lg * C) + _dot_tn(kd, vb)
        o = o * lax.rsqrt(jnp.mean(o * o, axis=-1, keepdims=True) + NORM_EPS)
        o_ref[:, sl] = (g * _sigmoid(g) * o).astype(o_ref.dtype)


def retention_mix(p, batch, cos, sin):
    tp = p.shape[0]
    lp = tp // batch
    C = RET_CHUNK
    nchunk = lp // C
    base = OFF_RET // RET_WIDTH

    def pspec(j):
        return pl.BlockSpec((C, RET_WIDTH), lambda b, c: (b * nchunk + c, base + j))

    tab = pl.BlockSpec((C, RET_HEAD_DIM), lambda b, c: (c, 0))
    return pl.pallas_call(
        _ret_body,
        grid=(batch, nchunk),
        in_specs=[pspec(0), pspec(1), pspec(2), pspec(3), tab, tab],
        out_specs=pl.BlockSpec((C, RET_WIDTH), lambda b, c: (b * nchunk + c, 0)),
        out_shape=jax.ShapeDtypeStruct((tp, RET_WIDTH), BF16),
        scratch_shapes=[pltpu.VMEM((RET_HEADS, RET_HEAD_DIM, RET_HEAD_DIM), F32)],
        compiler_params=_params("parallel", "arbitrary"),
        name="retention_mix",
    )(p, p, p, p, cos, sin)


def _mla_proj_body(qd_ref, kvd_ref, krd_ref, nq_ref, nkv_ref, wuq_ref, wukv_ref, cos_ref, sin_ref,
                   q_out, k_out, v_out):
    cos = cos_ref[...]
    sin = sin_ref[...]

    def rope(x):
        return x * cos + (pltpu.roll(x, MLA_ROPE // 2, 1) + pltpu.roll(x, LANE - MLA_ROPE // 2, 1)) * sin

    def norm(x, g):
        return x * lax.rsqrt(jnp.mean(x * x, axis=-1, keepdims=True) + NORM_EPS) * g

    scale = (MLA_NOPE + MLA_ROPE) ** -0.5 * math.log2(math.e)
    q = _dot(norm(qd_ref[...], nq_ref[...]).astype(BF16), wuq_ref[...]) * scale
    kv = _dot(norm(kvd_ref[...], nkv_ref[...]).astype(BF16), wukv_ref[...])
    kr = rope(krd_ref[...]).astype(k_out.dtype)
    for h in range(MLA_HEADS):
        lo = MLA_QK_PAD * h
        q_out[:, lo:lo + LANE] = q[:, lo:lo + LANE].astype(q_out.dtype)
        q_out[:, lo + LANE:lo + 2 * LANE] = rope(q[:, lo + LANE:lo + 2 * LANE]).astype(q_out.dtype)
        k_out[:, lo:lo + LANE] = kv[:, MLA_NOPE * h:MLA_NOPE * (h + 1)].astype(k_out.dtype)
        k_out[:, lo + LANE:lo + 2 * LANE] = kr
    v_out[...] = kv[:, MLA_HEADS * MLA_NOPE:].astype(v_out.dtype)


def mla_proj(p, batch, norm_q, norm_kv, w_uq, w_ukv, cos, sin, tm):
    tp = p.shape[0]
    lp = tp // batch
    per_seq = lp // tm
    qk_w = MLA_HEADS * MLA_QK_PAD
    const = lambda i: (0, 0)
    return pl.pallas_call(
        _mla_proj_body,
        grid=(tp // tm,),
        in_specs=[
            pl.BlockSpec((tm, MLA_Q_RANK), lambda i: (i, OFF_MLA_Q // MLA_Q_RANK)),
            pl.BlockSpec((tm, MLA_KV_RANK), lambda i: (i, OFF_MLA_KV // MLA_KV_RANK)),
            pl.BlockSpec((tm, LANE), lambda i: (i, OFF_MLA_KR // LANE)),
            pl.BlockSpec((1, MLA_Q_RANK), const),
            pl.BlockSpec((1, MLA_KV_RANK), const),
            pl.BlockSpec((MLA_Q_RANK, qk_w), const),
            pl.BlockSpec((MLA_KV_RANK, MLA_HEADS * (MLA_NOPE + MLA_V)), const),
            pl.BlockSpec((tm, LANE), lambda i: (i % per_seq, 0)),
            pl.BlockSpec((tm, LANE), lambda i: (i % per_seq, 0)),
        ],
        out_specs=[
            pl.BlockSpec((tm, qk_w), lambda i: (i, 0)),
            pl.BlockSpec((tm, qk_w), lambda i: (i, 0)),
            pl.BlockSpec((tm, MLA_WIDTH), lambda i: (i, 0)),
        ],
        out_shape=[
            jax.ShapeDtypeStruct((tp, qk_w), BF16),
            jax.ShapeDtypeStruct((tp, qk_w), BF16),
            jax.ShapeDtypeStruct((tp, MLA_WIDTH), BF16),
        ],
        compiler_params=_params("parallel"),
        name="mla_proj",
    )(p, p, p, norm_q.reshape(1, -1), norm_kv.reshape(1, -1), w_uq, w_ukv, cos, sin)


def _attn_body(q_ref, k_ref, v_ref, o_ref, *, tq):
    i = pl.program_id(2)
    heads = range(ATTN_HEADS_PER_STEP)
    qs = [q_ref[:, MLA_QK_PAD * h:MLA_QK_PAD * (h + 1)] for h in heads]
    row = lax.broadcasted_iota(jnp.int32, (tq, tq), 0)
    col = lax.broadcasted_iota(jnp.int32, (tq, tq), 1)

    def step(j, carry, diagonal):
        ms, ls, accs = carry
        off = pl.multiple_of(j * tq, tq)
        ss = [_dot_nt(qs[h], k_ref[pl.ds(off, tq), MLA_QK_PAD * h:MLA_QK_PAD * (h + 1)]) for h in heads]
        if diagonal:
            ss = [jnp.where(col <= row, s, -jnp.inf) for s in ss]
        m_new = [jnp.maximum(ms[h], jnp.max(ss[h], axis=-1, keepdims=True)) for h in heads]
        alpha = [jnp.exp2(ms[h] - m_new[h]) for h in heads]
        ps = [jnp.exp2(ss[h] - m_new[h]) for h in heads]
        ls = [alpha[h] * ls[h] + jnp.sum(ps[h], axis=-1, keepdims=True) for h in heads]
        pv = [_dot(ps[h].astype(BF16), v_ref[pl.ds(off, tq), MLA_V * h:MLA_V * (h + 1)]) for h in heads]
        accs = [alpha[h] * accs[h] + pv[h] for h in heads]
        return tuple(m_new), tuple(ls), tuple(accs)

    init = (tuple(jnp.full((tq, 1), -1e30, F32) for _ in heads),
            tuple(jnp.zeros((tq, 1), F32) for _ in heads),
            tuple(jnp.zeros((tq, MLA_V), F32) for _ in heads))
    carry = lax.fori_loop(0, i, lambda j, cr: step(j, cr, False), init)
    _, ls, accs = step(i, carry, True)
    for h in heads:
        o_ref[:, MLA_V * h:MLA_V * (h + 1)] = (accs[h] / ls[h]).astype(o_ref.dtype)


def mla_attention(q, k, v, batch, tq):
    tp = q.shape[0]
    lp = tp // batch
    nq = lp // tq
    hs = ATTN_HEADS_PER_STEP
    return pl.pallas_call(
        functools.partial(_attn_body, tq=tq),
        grid=(batch, MLA_HEADS // hs, nq),
        in_specs=[
            pl.BlockSpec((tq, hs * MLA_QK_PAD), lambda b, h, i: (b * nq + i, h)),
            pl.BlockSpec((lp, hs * MLA_QK_PAD), lambda b, h, i: (b, h)),
            pl.BlockSpec((lp, hs * MLA_V), lambda b, h, i: (b, h)),
        ],
        out_specs=pl.BlockSpec((tq, hs * MLA_V), lambda b, h, i: (b * nq + i, h)),
        out_shape=jax.ShapeDtypeStruct((tp, MLA_WIDTH), BF16),
        compiler_params=_params("parallel", "parallel", "arbitrary"),
        name="mla_attention",
    )(q, k, v)


def _merge_body(ya_ref, yb_ref, yc_ref, wa_ref, wb_ref, wc_ref, ga_ref, gb_ref, gc_ref, o_ref):
    def branch(y_ref, w_ref, g_ref):
        return _sigmoid(g_ref[...]) * _dot(y_ref[...], w_ref[...])

    o_ref[...] = (branch(ya_ref, wa_ref, ga_ref) + branch(yb_ref, wb_ref, gb_ref)
                  + branch(yc_ref, wc_ref, gc_ref)).astype(o_ref.dtype)


def merge_branches(ya, yb, yc, wa, wb, wc, p, tm, tn):
    tp = ya.shape[0]

    def yspec(width):
        return pl.BlockSpec((tm, width), lambda i, j: (i, 0))

    def wspec(width):
        return pl.BlockSpec((width, tn), lambda i, j: (0, j))

    def gspec(branch):
        base = (OFF_GATE + branch * D_MODEL) // tn
        return pl.BlockSpec((tm, tn), lambda i, j: (i, base + j))

    return pl.pallas_call(
        _merge_body,
        grid=(tp // tm, D_MODEL // tn),
        in_specs=[yspec(RW_WIDTH), yspec(RET_WIDTH), yspec(MLA_WIDTH),
                  wspec(RW_WIDTH), wspec(RET_WIDTH), wspec(MLA_WIDTH),
                  gspec(0), gspec(1), gspec(2)],
        out_specs=pl.BlockSpec((tm, tn), lambda i, j: (i, j)),
        out_shape=jax.ShapeDtypeStruct((tp, D_MODEL), BF16),
        compiler_params=_params("parallel", "parallel"),
        name="merge_branches",
    )(ya, yb, yc, wa, wb, wc, p, p, p)


def _resid_body(x_ref, w_ref, h_ref, nw_ref, hn_ref, u_ref, *, nk):
    kstep = pl.program_id(1)
    part = _dot(x_ref[...], w_ref[...])

    @pl.when(kstep == 0)
    def _():
        hn_ref[...] = h_ref[...] + part

    @pl.when(kstep > 0)
    def _():
        hn_ref[...] += part

    @pl.when(kstep == nk - 1)
    def _():
        hn = hn_ref[...]
        y = hn * lax.rsqrt(jnp.mean(hn * hn, axis=-1, keepdims=True) + NORM_EPS)
        u_ref[...] = (y * nw_ref[...]).astype(u_ref.dtype)


def resid_matmul_norm(x, w, h, norm_w, tm, tk, u_dtype):
    m, kdim = x.shape
    n = w.shape[1]
    nk = kdim // tk
    return pl.pallas_call(
        functools.partial(_resid_body, nk=nk),
        grid=(m // tm, nk),
        in_specs=[
            pl.BlockSpec((tm, tk), lambda i, k: (i, k)),
            pl.BlockSpec((tk, n), lambda i, k: (k, 0)),
            pl.BlockSpec((tm, n), lambda i, k: (i, 0)),
            pl.BlockSpec((1, n), lambda i, k: (0, 0)),
        ],
        out_specs=[pl.BlockSpec((tm, n), lambda i, k: (i, 0)), pl.BlockSpec((tm, n), lambda i, k: (i, 0))],
        out_shape=[jax.ShapeDtypeStruct((m, n), F32), jax.ShapeDtypeStruct((m, n), u_dtype)],
        compiler_params=_params("parallel", "arbitrary"),
        name="resid_matmul_norm",
    )(x, w, h, norm_w.reshape(1, n))


def _ffn_up_body(x_ref, wg_ref, wu_ref, o_ref):
    x = x_ref[...]
    hg = _dot(x, wg_ref[...])
    hu = _dot(x, wu_ref[...])
    o_ref[...] = (hg * _sigmoid(hg) * hu).astype(o_ref.dtype)


def ffn_up(x, w_gate_up, tm, tn):
    m, kdim = x.shape
    hidden = w_gate_up.shape[1] // 2
    nj = hidden // tn
    return pl.pallas_call(
        _ffn_up_body,
        grid=(m // tm, nj),
        in_specs=[
            pl.BlockSpec((tm, kdim), lambda i, j: (i, 0)),
            pl.BlockSpec((kdim, tn), lambda i, j: (0, j)),
            pl.BlockSpec((kdim, tn), lambda i, j: (0, nj + j)),
        ],
        out_specs=pl.BlockSpec((tm, tn), lambda i, j: (i, j)),
        out_shape=jax.ShapeDtypeStruct((m, hidden), BF16),
        compiler_params=_params("parallel", "parallel"),
        name="ffn_up",
    )(x, w_gate_up, w_gate_up)


def _pad_cols(w, width):
    return jnp.pad(w, [(0, 0)] * (w.ndim - 1) + [(0, width - w.shape[-1])])


def _pack_w_in(w_in):
    rw_cols = 3 * RW_WIDTH + RW_DECAY_LORA + RW_A_LORA + RW_GATE_LORA
    ret_cols = 4 * RET_WIDTH
    o = 0
    rw = w_in[..., o:o + rw_cols]
    o += rw_cols
    ret = w_in[..., o:o + ret_cols]
    o += ret_cols
    qd = w_in[..., o:o + MLA_Q_RANK]
    o += MLA_Q_RANK
    kvd = w_in[..., o:o + MLA_KV_RANK]
    o += MLA_KV_RANK
    krd = w_in[..., o:o + MLA_ROPE]
    o += MLA_ROPE
    gates = w_in[..., o:]
    rkv = rw[..., :3 * RW_WIDTH]
    wd = rw[..., 3 * RW_WIDTH:3 * RW_WIDTH + RW_DECAY_LORA]
    ad = rw[..., 3 * RW_WIDTH + RW_DECAY_LORA:3 * RW_WIDTH + RW_DECAY_LORA + RW_A_LORA]
    gd = rw[..., 3 * RW_WIDTH + RW_DECAY_LORA + RW_A_LORA:]
    parts = [ret, gates, rkv, _pad_cols(wd, LANE), _pad_cols(ad, LANE), gd, qd, kvd, _pad_cols(krd, LANE)]
    packed = jnp.concatenate(parts, axis=-1)
    return _pad_cols(packed, P_COLS).astype(BF16)


def _pack_mu(mu):
    rkv = mu[..., :3 * RW_WIDTH]
    wd = mu[..., 3 * RW_WIDTH:3 * RW_WIDTH + RW_DECAY_LORA]
    ad = mu[..., 3 * RW_WIDTH + RW_DECAY_LORA:3 * RW_WIDTH + RW_DECAY_LORA + RW_A_LORA]
    gd = mu[..., 3 * RW_WIDTH + RW_DECAY_LORA + RW_A_LORA:]
    return jnp.concatenate([rkv, _pad_cols(wd, LANE), _pad_cols(ad, LANE), gd], axis=-1)


def _pad_rows(w, rows):
    return jnp.pad(w, [(0, 0)] * (w.ndim - 2) + [(0, rows - w.shape[-2]), (0, 0)])


def _pack_w_uq(w):
    nl, rank, _ = w.shape
    w = w.reshape(nl, rank, MLA_HEADS, MLA_NOPE + MLA_ROPE)
    w = jnp.pad(w, ((0, 0), (0, 0), (0, 0), (0, MLA_QK_PAD - MLA_NOPE - MLA_ROPE)))
    return w.reshape(nl, rank, MLA_HEADS * MLA_QK_PAD).astype(BF16)


def _pack_w_ukv(w):
    nl, rank, _ = w.shape
    w = w.reshape(nl, rank, MLA_HEADS, 2, MLA_NOPE)
    w = jnp.swapaxes(w, 2, 3)
    return w.reshape(nl, rank, 2 * MLA_HEADS * MLA_NOPE).astype(BF16)


def _rope_tables(lp):
    pos = jnp.arange(lp, dtype=F32)

    def tables(dim):
        inv = ROPE_BASE ** (-jnp.arange(0, dim, 2, dtype=F32) / dim)
        ang = pos[:, None] * inv[None, :]
        return jnp.cos(ang), jnp.sin(ang)

    c, s = tables(RET_HEAD_DIM)
    ret = (jnp.concatenate([c, c], axis=1), jnp.concatenate([-s, s], axis=1))
    c, s = tables(MLA_ROPE)
    z = jnp.zeros((lp, LANE - MLA_ROPE), F32)
    mla = (jnp.concatenate([c, c, z], axis=1), jnp.concatenate([-s, s, z], axis=1))
    return ret, mla


def kernel(x, meta_tokens, norm_mix, w_in, rw_mu, rw_w0, rw_w_up, rw_a0, rw_a_up, rw_g_up, rw_k_k, rw_k_a, rw_r_k, rw_ln_w, rw_ln_b, mla_norm_q, mla_norm_kv, mla_w_uq, mla_w_ukv, w_br_rwkv, w_br_ret, w_br_mla, w_out, norm_ffn, w_gate_up, w_down, final_norm):
    batch, seq, d = x.shape
    depth = w_in.shape[0]
    lp = -(-(N_META + seq) // SEQ_ALIGN) * SEQ_ALIGN
    tp = batch * lp

    meta = jnp.broadcast_to(meta_tokens[None].astype(x.dtype), (batch, N_META, d))
    pad = jnp.zeros((batch, lp - N_META - seq, d), x.dtype)
    h = jnp.concatenate([meta, x, pad], axis=1).reshape(tp, d)

    wp = _pack_w_in(w_in)
    mu = _pack_mu(rw_mu)
    w_up = _pad_rows(rw_w_up, LANE).astype(BF16)
    a_up = _pad_rows(rw_a_up, LANE).astype(BF16)
    g_up = rw_g_up.astype(BF16)
    wuq = _pack_w_uq(mla_w_uq)
    wukv = _pack_w_ukv(mla_w_ukv)
    wa = w_br_rwkv.astype(BF16)
    wb = w_br_ret.astype(BF16)
    wc = w_br_mla.astype(BF16)
    wo = w_out.astype(BF16)
    wgu = w_gate_up.astype(BF16)
    wdn = w_down.astype(BF16)
    (cos_ret, sin_ret), (cos_mla, sin_mla) = _rope_tables(lp)

    def row_tile(pref):
        return next((t for t in pref if tp % t == 0), SEQ_ALIGN)

    tm = row_tile((768,))
    tm_wide = row_tile((1536, 768))
    tm_down = row_tile((512,))
    tm_seq = 384 if lp % 384 == 0 else SEQ_ALIGN
    tk_down = FFN_HIDDEN // 4

    u = rmsnorm(h, norm_mix[0], tm, BF16)
    for l in range(depth):
        p = matmul(u, wp[l], tm_wide, P_TILE_N, F32)
        ya = rwkv_mix(p, batch, mu[l:l + 1], rw_w0[l], rw_a0[l], rw_k_k[l], rw_k_a[l], rw_r_k[l],
                      rw_ln_w[l], rw_ln_b[l], w_up[l], a_up[l], g_up[l])
        yb = retention_mix(p, batch, cos_ret, sin_ret)
        q, k, v = mla_proj(p, batch, mla_norm_q[l], mla_norm_kv[l], wuq[l], wukv[l], cos_mla, sin_mla, tm_seq)
        yc = mla_attention(q, k, v, batch, tm_seq)
        merged = merge_branches(ya, yb, yc, wa[l], wb[l], wc[l], p, tm, 512)
        h, u2 = resid_matmul_norm(merged, wo[l], h, norm_ffn[l], tm_seq, d, BF16)
        act = ffn_up(u2, wgu[l], tm_wide, 512)
        last = l == depth - 1
        h, u = resid_matmul_norm(act, wdn[l], h, final_norm if last else norm_mix[l + 1], tm_down, tk_down,
                                 F32 if last else BF16)
    return u.reshape(batch, lp, d)[:, N_META:N_META + seq]
```

```python
import functools
import math

import jax
import jax.numpy as jnp
from jax import lax
from jax.experimental import pallas as pl
from jax.experimental.pallas import tpu as pltpu

F32 = jnp.float32
BF16 = jnp.bfloat16

D_MODEL = 2048
N_META = 16
NORM_EPS = 1e-6
ROPE_BASE = 10000.0

RW_HEADS = 16
RW_HEAD_DIM = 64
RW_WIDTH = RW_HEADS * RW_HEAD_DIM
RW_DECAY_LORA = 96
RW_A_LORA = 96
RW_GATE_LORA = 256
RW_GN_EPS = RW_HEAD_DIM * 1e-5
RW_CHUNK = 64
RW_SUB = 2
RW_PAIRS = RW_WIDTH // 128

RET_HEADS = 8
RET_HEAD_DIM = 128
RET_WIDTH = RET_HEADS * RET_HEAD_DIM
RET_CHUNK = 128

MLA_HEADS = 8
MLA_NOPE = 128
MLA_ROPE = 64
MLA_V = 128
MLA_Q_RANK = 512
MLA_KV_RANK = 256
MLA_WIDTH = MLA_HEADS * MLA_V
MLA_QK_PAD = 256
ATTN_HEADS_PER_STEP = 2

FFN_HIDDEN = -(-8 * D_MODEL // (3 * 256)) * 256

LANE = 128
SEQ_ALIGN = 128

OFF_RET = 0
OFF_GATE = OFF_RET + 4 * RET_WIDTH
OFF_RW = OFF_GATE + 3 * D_MODEL
OFF_RW_WD = OFF_RW + 3 * RW_WIDTH
OFF_RW_AD = OFF_RW_WD + LANE
OFF_RW_GD = OFF_RW_AD + LANE
OFF_MLA_Q = OFF_RW_GD + RW_GATE_LORA
OFF_MLA_KV = OFF_MLA_Q + MLA_Q_RANK
OFF_MLA_KR = OFF_MLA_KV + MLA_KV_RANK
P_COLS_USED = OFF_MLA_KR + LANE
P_TILE_N = 512
P_COLS = -(-P_COLS_USED // P_TILE_N) * P_TILE_N

VMEM_LIMIT = 48 * 1024 * 1024


def _params(*sem):
    return pltpu.CompilerParams(dimension_semantics=sem, vmem_limit_bytes=VMEM_LIMIT)


def _sigmoid(x):
    return 1.0 / (1.0 + jnp.exp(-x))


def _dot(a, b):
    return jnp.dot(a, b, preferred_element_type=F32)


def _dot_nt(a, b):
    return lax.dot_general(a, b, (((1,), (1,)), ((), ())), preferred_element_type=F32)


def _dot_tn(a, b):
    return lax.dot_general(a, b, (((0,), (0,)), ((), ())), preferred_element_type=F32)


def _rmsnorm_body(x_ref, g_ref, o_ref):
    x = x_ref[...]
    y = x * lax.rsqrt(jnp.mean(x * x, axis=-1, keepdims=True) + NORM_EPS)
    o_ref[...] = (y * g_ref[...]).astype(o_ref.dtype)


def rmsnorm(x, g, tm, out_dtype):
    m, d = x.shape
    return pl.pallas_call(
        _rmsnorm_body,
        grid=(m // tm,),
        in_specs=[pl.BlockSpec((tm, d), lambda i: (i, 0)), pl.BlockSpec((1, d), lambda i: (0, 0))],
        out_specs=pl.BlockSpec((tm, d), lambda i: (i, 0)),
        out_shape=jax.ShapeDtypeStruct((m, d), out_dtype),
        compiler_params=_params("parallel"),
        name="rmsnorm",
    )(x, g.reshape(1, d))


def _matmul_body(x_ref, w_ref, o_ref):
    o_ref[...] = _dot(x_ref[...], w_ref[...]).astype(o_ref.dtype)


def matmul(x, w, tm, tn, out_dtype):
    m, k = x.shape
    n = w.shape[1]
    return pl.pallas_call(
        _matmul_body,
        grid=(m // tm, n // tn),
        in_specs=[pl.BlockSpec((tm, k), lambda i, j: (i, 0)), pl.BlockSpec((k, tn), lambda i, j: (0, j))],
        out_specs=pl.BlockSpec((tm, tn), lambda i, j: (i, j)),
        out_shape=jax.ShapeDtypeStruct((m, n), out_dtype),
        compiler_params=_params("parallel", "parallel"),
        name="in_proj",
    )(x, w)


def _rwkv_body(r_ref, k_ref, v_ref, wd_ref, ad_ref, gd_ref, mu_ref,
               w0_ref, a0_ref, kk_ref, ka_ref, rk_ref, lnw_ref, lnb_ref,
               wup_ref, aup_ref, gup_ref,
               o_ref,
               s_ref, pr_ref, pk_ref, pv_ref, pwd_ref, pad_ref, pgd_ref):
    c = pl.program_id(1)
    C = RW_CHUNK
    RS = RW_SUB * C
    HD = RW_HEAD_DIM

    @pl.when(c == 0)
    def _():
        s_ref[...] = jnp.zeros_like(s_ref)
        pr_ref[...] = jnp.zeros_like(pr_ref)
        pk_ref[...] = jnp.zeros_like(pk_ref)
        pv_ref[...] = jnp.zeros_like(pv_ref)
        pwd_ref[...] = jnp.zeros_like(pwd_ref)
        pad_ref[...] = jnp.zeros_like(pad_ref)
        pgd_ref[...] = jnp.zeros_like(pgd_ref)

    def shift(x_ref, prev_ref, mu, sl):
        z = x_ref[:, sl]
        first = lax.broadcasted_iota(jnp.int32, z.shape, 0) == 0
        zs = jnp.where(first, prev_ref[0:1, sl], pltpu.roll(z, 1, 0))
        prev_ref[0:1, sl] = z[RS - 1:RS, :]
        return z + (zs - z) * mu

    mu_lora = 3 * RW_WIDTH
    full = slice(None)
    wd = shift(wd_ref, pwd_ref, mu_ref[:, mu_lora:mu_lora + LANE], full)
    ad = shift(ad_ref, pad_ref, mu_ref[:, mu_lora + LANE:mu_lora + 2 * LANE], full)
    gd = shift(gd_ref, pgd_ref, mu_ref[:, mu_lora + 2 * LANE:], full)
    subs = range(RW_SUB)

    def rows(t, s):
        return t[C * s:C * (s + 1)]

    tanh_wd = [rows(jnp.tanh(wd), s).astype(BF16) for s in subs]
    ad_b = [rows(ad, s).astype(BF16) for s in subs]
    sig_gd = [rows(_sigmoid(gd), s).astype(BF16) for s in subs]

    lane_sq = lax.broadcasted_iota(jnp.int32, (LANE, LANE), 1)
    row_sq = lax.broadcasted_iota(jnp.int32, (LANE, LANE), 0)
    same_head = (lane_sq < HD) == (row_sq < HD)
    head_ones = same_head.astype(BF16)
    eye = (lane_sq == row_sq).astype(F32)
    rc = lax.broadcasted_iota(jnp.int32, (C, C), 0)
    cc = lax.broadcasted_iota(jnp.int32, (C, C), 1)
    tril_incl = (cc <= rc).astype(BF16)
    lane_tall = lax.broadcasted_iota(jnp.int32, (2 * C, LANE), 1)
    lane_c = lax.broadcasted_iota(jnp.int32, (C, LANE), 1)
    row_c = lax.broadcasted_iota(jnp.int32, (C, LANE), 0)
    head0_c = lane_c < HD
    strict_lo = lane_c < row_c
    strict_hi = (lane_c >= C) & (lane_c - C < row_c)
    incl_lo = lane_c <= row_c
    incl_hi = (lane_c >= C) & (lane_c - C <= row_c)
    inv_n = 1.0 / HD

    P = range(RW_PAIRS)

    def head_sums(ts):
        his = [t.astype(BF16) for t in ts]
        los = [(t - hi.astype(F32)).astype(BF16) for t, hi in zip(ts, his)]
        out = _dot(jnp.concatenate(his + los, axis=0), head_ones)
        n = len(ts)
        return [out[C * i:C * (i + 1)] + out[C * (n + i):C * (n + i + 1)] for i in range(n)]

    pair_sls = [slice(LANE * i, LANE * (i + 1)) for i in P]

    def each(fn, *lists):
        return [fn(*args) for args in zip(*lists)]

    def items(per_pair):
        return [rows(t, s) for s in subs for t in per_pair]

    r = items([shift(r_ref, pr_ref, mu_ref[:, sl], sl) for sl in pair_sls])
    k = items([shift(k_ref, pk_ref, mu_ref[:, RW_WIDTH + sl.start:RW_WIDTH + sl.stop], sl) for sl in pair_sls])
    v = items([shift(v_ref, pv_ref, mu_ref[:, 2 * RW_WIDTH + sl.start:2 * RW_WIDTH + sl.stop], sl)
               for sl in pair_sls])
    sls = pair_sls * RW_SUB
    sub_of = [s for s in subs for _ in P]

    def log_decay(s, sl):
        x = -(w0_ref[:, sl] + _dot(tanh_wd[s], wup_ref[:, sl]))
        softplus = jnp.maximum(x, 0.0) + jnp.log1p(jnp.exp(-jnp.abs(x)))
        return -jnp.exp(-softplus - 0.5)

    lw = each(log_decay, sub_of, sls)
    a = each(lambda s, sl: _sigmoid(a0_ref[:, sl] + _dot(ad_b[s], aup_ref[:, sl])), sub_of, sls)
    g = each(lambda s, sl: _dot(sig_gd[s], gup_ref[:, sl]), sub_of, sls)

    kkr = each(lambda ki, sl: ki * kk_ref[:, sl], k, sls)
    ksq = head_sums(each(lambda t: t * t, kkr))
    kkn = each(lambda t, ss: t / jnp.maximum(jnp.sqrt(ss), 1e-12), kkr, ksq)
    kmod = each(lambda ki, ai, sl: ki * (1.0 + (ai - 1.0) * ka_ref[:, sl]), k, a, sls)
    beta = each(lambda ai, t: ai * t, a, kkn)

    def running_sum(lwi):
        hi = lwi.astype(BF16)
        both = _dot(tril_incl, jnp.concatenate([hi, (lwi - hi.astype(F32)).astype(BF16)], axis=1))
        return both[:, :LANE] + both[:, LANE:]

    lcum = each(running_sum, lw)
    lend = each(lambda t: t[C - 1:C, :], lcum)
    rh = each(lambda ri, lc: ri * jnp.exp(lc), r, lcum)
    kh = each(lambda t, lc, lwi: t * jnp.exp(lc - lwi), kkn, lcum, lw)
    e_neg = each(lambda lc: jnp.exp(-lc), lcum)
    e_end = each(lambda le, lc: jnp.exp(le - lc), lend, lcum)
    kb = each(lambda t, e: t * e, kmod, e_neg)
    bb = each(lambda t, e: t * e, beta, e_neg)
    kbe = each(lambda t, e: t * e, kmod, e_end)
    bbe = each(lambda t, e: t * e, beta, e_end)

    kr_f = each(lambda x1, x2: jnp.concatenate([x1, x2], axis=0), kh, rh)
    bk = each(lambda x1, x2: jnp.concatenate([x1, x2], axis=0).astype(BF16), bb, kb)
    vb = each(lambda t: t.astype(BF16), v)

    def gram(krf, bki):
        kr2 = jnp.concatenate([jnp.where(lane_tall < HD, krf, 0.0), jnp.where(lane_tall >= HD, krf, 0.0)], axis=0)
        return _dot_nt(kr2.astype(BF16), bki)

    g_all = each(gram, kr_f, bk)

    n_bd = each(lambda ga: jnp.concatenate([jnp.where(strict_lo, -ga[0:C], 0.0),
                                            jnp.where(strict_hi, -pltpu.roll(ga[2 * C:3 * C], C, 1), 0.0)], axis=0),
                g_all)
    t = each(lambda n: eye + n, n_bd)
    pw = each(lambda n: _dot(n.astype(BF16), n.astype(BF16)), n_bd)
    for _ in range(4):
        both = each(lambda ti, pi: _dot(jnp.concatenate([ti, pi], axis=0).astype(BF16), pi.astype(BF16)), t, pw)
        t = each(lambda ti, bi: ti + bi[:LANE], t, both)
        pw = each(lambda bi: bi[LANE:], both)
    t = each(lambda ti, pi: ti + _dot(ti.astype(BF16), pi.astype(BF16)), t, pw)

    def intra_rhs(ga, vbi):
        m1s = jnp.concatenate([jnp.where(strict_hi, ga[0:C], 0.0), jnp.where(strict_hi, ga[2 * C:3 * C], 0.0)], axis=0)
        return _dot(m1s.astype(BF16), jnp.concatenate([vbi, vbi], axis=0))

    q_intra = each(intra_rhs, g_all, vb)

    def m2(gb):
        return jnp.where(incl_lo, -gb, jnp.where(incl_hi, gb, 0.0))

    m2s = each(lambda ga: jnp.concatenate([m2(ga[C:2 * C]), m2(ga[3 * C:4 * C])], axis=0).astype(BF16), g_all)
    kbe_all = each(lambda kbei, bbei: jnp.concatenate([kbei, -bbei], axis=0).astype(BF16), kbe, bbe)
    s_decay = each(jnp.exp, lend)

    state = [s_ref[i] for i in P]
    y = []
    for s in subs:
        of = lambda lst: lst[RW_PAIRS * s:RW_PAIRS * (s + 1)]
        p_all = each(lambda x1, si: _dot_nt(x1.astype(BF16), si.astype(BF16)), of(kr_f), state)
        q_s = each(lambda qi, pa: jnp.where(same_head, qi + jnp.concatenate([pa[:C], pa[:C]], axis=0), 0.0),
                   of(q_intra), p_all)
        u_s = each(lambda ti, qi: _dot(ti.astype(BF16), qi.astype(BF16)), of(t), q_s)
        u = each(lambda us: us[:C] + us[C:], u_s)
        y_s = each(lambda mi, ui, vi: _dot(mi, jnp.concatenate([ui, vi], axis=0).astype(BF16)), of(m2s), u, of(v))
        y += each(lambda pa, ys: pa[C:] + jnp.where(head0_c, ys[:C], ys[C:]), p_all, y_s)
        ds = each(lambda vi, ui, kb_all: _dot_tn(jnp.concatenate([vi, ui], axis=0).astype(BF16), kb_all),
                  of(v), u, of(kbe_all))
        state = each(lambda si, di, dec: si * dec + jnp.where(same_head, di, 0.0), state, ds, of(s_decay))
    for i in P:
        s_ref[i] = state[i]

    d = each(lambda yi, si: yi - si * inv_n, y, head_sums(y))
    var = each(lambda si: si * inv_n, head_sums(each(lambda di: di * di, d)))
    bsum = head_sums(each(lambda ri, ki, sl: ri * ki * rk_ref[:, sl], r, kmod, sls))
    for i, (s, sl) in enumerate(zip(sub_of, sls)):
        yn = d[i] * lax.rsqrt(var[i] + RW_GN_EPS) * lnw_ref[:, sl] + lnb_ref[:, sl]
        o_ref[C * s:C * (s + 1), sl] = ((yn + bsum[i] * v[i]) * g[i]).astype(o_ref.dtype)


def rwkv_mix(p, batch, mu, w0, a0, k_k, k_a, r_k, ln_w, ln_b, w_up, a_up, g_up):
    tp = p.shape[0]
    lp = tp // batch
    C = RW_CHUNK * RW_SUB
    nchunk = lp // C

    def pspec(width, base):
        return pl.BlockSpec((C, width), lambda b, c: (b * nchunk + c, base // width))

    def const(shape):
        return pl.BlockSpec(shape, lambda b, c: (0, 0))

    in_specs = [
        pspec(RW_WIDTH, OFF_RW), pspec(RW_WIDTH, OFF_RW + RW_WIDTH), pspec(RW_WIDTH, OFF_RW + 2 * RW_WIDTH),
        pspec(LANE, OFF_RW_WD), pspec(LANE, OFF_RW_AD), pspec(RW_GATE_LORA, OFF_RW_GD),
        const(mu.shape),
    ] + [const((1, RW_WIDTH))] * 7 + [const(w_up.shape), const(a_up.shape), const(g_up.shape)]
    row = lambda t: t.reshape(1, -1)
    return pl.pallas_call(
        _rwkv_body,
        grid=(batch, nchunk),
        in_specs=in_specs,
        out_specs=pl.BlockSpec((C, RW_WIDTH), lambda b, c: (b * nchunk + c, 0)),
        out_shape=jax.ShapeDtypeStruct((tp, RW_WIDTH), BF16),
        scratch_shapes=[pltpu.VMEM((RW_PAIRS, LANE, LANE), F32)] + [pltpu.VMEM((8, RW_WIDTH), F32)] * 3
        + [pltpu.VMEM((8, LANE), F32)] * 2 + [pltpu.VMEM((8, RW_GATE_LORA), F32)],
        compiler_params=_params("parallel", "arbitrary"),
        name="rwkv7_mix",
    )(p, p, p, p, p, p, mu, row(w0), row(a0), row(k_k), row(k_a), row(r_k), row(ln_w), row(ln_b),
      w_up, a_up, g_up)


def _ret_body(q_ref, k_ref, v_ref, g_ref, cos_ref, sin_ref, o_ref, state_ref):
    c = pl.program_id(1)
    C = RET_CHUNK
    d = RET_HEAD_DIM

    @pl.when(c == 0)
    def _():
        state_ref[...] = jnp.zeros_like(state_ref)

    row = lax.broadcasted_iota(jnp.int32, (C, C), 0).astype(F32)
    col = lax.broadcasted_iota(jnp.int32, (C, C), 1).astype(F32)
    diff = row - col
    causal = diff >= 0
    cos = cos_ref[...]
    sin = sin_ref[...]
    for h in range(RET_HEADS):
        lg = math.log1p(-(2.0 ** (-5.0 - h)))
        sl = slice(d * h, d * (h + 1))
        q = q_ref[:, sl]
        k = k_ref[:, sl]
        q = q * cos + pltpu.roll(q, d // 2, 1) * sin
        k = (k * cos + pltpu.roll(k, d // 2, 1) * sin) * (d ** -0.5)
        g = g_ref[:, sl]
        qb = q.astype(BF16)
        kb = k.astype(BF16)
        vb = v_ref[:, sl].astype(BF16)
        dmat = jnp.where(causal, jnp.exp(lg * jnp.maximum(diff, 0.0)), 0.0)
        s = _dot_nt(qb, kb) * dmat
        state = state_ref[h]
        o = _dot(s.astype(BF16), vb) + _dot(qb, state.astype(BF16)) * jnp.exp(lg * (row + 1.0))
        kd = (k * jnp.exp(lg * (C - 1.0 - row))).astype(BF16)
        state_ref[h] = state * math.exp(lg * C) + _dot_tn(kd, vb)
        o = o * lax.rsqrt(jnp.mean(o * o, axis=-1, keepdims=True) + NORM_EPS)
        o_ref[:, sl] = (g * _sigmoid(g) * o).astype(o_ref.dtype)


def retention_mix(p, batch, cos, sin):
    tp = p.shape[0]
    lp = tp // batch
    C = RET_CHUNK
    nchunk = lp // C
    base = OFF_RET // RET_WIDTH

    def pspec(j):
        return pl.BlockSpec((C, RET_WIDTH), lambda b, c: (b * nchunk + c, base + j))

    tab = pl.BlockSpec((C, RET_HEAD_DIM), lambda b, c: (c, 0))
    return pl.pallas_call(
        _ret_body,
        grid=(batch, nchunk),
        in_specs=[pspec(0), pspec(1), pspec(2), pspec(3), tab, tab],
        out_specs=pl.BlockSpec((C, RET_WIDTH), lambda b, c: (b * nchunk + c, 0)),
        out_shape=jax.ShapeDtypeStruct((tp, RET_WIDTH), BF16),
        scratch_shapes=[pltpu.VMEM((RET_HEADS, RET_HEAD_DIM, RET_HEAD_DIM), F32)],
        compiler_params=_params("parallel", "arbitrary"),
        name="retention_mix",
    )(p, p, p, p, cos, sin)


def _mla_proj_body(qd_ref, kvd_ref, krd_ref, nq_ref, nkv_ref, wuq_ref, wukv_ref, cos_ref, sin_ref,
                   q_out, k_out, v_out):
    cos = cos_ref[...]
    sin = sin_ref[...]

    def rope(x):
        return x * cos + (pltpu.roll(x, MLA_ROPE // 2, 1) + pltpu.roll(x, LANE - MLA_ROPE // 2, 1)) * sin

    def norm(x, g):
        return x * lax.rsqrt(jnp.mean(x * x, axis=-1, keepdims=True) + NORM_EPS) * g

    scale = (MLA_NOPE + MLA_ROPE) ** -0.5 * math.log2(math.e)
    q = _dot(norm(qd_ref[...], nq_ref[...]).astype(BF16), wuq_ref[...]) * scale
    kv = _dot(norm(kvd_ref[...], nkv_ref[...]).astype(BF16), wukv_ref[...])
    kr = rope(krd_ref[...]).astype(k_out.dtype)
    for h in range(MLA_HEADS):
        lo = MLA_QK_PAD * h
        q_out[:, lo:lo + LANE] = q[:, lo:lo + LANE].astype(q_out.dtype)
        q_out[:, lo + LANE:lo + 2 * LANE] = rope(q[:, lo + LANE:lo + 2 * LANE]).astype(q_out.dtype)
        k_out[:, lo:lo + LANE] = kv[:, MLA_NOPE * h:MLA_NOPE * (h + 1)].astype(k_out.dtype)
        k_out[:, lo + LANE:lo + 2 * LANE] = kr
    v_out[...] = kv[:, MLA_HEADS * MLA_NOPE:].astype(v_out.dtype)


def mla_proj(p, batch, norm_q, norm_kv, w_uq, w_ukv, cos, sin, tm):
    tp = p.shape[0]
    lp = tp // batch
    per_seq = lp // tm
    qk_w = MLA_HEADS * MLA_QK_PAD
    const = lambda i: (0, 0)
    return pl.pallas_call(
        _mla_proj_body,
        grid=(tp // tm,),
        in_specs=[
            pl.BlockSpec((tm, MLA_Q_RANK), lambda i: (i, OFF_MLA_Q // MLA_Q_RANK)),
            pl.BlockSpec((tm, MLA_KV_RANK), lambda i: (i, OFF_MLA_KV // MLA_KV_RANK)),
            pl.BlockSpec((tm, LANE), lambda i: (i, OFF_MLA_KR // LANE)),
            pl.BlockSpec((1, MLA_Q_RANK), const),
            pl.BlockSpec((1, MLA_KV_RANK), const),
            pl.BlockSpec((MLA_Q_RANK, qk_w), const),
            pl.BlockSpec((MLA_KV_RANK, MLA_HEADS * (MLA_NOPE + MLA_V)), const),
            pl.BlockSpec((tm, LANE), lambda i: (i % per_seq, 0)),
            pl.BlockSpec((tm, LANE), lambda i: (i % per_seq, 0)),
        ],
        out_specs=[
            pl.BlockSpec((tm, qk_w), lambda i: (i, 0)),
            pl.BlockSpec((tm, qk_w), lambda i: (i, 0)),
            pl.BlockSpec((tm, MLA_WIDTH), lambda i: (i, 0)),
        ],
        out_shape=[
            jax.ShapeDtypeStruct((tp, qk_w), BF16),
            jax.ShapeDtypeStruct((tp, qk_w), BF16),
            jax.ShapeDtypeStruct((tp, MLA_WIDTH), BF16),
        ],
        compiler_params=_params("parallel"),
        name="mla_proj",
    )(p, p, p, norm_q.reshape(1, -1), norm_kv.reshape(1, -1), w_uq, w_ukv, cos, sin)


def _attn_body(q_ref, k_ref, v_ref, o_ref, *, tq):
    i = pl.program_id(2)
    heads = range(ATTN_HEADS_PER_STEP)
    qs = [q_ref[:, MLA_QK_PAD * h:MLA_QK_PAD * (h + 1)] for h in heads]
    row = lax.broadcasted_iota(jnp.int32, (tq, tq), 0)
    col = lax.broadcasted_iota(jnp.int32, (tq, tq), 1)

    def step(j, carry, diagonal):
        ms, ls, accs = carry
        off = pl.multiple_of(j * tq, tq)
        ss = [_dot_nt(qs[h], k_ref[pl.ds(off, tq), MLA_QK_PAD * h:MLA_QK_PAD * (h + 1)]) for h in heads]
        if diagonal:
            ss = [jnp.where(col <= row, s, -jnp.inf) for s in ss]
        m_new = [jnp.maximum(ms[h], jnp.max(ss[h], axis=-1, keepdims=True)) for h in heads]
        alpha = [jnp.exp2(ms[h] - m_new[h]) for h in heads]
        ps = [jnp.exp2(ss[h] - m_new[h]) for h in heads]
        ls = [alpha[h] * ls[h] + jnp.sum(ps[h], axis=-1, keepdims=True) for h in heads]
        pv = [_dot(ps[h].astype(BF16), v_ref[pl.ds(off, tq), MLA_V * h:MLA_V * (h + 1)]) for h in heads]
        accs = [alpha[h] * accs[h] + pv[h] for h in heads]
        return tuple(m_new), tuple(ls), tuple(accs)

    init = (tuple(jnp.full((tq, 1), -1e30, F32) for _ in heads),
            tuple(jnp.zeros((tq, 1), F32) for _ in heads),
            tuple(jnp.zeros((tq, MLA_V), F32) for _ in heads))
    carry = lax.fori_loop(0, i, lambda j, cr: step(j, cr, False), init)
    _, ls, accs = step(i, carry, True)
    for h in heads:
        o_ref[:, MLA_V * h:MLA_V * (h + 1)] = (accs[h] / ls[h]).astype(o_ref.dtype)


def mla_attention(q, k, v, batch, tq):
    tp = q.shape[0]
    lp = tp // batch
    nq = lp // tq
    hs = ATTN_HEADS_PER_STEP
    return pl.pallas_call(
        functools.partial(_attn_body, tq=tq),
        grid=(batch, MLA_HEADS // hs, nq),
        in_specs=[
            pl.BlockSpec((tq, hs * MLA_QK_PAD), lambda b, h, i: (b * nq + i, h)),
            pl.BlockSpec((lp, hs * MLA_QK_PAD), lambda b, h, i: (b, h)),
            pl.BlockSpec((lp, hs * MLA_V), lambda b, h, i: (b, h)),
        ],
        out_specs=pl.BlockSpec((tq, hs * MLA_V), lambda b, h, i: (b * nq + i, h)),
        out_shape=jax.ShapeDtypeStruct((tp, MLA_WIDTH), BF16),
        compiler_params=_params("parallel", "parallel", "arbitrary"),
        name="mla_attention",
    )(q, k, v)


def _merge_body(ya_ref, yb_ref, yc_ref, wa_ref, wb_ref, wc_ref, ga_ref, gb_ref, gc_ref, o_ref):
    def branch(y_ref, w_ref, g_ref):
        return _sigmoid(g_ref[...]) * _dot(y_ref[...], w_ref[...])

    o_ref[...] = (branch(ya_ref, wa_ref, ga_ref) + branch(yb_ref, wb_ref, gb_ref)
                  + branch(yc_ref, wc_ref, gc_ref)).astype(o_ref.dtype)


def merge_branches(ya, yb, yc, wa, wb, wc, p, tm, tn):
    tp = ya.shape[0]

    def yspec(width):
        return pl.BlockSpec((tm, width), lambda i, j: (i, 0))

    def wspec(width):
        return pl.BlockSpec((width, tn), lambda i, j: (0, j))

    def gspec(branch):
        base = (OFF_GATE + branch * D_MODEL) // tn
        return pl.BlockSpec((tm, tn), lambda i, j: (i, base + j))

    return pl.pallas_call(
        _merge_body,
        grid=(tp // tm, D_MODEL // tn),
        in_specs=[yspec(RW_WIDTH), yspec(RET_WIDTH), yspec(MLA_WIDTH),
                  wspec(RW_WIDTH), wspec(RET_WIDTH), wspec(MLA_WIDTH),
                  gspec(0), gspec(1), gspec(2)],
        out_specs=pl.BlockSpec((tm, tn), lambda i, j: (i, j)),
        out_shape=jax.ShapeDtypeStruct((tp, D_MODEL), BF16),
        compiler_params=_params("parallel", "parallel"),
        name="merge_branches",
    )(ya, yb, yc, wa, wb, wc, p, p, p)


def _resid_body(x_ref, w_ref, h_ref, nw_ref, hn_ref, u_ref, *, nk):
    kstep = pl.program_id(1)
    part = _dot(x_ref[...], w_ref[...])

    @pl.when(kstep == 0)
    def _():
        hn_ref[...] = h_ref[...] + part

    @pl.when(kstep > 0)
    def _():
        hn_ref[...] += part

    @pl.when(kstep == nk - 1)
    def _():
        hn = hn_ref[...]
        y = hn * lax.rsqrt(jnp.mean(hn * hn, axis=-1, keepdims=True) + NORM_EPS)
        u_ref[...] = (y * nw_ref[...]).astype(u_ref.dtype)


def resid_matmul_norm(x, w, h, norm_w, tm, tk, u_dtype):
    m, kdim = x.shape
    n = w.shape[1]
    nk = kdim // tk
    return pl.pallas_call(
        functools.partial(_resid_body, nk=nk),
        grid=(m // tm, nk),
        in_specs=[
            pl.BlockSpec((tm, tk), lambda i, k: (i, k)),
            pl.BlockSpec((tk, n), lambda i, k: (k, 0)),
            pl.BlockSpec((tm, n), lambda i, k: (i, 0)),
            pl.BlockSpec((1, n), lambda i, k: (0, 0)),
        ],
        out_specs=[pl.BlockSpec((tm, n), lambda i, k: (i, 0)), pl.BlockSpec((tm, n), lambda i, k: (i, 0))],
        out_shape=[jax.ShapeDtypeStruct((m, n), F32), jax.ShapeDtypeStruct((m, n), u_dtype)],
        compiler_params=_params("parallel", "arbitrary"),
        name="resid_matmul_norm",
    )(x, w, h, norm_w.reshape(1, n))


def _ffn_up_body(x_ref, wg_ref, wu_ref, o_ref):
    x = x_ref[...]
    hg = _dot(x, wg_ref[...])
    hu = _dot(x, wu_ref[...])
    o_ref[...] = (hg * _sigmoid(hg) * hu).astype(o_ref.dtype)


def ffn_up(x, w_gate_up, tm, tn):
    m, kdim = x.shape
    hidden = w_gate_up.shape[1] // 2
    nj = hidden // tn
    return pl.pallas_call(
        _ffn_up_body,
        grid=(m // tm, nj),
        in_specs=[
            pl.BlockSpec((tm, kdim), lambda i, j: (i, 0)),
            pl.BlockSpec((kdim, tn), lambda i, j: (0, j)),
            pl.BlockSpec((kdim, tn), lambda i, j: (0, nj + j)),
        ],
        out_specs=pl.BlockSpec((tm, tn), lambda i, j: (i, j)),
        out_shape=jax.ShapeDtypeStruct((m, hidden), BF16),
        compiler_params=_params("parallel", "parallel"),
        name="ffn_up",
    )(x, w_gate_up, w_gate_up)


def _pad_cols(w, width):
    return jnp.pad(w, [(0, 0)] * (w.ndim - 1) + [(0, width - w.shape[-1])])


def _pack_w_in(w_in):
    rw_cols = 3 * RW_WIDTH + RW_DECAY_LORA + RW_A_LORA + RW_GATE_LORA
    ret_cols = 4 * RET_WIDTH
    o = 0
    rw = w_in[..., o:o + rw_cols]
    o += rw_cols
    ret = w_in[..., o:o + ret_cols]
    o += ret_cols
    qd = w_in[..., o:o + MLA_Q_RANK]
    o += MLA_Q_RANK
    kvd = w_in[..., o:o + MLA_KV_RANK]
    o += MLA_KV_RANK
    krd = w_in[..., o:o + MLA_ROPE]
    o += MLA_ROPE
    gates = w_in[..., o:]
    rkv = rw[..., :3 * RW_WIDTH]
    wd = rw[..., 3 * RW_WIDTH:3 * RW_WIDTH + RW_DECAY_LORA]
    ad = rw[..., 3 * RW_WIDTH + RW_DECAY_LORA:3 * RW_WIDTH + RW_DECAY_LORA + RW_A_LORA]
    gd = rw[..., 3 * RW_WIDTH + RW_DECAY_LORA + RW_A_LORA:]
    parts = [ret, gates, rkv, _pad_cols(wd, LANE), _pad_cols(ad, LANE), gd, qd, kvd, _pad_cols(krd, LANE)]
    packed = jnp.concatenate(parts, axis=-1)
    return _pad_cols(packed, P_COLS).astype(BF16)


def _pack_mu(mu):
    rkv = mu[..., :3 * RW_WIDTH]
    wd = mu[..., 3 * RW_WIDTH:3 * RW_WIDTH + RW_DECAY_LORA]
    ad = mu[..., 3 * RW_WIDTH + RW_DECAY_LORA:3 * RW_WIDTH + RW_DECAY_LORA + RW_A_LORA]
    gd = mu[..., 3 * RW_WIDTH + RW_DECAY_LORA + RW_A_LORA:]
    return jnp.concatenate([rkv, _pad_cols(wd, LANE), _pad_cols(ad, LANE), gd], axis=-1)


def _pad_rows(w, rows):
    return jnp.pad(w, [(0, 0)] * (w.ndim - 2) + [(0, rows - w.shape[-2]), (0, 0)])


def _pack_w_uq(w):
    nl, rank, _ = w.shape
    w = w.reshape(nl, rank, MLA_HEADS, MLA_NOPE + MLA_ROPE)
    w = jnp.pad(w, ((0, 0), (0, 0), (0, 0), (0, MLA_QK_PAD - MLA_NOPE - MLA_ROPE)))
    return w.reshape(nl, rank, MLA_HEADS * MLA_QK_PAD).astype(BF16)


def _pack_w_ukv(w):
    nl, rank, _ = w.shape
    w = w.reshape(nl, rank, MLA_HEADS, 2, MLA_NOPE)
    w = jnp.swapaxes(w, 2, 3)
    return w.reshape(nl, rank, 2 * MLA_HEADS * MLA_NOPE).astype(BF16)


def _rope_tables(lp):
    pos = jnp.arange(lp, dtype=F32)

    def tables(dim):
        inv = ROPE_BASE ** (-jnp.arange(0, dim, 2, dtype=F32) / dim)
        ang = pos[:, None] * inv[None, :]
        return jnp.cos(ang), jnp.sin(ang)

    c, s = tables(RET_HEAD_DIM)
    ret = (jnp.concatenate([c, c], axis=1), jnp.concatenate([-s, s], axis=1))
    c, s = tables(MLA_ROPE)
    z = jnp.zeros((lp, LANE - MLA_ROPE), F32)
    mla = (jnp.concatenate([c, c, z], axis=1), jnp.concatenate([-s, s, z], axis=1))
    return ret, mla


def kernel(x, meta_tokens, norm_mix, w_in, rw_mu, rw_w0, rw_w_up, rw_a0, rw_a_up, rw_g_up, rw_k_k, rw_k_a, rw_r_k, rw_ln_w, rw_ln_b, mla_norm_q, mla_norm_kv, mla_w_uq, mla_w_ukv, w_br_rwkv, w_br_ret, w_br_mla, w_out, norm_ffn, w_gate_up, w_down, final_norm):
    batch, seq, d = x.shape
    depth = w_in.shape[0]
    lp = -(-(N_META + seq) // SEQ_ALIGN) * SEQ_ALIGN
    tp = batch * lp

    meta = jnp.broadcast_to(meta_tokens[None].astype(x.dtype), (batch, N_META, d))
    pad = jnp.zeros((batch, lp - N_META - seq, d), x.dtype)
    h = jnp.concatenate([meta, x, pad], axis=1).reshape(tp, d)

    wp = _pack_w_in(w_in)
    mu = _pack_mu(rw_mu)
    w_up = _pad_rows(rw_w_up, LANE).astype(BF16)
    a_up = _pad_rows(rw_a_up, LANE).astype(BF16)
    g_up = rw_g_up.astype(BF16)
    wuq = _pack_w_uq(mla_w_uq)
    wukv = _pack_w_ukv(mla_w_ukv)
    wa = w_br_rwkv.astype(BF16)
    wb = w_br_ret.astype(BF16)
    wc = w_br_mla.astype(BF16)
    wo = w_out.astype(BF16)
    wgu = w_gate_up.astype(BF16)
    wdn = w_down.astype(BF16)
    (cos_ret, sin_ret), (cos_mla, sin_mla) = _rope_tables(lp)

    def row_tile(pref):
        return next((t for t in pref if tp % t == 0), SEQ_ALIGN)

    tm = row_tile((768,))
    tm_wide = row_tile((1536, 768))
    tm_down = row_tile((512,))
    tm_seq = 384 if lp % 384 == 0 else SEQ_ALIGN
    tk_down = FFN_HIDDEN // 4

    u = rmsnorm(h, norm_mix[0], tm, BF16)
    for l in range(depth):
        p = matmul(u, wp[l], tm_wide, P_TILE_N, F32)
        ya = rwkv_mix(p, batch, mu[l:l + 1], rw_w0[l], rw_a0[l], rw_k_k[l], rw_k_a[l], rw_r_k[l],
                      rw_ln_w[l], rw_ln_b[l], w_up[l], a_up[l], g_up[l])
        yb = retention_mix(p, batch, cos_ret, sin_ret)
        q, k, v = mla_proj(p, batch, mla_norm_q[l], mla_norm_kv[l], wuq[l], wukv[l], cos_mla, sin_mla, tm_seq)
        yc = mla_attention(q, k, v, batch, tm_seq)
        merged = merge_branches(ya, yb, yc, wa[l], wb[l], wc[l], p, tm, 512)
        h, u2 = resid_matmul_norm(merged, wo[l], h, norm_ffn[l], tm_seq, d, BF16)
        act = ffn_up(u2, wgu[l], tm_wide, 512)
        last = l == depth - 1
        h, u = resid_matmul_norm(act, wdn[l], h, final_norm if last else norm_mix[l + 1], tm_down, tk_down,
                                 F32 if last else BF16)
    return u.reshape(batch, lp, d)[:, N_META:N_META + seq]
```

```python
import functools
import math

import jax
import jax.numpy as jnp
from jax import lax
from jax.experimental import pallas as pl
from jax.experimental.pallas import tpu as pltpu

F32 = jnp.float32
BF16 = jnp.bfloat16

D_MODEL = 2048
N_META = 16
NORM_EPS = 1e-6
ROPE_BASE = 10000.0

RW_HEADS = 16
RW_HEAD_DIM = 64
RW_WIDTH = RW_HEADS * RW_HEAD_DIM
RW_DECAY_LORA = 96
RW_A_LORA = 96
RW_GATE_LORA = 256
RW_GN_EPS = RW_HEAD_DIM * 1e-5
RW_CHUNK = 64
RW_SUB = 2
RW_PAIRS = RW_WIDTH // 128

RET_HEADS = 8
RET_HEAD_DIM = 128
RET_WIDTH = RET_HEADS * RET_HEAD_DIM
RET_CHUNK = 128

MLA_HEADS = 8
MLA_NOPE = 128
MLA_ROPE = 64
MLA_V = 128
MLA_Q_RANK = 512
MLA_KV_RANK = 256
MLA_WIDTH = MLA_HEADS * MLA_V
MLA_QK_PAD = 256
ATTN_HEADS_PER_STEP = 2

FFN_HIDDEN = -(-8 * D_MODEL // (3 * 256)) * 256

LANE = 128
SEQ_ALIGN = 128

OFF_RET = 0
OFF_GATE = OFF_RET + 4 * RET_WIDTH
OFF_RW = OFF_GATE + 3 * D_MODEL
OFF_RW_WD = OFF_RW + 3 * RW_WIDTH
OFF_RW_AD = OFF_RW_WD + LANE
OFF_RW_GD = OFF_RW_AD + LANE
OFF_MLA_Q = OFF_RW_GD + RW_GATE_LORA
OFF_MLA_KV = OFF_MLA_Q + MLA_Q_RANK
OFF_MLA_KR = OFF_MLA_KV + MLA_KV_RANK
P_COLS_USED = OFF_MLA_KR + LANE
P_TILE_N = 512
P_COLS = -(-P_COLS_USED // P_TILE_N) * P_TILE_N

VMEM_LIMIT = 48 * 1024 * 1024


def _params(*sem):
    return pltpu.CompilerParams(dimension_semantics=sem, vmem_limit_bytes=VMEM_LIMIT)


def _sigmoid(x):
    return 1.0 / (1.0 + jnp.exp(-x))


def _dot(a, b):
    return jnp.dot(a, b, preferred_element_type=F32)


def _dot_nt(a, b):
    return lax.dot_general(a, b, (((1,), (1,)), ((), ())), preferred_element_type=F32)


def _dot_tn(a, b):
    return lax.dot_general(a, b, (((0,), (0,)), ((), ())), preferred_element_type=F32)


def _rmsnorm_body(x_ref, g_ref, o_ref):
    x = x_ref[...]
    y = x * lax.rsqrt(jnp.mean(x * x, axis=-1, keepdims=True) + NORM_EPS)
    o_ref[...] = (y * g_ref[...]).astype(o_ref.dtype)


def rmsnorm(x, g, tm, out_dtype):
    m, d = x.shape
    return pl.pallas_call(
        _rmsnorm_body,
        grid=(m // tm,),
        in_specs=[pl.BlockSpec((tm, d), lambda i: (i, 0)), pl.BlockSpec((1, d), lambda i: (0, 0))],
        out_specs=pl.BlockSpec((tm, d), lambda i: (i, 0)),
        out_shape=jax.ShapeDtypeStruct((m, d), out_dtype),
        compiler_params=_params("parallel"),
        name="rmsnorm",
    )(x, g.reshape(1, d))


def _matmul_body(x_ref, w_ref, o_ref):
    o_ref[...] = _dot(x_ref[...], w_ref[...]).astype(o_ref.dtype)


def matmul(x, w, tm, tn, out_dtype):
    m, k = x.shape
    n = w.shape[1]
    return pl.pallas_call(
        _matmul_body,
        grid=(m // tm, n // tn),
        in_specs=[pl.BlockSpec((tm, k), lambda i, j: (i, 0)), pl.BlockSpec((k, tn), lambda i, j: (0, j))],
        out_specs=pl.BlockSpec((tm, tn), lambda i, j: (i, j)),
        out_shape=jax.ShapeDtypeStruct((m, n), out_dtype),
        compiler_params=_params("parallel", "parallel"),
        name="in_proj",
    )(x, w)


def _rwkv_body(r_ref, k_ref, v_ref, wd_ref, ad_ref, gd_ref, mu_ref,
               w0_ref, a0_ref, kk_ref, ka_ref, rk_ref, lnw_ref, lnb_ref,
               wup_ref, aup_ref, gup_ref,
               o_ref,
               s_ref, pr_ref, pk_ref, pv_ref, pwd_ref, pad_ref, pgd_ref):
    c = pl.program_id(1)
    C = RW_CHUNK
    RS = RW_SUB * C
    HD = RW_HEAD_DIM

    @pl.when(c == 0)
    def _():
        s_ref[...] = jnp.zeros_like(s_ref)
        pr_ref[...] = jnp.zeros_like(pr_ref)
        pk_ref[...] = jnp.zeros_like(pk_ref)
        pv_ref[...] = jnp.zeros_like(pv_ref)
        pwd_ref[...] = jnp.zeros_like(pwd_ref)
        pad_ref[...] = jnp.zeros_like(pad_ref)
        pgd_ref[...] = jnp.zeros_like(pgd_ref)

    def shift(x_ref, prev_ref, mu, sl):
        z = x_ref[:, sl]
        first = lax.broadcasted_iota(jnp.int32, z.shape, 0) == 0
        zs = jnp.where(first, prev_ref[0:1, sl], pltpu.roll(z, 1, 0))
        prev_ref[0:1, sl] = z[RS - 1:RS, :]
        return z + (zs - z) * mu

    mu_lora = 3 * RW_WIDTH
    full = slice(None)
    wd = shift(wd_ref, pwd_ref, mu_ref[:, mu_lora:mu_lora + LANE], full)
    ad = shift(ad_ref, pad_ref, mu_ref[:, mu_lora + LANE:mu_lora + 2 * LANE], full)
    gd = shift(gd_ref, pgd_ref, mu_ref[:, mu_lora + 2 * LANE:], full)
    subs = range(RW_SUB)

    def rows(t, s):
        return t[C * s:C * (s + 1)]

    tanh_wd = [rows(jnp.tanh(wd), s).astype(BF16) for s in subs]
    ad_b = [rows(ad, s).astype(BF16) for s in subs]
    sig_gd = [rows(_sigmoid(gd), s).astype(BF16) for s in subs]

    lane_sq = lax.broadcasted_iota(jnp.int32, (LANE, LANE), 1)
    row_sq = lax.broadcasted_iota(jnp.int32, (LANE, LANE), 0)
    same_head = (lane_sq < HD) == (row_sq < HD)
    head_ones = same_head.astype(BF16)
    eye = (lane_sq == row_sq).astype(F32)
    rc = lax.broadcasted_iota(jnp.int32, (C, C), 0)
    cc = lax.broadcasted_iota(jnp.int32, (C, C), 1)
    tril_incl = (cc <= rc).astype(BF16)
    lane_tall = lax.broadcasted_iota(jnp.int32, (2 * C, LANE), 1)
    lane_c = lax.broadcasted_iota(jnp.int32, (C, LANE), 1)
    row_c = lax.broadcasted_iota(jnp.int32, (C, LANE), 0)
    head0_c = lane_c < HD
    strict_lo = lane_c < row_c
    strict_hi = (lane_c >= C) & (lane_c - C < row_c)
    incl_lo = lane_c <= row_c
    incl_hi = (lane_c >= C) & (lane_c - C <= row_c)
    inv_n = 1.0 / HD

    P = range(RW_PAIRS)

    def head_sums(ts):
        his = [t.astype(BF16) for t in ts]
        los = [(t - hi.astype(F32)).astype(BF16) for t, hi in zip(ts, his)]
        out = _dot(jnp.concatenate(his + los, axis=0), head_ones)
        n = len(ts)
        return [out[C * i:C * (i + 1)] + out[C * (n + i):C * (n + i + 1)] for i in range(n)]

    pair_sls = [slice(LANE * i, LANE * (i + 1)) for i in P]

    def each(fn, *lists):
        return [fn(*args) for args in zip(*lists)]

    def items(per_pair):
        return [rows(t, s) for s in subs for t in per_pair]

    r = items([shift(r_ref, pr_ref, mu_ref[:, sl], sl) for sl in pair_sls])
    k = items([shift(k_ref, pk_ref, mu_ref[:, RW_WIDTH + sl.start:RW_WIDTH + sl.stop], sl) for sl in pair_sls])
    v = items([shift(v_ref, pv_ref, mu_ref[:, 2 * RW_WIDTH + sl.start:2 * RW_WIDTH + sl.stop], sl)
               for sl in pair_sls])
    sls = pair_sls * RW_SUB
    sub_of = [s for s in subs for _ in P]

    def log_decay(s, sl):
        x = -(w0_ref[:, sl] + _dot(tanh_wd[s], wup_ref[:, sl]))
        softplus = jnp.maximum(x, 0.0) + jnp.log1p(jnp.exp(-jnp.abs(x)))
        return -jnp.exp(-softplus - 0.5)

    lw = each(log_decay, sub_of, sls)
    a = each(lambda s, sl: _sigmoid(a0_ref[:, sl] + _dot(ad_b[s], aup_ref[:, sl])), sub_of, sls)
    g = each(lambda s, sl: _dot(sig_gd[s], gup_ref[:, sl]), sub_of, sls)

    kkr = each(lambda ki, sl: ki * kk_ref[:, sl], k, sls)
    ksq = head_sums(each(lambda t: t * t, kkr))
    kkn = each(lambda t, ss: t / jnp.maximum(jnp.sqrt(ss), 1e-12), kkr, ksq)
    kmod = each(lambda ki, ai, sl: ki * (1.0 + (ai - 1.0) * ka_ref[:, sl]), k, a, sls)
    beta = each(lambda ai, t: ai * t, a, kkn)

    def running_sum(lwi):
        hi = lwi.astype(BF16)
        both = _dot(tril_incl, jnp.concatenate([hi, (lwi - hi.astype(F32)).astype(BF16)], axis=1))
        return both[:, :LANE] + both[:, LANE:]

    lcum = each(running_sum, lw)
    lend = each(lambda t: t[C - 1:C, :], lcum)
    rh = each(lambda ri, lc: ri * jnp.exp(lc), r, lcum)
    kh = each(lambda t, lc, lwi: t * jnp.exp(lc - lwi), kkn, lcum, lw)
    e_neg = each(lambda lc: jnp.exp(-lc), lcum)
    e_end = each(lambda le, lc: jnp.exp(le - lc), lend, lcum)
    kb = each(lambda t, e: t * e, kmod, e_neg)
    bb = each(lambda t, e: t * e, beta, e_neg)
    kbe = each(lambda t, e: t * e, kmod, e_end)
    bbe = each(lambda t, e: t * e, beta, e_end)

    kr_f = each(lambda x1, x2: jnp.concatenate([x1, x2], axis=0), kh, rh)
    bk = each(lambda x1, x2: jnp.concatenate([x1, x2], axis=0).astype(BF16), bb, kb)
    vb = each(lambda t: t.astype(BF16), v)

    def gram(krf, bki):
        kr2 = jnp.concatenate([jnp.where(lane_tall < HD, krf, 0.0), jnp.where(lane_tall >= HD, krf, 0.0)], axis=0)
        return _dot_nt(kr2.astype(BF16), bki)

    g_all = each(gram, kr_f, bk)

    n_bd = each(lambda ga: jnp.concatenate([jnp.where(strict_lo, -ga[0:C], 0.0),
                                            jnp.where(strict_hi, -pltpu.roll(ga[2 * C:3 * C], C, 1), 0.0)], axis=0),
                g_all)
    t = each(lambda n: eye + n, n_bd)
    pw = each(lambda n: _dot(n.astype(BF16), n.astype(BF16)), n_bd)
    for _ in range(4):
        both = each(lambda ti, pi: _dot(jnp.concatenate([ti, pi], axis=0).astype(BF16), pi.astype(BF16)), t, pw)
        t = each(lambda ti, bi: ti + bi[:LANE], t, both)
        pw = each(lambda bi: bi[LANE:], both)
    t = each(lambda ti, pi: ti + _dot(ti.astype(BF16), pi.astype(BF16)), t, pw)

    def intra_rhs(ga, vbi):
        m1s = jnp.concatenate([jnp.where(strict_hi, ga[0:C], 0.0), jnp.where(strict_hi, ga[2 * C:3 * C], 0.0)], axis=0)
        return _dot(m1s.astype(BF16), jnp.concatenate([vbi, vbi], axis=0))

    q_intra = each(intra_rhs, g_all, vb)

    def m2(gb):
        return jnp.where(incl_lo, -gb, jnp.where(incl_hi, gb, 0.0))

    m2s = each(lambda ga: jnp.concatenate([m2(ga[C:2 * C]), m2(ga[3 * C:4 * C])], axis=0).astype(BF16), g_all)
    kbe_all = each(lambda kbei, bbei: jnp.concatenate([kbei, -bbei], axis=0).astype(BF16), kbe, bbe)
    s_decay = each(jnp.exp, lend)

    state = [s_ref[i] for i in P]
    y = []
    for s in subs:
        of = lambda lst: lst[RW_PAIRS * s:RW_PAIRS * (s + 1)]
        p_all = each(lambda x1, si: _dot_nt(x1.astype(BF16), si.astype(BF16)), of(kr_f), state)
        q_s = each(lambda qi, pa: jnp.where(same_head, qi + jnp.concatenate([pa[:C], pa[:C]], axis=0), 0.0),
                   of(q_intra), p_all)
        u_s = each(lambda ti, qi: _dot(ti.astype(BF16), qi.astype(BF16)), of(t), q_s)
        u = each(lambda us: us[:C] + us[C:], u_s)
        y_s = each(lambda mi, ui, vi: _dot(mi, jnp.concatenate([ui, vi], axis=0).astype(BF16)), of(m2s), u, of(v))
        y += each(lambda pa, ys: pa[C:] + jnp.where(head0_c, ys[:C], ys[C:]), p_all, y_s)
        ds = each(lambda vi, ui, kb_all: _dot_tn(jnp.concatenate([vi, ui], axis=0).astype(BF16), kb_all),
                  of(v), u, of(kbe_all))
        state = each(lambda si, di, dec: si * dec + jnp.where(same_head, di, 0.0), state, ds, of(s_decay))
    for i in P:
        s_ref[i] = state[i]

    d = each(lambda yi, si: yi - si * inv_n, y, head_sums(y))
    var = each(lambda si: si * inv_n, head_sums(each(lambda di: di * di, d)))
    bsum = head_sums(each(lambda ri, ki, sl: ri * ki * rk_ref[:, sl], r, kmod, sls))
    for i, (s, sl) in enumerate(zip(sub_of, sls)):
        yn = d[i] * lax.rsqrt(var[i] + RW_GN_EPS) * lnw_ref[:, sl] + lnb_ref[:, sl]
        o_ref[C * s:C * (s + 1), sl] = ((yn + bsum[i] * v[i]) * g[i]).astype(o_ref.dtype)


def rwkv_mix(p, batch, mu, w0, a0, k_k, k_a, r_k, ln_w, ln_b, w_up, a_up, g_up):
    tp = p.shape[0]
    lp = tp // batch
    C = RW_CHUNK * RW_SUB
    nchunk = lp // C

    def pspec(width, base):
        return pl.BlockSpec((C, width), lambda b, c: (b * nchunk + c, base // width))

    def const(shape):
        return pl.BlockSpec(shape, lambda b, c: (0, 0))

    in_specs = [
        pspec(RW_WIDTH, OFF_RW), pspec(RW_WIDTH, OFF_RW + RW_WIDTH), pspec(RW_WIDTH, OFF_RW + 2 * RW_WIDTH),
        pspec(LANE, OFF_RW_WD), pspec(LANE, OFF_RW_AD), pspec(RW_GATE_LORA, OFF_RW_GD),
        const(mu.shape),
    ] + [const((1, RW_WIDTH))] * 7 + [const(w_up.shape), const(a_up.shape), const(g_up.shape)]
    row = lambda t: t.reshape(1, -1)
    return pl.pallas_call(
        _rwkv_body,
        grid=(batch, nchunk),
        in_specs=in_specs,
        out_specs=pl.BlockSpec((C, RW_WIDTH), lambda b, c: (b * nchunk + c, 0)),
        out_shape=jax.ShapeDtypeStruct((tp, RW_WIDTH), BF16),
        scratch_shapes=[pltpu.VMEM((RW_PAIRS, LANE, LANE), F32)] + [pltpu.VMEM((8, RW_WIDTH), F32)] * 3
        + [pltpu.VMEM((8, LANE), F32)] * 2 + [pltpu.VMEM((8, RW_GATE_LORA), F32)],
        compiler_params=_params("parallel", "arbitrary"),
        name="rwkv7_mix",
    )(p, p, p, p, p, p, mu, row(w0), row(a0), row(k_k), row(k_a), row(r_k), row(ln_w), row(ln_b),
      w_up, a_up, g_up)


def _ret_body(q_ref, k_ref, v_ref, g_ref, cos_ref, sin_ref, o_ref, state_ref):
    c = pl.program_id(1)
    C = RET_CHUNK
    d = RET_HEAD_DIM

    @pl.when(c == 0)
    def _():
        state_ref[...] = jnp.zeros_like(state_ref)

    row = lax.broadcasted_iota(jnp.int32, (C, C), 0).astype(F32)
    col = lax.broadcasted_iota(jnp.int32, (C, C), 1).astype(F32)
    diff = row - col
    causal = diff >= 0
    cos = cos_ref[...]
    sin = sin_ref[...]
    heads = range(RET_HEADS)
    lgs = [math.log1p(-(2.0 ** (-5.0 - h))) for h in heads]
    sls = [slice(d * h, d * (h + 1)) for h in heads]

    def each(fn, *lists):
        return [fn(*args) for args in zip(*lists)]

    def rope(x):
        return x * cos + pltpu.roll(x, d // 2, 1) * sin

    qb = each(lambda sl: rope(q_ref[:, sl]).astype(BF16), sls)
    k = each(lambda sl: rope(k_ref[:, sl]) * (d ** -0.5), sls)
    kb = each(lambda t: t.astype(BF16), k)
    vb = each(lambda sl: v_ref[:, sl].astype(BF16), sls)
    state = [state_ref[h] for h in heads]
    s = each(lambda qi, ki, lg: _dot_nt(qi, ki) * jnp.where(causal, jnp.exp(lg * jnp.maximum(diff, 0.0)), 0.0),
             qb, kb, lgs)
    cross = each(lambda qi, st, lg: _dot(qi, st.astype(BF16)) * jnp.exp(lg * (row + 1.0)), qb, state, lgs)
    o = each(lambda si, vi, ci: _dot(si.astype(BF16), vi) + ci, s, vb, cross)
    kd = each(lambda ki, lg: (ki * jnp.exp(lg * (C - 1.0 - row))).astype(BF16), k, lgs)
    new_state = each(lambda st, ki, vi, lg: st * math.exp(lg * C) + _dot_tn(ki, vi), state, kd, vb, lgs)
    for h in heads:
        state_ref[h] = new_state[h]
    o = each(lambda oi: oi * lax.rsqrt(jnp.mean(oi * oi, axis=-1, keepdims=True) + NORM_EPS), o)
    for h in heads:
        g = g_ref[:, sls[h]]
        o_ref[:, sls[h]] = (g * _sigmoid(g) * o[h]).astype(o_ref.dtype)


def retention_mix(p, batch, cos, sin):
    tp = p.shape[0]
    lp = tp // batch
    C = RET_CHUNK
    nchunk = lp // C
    base = OFF_RET // RET_WIDTH

    def pspec(j):
        return pl.BlockSpec((C, RET_WIDTH), lambda b, c: (b * nchunk + c, base + j))

    tab = pl.BlockSpec((C, RET_HEAD_DIM), lambda b, c: (c, 0))
    return pl.pallas_call(
        _ret_body,
        grid=(batch, nchunk),
        in_specs=[pspec(0), pspec(1), pspec(2), pspec(3), tab, tab],
        out_specs=pl.BlockSpec((C, RET_WIDTH), lambda b, c: (b * nchunk + c, 0)),
        out_shape=jax.ShapeDtypeStruct((tp, RET_WIDTH), BF16),
        scratch_shapes=[pltpu.VMEM((RET_HEADS, RET_HEAD_DIM, RET_HEAD_DIM), F32)],
        compiler_params=_params("parallel", "arbitrary"),
        name="retention_mix",
    )(p, p, p, p, cos, sin)


def _mla_proj_body(qd_ref, kvd_ref, krd_ref, nq_ref, nkv_ref, wuq_ref, wukv_ref, cos_ref, sin_ref,
                   q_out, k_out, v_out):
    cos = cos_ref[...]
    sin = sin_ref[...]

    def rope(x):
        return x * cos + (pltpu.roll(x, MLA_ROPE // 2, 1) + pltpu.roll(x, LANE - MLA_ROPE // 2, 1)) * sin

    def norm(x, g):
        return x * lax.rsqrt(jnp.mean(x * x, axis=-1, keepdims=True) + NORM_EPS) * g

    scale = (MLA_NOPE + MLA_ROPE) ** -0.5 * math.log2(math.e)
    q = _dot(norm(qd_ref[...], nq_ref[...]).astype(BF16), wuq_ref[...]) * scale
    kv = _dot(norm(kvd_ref[...], nkv_ref[...]).astype(BF16), wukv_ref[...])
    kr = rope(krd_ref[...]).astype(k_out.dtype)
    for h in range(MLA_HEADS):
        lo = MLA_QK_PAD * h
        q_out[:, lo:lo + LANE] = q[:, lo:lo + LANE].astype(q_out.dtype)
        q_out[:, lo + LANE:lo + 2 * LANE] = rope(q[:, lo + LANE:lo + 2 * LANE]).astype(q_out.dtype)
        k_out[:, lo:lo + LANE] = kv[:, MLA_NOPE * h:MLA_NOPE * (h + 1)].astype(k_out.dtype)
        k_out[:, lo + LANE:lo + 2 * LANE] = kr
    v_out[...] = kv[:, MLA_HEADS * MLA_NOPE:].astype(v_out.dtype)


def mla_proj(p, batch, norm_q, norm_kv, w_uq, w_ukv, cos, sin, tm):
    tp = p.shape[0]
    lp = tp // batch
    per_seq = lp // tm
    qk_w = MLA_HEADS * MLA_QK_PAD
    const = lambda i: (0, 0)
    return pl.pallas_call(
        _mla_proj_body,
        grid=(tp // tm,),
        in_specs=[
            pl.BlockSpec((tm, MLA_Q_RANK), lambda i: (i, OFF_MLA_Q // MLA_Q_RANK)),
            pl.BlockSpec((tm, MLA_KV_RANK), lambda i: (i, OFF_MLA_KV // MLA_KV_RANK)),
            pl.BlockSpec((tm, LANE), lambda i: (i, OFF_MLA_KR // LANE)),
            pl.BlockSpec((1, MLA_Q_RANK), const),
            pl.BlockSpec((1, MLA_KV_RANK), const),
            pl.BlockSpec((MLA_Q_RANK, qk_w), const),
            pl.BlockSpec((MLA_KV_RANK, MLA_HEADS * (MLA_NOPE + MLA_V)), const),
            pl.BlockSpec((tm, LANE), lambda i: (i % per_seq, 0)),
            pl.BlockSpec((tm, LANE), lambda i: (i % per_seq, 0)),
        ],
        out_specs=[
            pl.BlockSpec((tm, qk_w), lambda i: (i, 0)),
            pl.BlockSpec((tm, qk_w), lambda i: (i, 0)),
            pl.BlockSpec((tm, MLA_WIDTH), lambda i: (i, 0)),
        ],
        out_shape=[
            jax.ShapeDtypeStruct((tp, qk_w), BF16),
            jax.ShapeDtypeStruct((tp, qk_w), BF16),
            jax.ShapeDtypeStruct((tp, MLA_WIDTH), BF16),
        ],
        compiler_params=_params("parallel"),
        name="mla_proj",
    )(p, p, p, norm_q.reshape(1, -1), norm_kv.reshape(1, -1), w_uq, w_ukv, cos, sin)


def _attn_body(q_ref, k_ref, v_ref, o_ref, *, tq):
    i = pl.program_id(2)
    heads = range(ATTN_HEADS_PER_STEP)
    qs = [q_ref[:, MLA_QK_PAD * h:MLA_QK_PAD * (h + 1)] for h in heads]
    row = lax.broadcasted_iota(jnp.int32, (tq, tq), 0)
    col = lax.broadcasted_iota(jnp.int32, (tq, tq), 1)

    def step(j, carry, diagonal):
        ms, ls, accs = carry
        off = pl.multiple_of(j * tq, tq)
        ss = [_dot_nt(qs[h], k_ref[pl.ds(off, tq), MLA_QK_PAD * h:MLA_QK_PAD * (h + 1)]) for h in heads]
        if diagonal:
            ss = [jnp.where(col <= row, s, -jnp.inf) for s in ss]
        m_new = [jnp.maximum(ms[h], jnp.max(ss[h], axis=-1, keepdims=True)) for h in heads]
        alpha = [jnp.exp2(ms[h] - m_new[h]) for h in heads]
        ps = [jnp.exp2(ss[h] - m_new[h]) for h in heads]
        ls = [alpha[h] * ls[h] + jnp.sum(ps[h], axis=-1, keepdims=True) for h in heads]
        pv = [_dot(ps[h].astype(BF16), v_ref[pl.ds(off, tq), MLA_V * h:MLA_V * (h + 1)]) for h in heads]
        accs = [alpha[h] * accs[h] + pv[h] for h in heads]
        return tuple(m_new), tuple(ls), tuple(accs)

    init = (tuple(jnp.full((tq, 1), -1e30, F32) for _ in heads),
            tuple(jnp.zeros((tq, 1), F32) for _ in heads),
            tuple(jnp.zeros((tq, MLA_V), F32) for _ in heads))
    carry = lax.fori_loop(0, i, lambda j, cr: step(j, cr, False), init)
    _, ls, accs = step(i, carry, True)
    for h in heads:
        o_ref[:, MLA_V * h:MLA_V * (h + 1)] = (accs[h] / ls[h]).astype(o_ref.dtype)


def mla_attention(q, k, v, batch, tq):
    tp = q.shape[0]
    lp = tp // batch
    nq = lp // tq
    hs = ATTN_HEADS_PER_STEP
    return pl.pallas_call(
        functools.partial(_attn_body, tq=tq),
        grid=(batch, MLA_HEADS // hs, nq),
        in_specs=[
            pl.BlockSpec((tq, hs * MLA_QK_PAD), lambda b, h, i: (b * nq + i, h)),
            pl.BlockSpec((lp, hs * MLA_QK_PAD), lambda b, h, i: (b, h)),
            pl.BlockSpec((lp, hs * MLA_V), lambda b, h, i: (b, h)),
        ],
        out_specs=pl.BlockSpec((tq, hs * MLA_V), lambda b, h, i: (b * nq + i, h)),
        out_shape=jax.ShapeDtypeStruct((tp, MLA_WIDTH), BF16),
        compiler_params=_params("parallel", "parallel", "arbitrary"),
        name="mla_attention",
    )(q, k, v)


def _merge_body(ya_ref, yb_ref, yc_ref, wa_ref, wb_ref, wc_ref, ga_ref, gb_ref, gc_ref, o_ref):
    def branch(y_ref, w_ref, g_ref):
        return _sigmoid(g_ref[...]) * _dot(y_ref[...], w_ref[...])

    o_ref[...] = (branch(ya_ref, wa_ref, ga_ref) + branch(yb_ref, wb_ref, gb_ref)
                  + branch(yc_ref, wc_ref, gc_ref)).astype(o_ref.dtype)


def merge_branches(ya, yb, yc, wa, wb, wc, p, tm, tn):
    tp = ya.shape[0]

    def yspec(width):
        return pl.BlockSpec((tm, width), lambda i, j: (i, 0))

    def wspec(width):
        return pl.BlockSpec((width, tn), lambda i, j: (0, j))

    def gspec(branch):
        base = (OFF_GATE + branch * D_MODEL) // tn
        return pl.BlockSpec((tm, tn), lambda i, j: (i, base + j))

    return pl.pallas_call(
        _merge_body,
        grid=(tp // tm, D_MODEL // tn),
        in_specs=[yspec(RW_WIDTH), yspec(RET_WIDTH), yspec(MLA_WIDTH),
                  wspec(RW_WIDTH), wspec(RET_WIDTH), wspec(MLA_WIDTH),
                  gspec(0), gspec(1), gspec(2)],
        out_specs=pl.BlockSpec((tm, tn), lambda i, j: (i, j)),
        out_shape=jax.ShapeDtypeStruct((tp, D_MODEL), BF16),
        compiler_params=_params("parallel", "parallel"),
        name="merge_branches",
    )(ya, yb, yc, wa, wb, wc, p, p, p)


def _resid_body(x_ref, w_ref, h_ref, nw_ref, hn_ref, u_ref, *, nk):
    kstep = pl.program_id(1)
    part = _dot(x_ref[...], w_ref[...])

    @pl.when(kstep == 0)
    def _():
        hn_ref[...] = h_ref[...] + part

    @pl.when(kstep > 0)
    def _():
        hn_ref[...] += part

    @pl.when(kstep == nk - 1)
    def _():
        hn = hn_ref[...]
        y = hn * lax.rsqrt(jnp.mean(hn * hn, axis=-1, keepdims=True) + NORM_EPS)
        u_ref[...] = (y * nw_ref[...]).astype(u_ref.dtype)


def resid_matmul_norm(x, w, h, norm_w, tm, tk, u_dtype):
    m, kdim = x.shape
    n = w.shape[1]
    nk = kdim // tk
    return pl.pallas_call(
        functools.partial(_resid_body, nk=nk),
        grid=(m // tm, nk),
        in_specs=[
            pl.BlockSpec((tm, tk), lambda i, k: (i, k)),
            pl.BlockSpec((tk, n), lambda i, k: (k, 0)),
            pl.BlockSpec((tm, n), lambda i, k: (i, 0)),
            pl.BlockSpec((1, n), lambda i, k: (0, 0)),
        ],
        out_specs=[pl.BlockSpec((tm, n), lambda i, k: (i, 0)), pl.BlockSpec((tm, n), lambda i, k: (i, 0))],
        out_shape=[jax.ShapeDtypeStruct((m, n), F32), jax.ShapeDtypeStruct((m, n), u_dtype)],
        compiler_params=_params("parallel", "arbitrary"),
        name="resid_matmul_norm",
    )(x, w, h, norm_w.reshape(1, n))


def _ffn_up_body(x_ref, wg_ref, wu_ref, o_ref):
    x = x_ref[...]
    hg = _dot(x, wg_ref[...])
    hu = _dot(x, wu_ref[...])
    o_ref[...] = (hg * _sigmoid(hg) * hu).astype(o_ref.dtype)


def ffn_up(x, w_gate_up, tm, tn):
    m, kdim = x.shape
    hidden = w_gate_up.shape[1] // 2
    nj = hidden // tn
    return pl.pallas_call(
        _ffn_up_body,
        grid=(m // tm, nj),
        in_specs=[
            pl.BlockSpec((tm, kdim), lambda i, j: (i, 0)),
            pl.BlockSpec((kdim, tn), lambda i, j: (0, j)),
            pl.BlockSpec((kdim, tn), lambda i, j: (0, nj + j)),
        ],
        out_specs=pl.BlockSpec((tm, tn), lambda i, j: (i, j)),
        out_shape=jax.ShapeDtypeStruct((m, hidden), BF16),
        compiler_params=_params("parallel", "parallel"),
        name="ffn_up",
    )(x, w_gate_up, w_gate_up)


def _pad_cols(w, width):
    return jnp.pad(w, [(0, 0)] * (w.ndim - 1) + [(0, width - w.shape[-1])])


def _pack_w_in(w_in):
    rw_cols = 3 * RW_WIDTH + RW_DECAY_LORA + RW_A_LORA + RW_GATE_LORA
    ret_cols = 4 * RET_WIDTH
    o = 0
    rw = w_in[..., o:o + rw_cols]
    o += rw_cols
    ret = w_in[..., o:o + ret_cols]
    o += ret_cols
    qd = w_in[..., o:o + MLA_Q_RANK]
    o += MLA_Q_RANK
    kvd = w_in[..., o:o + MLA_KV_RANK]
    o += MLA_KV_RANK
    krd = w_in[..., o:o + MLA_ROPE]
    o += MLA_ROPE
    gates = w_in[..., o:]
    rkv = rw[..., :3 * RW_WIDTH]
    wd = rw[..., 3 * RW_WIDTH:3 * RW_WIDTH + RW_DECAY_LORA]
    ad = rw[..., 3 * RW_WIDTH + RW_DECAY_LORA:3 * RW_WIDTH + RW_DECAY_LORA + RW_A_LORA]
    gd = rw[..., 3 * RW_WIDTH + RW_DECAY_LORA + RW_A_LORA:]
    parts = [ret, gates, rkv, _pad_cols(wd, LANE), _pad_cols(ad, LANE), gd, qd, kvd, _pad_cols(krd, LANE)]
    packed = jnp.concatenate(parts, axis=-1)
    return _pad_cols(packed, P_COLS).astype(BF16)


def _pack_mu(mu):
    rkv = mu[..., :3 * RW_WIDTH]
    wd = mu[..., 3 * RW_WIDTH:3 * RW_WIDTH + RW_DECAY_LORA]
    ad = mu[..., 3 * RW_WIDTH + RW_DECAY_LORA:3 * RW_WIDTH + RW_DECAY_LORA + RW_A_LORA]
    gd = mu[..., 3 * RW_WIDTH + RW_DECAY_LORA + RW_A_LORA:]
    return jnp.concatenate([rkv, _pad_cols(wd, LANE), _pad_cols(ad, LANE), gd], axis=-1)


def _pad_rows(w, rows):
    return jnp.pad(w, [(0, 0)] * (w.ndim - 2) + [(0, rows - w.shape[-2]), (0, 0)])


def _pack_w_uq(w):
    nl, rank, _ = w.shape
    w = w.reshape(nl, rank, MLA_HEADS, MLA_NOPE + MLA_ROPE)
    w = jnp.pad(w, ((0, 0), (0, 0), (0, 0), (0, MLA_QK_PAD - MLA_NOPE - MLA_ROPE)))
    return w.reshape(nl, rank, MLA_HEADS * MLA_QK_PAD).astype(BF16)


def _pack_w_ukv(w):
    nl, rank, _ = w.shape
    w = w.reshape(nl, rank, MLA_HEADS, 2, MLA_NOPE)
    w = jnp.swapaxes(w, 2, 3)
    return w.reshape(nl, rank, 2 * MLA_HEADS * MLA_NOPE).astype(BF16)


def _rope_tables(lp):
    pos = jnp.arange(lp, dtype=F32)

    def tables(dim):
        inv = ROPE_BASE ** (-jnp.arange(0, dim, 2, dtype=F32) / dim)
        ang = pos[:, None] * inv[None, :]
        return jnp.cos(ang), jnp.sin(ang)

    c, s = tables(RET_HEAD_DIM)
    ret = (jnp.concatenate([c, c], axis=1), jnp.concatenate([-s, s], axis=1))
    c, s = tables(MLA_ROPE)
    z = jnp.zeros((lp, LANE - MLA_ROPE), F32)
    mla = (jnp.concatenate([c, c, z], axis=1), jnp.concatenate([-s, s, z], axis=1))
    return ret, mla


def kernel(x, meta_tokens, norm_mix, w_in, rw_mu, rw_w0, rw_w_up, rw_a0, rw_a_up, rw_g_up, rw_k_k, rw_k_a, rw_r_k, rw_ln_w, rw_ln_b, mla_norm_q, mla_norm_kv, mla_w_uq, mla_w_ukv, w_br_rwkv, w_br_ret, w_br_mla, w_out, norm_ffn, w_gate_up, w_down, final_norm):
    batch, seq, d = x.shape
    depth = w_in.shape[0]
    lp = -(-(N_META + seq) // SEQ_ALIGN) * SEQ_ALIGN
    tp = batch * lp

    meta = jnp.broadcast_to(meta_tokens[None].astype(x.dtype), (batch, N_META, d))
    pad = jnp.zeros((batch, lp - N_META - seq, d), x.dtype)
    h = jnp.concatenate([meta, x, pad], axis=1).reshape(tp, d)

    wp = _pack_w_in(w_in)
    mu = _pack_mu(rw_mu)
    w_up = _pad_rows(rw_w_up, LANE).astype(BF16)
    a_up = _pad_rows(rw_a_up, LANE).astype(BF16)
    g_up = rw_g_up.astype(BF16)
    wuq = _pack_w_uq(mla_w_uq)
    wukv = _pack_w_ukv(mla_w_ukv)
    wa = w_br_rwkv.astype(BF16)
    wb = w_br_ret.astype(BF16)
    wc = w_br_mla.astype(BF16)
    wo = w_out.astype(BF16)
    wgu = w_gate_up.astype(BF16)
    wdn = w_down.astype(BF16)
    (cos_ret, sin_ret), (cos_mla, sin_mla) = _rope_tables(lp)

    def row_tile(pref):
        return next((t for t in pref if tp % t == 0), SEQ_ALIGN)

    tm = row_tile((768,))
    tm_wide = row_tile((1536, 768))
    tm_down = row_tile((512,))
    tm_seq = 384 if lp % 384 == 0 else SEQ_ALIGN
    tk_down = FFN_HIDDEN // 4

    u = rmsnorm(h, norm_mix[0], tm, BF16)
    for l in range(depth):
        p = matmul(u, wp[l], tm_wide, P_TILE_N, F32)
        ya = rwkv_mix(p, batch, mu[l:l + 1], rw_w0[l], rw_a0[l], rw_k_k[l], rw_k_a[l], rw_r_k[l],
                      rw_ln_w[l], rw_ln_b[l], w_up[l], a_up[l], g_up[l])
        yb = retention_mix(p, batch, cos_ret, sin_ret)
        q, k, v = mla_proj(p, batch, mla_norm_q[l], mla_norm_kv[l], wuq[l], wukv[l], cos_mla, sin_mla, tm_seq)
        yc = mla_attention(q, k, v, batch, tm_seq)
        merged = merge_branches(ya, yb, yc, wa[l], wb[l], wc[l], p, tm, 1024)
        h, u2 = resid_matmul_norm(merged, wo[l], h, norm_ffn[l], tm_seq, d, BF16)
        act = ffn_up(u2, wgu[l], tm_wide, 512)
        last = l == depth - 1
        h, u = resid_matmul_norm(act, wdn[l], h, final_norm if last else norm_mix[l + 1], tm_down, tk_down,
                                 F32 if last else BF16)
    return u.reshape(batch, lp, d)[:, N_META:N_META + seq]
```

```python
import functools
import math

import jax
import jax.numpy as jnp
from jax import lax
from jax.experimental import pallas as pl
from jax.experimental.pallas import tpu as pltpu

F32 = jnp.float32
BF16 = jnp.bfloat16

D_MODEL = 2048
N_META = 16
NORM_EPS = 1e-6
ROPE_BASE = 10000.0

RW_HEADS = 16
RW_HEAD_DIM = 64
RW_WIDTH = RW_HEADS * RW_HEAD_DIM
RW_DECAY_LORA = 96
RW_A_LORA = 96
RW_GATE_LORA = 256
RW_GN_EPS = RW_HEAD_DIM * 1e-5
RW_CHUNK = 64
RW_SUB = 2
RW_PAIRS = RW_WIDTH // 128

RET_HEADS = 8
RET_HEAD_DIM = 128
RET_WIDTH = RET_HEADS * RET_HEAD_DIM
RET_CHUNK = 128

MLA_HEADS = 8
MLA_NOPE = 128
MLA_ROPE = 64
MLA_V = 128
MLA_Q_RANK = 512
MLA_KV_RANK = 256
MLA_WIDTH = MLA_HEADS * MLA_V
MLA_QK_PAD = 256
ATTN_HEADS_PER_STEP = 2

FFN_HIDDEN = -(-8 * D_MODEL // (3 * 256)) * 256

LANE = 128
SEQ_ALIGN = 128

OFF_RET = 0
OFF_GATE = OFF_RET + 4 * RET_WIDTH
OFF_RW = OFF_GATE + 3 * D_MODEL
OFF_RW_WD = OFF_RW + 3 * RW_WIDTH
OFF_RW_AD = OFF_RW_WD + LANE
OFF_RW_GD = OFF_RW_AD + LANE
OFF_MLA_Q = OFF_RW_GD + RW_GATE_LORA
OFF_MLA_KV = OFF_MLA_Q + MLA_Q_RANK
OFF_MLA_KR = OFF_MLA_KV + MLA_KV_RANK
P_COLS_USED = OFF_MLA_KR + LANE
P_TILE_N = 512
P_COLS = -(-P_COLS_USED // P_TILE_N) * P_TILE_N

VMEM_LIMIT = 48 * 1024 * 1024


def _params(*sem):
    return pltpu.CompilerParams(dimension_semantics=sem, vmem_limit_bytes=VMEM_LIMIT)


def _sigmoid(x):
    return 1.0 / (1.0 + jnp.exp(-x))


def _dot(a, b):
    return jnp.dot(a, b, preferred_element_type=F32)


def _dot_nt(a, b):
    return lax.dot_general(a, b, (((1,), (1,)), ((), ())), preferred_element_type=F32)


def _dot_tn(a, b):
    return lax.dot_general(a, b, (((0,), (0,)), ((), ())), preferred_element_type=F32)


def _rmsnorm_body(x_ref, g_ref, o_ref):
    x = x_ref[...]
    y = x * lax.rsqrt(jnp.mean(x * x, axis=-1, keepdims=True) + NORM_EPS)
    o_ref[...] = (y * g_ref[...]).astype(o_ref.dtype)


def rmsnorm(x, g, tm, out_dtype):
    m, d = x.shape
    return pl.pallas_call(
        _rmsnorm_body,
        grid=(m // tm,),
        in_specs=[pl.BlockSpec((tm, d), lambda i: (i, 0)), pl.BlockSpec((1, d), lambda i: (0, 0))],
        out_specs=pl.BlockSpec((tm, d), lambda i: (i, 0)),
        out_shape=jax.ShapeDtypeStruct((m, d), out_dtype),
        compiler_params=_params("parallel"),
        name="rmsnorm",
    )(x, g.reshape(1, d))


def _normed_rows(x_ref, g_ref, xn_ref):
    @pl.when(pl.program_id(1) == 0)
    def _():
        x = x_ref[...]
        y = x * lax.rsqrt(jnp.mean(x * x, axis=-1, keepdims=True) + NORM_EPS)
        xn_ref[...] = (y * g_ref[...]).astype(xn_ref.dtype)

    return xn_ref[...]


def _norm_matmul_body(x_ref, g_ref, w_ref, o_ref, xn_ref):
    o_ref[...] = _dot(_normed_rows(x_ref, g_ref, xn_ref), w_ref[...]).astype(o_ref.dtype)


def norm_matmul(x, g, w, tm, tn, out_dtype):
    m, k = x.shape
    n = w.shape[1]
    return pl.pallas_call(
        _norm_matmul_body,
        grid=(m // tm, n // tn),
        in_specs=[pl.BlockSpec((tm, k), lambda i, j: (i, 0)), pl.BlockSpec((1, k), lambda i, j: (0, 0)),
                  pl.BlockSpec((k, tn), lambda i, j: (0, j))],
        out_specs=pl.BlockSpec((tm, tn), lambda i, j: (i, j)),
        out_shape=jax.ShapeDtypeStruct((m, n), out_dtype),
        scratch_shapes=[pltpu.VMEM((tm, k), BF16)],
        compiler_params=_params("parallel", "arbitrary"),
        name="in_proj",
    )(x, g.reshape(1, k), w)


def _rwkv_body(r_ref, k_ref, v_ref, wd_ref, ad_ref, gd_ref, mu_ref,
               w0_ref, a0_ref, kk_ref, ka_ref, rk_ref, lnw_ref, lnb_ref,
               wup_ref, aup_ref, gup_ref,
               o_ref,
               s_ref, pr_ref, pk_ref, pv_ref, pwd_ref, pad_ref, pgd_ref):
    c = pl.program_id(1)
    C = RW_CHUNK
    RS = RW_SUB * C
    HD = RW_HEAD_DIM

    @pl.when(c == 0)
    def _():
        s_ref[...] = jnp.zeros_like(s_ref)
        pr_ref[...] = jnp.zeros_like(pr_ref)
        pk_ref[...] = jnp.zeros_like(pk_ref)
        pv_ref[...] = jnp.zeros_like(pv_ref)
        pwd_ref[...] = jnp.zeros_like(pwd_ref)
        pad_ref[...] = jnp.zeros_like(pad_ref)
        pgd_ref[...] = jnp.zeros_like(pgd_ref)

    def shift(x_ref, prev_ref, mu, sl):
        z = x_ref[:, sl]
        first = lax.broadcasted_iota(jnp.int32, z.shape, 0) == 0
        zs = jnp.where(first, prev_ref[0:1, sl], pltpu.roll(z, 1, 0))
        prev_ref[0:1, sl] = z[RS - 1:RS, :]
        return z + (zs - z) * mu

    mu_lora = 3 * RW_WIDTH
    full = slice(None)
    wd = shift(wd_ref, pwd_ref, mu_ref[:, mu_lora:mu_lora + LANE], full)
    ad = shift(ad_ref, pad_ref, mu_ref[:, mu_lora + LANE:mu_lora + 2 * LANE], full)
    gd = shift(gd_ref, pgd_ref, mu_ref[:, mu_lora + 2 * LANE:], full)
    subs = range(RW_SUB)

    def rows(t, s):
        return t[C * s:C * (s + 1)]

    tanh_wd = [rows(jnp.tanh(wd), s).astype(BF16) for s in subs]
    ad_b = [rows(ad, s).astype(BF16) for s in subs]
    sig_gd = [rows(_sigmoid(gd), s).astype(BF16) for s in subs]

    lane_sq = lax.broadcasted_iota(jnp.int32, (LANE, LANE), 1)
    row_sq = lax.broadcasted_iota(jnp.int32, (LANE, LANE), 0)
    same_head = (lane_sq < HD) == (row_sq < HD)
    head_ones = same_head.astype(BF16)
    eye = (lane_sq == row_sq).astype(F32)
    rc = lax.broadcasted_iota(jnp.int32, (C, C), 0)
    cc = lax.broadcasted_iota(jnp.int32, (C, C), 1)
    tril_incl = (cc <= rc).astype(BF16)
    lane_tall = lax.broadcasted_iota(jnp.int32, (2 * C, LANE), 1)
    lane_c = lax.broadcasted_iota(jnp.int32, (C, LANE), 1)
    row_c = lax.broadcasted_iota(jnp.int32, (C, LANE), 0)
    head0_c = lane_c < HD
    strict_lo = lane_c < row_c
    strict_hi = (lane_c >= C) & (lane_c - C < row_c)
    incl_lo = lane_c <= row_c
    incl_hi = (lane_c >= C) & (lane_c - C <= row_c)
    inv_n = 1.0 / HD

    P = range(RW_PAIRS)

    def head_sums(ts):
        his = [t.astype(BF16) for t in ts]
        los = [(t - hi.astype(F32)).astype(BF16) for t, hi in zip(ts, his)]
        out = _dot(jnp.concatenate(his + los, axis=0), head_ones)
        n = len(ts)
        return [out[C * i:C * (i + 1)] + out[C * (n + i):C * (n + i + 1)] for i in range(n)]

    pair_sls = [slice(LANE * i, LANE * (i + 1)) for i in P]

    def each(fn, *lists):
        return [fn(*args) for args in zip(*lists)]

    def items(per_pair):
        return [rows(t, s) for s in subs for t in per_pair]

    r = items([shift(r_ref, pr_ref, mu_ref[:, sl], sl) for sl in pair_sls])
    k = items([shift(k_ref, pk_ref, mu_ref[:, RW_WIDTH + sl.start:RW_WIDTH + sl.stop], sl) for sl in pair_sls])
    v = items([shift(v_ref, pv_ref, mu_ref[:, 2 * RW_WIDTH + sl.start:2 * RW_WIDTH + sl.stop], sl)
               for sl in pair_sls])
    sls = pair_sls * RW_SUB
    sub_of = [s for s in subs for _ in P]

    def log_decay(s, sl):
        x = -(w0_ref[:, sl] + _dot(tanh_wd[s], wup_ref[:, sl]))
        softplus = jnp.maximum(x, 0.0) + jnp.log1p(jnp.exp(-jnp.abs(x)))
        return -jnp.exp(-softplus - 0.5)

    lw = each(log_decay, sub_of, sls)
    a = each(lambda s, sl: _sigmoid(a0_ref[:, sl] + _dot(ad_b[s], aup_ref[:, sl])), sub_of, sls)
    g = each(lambda s, sl: _dot(sig_gd[s], gup_ref[:, sl]), sub_of, sls)

    kkr = each(lambda ki, sl: ki * kk_ref[:, sl], k, sls)
    ksq = head_sums(each(lambda t: t * t, kkr))
    kkn = each(lambda t, ss: t / jnp.maximum(jnp.sqrt(ss), 1e-12), kkr, ksq)
    kmod = each(lambda ki, ai, sl: ki * (1.0 + (ai - 1.0) * ka_ref[:, sl]), k, a, sls)
    beta = each(lambda ai, t: ai * t, a, kkn)

    def running_sum(lwi):
        hi = lwi.astype(BF16)
        both = _dot(tril_incl, jnp.concatenate([hi, (lwi - hi.astype(F32)).astype(BF16)], axis=1))
        return both[:, :LANE] + both[:, LANE:]

    lcum = each(running_sum, lw)
    lend = each(lambda t: t[C - 1:C, :], lcum)
    rh = each(lambda ri, lc: ri * jnp.exp(lc), r, lcum)
    kh = each(lambda t, lc, lwi: t * jnp.exp(lc - lwi), kkn, lcum, lw)
    e_neg = each(lambda lc: jnp.exp(-lc), lcum)
    e_end = each(lambda le, lc: jnp.exp(le - lc), lend, lcum)
    kb = each(lambda t, e: t * e, kmod, e_neg)
    bb = each(lambda t, e: t * e, beta, e_neg)
    kbe = each(lambda t, e: t * e, kmod, e_end)
    bbe = each(lambda t, e: t * e, beta, e_end)

    kr_f = each(lambda x1, x2: jnp.concatenate([x1, x2], axis=0), kh, rh)
    bk = each(lambda x1, x2: jnp.concatenate([x1, x2], axis=0).astype(BF16), bb, kb)
    vb = each(lambda t: t.astype(BF16), v)

    def gram(krf, bki):
        kr2 = jnp.concatenate([jnp.where(lane_tall < HD, krf, 0.0), jnp.where(lane_tall >= HD, krf, 0.0)], axis=0)
        return _dot_nt(kr2.astype(BF16), bki)

    g_all = each(gram, kr_f, bk)

    n_bd = each(lambda ga: jnp.concatenate([jnp.where(strict_lo, -ga[0:C], 0.0),
                                            jnp.where(strict_hi, -pltpu.roll(ga[2 * C:3 * C], C, 1), 0.0)], axis=0),
                g_all)
    t = each(lambda n: eye + n, n_bd)
    pw = each(lambda n: _dot(n.astype(BF16), n.astype(BF16)), n_bd)
    for _ in range(4):
        both = each(lambda ti, pi: _dot(jnp.concatenate([ti, pi], axis=0).astype(BF16), pi.astype(BF16)), t, pw)
        t = each(lambda ti, bi: ti + bi[:LANE], t, both)
        pw = each(lambda bi: bi[LANE:], both)
    t = each(lambda ti, pi: ti + _dot(ti.astype(BF16), pi.astype(BF16)), t, pw)

    def intra_rhs(ga, vbi):
        m1s = jnp.concatenate([jnp.where(strict_hi, ga[0:C], 0.0), jnp.where(strict_hi, ga[2 * C:3 * C], 0.0)], axis=0)
        return _dot(m1s.astype(BF16), jnp.concatenate([vbi, vbi], axis=0))

    q_intra = each(intra_rhs, g_all, vb)

    def m2(gb):
        return jnp.where(incl_lo, -gb, jnp.where(incl_hi, gb, 0.0))

    m2s = each(lambda ga: jnp.concatenate([m2(ga[C:2 * C]), m2(ga[3 * C:4 * C])], axis=0).astype(BF16), g_all)
    kbe_all = each(lambda kbei, bbei: jnp.concatenate([kbei, -bbei], axis=0).astype(BF16), kbe, bbe)
    s_decay = each(jnp.exp, lend)

    state = [s_ref[i] for i in P]
    y = []
    for s in subs:
        of = lambda lst: lst[RW_PAIRS * s:RW_PAIRS * (s + 1)]
        p_all = each(lambda x1, si: _dot_nt(x1.astype(BF16), si.astype(BF16)), of(kr_f), state)
        q_s = each(lambda qi, pa: jnp.where(same_head, qi + jnp.concatenate([pa[:C], pa[:C]], axis=0), 0.0),
                   of(q_intra), p_all)
        u_s = each(lambda ti, qi: _dot(ti.astype(BF16), qi.astype(BF16)), of(t), q_s)
        u = each(lambda us: us[:C] + us[C:], u_s)
        y_s = each(lambda mi, ui, vi: _dot(mi, jnp.concatenate([ui, vi], axis=0).astype(BF16)), of(m2s), u, of(v))
        y += each(lambda pa, ys: pa[C:] + jnp.where(head0_c, ys[:C], ys[C:]), p_all, y_s)
        ds = each(lambda vi, ui, kb_all: _dot_tn(jnp.concatenate([vi, ui], axis=0).astype(BF16), kb_all),
                  of(v), u, of(kbe_all))
        state = each(lambda si, di, dec: si * dec + jnp.where(same_head, di, 0.0), state, ds, of(s_decay))
    for i in P:
        s_ref[i] = state[i]

    d = each(lambda yi, si: yi - si * inv_n, y, head_sums(y))
    var = each(lambda si: si * inv_n, head_sums(each(lambda di: di * di, d)))
    bsum = head_sums(each(lambda ri, ki, sl: ri * ki * rk_ref[:, sl], r, kmod, sls))
    for i, (s, sl) in enumerate(zip(sub_of, sls)):
        yn = d[i] * lax.rsqrt(var[i] + RW_GN_EPS) * lnw_ref[:, sl] + lnb_ref[:, sl]
        o_ref[C * s:C * (s + 1), sl] = ((yn + bsum[i] * v[i]) * g[i]).astype(o_ref.dtype)


def rwkv_mix(p, batch, mu, w0, a0, k_k, k_a, r_k, ln_w, ln_b, w_up, a_up, g_up):
    tp = p.shape[0]
    lp = tp // batch
    C = RW_CHUNK * RW_SUB
    nchunk = lp // C

    def pspec(width, base):
        return pl.BlockSpec((C, width), lambda b, c: (b * nchunk + c, base // width))

    def const(shape):
        return pl.BlockSpec(shape, lambda b, c: (0, 0))

    in_specs = [
        pspec(RW_WIDTH, OFF_RW), pspec(RW_WIDTH, OFF_RW + RW_WIDTH), pspec(RW_WIDTH, OFF_RW + 2 * RW_WIDTH),
        pspec(LANE, OFF_RW_WD), pspec(LANE, OFF_RW_AD), pspec(RW_GATE_LORA, OFF_RW_GD),
        const(mu.shape),
    ] + [const((1, RW_WIDTH))] * 7 + [const(w_up.shape), const(a_up.shape), const(g_up.shape)]
    row = lambda t: t.reshape(1, -1)
    return pl.pallas_call(
        _rwkv_body,
        grid=(batch, nchunk),
        in_specs=in_specs,
        out_specs=pl.BlockSpec((C, RW_WIDTH), lambda b, c: (b * nchunk + c, 0)),
        out_shape=jax.ShapeDtypeStruct((tp, RW_WIDTH), BF16),
        scratch_shapes=[pltpu.VMEM((RW_PAIRS, LANE, LANE), F32)] + [pltpu.VMEM((8, RW_WIDTH), F32)] * 3
        + [pltpu.VMEM((8, LANE), F32)] * 2 + [pltpu.VMEM((8, RW_GATE_LORA), F32)],
        compiler_params=_params("parallel", "arbitrary"),
        name="rwkv7_mix",
    )(p, p, p, p, p, p, mu, row(w0), row(a0), row(k_k), row(k_a), row(r_k), row(ln_w), row(ln_b),
      w_up, a_up, g_up)


def _ret_body(q_ref, k_ref, v_ref, g_ref, cos_ref, sin_ref, o_ref, state_ref):
    c = pl.program_id(1)
    C = RET_CHUNK
    d = RET_HEAD_DIM

    @pl.when(c == 0)
    def _():
        state_ref[...] = jnp.zeros_like(state_ref)

    row = lax.broadcasted_iota(jnp.int32, (C, C), 0).astype(F32)
    col = lax.broadcasted_iota(jnp.int32, (C, C), 1).astype(F32)
    diff = row - col
    causal = diff >= 0
    cos = cos_ref[...]
    sin = sin_ref[...]
    heads = range(RET_HEADS)
    lgs = [math.log1p(-(2.0 ** (-5.0 - h))) for h in heads]
    sls = [slice(d * h, d * (h + 1)) for h in heads]

    def each(fn, *lists):
        return [fn(*args) for args in zip(*lists)]

    def rope(x):
        return x * cos + pltpu.roll(x, d // 2, 1) * sin

    qb = each(lambda sl: rope(q_ref[:, sl]).astype(BF16), sls)
    k = each(lambda sl: rope(k_ref[:, sl]) * (d ** -0.5), sls)
    kb = each(lambda t: t.astype(BF16), k)
    vb = each(lambda sl: v_ref[:, sl].astype(BF16), sls)
    state = [state_ref[h] for h in heads]
    s = each(lambda qi, ki, lg: _dot_nt(qi, ki) * jnp.where(causal, jnp.exp(lg * jnp.maximum(diff, 0.0)), 0.0),
             qb, kb, lgs)
    cross = each(lambda qi, st, lg: _dot(qi, st.astype(BF16)) * jnp.exp(lg * (row + 1.0)), qb, state, lgs)
    o = each(lambda si, vi, ci: _dot(si.astype(BF16), vi) + ci, s, vb, cross)
    kd = each(lambda ki, lg: (ki * jnp.exp(lg * (C - 1.0 - row))).astype(BF16), k, lgs)
    new_state = each(lambda st, ki, vi, lg: st * math.exp(lg * C) + _dot_tn(ki, vi), state, kd, vb, lgs)
    for h in heads:
        state_ref[h] = new_state[h]
    o = each(lambda oi: oi * lax.rsqrt(jnp.mean(oi * oi, axis=-1, keepdims=True) + NORM_EPS), o)
    for h in heads:
        g = g_ref[:, sls[h]]
        o_ref[:, sls[h]] = (g * _sigmoid(g) * o[h]).astype(o_ref.dtype)


def retention_mix(p, batch, cos, sin):
    tp = p.shape[0]
    lp = tp // batch
    C = RET_CHUNK
    nchunk = lp // C
    base = OFF_RET // RET_WIDTH

    def pspec(j):
        return pl.BlockSpec((C, RET_WIDTH), lambda b, c: (b * nchunk + c, base + j))

    tab = pl.BlockSpec((C, RET_HEAD_DIM), lambda b, c: (c, 0))
    return pl.pallas_call(
        _ret_body,
        grid=(batch, nchunk),
        in_specs=[pspec(0), pspec(1), pspec(2), pspec(3), tab, tab],
        out_specs=pl.BlockSpec((C, RET_WIDTH), lambda b, c: (b * nchunk + c, 0)),
        out_shape=jax.ShapeDtypeStruct((tp, RET_WIDTH), BF16),
        scratch_shapes=[pltpu.VMEM((RET_HEADS, RET_HEAD_DIM, RET_HEAD_DIM), F32)],
        compiler_params=_params("parallel", "arbitrary"),
        name="retention_mix",
    )(p, p, p, p, cos, sin)


def _mla_proj_body(qd_ref, kvd_ref, krd_ref, nq_ref, nkv_ref, wuq_ref, wukv_ref, cos_ref, sin_ref,
                   q_out, k_out, v_out):
    cos = cos_ref[...]
    sin = sin_ref[...]

    def rope(x):
        return x * cos + (pltpu.roll(x, MLA_ROPE // 2, 1) + pltpu.roll(x, LANE - MLA_ROPE // 2, 1)) * sin

    def norm(x, g):
        return x * lax.rsqrt(jnp.mean(x * x, axis=-1, keepdims=True) + NORM_EPS) * g

    scale = (MLA_NOPE + MLA_ROPE) ** -0.5 * math.log2(math.e)
    q = _dot(norm(qd_ref[...], nq_ref[...]).astype(BF16), wuq_ref[...]) * scale
    kv = _dot(norm(kvd_ref[...], nkv_ref[...]).astype(BF16), wukv_ref[...])
    kr = rope(krd_ref[...]).astype(k_out.dtype)
    for h in range(MLA_HEADS):
        lo = MLA_QK_PAD * h
        q_out[:, lo:lo + LANE] = q[:, lo:lo + LANE].astype(q_out.dtype)
        q_out[:, lo + LANE:lo + 2 * LANE] = rope(q[:, lo + LANE:lo + 2 * LANE]).astype(q_out.dtype)
        k_out[:, lo:lo + LANE] = kv[:, MLA_NOPE * h:MLA_NOPE * (h + 1)].astype(k_out.dtype)
        k_out[:, lo + LANE:lo + 2 * LANE] = kr
    v_out[...] = kv[:, MLA_HEADS * MLA_NOPE:].astype(v_out.dtype)


def mla_proj(p, batch, norm_q, norm_kv, w_uq, w_ukv, cos, sin, tm):
    tp = p.shape[0]
    lp = tp // batch
    per_seq = lp // tm
    qk_w = MLA_HEADS * MLA_QK_PAD
    const = lambda i: (0, 0)
    return pl.pallas_call(
        _mla_proj_body,
        grid=(tp // tm,),
        in_specs=[
            pl.BlockSpec((tm, MLA_Q_RANK), lambda i: (i, OFF_MLA_Q // MLA_Q_RANK)),
            pl.BlockSpec((tm, MLA_KV_RANK), lambda i: (i, OFF_MLA_KV // MLA_KV_RANK)),
            pl.BlockSpec((tm, LANE), lambda i: (i, OFF_MLA_KR // LANE)),
            pl.BlockSpec((1, MLA_Q_RANK), const),
            pl.BlockSpec((1, MLA_KV_RANK), const),
            pl.BlockSpec((MLA_Q_RANK, qk_w), const),
            pl.BlockSpec((MLA_KV_RANK, MLA_HEADS * (MLA_NOPE + MLA_V)), const),
            pl.BlockSpec((tm, LANE), lambda i: (i % per_seq, 0)),
            pl.BlockSpec((tm, LANE), lambda i: (i % per_seq, 0)),
        ],
        out_specs=[
            pl.BlockSpec((tm, qk_w), lambda i: (i, 0)),
            pl.BlockSpec((tm, qk_w), lambda i: (i, 0)),
            pl.BlockSpec((tm, MLA_WIDTH), lambda i: (i, 0)),
        ],
        out_shape=[
            jax.ShapeDtypeStruct((tp, qk_w), BF16),
            jax.ShapeDtypeStruct((tp, qk_w), BF16),
            jax.ShapeDtypeStruct((tp, MLA_WIDTH), BF16),
        ],
        compiler_params=_params("parallel"),
        name="mla_proj",
    )(p, p, p, norm_q.reshape(1, -1), norm_kv.reshape(1, -1), w_uq, w_ukv, cos, sin)


def _attn_body(q_ref, k_ref, v_ref, o_ref, *, tq):
    i = pl.program_id(2)
    heads = range(ATTN_HEADS_PER_STEP)
    qs = [q_ref[:, MLA_QK_PAD * h:MLA_QK_PAD * (h + 1)] for h in heads]
    row = lax.broadcasted_iota(jnp.int32, (tq, tq), 0)
    col = lax.broadcasted_iota(jnp.int32, (tq, tq), 1)

    def step(j, carry, diagonal):
        ms, ls, accs = carry
        off = pl.multiple_of(j * tq, tq)
        ss = [_dot_nt(qs[h], k_ref[pl.ds(off, tq), MLA_QK_PAD * h:MLA_QK_PAD * (h + 1)]) for h in heads]
        if diagonal:
            ss = [jnp.where(col <= row, s, -jnp.inf) for s in ss]
        m_new = [jnp.maximum(ms[h], jnp.max(ss[h], axis=-1, keepdims=True)) for h in heads]
        alpha = [jnp.exp2(ms[h] - m_new[h]) for h in heads]
        ps = [jnp.exp2(ss[h] - m_new[h]) for h in heads]
        ls = [alpha[h] * ls[h] + jnp.sum(ps[h], axis=-1, keepdims=True) for h in heads]
        pv = [_dot(ps[h].astype(BF16), v_ref[pl.ds(off, tq), MLA_V * h:MLA_V * (h + 1)]) for h in heads]
        accs = [alpha[h] * accs[h] + pv[h] for h in heads]
        return tuple(m_new), tuple(ls), tuple(accs)

    init = (tuple(jnp.full((tq, 1), -1e30, F32) for _ in heads),
            tuple(jnp.zeros((tq, 1), F32) for _ in heads),
            tuple(jnp.zeros((tq, MLA_V), F32) for _ in heads))
    carry = lax.fori_loop(0, i, lambda j, cr: step(j, cr, False), init)
    _, ls, accs = step(i, carry, True)
    for h in heads:
        o_ref[:, MLA_V * h:MLA_V * (h + 1)] = (accs[h] / ls[h]).astype(o_ref.dtype)


def mla_attention(q, k, v, batch, tq):
    tp = q.shape[0]
    lp = tp // batch
    nq = lp // tq
    hs = ATTN_HEADS_PER_STEP
    return pl.pallas_call(
        functools.partial(_attn_body, tq=tq),
        grid=(batch, MLA_HEADS // hs, nq),
        in_specs=[
            pl.BlockSpec((tq, hs * MLA_QK_PAD), lambda b, h, i: (b * nq + i, h)),
            pl.BlockSpec((lp, hs * MLA_QK_PAD), lambda b, h, i: (b, h)),
            pl.BlockSpec((lp, hs * MLA_V), lambda b, h, i: (b, h)),
        ],
        out_specs=pl.BlockSpec((tq, hs * MLA_V), lambda b, h, i: (b * nq + i, h)),
        out_shape=jax.ShapeDtypeStruct((tp, MLA_WIDTH), BF16),
        compiler_params=_params("parallel", "parallel", "arbitrary"),
        name="mla_attention",
    )(q, k, v)


def _merge_body(ya_ref, yb_ref, yc_ref, wa_ref, wb_ref, wc_ref, ga_ref, gb_ref, gc_ref, o_ref):
    def branch(y_ref, w_ref, g_ref):
        return _sigmoid(g_ref[...]) * _dot(y_ref[...], w_ref[...])

    o_ref[...] = (branch(ya_ref, wa_ref, ga_ref) + branch(yb_ref, wb_ref, gb_ref)
                  + branch(yc_ref, wc_ref, gc_ref)).astype(o_ref.dtype)


def merge_branches(ya, yb, yc, wa, wb, wc, p, tm, tn):
    tp = ya.shape[0]

    def yspec(width):
        return pl.BlockSpec((tm, width), lambda i, j: (i, 0))

    def wspec(width):
        return pl.BlockSpec((width, tn), lambda i, j: (0, j))

    def gspec(branch):
        base = (OFF_GATE + branch * D_MODEL) // tn
        return pl.BlockSpec((tm, tn), lambda i, j: (i, base + j))

    return pl.pallas_call(
        _merge_body,
        grid=(tp // tm, D_MODEL // tn),
        in_specs=[yspec(RW_WIDTH), yspec(RET_WIDTH), yspec(MLA_WIDTH),
                  wspec(RW_WIDTH), wspec(RET_WIDTH), wspec(MLA_WIDTH),
                  gspec(0), gspec(1), gspec(2)],
        out_specs=pl.BlockSpec((tm, tn), lambda i, j: (i, j)),
        out_shape=jax.ShapeDtypeStruct((tp, D_MODEL), BF16),
        compiler_params=_params("parallel", "parallel"),
        name="merge_branches",
    )(ya, yb, yc, wa, wb, wc, p, p, p)


def _resid_body(x_ref, w_ref, h_ref, o_ref):
    o_ref[...] = h_ref[...] + _dot(x_ref[...], w_ref[...])


def resid_matmul(x, w, h, tm, tn):
    m, kdim = x.shape
    n = w.shape[1]
    return pl.pallas_call(
        _resid_body,
        grid=(m // tm, n // tn),
        in_specs=[
            pl.BlockSpec((tm, kdim), lambda i, j: (i, 0)),
            pl.BlockSpec((kdim, tn), lambda i, j: (0, j)),
            pl.BlockSpec((tm, tn), lambda i, j: (i, j)),
        ],
        out_specs=pl.BlockSpec((tm, tn), lambda i, j: (i, j)),
        out_shape=jax.ShapeDtypeStruct((m, n), F32),
        compiler_params=_params("parallel", "parallel"),
        name="resid_matmul",
    )(x, w, h)


def _ffn_up_body(x_ref, g_ref, wg_ref, wu_ref, o_ref, xn_ref):
    x = _normed_rows(x_ref, g_ref, xn_ref)
    hg = _dot(x, wg_ref[...])
    hu = _dot(x, wu_ref[...])
    o_ref[...] = (hg * _sigmoid(hg) * hu).astype(o_ref.dtype)


def ffn_up(x, g, w_gate_up, tm, tn):
    m, kdim = x.shape
    hidden = w_gate_up.shape[1] // 2
    nj = hidden // tn
    return pl.pallas_call(
        _ffn_up_body,
        grid=(m // tm, nj),
        in_specs=[
            pl.BlockSpec((tm, kdim), lambda i, j: (i, 0)),
            pl.BlockSpec((1, kdim), lambda i, j: (0, 0)),
            pl.BlockSpec((kdim, tn), lambda i, j: (0, j)),
            pl.BlockSpec((kdim, tn), lambda i, j: (0, nj + j)),
        ],
        out_specs=pl.BlockSpec((tm, tn), lambda i, j: (i, j)),
        out_shape=jax.ShapeDtypeStruct((m, hidden), BF16),
        scratch_shapes=[pltpu.VMEM((tm, kdim), BF16)],
        compiler_params=_params("parallel", "arbitrary"),
        name="ffn_up",
    )(x, g.reshape(1, kdim), w_gate_up, w_gate_up)


def _pad_cols(w, width):
    return jnp.pad(w, [(0, 0)] * (w.ndim - 1) + [(0, width - w.shape[-1])])


def _pack_w_in(w_in):
    rw_cols = 3 * RW_WIDTH + RW_DECAY_LORA + RW_A_LORA + RW_GATE_LORA
    ret_cols = 4 * RET_WIDTH
    o = 0
    rw = w_in[..., o:o + rw_cols]
    o += rw_cols
    ret = w_in[..., o:o + ret_cols]
    o += ret_cols
    qd = w_in[..., o:o + MLA_Q_RANK]
    o += MLA_Q_RANK
    kvd = w_in[..., o:o + MLA_KV_RANK]
    o += MLA_KV_RANK
    krd = w_in[..., o:o + MLA_ROPE]
    o += MLA_ROPE
    gates = w_in[..., o:]
    rkv = rw[..., :3 * RW_WIDTH]
    wd = rw[..., 3 * RW_WIDTH:3 * RW_WIDTH + RW_DECAY_LORA]
    ad = rw[..., 3 * RW_WIDTH + RW_DECAY_LORA:3 * RW_WIDTH + RW_DECAY_LORA + RW_A_LORA]
    gd = rw[..., 3 * RW_WIDTH + RW_DECAY_LORA + RW_A_LORA:]
    parts = [ret, gates, rkv, _pad_cols(wd, LANE), _pad_cols(ad, LANE), gd, qd, kvd, _pad_cols(krd, LANE)]
    packed = jnp.concatenate(parts, axis=-1)
    return _pad_cols(packed, P_COLS).astype(BF16)


def _pack_mu(mu):
    rkv = mu[..., :3 * RW_WIDTH]
    wd = mu[..., 3 * RW_WIDTH:3 * RW_WIDTH + RW_DECAY_LORA]
    ad = mu[..., 3 * RW_WIDTH + RW_DECAY_LORA:3 * RW_WIDTH + RW_DECAY_LORA + RW_A_LORA]
    gd = mu[..., 3 * RW_WIDTH + RW_DECAY_LORA + RW_A_LORA:]
    return jnp.concatenate([rkv, _pad_cols(wd, LANE), _pad_cols(ad, LANE), gd], axis=-1)


def _pad_rows(w, rows):
    return jnp.pad(w, [(0, 0)] * (w.ndim - 2) + [(0, rows - w.shape[-2]), (0, 0)])


def _pack_w_uq(w):
    nl, rank, _ = w.shape
    w = w.reshape(nl, rank, MLA_HEADS, MLA_NOPE + MLA_ROPE)
    w = jnp.pad(w, ((0, 0), (0, 0), (0, 0), (0, MLA_QK_PAD - MLA_NOPE - MLA_ROPE)))
    return w.reshape(nl, rank, MLA_HEADS * MLA_QK_PAD).astype(BF16)


def _pack_w_ukv(w):
    nl, rank, _ = w.shape
    w = w.reshape(nl, rank, MLA_HEADS, 2, MLA_NOPE)
    w = jnp.swapaxes(w, 2, 3)
    return w.reshape(nl, rank, 2 * MLA_HEADS * MLA_NOPE).astype(BF16)


def _rope_tables(lp):
    pos = jnp.arange(lp, dtype=F32)

    def tables(dim):
        inv = ROPE_BASE ** (-jnp.arange(0, dim, 2, dtype=F32) / dim)
        ang = pos[:, None] * inv[None, :]
        return jnp.cos(ang), jnp.sin(ang)

    c, s = tables(RET_HEAD_DIM)
    ret = (jnp.concatenate([c, c], axis=1), jnp.concatenate([-s, s], axis=1))
    c, s = tables(MLA_ROPE)
    z = jnp.zeros((lp, LANE - MLA_ROPE), F32)
    mla = (jnp.concatenate([c, c, z], axis=1), jnp.concatenate([-s, s, z], axis=1))
    return ret, mla


def kernel(x, meta_tokens, norm_mix, w_in, rw_mu, rw_w0, rw_w_up, rw_a0, rw_a_up, rw_g_up, rw_k_k, rw_k_a, rw_r_k, rw_ln_w, rw_ln_b, mla_norm_q, mla_norm_kv, mla_w_uq, mla_w_ukv, w_br_rwkv, w_br_ret, w_br_mla, w_out, norm_ffn, w_gate_up, w_down, final_norm):
    batch, seq, d = x.shape
    depth = w_in.shape[0]
    lp = -(-(N_META + seq) // SEQ_ALIGN) * SEQ_ALIGN
    tp = batch * lp

    meta = jnp.broadcast_to(meta_tokens[None].astype(x.dtype), (batch, N_META, d))
    pad = jnp.zeros((batch, lp - N_META - seq, d), x.dtype)
    h = jnp.concatenate([meta, x, pad], axis=1).reshape(tp, d)

    wp = _pack_w_in(w_in)
    mu = _pack_mu(rw_mu)
    w_up = _pad_rows(rw_w_up, LANE).astype(BF16)
    a_up = _pad_rows(rw_a_up, LANE).astype(BF16)
    g_up = rw_g_up.astype(BF16)
    wuq = _pack_w_uq(mla_w_uq)
    wukv = _pack_w_ukv(mla_w_ukv)
    wa = w_br_rwkv.astype(BF16)
    wb = w_br_ret.astype(BF16)
    wc = w_br_mla.astype(BF16)
    wo = w_out.astype(BF16)
    wgu = w_gate_up.astype(BF16)
    wdn = w_down.astype(BF16)
    (cos_ret, sin_ret), (cos_mla, sin_mla) = _rope_tables(lp)

    def row_tile(pref):
        return next((t for t in pref if tp % t == 0), SEQ_ALIGN)

    tm = row_tile((768,))
    tm_wide = row_tile((1536, 768))
    tm_seq = 384 if lp % 384 == 0 else SEQ_ALIGN

    for l in range(depth):
        p = norm_matmul(h, norm_mix[l], wp[l], tm_wide, P_TILE_N, F32)
        ya = rwkv_mix(p, batch, mu[l:l + 1], rw_w0[l], rw_a0[l], rw_k_k[l], rw_k_a[l], rw_r_k[l],
                      rw_ln_w[l], rw_ln_b[l], w_up[l], a_up[l], g_up[l])
        yb = retention_mix(p, batch, cos_ret, sin_ret)
        q, k, v = mla_proj(p, batch, mla_norm_q[l], mla_norm_kv[l], wuq[l], wukv[l], cos_mla, sin_mla, tm_seq)
        yc = mla_attention(q, k, v, batch, tm_seq)
        merged = merge_branches(ya, yb, yc, wa[l], wb[l], wc[l], p, tm, 1024)
        h = resid_matmul(merged, wo[l], h, tm, 512)
        act = ffn_up(h, norm_ffn[l], wgu[l], tm, 512)
        h = resid_matmul(act, wdn[l], h, tm, 512)
    out = rmsnorm(h, final_norm, tm, F32)
    return out.reshape(batch, lp, d)[:, N_META:N_META + seq]
```

```python
import functools
import math

import jax
import jax.numpy as jnp
from jax import lax
from jax.experimental import pallas as pl
from jax.experimental.pallas import tpu as pltpu

F32 = jnp.float32
BF16 = jnp.bfloat16

D_MODEL = 2048
N_META = 16
NORM_EPS = 1e-6
ROPE_BASE = 10000.0

RW_HEADS = 16
RW_HEAD_DIM = 64
RW_WIDTH = RW_HEADS * RW_HEAD_DIM
RW_DECAY_LORA = 96
RW_A_LORA = 96
RW_GATE_LORA = 256
RW_GN_EPS = RW_HEAD_DIM * 1e-5
RW_CHUNK = 64
RW_SUB = 2
RW_PAIRS = RW_WIDTH // 128

RET_HEADS = 8
RET_HEAD_DIM = 128
RET_WIDTH = RET_HEADS * RET_HEAD_DIM
RET_CHUNK = 128
RET_SUB = 3

MLA_HEADS = 8
MLA_NOPE = 128
MLA_ROPE = 64
MLA_V = 128
MLA_Q_RANK = 512
MLA_KV_RANK = 256
MLA_WIDTH = MLA_HEADS * MLA_V
MLA_QK_PAD = 256
ATTN_HEADS_PER_STEP = 2

FFN_HIDDEN = -(-8 * D_MODEL // (3 * 256)) * 256

LANE = 128
SEQ_ALIGN = 128

OFF_RET = 0
OFF_GATE = OFF_RET + 4 * RET_WIDTH
OFF_RW = OFF_GATE + 3 * D_MODEL
OFF_RW_WD = OFF_RW + 3 * RW_WIDTH
OFF_RW_AD = OFF_RW_WD + LANE
OFF_RW_GD = OFF_RW_AD + LANE
OFF_MLA_Q = OFF_RW_GD + RW_GATE_LORA
OFF_MLA_KV = OFF_MLA_Q + MLA_Q_RANK
OFF_MLA_KR = OFF_MLA_KV + MLA_KV_RANK
P_COLS_USED = OFF_MLA_KR + LANE
P_TILE_N = 512
P_COLS = -(-P_COLS_USED // P_TILE_N) * P_TILE_N

VMEM_LIMIT = 48 * 1024 * 1024


def _params(*sem):
    return pltpu.CompilerParams(dimension_semantics=sem, vmem_limit_bytes=VMEM_LIMIT)


def _sigmoid(x):
    return 1.0 / (1.0 + jnp.exp(-x))


def _dot(a, b):
    return jnp.dot(a, b, preferred_element_type=F32)


def _dot_nt(a, b):
    return lax.dot_general(a, b, (((1,), (1,)), ((), ())), preferred_element_type=F32)


def _dot_tn(a, b):
    return lax.dot_general(a, b, (((0,), (0,)), ((), ())), preferred_element_type=F32)


def _rmsnorm_body(x_ref, g_ref, o_ref):
    x = x_ref[...]
    y = x * lax.rsqrt(jnp.mean(x * x, axis=-1, keepdims=True) + NORM_EPS)
    o_ref[...] = (y * g_ref[...]).astype(o_ref.dtype)


def rmsnorm(x, g, tm, out_dtype):
    m, d = x.shape
    return pl.pallas_call(
        _rmsnorm_body,
        grid=(m // tm,),
        in_specs=[pl.BlockSpec((tm, d), lambda i: (i, 0)), pl.BlockSpec((1, d), lambda i: (0, 0))],
        out_specs=pl.BlockSpec((tm, d), lambda i: (i, 0)),
        out_shape=jax.ShapeDtypeStruct((m, d), out_dtype),
        compiler_params=_params("parallel"),
        name="rmsnorm",
    )(x, g.reshape(1, d))


def _normed_rows(x_ref, g_ref, xn_ref):
    @pl.when(pl.program_id(1) == 0)
    def _():
        x = x_ref[...]
        y = x * lax.rsqrt(jnp.mean(x * x, axis=-1, keepdims=True) + NORM_EPS)
        xn_ref[...] = (y * g_ref[...]).astype(xn_ref.dtype)

    return xn_ref[...]


def _norm_matmul_body(x_ref, g_ref, w_ref, o_ref, xn_ref):
    o_ref[...] = _dot(_normed_rows(x_ref, g_ref, xn_ref), w_ref[...]).astype(o_ref.dtype)


def norm_matmul(x, g, w, tm, tn, out_dtype):
    m, k = x.shape
    n = w.shape[1]
    return pl.pallas_call(
        _norm_matmul_body,
        grid=(m // tm, n // tn),
        in_specs=[pl.BlockSpec((tm, k), lambda i, j: (i, 0)), pl.BlockSpec((1, k), lambda i, j: (0, 0)),
                  pl.BlockSpec((k, tn), lambda i, j: (0, j))],
        out_specs=pl.BlockSpec((tm, tn), lambda i, j: (i, j)),
        out_shape=jax.ShapeDtypeStruct((m, n), out_dtype),
        scratch_shapes=[pltpu.VMEM((tm, k), BF16)],
        compiler_params=_params("parallel", "arbitrary"),
        name="in_proj",
    )(x, g.reshape(1, k), w)


def _rwkv_body(r_ref, k_ref, v_ref, wd_ref, ad_ref, gd_ref, mu_ref,
               w0_ref, a0_ref, kk_ref, ka_ref, rk_ref, lnw_ref, lnb_ref,
               wup_ref, aup_ref, gup_ref,
               o_ref,
               s_ref, pr_ref, pk_ref, pv_ref, pwd_ref, pad_ref, pgd_ref):
    c = pl.program_id(1)
    C = RW_CHUNK
    RS = RW_SUB * C
    HD = RW_HEAD_DIM

    @pl.when(c == 0)
    def _():
        s_ref[...] = jnp.zeros_like(s_ref)
        pr_ref[...] = jnp.zeros_like(pr_ref)
        pk_ref[...] = jnp.zeros_like(pk_ref)
        pv_ref[...] = jnp.zeros_like(pv_ref)
        pwd_ref[...] = jnp.zeros_like(pwd_ref)
        pad_ref[...] = jnp.zeros_like(pad_ref)
        pgd_ref[...] = jnp.zeros_like(pgd_ref)

    def shift(x_ref, prev_ref, mu, sl):
        z = x_ref[:, sl]
        first = lax.broadcasted_iota(jnp.int32, z.shape, 0) == 0
        zs = jnp.where(first, prev_ref[0:1, sl], pltpu.roll(z, 1, 0))
        prev_ref[0:1, sl] = z[RS - 1:RS, :]
        return z + (zs - z) * mu

    mu_lora = 3 * RW_WIDTH
    full = slice(None)
    wd = shift(wd_ref, pwd_ref, mu_ref[:, mu_lora:mu_lora + LANE], full)
    ad = shift(ad_ref, pad_ref, mu_ref[:, mu_lora + LANE:mu_lora + 2 * LANE], full)
    gd = shift(gd_ref, pgd_ref, mu_ref[:, mu_lora + 2 * LANE:], full)
    subs = range(RW_SUB)

    def rows(t, s):
        return t[C * s:C * (s + 1)]

    tanh_wd = [rows(jnp.tanh(wd), s).astype(BF16) for s in subs]
    ad_b = [rows(ad, s).astype(BF16) for s in subs]
    sig_gd = [rows(_sigmoid(gd), s).astype(BF16) for s in subs]

    lane_sq = lax.broadcasted_iota(jnp.int32, (LANE, LANE), 1)
    row_sq = lax.broadcasted_iota(jnp.int32, (LANE, LANE), 0)
    same_head = (lane_sq < HD) == (row_sq < HD)
    head_ones = same_head.astype(BF16)
    eye = (lane_sq == row_sq).astype(F32)
    rc = lax.broadcasted_iota(jnp.int32, (C, C), 0)
    cc = lax.broadcasted_iota(jnp.int32, (C, C), 1)
    tril_incl = (cc <= rc).astype(BF16)
    lane_tall = lax.broadcasted_iota(jnp.int32, (2 * C, LANE), 1)
    lane_c = lax.broadcasted_iota(jnp.int32, (C, LANE), 1)
    row_c = lax.broadcasted_iota(jnp.int32, (C, LANE), 0)
    head0_c = lane_c < HD
    strict_lo = lane_c < row_c
    strict_hi = (lane_c >= C) & (lane_c - C < row_c)
    incl_lo = lane_c <= row_c
    incl_hi = (lane_c >= C) & (lane_c - C <= row_c)
    inv_n = 1.0 / HD

    P = range(RW_PAIRS)

    def head_sums(ts):
        his = [t.astype(BF16) for t in ts]
        los = [(t - hi.astype(F32)).astype(BF16) for t, hi in zip(ts, his)]
        out = _dot(jnp.concatenate(his + los, axis=0), head_ones)
        n = len(ts)
        return [out[C * i:C * (i + 1)] + out[C * (n + i):C * (n + i + 1)] for i in range(n)]

    pair_sls = [slice(LANE * i, LANE * (i + 1)) for i in P]

    def each(fn, *lists):
        return [fn(*args) for args in zip(*lists)]

    def items(per_pair):
        return [rows(t, s) for s in subs for t in per_pair]

    r = items([shift(r_ref, pr_ref, mu_ref[:, sl], sl) for sl in pair_sls])
    k = items([shift(k_ref, pk_ref, mu_ref[:, RW_WIDTH + sl.start:RW_WIDTH + sl.stop], sl) for sl in pair_sls])
    v = items([shift(v_ref, pv_ref, mu_ref[:, 2 * RW_WIDTH + sl.start:2 * RW_WIDTH + sl.stop], sl)
               for sl in pair_sls])
    sls = pair_sls * RW_SUB
    sub_of = [s for s in subs for _ in P]

    def log_decay(s, sl):
        x = -(w0_ref[:, sl] + _dot(tanh_wd[s], wup_ref[:, sl]))
        softplus = jnp.maximum(x, 0.0) + jnp.log1p(jnp.exp(-jnp.abs(x)))
        return -jnp.exp(-softplus - 0.5)

    lw = each(log_decay, sub_of, sls)
    a = each(lambda s, sl: _sigmoid(a0_ref[:, sl] + _dot(ad_b[s], aup_ref[:, sl])), sub_of, sls)
    g = each(lambda s, sl: _dot(sig_gd[s], gup_ref[:, sl]), sub_of, sls)

    kkr = each(lambda ki, sl: ki * kk_ref[:, sl], k, sls)
    ksq = head_sums(each(lambda t: t * t, kkr))
    kkn = each(lambda t, ss: t / jnp.maximum(jnp.sqrt(ss), 1e-12), kkr, ksq)
    kmod = each(lambda ki, ai, sl: ki * (1.0 + (ai - 1.0) * ka_ref[:, sl]), k, a, sls)
    beta = each(lambda ai, t: ai * t, a, kkn)

    def running_sum(lwi):
        hi = lwi.astype(BF16)
        both = _dot(tril_incl, jnp.concatenate([hi, (lwi - hi.astype(F32)).astype(BF16)], axis=1))
        return both[:, :LANE] + both[:, LANE:]

    lcum = each(running_sum, lw)
    lend = each(lambda t: t[C - 1:C, :], lcum)
    rh = each(lambda ri, lc: ri * jnp.exp(lc), r, lcum)
    kh = each(lambda t, lc, lwi: t * jnp.exp(lc - lwi), kkn, lcum, lw)
    e_neg = each(lambda lc: jnp.exp(-lc), lcum)
    e_end = each(lambda le, lc: jnp.exp(le - lc), lend, lcum)
    kb = each(lambda t, e: t * e, kmod, e_neg)
    bb = each(lambda t, e: t * e, beta, e_neg)
    kbe = each(lambda t, e: t * e, kmod, e_end)
    bbe = each(lambda t, e: t * e, beta, e_end)

    kr_f = each(lambda x1, x2: jnp.concatenate([x1, x2], axis=0), kh, rh)
    bk = each(lambda x1, x2: jnp.concatenate([x1, x2], axis=0).astype(BF16), bb, kb)
    vb = each(lambda t: t.astype(BF16), v)

    def gram(krf, bki):
        kr2 = jnp.concatenate([jnp.where(lane_tall < HD, krf, 0.0), jnp.where(lane_tall >= HD, krf, 0.0)], axis=0)
        return _dot_nt(kr2.astype(BF16), bki)

    g_all = each(gram, kr_f, bk)

    n_bd = each(lambda ga: jnp.concatenate([jnp.where(strict_lo, -ga[0:C], 0.0),
                                            jnp.where(strict_hi, -pltpu.roll(ga[2 * C:3 * C], C, 1), 0.0)], axis=0),
                g_all)
    t = each(lambda n: eye + n, n_bd)
    pw = each(lambda n: _dot(n.astype(BF16), n.astype(BF16)), n_bd)
    for _ in range(4):
        both = each(lambda ti, pi: _dot(jnp.concatenate([ti, pi], axis=0).astype(BF16), pi.astype(BF16)), t, pw)
        t = each(lambda ti, bi: ti + bi[:LANE], t, both)
        pw = each(lambda bi: bi[LANE:], both)
    t = each(lambda ti, pi: ti + _dot(ti.astype(BF16), pi.astype(BF16)), t, pw)

    def intra_rhs(ga, vbi):
        m1s = jnp.concatenate([jnp.where(strict_hi, ga[0:C], 0.0), jnp.where(strict_hi, ga[2 * C:3 * C], 0.0)], axis=0)
        return _dot(m1s.astype(BF16), jnp.concatenate([vbi, vbi], axis=0))

    q_intra = each(intra_rhs, g_all, vb)

    def m2(gb):
        return jnp.where(incl_lo, -gb, jnp.where(incl_hi, gb, 0.0))

    m2s = each(lambda ga: jnp.concatenate([m2(ga[C:2 * C]), m2(ga[3 * C:4 * C])], axis=0).astype(BF16), g_all)
    kbe_all = each(lambda kbei, bbei: jnp.concatenate([kbei, -bbei], axis=0).astype(BF16), kbe, bbe)
    s_decay = each(jnp.exp, lend)

    state = [s_ref[i] for i in P]
    y = []
    for s in subs:
        of = lambda lst: lst[RW_PAIRS * s:RW_PAIRS * (s + 1)]
        p_all = each(lambda x1, si: _dot_nt(x1.astype(BF16), si.astype(BF16)), of(kr_f), state)
        q_s = each(lambda qi, pa: jnp.where(same_head, qi + jnp.concatenate([pa[:C], pa[:C]], axis=0), 0.0),
                   of(q_intra), p_all)
        u_s = each(lambda ti, qi: _dot(ti.astype(BF16), qi.astype(BF16)), of(t), q_s)
        u = each(lambda us: us[:C] + us[C:], u_s)
        y_s = each(lambda mi, ui, vi: _dot(mi, jnp.concatenate([ui, vi], axis=0).astype(BF16)), of(m2s), u, of(v))
        y += each(lambda pa, ys: pa[C:] + jnp.where(head0_c, ys[:C], ys[C:]), p_all, y_s)
        ds = each(lambda vi, ui, kb_all: _dot_tn(jnp.concatenate([vi, ui], axis=0).astype(BF16), kb_all),
                  of(v), u, of(kbe_all))
        state = each(lambda si, di, dec: si * dec + jnp.where(same_head, di, 0.0), state, ds, of(s_decay))
    for i in P:
        s_ref[i] = state[i]

    d = each(lambda yi, si: yi - si * inv_n, y, head_sums(y))
    var = each(lambda si: si * inv_n, head_sums(each(lambda di: di * di, d)))
    bsum = head_sums(each(lambda ri, ki, sl: ri * ki * rk_ref[:, sl], r, kmod, sls))
    for i, (s, sl) in enumerate(zip(sub_of, sls)):
        yn = d[i] * lax.rsqrt(var[i] + RW_GN_EPS) * lnw_ref[:, sl] + lnb_ref[:, sl]
        o_ref[C * s:C * (s + 1), sl] = ((yn + bsum[i] * v[i]) * g[i]).astype(o_ref.dtype)


def rwkv_mix(p, batch, mu, w0, a0, k_k, k_a, r_k, ln_w, ln_b, w_up, a_up, g_up):
    tp = p.shape[0]
    lp = tp // batch
    C = RW_CHUNK * RW_SUB
    nchunk = lp // C

    def pspec(width, base):
        return pl.BlockSpec((C, width), lambda b, c: (b * nchunk + c, base // width))

    def const(shape):
        return pl.BlockSpec(shape, lambda b, c: (0, 0))

    in_specs = [
        pspec(RW_WIDTH, OFF_RW), pspec(RW_WIDTH, OFF_RW + RW_WIDTH), pspec(RW_WIDTH, OFF_RW + 2 * RW_WIDTH),
        pspec(LANE, OFF_RW_WD), pspec(LANE, OFF_RW_AD), pspec(RW_GATE_LORA, OFF_RW_GD),
        const(mu.shape),
    ] + [const((1, RW_WIDTH))] * 7 + [const(w_up.shape), const(a_up.shape), const(g_up.shape)]
    row = lambda t: t.reshape(1, -1)
    return pl.pallas_call(
        _rwkv_body,
        grid=(batch, nchunk),
        in_specs=in_specs,
        out_specs=pl.BlockSpec((C, RW_WIDTH), lambda b, c: (b * nchunk + c, 0)),
        out_shape=jax.ShapeDtypeStruct((tp, RW_WIDTH), BF16),
        scratch_shapes=[pltpu.VMEM((RW_PAIRS, LANE, LANE), F32)] + [pltpu.VMEM((8, RW_WIDTH), F32)] * 3
        + [pltpu.VMEM((8, LANE), F32)] * 2 + [pltpu.VMEM((8, RW_GATE_LORA), F32)],
        compiler_params=_params("parallel", "arbitrary"),
        name="rwkv7_mix",
    )(p, p, p, p, p, p, mu, row(w0), row(a0), row(k_k), row(k_a), row(r_k), row(ln_w), row(ln_b),
      w_up, a_up, g_up)


def _ret_body(q_ref, k_ref, v_ref, g_ref, cos_ref, sin_ref, o_ref, state_ref):
    c = pl.program_id(1)
    C = RET_CHUNK
    d = RET_HEAD_DIM

    @pl.when(c == 0)
    def _():
        state_ref[...] = jnp.zeros_like(state_ref)

    row = lax.broadcasted_iota(jnp.int32, (C, C), 0).astype(F32)
    col = lax.broadcasted_iota(jnp.int32, (C, C), 1).astype(F32)
    diff = row - col
    causal = diff >= 0
    heads = range(RET_HEADS)
    subs = range(RET_SUB)
    rws = [slice(C * s, C * (s + 1)) for s in subs for _ in heads]
    sls = [slice(d * h, d * (h + 1)) for _ in subs for h in heads]
    lgs = [math.log1p(-(2.0 ** (-5.0 - h))) for _ in subs for h in heads]

    def each(fn, *lists):
        return [fn(*args) for args in zip(*lists)]

    def rope(x_ref, rw, sl):
        x = x_ref[rw, sl]
        return x * cos_ref[rw, :] + pltpu.roll(x, d // 2, 1) * sin_ref[rw, :]

    qb = each(lambda rw, sl: rope(q_ref, rw, sl).astype(BF16), rws, sls)
    k = each(lambda rw, sl: rope(k_ref, rw, sl) * (d ** -0.5), rws, sls)
    kb = each(lambda t: t.astype(BF16), k)
    vb = each(lambda rw, sl: v_ref[rw, sl].astype(BF16), rws, sls)
    s = each(lambda qi, ki, lg: _dot_nt(qi, ki) * jnp.where(causal, jnp.exp(lg * jnp.maximum(diff, 0.0)), 0.0),
             qb, kb, lgs)
    o_intra = each(lambda si, vi: _dot(si.astype(BF16), vi), s, vb)
    kd = each(lambda ki, lg: (ki * jnp.exp(lg * (C - 1.0 - row))).astype(BF16), k, lgs)
    kv = each(_dot_tn, kd, vb)

    state = [state_ref[h] for h in heads]
    o = []
    for sub in subs:
        of = lambda lst: lst[RET_HEADS * sub:RET_HEADS * (sub + 1)]
        o += each(lambda oi, qi, st, lg: oi + _dot(qi, st.astype(BF16)) * jnp.exp(lg * (row + 1.0)),
                  of(o_intra), of(qb), state, of(lgs))
        state = each(lambda st, kvi, lg: st * math.exp(lg * C) + kvi, state, of(kv), of(lgs))
    for h in heads:
        state_ref[h] = state[h]
    o = each(lambda oi: oi * lax.rsqrt(jnp.mean(oi * oi, axis=-1, keepdims=True) + NORM_EPS), o)
    for oi, rw, sl in zip(o, rws, sls):
        g = g_ref[rw, sl]
        o_ref[rw, sl] = (g * _sigmoid(g) * oi).astype(o_ref.dtype)


def retention_mix(p, batch, cos, sin):
    tp = p.shape[0]
    lp = tp // batch
    C = RET_CHUNK * RET_SUB
    nchunk = lp // C
    base = OFF_RET // RET_WIDTH

    def pspec(j):
        return pl.BlockSpec((C, RET_WIDTH), lambda b, c: (b * nchunk + c, base + j))

    tab = pl.BlockSpec((C, RET_HEAD_DIM), lambda b, c: (c, 0))
    return pl.pallas_call(
        _ret_body,
        grid=(batch, nchunk),
        in_specs=[pspec(0), pspec(1), pspec(2), pspec(3), tab, tab],
        out_specs=pl.BlockSpec((C, RET_WIDTH), lambda b, c: (b * nchunk + c, 0)),
        out_shape=jax.ShapeDtypeStruct((tp, RET_WIDTH), BF16),
        scratch_shapes=[pltpu.VMEM((RET_HEADS, RET_HEAD_DIM, RET_HEAD_DIM), F32)],
        compiler_params=_params("parallel", "arbitrary"),
        name="retention_mix",
    )(p, p, p, p, cos, sin)


def _mla_proj_body(qd_ref, kvd_ref, krd_ref, nq_ref, nkv_ref, wuq_ref, wukv_ref, cos_ref, sin_ref,
                   q_out, k_out, v_out):
    cos = cos_ref[...]
    sin = sin_ref[...]

    def rope(x):
        return x * cos + (pltpu.roll(x, MLA_ROPE // 2, 1) + pltpu.roll(x, LANE - MLA_ROPE // 2, 1)) * sin

    def norm(x, g):
        return x * lax.rsqrt(jnp.mean(x * x, axis=-1, keepdims=True) + NORM_EPS) * g

    scale = (MLA_NOPE + MLA_ROPE) ** -0.5 * math.log2(math.e)
    q = _dot(norm(qd_ref[...], nq_ref[...]).astype(BF16), wuq_ref[...]) * scale
    kv = _dot(norm(kvd_ref[...], nkv_ref[...]).astype(BF16), wukv_ref[...])
    kr = rope(krd_ref[...]).astype(k_out.dtype)
    for h in range(MLA_HEADS):
        lo = MLA_QK_PAD * h
        q_out[:, lo:lo + LANE] = q[:, lo:lo + LANE].astype(q_out.dtype)
        q_out[:, lo + LANE:lo + 2 * LANE] = rope(q[:, lo + LANE:lo + 2 * LANE]).astype(q_out.dtype)
        k_out[:, lo:lo + LANE] = kv[:, MLA_NOPE * h:MLA_NOPE * (h + 1)].astype(k_out.dtype)
        k_out[:, lo + LANE:lo + 2 * LANE] = kr
    v_out[...] = kv[:, MLA_HEADS * MLA_NOPE:].astype(v_out.dtype)


def mla_proj(p, batch, norm_q, norm_kv, w_uq, w_ukv, cos, sin, tm):
    tp = p.shape[0]
    lp = tp // batch
    per_seq = lp // tm
    qk_w = MLA_HEADS * MLA_QK_PAD
    const = lambda i: (0, 0)
    return pl.pallas_call(
        _mla_proj_body,
        grid=(tp // tm,),
        in_specs=[
            pl.BlockSpec((tm, MLA_Q_RANK), lambda i: (i, OFF_MLA_Q // MLA_Q_RANK)),
            pl.BlockSpec((tm, MLA_KV_RANK), lambda i: (i, OFF_MLA_KV // MLA_KV_RANK)),
            pl.BlockSpec((tm, LANE), lambda i: (i, OFF_MLA_KR // LANE)),
            pl.BlockSpec((1, MLA_Q_RANK), const),
            pl.BlockSpec((1, MLA_KV_RANK), const),
            pl.BlockSpec((MLA_Q_RANK, qk_w), const),
            pl.BlockSpec((MLA_KV_RANK, MLA_HEADS * (MLA_NOPE + MLA_V)), const),
            pl.BlockSpec((tm, LANE), lambda i: (i % per_seq, 0)),
            pl.BlockSpec((tm, LANE), lambda i: (i % per_seq, 0)),
        ],
        out_specs=[
            pl.BlockSpec((tm, qk_w), lambda i: (i, 0)),
            pl.BlockSpec((tm, qk_w), lambda i: (i, 0)),
            pl.BlockSpec((tm, MLA_WIDTH), lambda i: (i, 0)),
        ],
        out_shape=[
            jax.ShapeDtypeStruct((tp, qk_w), BF16),
            jax.ShapeDtypeStruct((tp, qk_w), BF16),
            jax.ShapeDtypeStruct((tp, MLA_WIDTH), BF16),
        ],
        compiler_params=_params("parallel"),
        name="mla_proj",
    )(p, p, p, norm_q.reshape(1, -1), norm_kv.reshape(1, -1), w_uq, w_ukv, cos, sin)


def _attn_body(q_ref, k_ref, v_ref, o_ref, *, tq):
    i = pl.program_id(2)
    heads = range(ATTN_HEADS_PER_STEP)
    qs = [q_ref[:, MLA_QK_PAD * h:MLA_QK_PAD * (h + 1)] for h in heads]

    def step(off, width, masked, carry):
        ms, ls, accs = carry
        off = pl.multiple_of(off, tq)
        ss = [_dot_nt(qs[h], k_ref[pl.ds(off, width), MLA_QK_PAD * h:MLA_QK_PAD * (h + 1)]) for h in heads]
        if masked:
            row = i * tq + lax.broadcasted_iota(jnp.int32, (tq, width), 0)
            col = off + lax.broadcasted_iota(jnp.int32, (tq, width), 1)
            ss = [jnp.where(col <= row, s, -jnp.inf) for s in ss]
        m_new = [jnp.maximum(ms[h], jnp.max(ss[h], axis=-1, keepdims=True)) for h in heads]
        alpha = [jnp.exp2(ms[h] - m_new[h]) for h in heads]
        ps = [jnp.exp2(ss[h] - m_new[h]) for h in heads]
        ls = [alpha[h] * ls[h] + jnp.sum(ps[h], axis=-1, keepdims=True) for h in heads]
        pv = [_dot(ps[h].astype(BF16), v_ref[pl.ds(off, width), MLA_V * h:MLA_V * (h + 1)]) for h in heads]
        accs = [alpha[h] * accs[h] + pv[h] for h in heads]
        return tuple(m_new), tuple(ls), tuple(accs)

    carry = (tuple(jnp.full((tq, 1), -1e30, F32) for _ in heads),
             tuple(jnp.zeros((tq, 1), F32) for _ in heads),
             tuple(jnp.zeros((tq, MLA_V), F32) for _ in heads))
    n_single = (i + 1) % 2
    n_pairs = (i + 1) // 2
    carry = lax.cond(i == 0, lambda c: step(0, tq, True, c), lambda c: c, carry)
    carry = lax.cond((n_single == 1) & (i > 0), lambda c: step(0, tq, False, c), lambda c: c, carry)
    pair_off = lambda p: (n_single + 2 * p) * tq
    carry = lax.fori_loop(0, n_pairs - 1, lambda p, c: step(pair_off(p), 2 * tq, False, c), carry)
    carry = lax.cond(n_pairs > 0, lambda c: step(pair_off(n_pairs - 1), 2 * tq, True, c), lambda c: c, carry)
    _, ls, accs = carry
    for h in heads:
        o_ref[:, MLA_V * h:MLA_V * (h + 1)] = (accs[h] / ls[h]).astype(o_ref.dtype)


def mla_attention(q, k, v, batch, tq):
    tp = q.shape[0]
    lp = tp // batch
    nq = lp // tq
    hs = ATTN_HEADS_PER_STEP
    return pl.pallas_call(
        functools.partial(_attn_body, tq=tq),
        grid=(batch, MLA_HEADS // hs, nq),
        in_specs=[
            pl.BlockSpec((tq, hs * MLA_QK_PAD), lambda b, h, i: (b * nq + i, h)),
            pl.BlockSpec((lp, hs * MLA_QK_PAD), lambda b, h, i: (b, h)),
            pl.BlockSpec((lp, hs * MLA_V), lambda b, h, i: (b, h)),
        ],
        out_specs=pl.BlockSpec((tq, hs * MLA_V), lambda b, h, i: (b * nq + i, h)),
        out_shape=jax.ShapeDtypeStruct((tp, MLA_WIDTH), BF16),
        compiler_params=_params("parallel", "parallel", "arbitrary"),
        name="mla_attention",
    )(q, k, v)


def _merge_body(ya_ref, yb_ref, yc_ref, wa_ref, wb_ref, wc_ref, ga_ref, gb_ref, gc_ref, o_ref):
    def branch(y_ref, w_ref, g_ref):
        return _sigmoid(g_ref[...]) * _dot(y_ref[...], w_ref[...])

    o_ref[...] = (branch(ya_ref, wa_ref, ga_ref) + branch(yb_ref, wb_ref, gb_ref)
                  + branch(yc_ref, wc_ref, gc_ref)).astype(o_ref.dtype)


def merge_branches(ya, yb, yc, wa, wb, wc, p, tm, tn):
    tp = ya.shape[0]

    def yspec(width):
        return pl.BlockSpec((tm, width), lambda i, j: (i, 0))

    def wspec(width):
        return pl.BlockSpec((width, tn), lambda i, j: (0, j))

    def gspec(branch):
        base = (OFF_GATE + branch * D_MODEL) // tn
        return pl.BlockSpec((tm, tn), lambda i, j: (i, base + j))

    return pl.pallas_call(
        _merge_body,
        grid=(tp // tm, D_MODEL // tn),
        in_specs=[yspec(RW_WIDTH), yspec(RET_WIDTH), yspec(MLA_WIDTH),
                  wspec(RW_WIDTH), wspec(RET_WIDTH), wspec(MLA_WIDTH),
                  gspec(0), gspec(1), gspec(2)],
        out_specs=pl.BlockSpec((tm, tn), lambda i, j: (i, j)),
        out_shape=jax.ShapeDtypeStruct((tp, D_MODEL), BF16),
        compiler_params=_params("parallel", "parallel"),
        name="merge_branches",
    )(ya, yb, yc, wa, wb, wc, p, p, p)


def _resid_body(x_ref, w_ref, h_ref, o_ref):
    o_ref[...] = h_ref[...] + _dot(x_ref[...], w_ref[...])


def resid_matmul(x, w, h, tm, tn):
    m, kdim = x.shape
    n = w.shape[1]
    return pl.pallas_call(
        _resid_body,
        grid=(m // tm, n // tn),
        in_specs=[
            pl.BlockSpec((tm, kdim), lambda i, j: (i, 0)),
            pl.BlockSpec((kdim, tn), lambda i, j: (0, j)),
            pl.BlockSpec((tm, tn), lambda i, j: (i, j)),
        ],
        out_specs=pl.BlockSpec((tm, tn), lambda i, j: (i, j)),
        out_shape=jax.ShapeDtypeStruct((m, n), F32),
        compiler_params=_params("parallel", "parallel"),
        name="resid_matmul",
    )(x, w, h)


def _ffn_up_body(x_ref, g_ref, wg_ref, wu_ref, o_ref, xn_ref):
    x = _normed_rows(x_ref, g_ref, xn_ref)
    hg = _dot(x, wg_ref[...])
    hu = _dot(x, wu_ref[...])
    o_ref[...] = (hg * _sigmoid(hg) * hu).astype(o_ref.dtype)


def ffn_up(x, g, w_gate_up, tm, tn):
    m, kdim = x.shape
    hidden = w_gate_up.shape[1] // 2
    nj = hidden // tn
    return pl.pallas_call(
        _ffn_up_body,
        grid=(m // tm, nj),
        in_specs=[
            pl.BlockSpec((tm, kdim), lambda i, j: (i, 0)),
            pl.BlockSpec((1, kdim), lambda i, j: (0, 0)),
            pl.BlockSpec((kdim, tn), lambda i, j: (0, j)),
            pl.BlockSpec((kdim, tn), lambda i, j: (0, nj + j)),
        ],
        out_specs=pl.BlockSpec((tm, tn), lambda i, j: (i, j)),
        out_shape=jax.ShapeDtypeStruct((m, hidden), BF16),
        scratch_shapes=[pltpu.VMEM((tm, kdim), BF16)],
        compiler_params=_params("parallel", "arbitrary"),
        name="ffn_up",
    )(x, g.reshape(1, kdim), w_gate_up, w_gate_up)


def _pad_cols(w, width):
    return jnp.pad(w, [(0, 0)] * (w.ndim - 1) + [(0, width - w.shape[-1])])


def _pack_w_in(w_in):
    rw_cols = 3 * RW_WIDTH + RW_DECAY_LORA + RW_A_LORA + RW_GATE_LORA
    ret_cols = 4 * RET_WIDTH
    o = 0
    rw = w_in[..., o:o + rw_cols]
    o += rw_cols
    ret = w_in[..., o:o + ret_cols]
    o += ret_cols
    qd = w_in[..., o:o + MLA_Q_RANK]
    o += MLA_Q_RANK
    kvd = w_in[..., o:o + MLA_KV_RANK]
    o += MLA_KV_RANK
    krd = w_in[..., o:o + MLA_ROPE]
    o += MLA_ROPE
    gates = w_in[..., o:]
    rkv = rw[..., :3 * RW_WIDTH]
    wd = rw[..., 3 * RW_WIDTH:3 * RW_WIDTH + RW_DECAY_LORA]
    ad = rw[..., 3 * RW_WIDTH + RW_DECAY_LORA:3 * RW_WIDTH + RW_DECAY_LORA + RW_A_LORA]
    gd = rw[..., 3 * RW_WIDTH + RW_DECAY_LORA + RW_A_LORA:]
    parts = [ret, gates, rkv, _pad_cols(wd, LANE), _pad_cols(ad, LANE), gd, qd, kvd, _pad_cols(krd, LANE)]
    packed = jnp.concatenate(parts, axis=-1)
    return _pad_cols(packed, P_COLS).astype(BF16)


def _pack_mu(mu):
    rkv = mu[..., :3 * RW_WIDTH]
    wd = mu[..., 3 * RW_WIDTH:3 * RW_WIDTH + RW_DECAY_LORA]
    ad = mu[..., 3 * RW_WIDTH + RW_DECAY_LORA:3 * RW_WIDTH + RW_DECAY_LORA + RW_A_LORA]
    gd = mu[..., 3 * RW_WIDTH + RW_DECAY_LORA + RW_A_LORA:]
    return jnp.concatenate([rkv, _pad_cols(wd, LANE), _pad_cols(ad, LANE), gd], axis=-1)


def _pad_rows(w, rows):
    return jnp.pad(w, [(0, 0)] * (w.ndim - 2) + [(0, rows - w.shape[-2]), (0, 0)])


def _pack_w_uq(w):
    nl, rank, _ = w.shape
    w = w.reshape(nl, rank, MLA_HEADS, MLA_NOPE + MLA_ROPE)
    w = jnp.pad(w, ((0, 0), (0, 0), (0, 0), (0, MLA_QK_PAD - MLA_NOPE - MLA_ROPE)))
    return w.reshape(nl, rank, MLA_HEADS * MLA_QK_PAD).astype(BF16)


def _pack_w_ukv(w):
    nl, rank, _ = w.shape
    w = w.reshape(nl, rank, MLA_HEADS, 2, MLA_NOPE)
    w = jnp.swapaxes(w, 2, 3)
    return w.reshape(nl, rank, 2 * MLA_HEADS * MLA_NOPE).astype(BF16)


def _rope_tables(lp):
    pos = jnp.arange(lp, dtype=F32)

    def tables(dim):
        inv = ROPE_BASE ** (-jnp.arange(0, dim, 2, dtype=F32) / dim)
        ang = pos[:, None] * inv[None, :]
        return jnp.cos(ang), jnp.sin(ang)

    c, s = tables(RET_HEAD_DIM)
    ret = (jnp.concatenate([c, c], axis=1), jnp.concatenate([-s, s], axis=1))
    c, s = tables(MLA_ROPE)
    z = jnp.zeros((lp, LANE - MLA_ROPE), F32)
    mla = (jnp.concatenate([c, c, z], axis=1), jnp.concatenate([-s, s, z], axis=1))
    return ret, mla


def kernel(x, meta_tokens, norm_mix, w_in, rw_mu, rw_w0, rw_w_up, rw_a0, rw_a_up, rw_g_up, rw_k_k, rw_k_a, rw_r_k, rw_ln_w, rw_ln_b, mla_norm_q, mla_norm_kv, mla_w_uq, mla_w_ukv, w_br_rwkv, w_br_ret, w_br_mla, w_out, norm_ffn, w_gate_up, w_down, final_norm):
    batch, seq, d = x.shape
    depth = w_in.shape[0]
    lp = -(-(N_META + seq) // SEQ_ALIGN) * SEQ_ALIGN
    tp = batch * lp

    meta = jnp.broadcast_to(meta_tokens[None].astype(x.dtype), (batch, N_META, d))
    pad = jnp.zeros((batch, lp - N_META - seq, d), x.dtype)
    h = jnp.concatenate([meta, x, pad], axis=1).reshape(tp, d)

    wp = _pack_w_in(w_in)
    mu = _pack_mu(rw_mu)
    w_up = _pad_rows(rw_w_up, LANE).astype(BF16)
    a_up = _pad_rows(rw_a_up, LANE).astype(BF16)
    g_up = rw_g_up.astype(BF16)
    wuq = _pack_w_uq(mla_w_uq)
    wukv = _pack_w_ukv(mla_w_ukv)
    wa = w_br_rwkv.astype(BF16)
    wb = w_br_ret.astype(BF16)
    wc = w_br_mla.astype(BF16)
    wo = w_out.astype(BF16)
    wgu = w_gate_up.astype(BF16)
    wdn = w_down.astype(BF16)
    (cos_ret, sin_ret), (cos_mla, sin_mla) = _rope_tables(lp)

    def row_tile(pref):
        return next((t for t in pref if tp % t == 0), SEQ_ALIGN)

    tm = row_tile((768,))
    tm_wide = row_tile((1536, 768))
    tm_seq = 384 if lp % 384 == 0 else SEQ_ALIGN

    for l in range(depth):
        p = norm_matmul(h, norm_mix[l], wp[l], tm_wide, P_TILE_N, F32)
        ya = rwkv_mix(p, batch, mu[l:l + 1], rw_w0[l], rw_a0[l], rw_k_k[l], rw_k_a[l], rw_r_k[l],
                      rw_ln_w[l], rw_ln_b[l], w_up[l], a_up[l], g_up[l])
        yb = retention_mix(p, batch, cos_ret, sin_ret)
        q, k, v = mla_proj(p, batch, mla_norm_q[l], mla_norm_kv[l], wuq[l], wukv[l], cos_mla, sin_mla, tm_seq)
        yc = mla_attention(q, k, v, batch, tm_seq)
        merged = merge_branches(ya, yb, yc, wa[l], wb[l], wc[l], p, tm, 1024)
        h = resid_matmul(merged, wo[l], h, tm_seq, d)
        act = ffn_up(h, norm_ffn[l], wgu[l], tm, 512)
        h = resid_matmul(act, wdn[l], h, tm, 512)
    out = rmsnorm(h, final_norm, tm, F32)
    return out.reshape(batch, lp, d)[:, N_META:N_META + seq]
```

```python
import functools
import math

import jax
import jax.numpy as jnp
from jax import lax
from jax.experimental import pallas as pl
from jax.experimental.pallas import tpu as pltpu

F32 = jnp.float32
BF16 = jnp.bfloat16

D_MODEL = 2048
N_META = 16
NORM_EPS = 1e-6
ROPE_BASE = 10000.0

RW_HEADS = 16
RW_HEAD_DIM = 64
RW_WIDTH = RW_HEADS * RW_HEAD_DIM
RW_DECAY_LORA = 96
RW_A_LORA = 96
RW_GATE_LORA = 256
RW_GN_EPS = RW_HEAD_DIM * 1e-5
RW_CHUNK = 64
RW_SUB = 2
RW_PAIRS = RW_WIDTH // 128

RET_HEADS = 8
RET_HEAD_DIM = 128
RET_WIDTH = RET_HEADS * RET_HEAD_DIM
RET_CHUNK = 128
RET_SUB = 3

MLA_HEADS = 8
MLA_NOPE = 128
MLA_ROPE = 64
MLA_V = 128
MLA_Q_RANK = 512
MLA_KV_RANK = 256
MLA_WIDTH = MLA_HEADS * MLA_V
MLA_QK_PAD = 256
ATTN_HEADS_PER_STEP = 2

FFN_HIDDEN = -(-8 * D_MODEL // (3 * 256)) * 256

LANE = 128
SEQ_ALIGN = 128

OFF_RET = 0
OFF_GATE = OFF_RET + 4 * RET_WIDTH
OFF_RW = OFF_GATE + 3 * D_MODEL
OFF_RW_WD = OFF_RW + 3 * RW_WIDTH
OFF_RW_AD = OFF_RW_WD + LANE
OFF_RW_GD = OFF_RW_AD + LANE
OFF_MLA_Q = OFF_RW_GD + RW_GATE_LORA
OFF_MLA_KV = OFF_MLA_Q + MLA_Q_RANK
OFF_MLA_KR = OFF_MLA_KV + MLA_KV_RANK
P_COLS_USED = OFF_MLA_KR + LANE
P_TILE_N = 512
P_COLS = -(-P_COLS_USED // P_TILE_N) * P_TILE_N

VMEM_LIMIT = 48 * 1024 * 1024


def _params(*sem):
    return pltpu.CompilerParams(dimension_semantics=sem, vmem_limit_bytes=VMEM_LIMIT)


def _sigmoid(x):
    return 1.0 / (1.0 + jnp.exp(-x))


def _dot(a, b):
    return jnp.dot(a, b, preferred_element_type=F32)


def _dot_nt(a, b):
    return lax.dot_general(a, b, (((1,), (1,)), ((), ())), preferred_element_type=F32)


def _dot_tn(a, b):
    return lax.dot_general(a, b, (((0,), (0,)), ((), ())), preferred_element_type=F32)


def _rmsnorm_body(x_ref, g_ref, o_ref):
    x = x_ref[...]
    y = x * lax.rsqrt(jnp.mean(x * x, axis=-1, keepdims=True) + NORM_EPS)
    o_ref[...] = (y * g_ref[...]).astype(o_ref.dtype)


def rmsnorm(x, g, tm, out_dtype):
    m, d = x.shape
    return pl.pallas_call(
        _rmsnorm_body,
        grid=(m // tm,),
        in_specs=[pl.BlockSpec((tm, d), lambda i: (i, 0)), pl.BlockSpec((1, d), lambda i: (0, 0))],
        out_specs=pl.BlockSpec((tm, d), lambda i: (i, 0)),
        out_shape=jax.ShapeDtypeStruct((m, d), out_dtype),
        compiler_params=_params("parallel"),
        name="rmsnorm",
    )(x, g.reshape(1, d))


def _row_scale(ssq_ref, d):
    ssq = ssq_ref[...]
    total = ssq[:, 0:1]
    for j in range(1, ssq.shape[1] // LANE):
        total = total + ssq[:, LANE * j:LANE * j + 1]
    return lax.rsqrt(total * (1.0 / d) + NORM_EPS)


def _ssq_block(x):
    return jnp.broadcast_to(jnp.sum(x * x, axis=-1, keepdims=True), (x.shape[0], LANE))


def _stream_prep_body(x_ref, xb_ref, ssq_ref):
    x = x_ref[...]
    xb_ref[...] = x.astype(xb_ref.dtype)
    ssq_ref[...] = _ssq_block(x)


def stream_prep(x, tm):
    m, d = x.shape
    return pl.pallas_call(
        _stream_prep_body,
        grid=(m // tm,),
        in_specs=[pl.BlockSpec((tm, d), lambda i: (i, 0))],
        out_specs=[pl.BlockSpec((tm, d), lambda i: (i, 0)), pl.BlockSpec((tm, LANE), lambda i: (i, 0))],
        out_shape=[jax.ShapeDtypeStruct((m, d), BF16), jax.ShapeDtypeStruct((m, LANE), F32)],
        compiler_params=_params("parallel"),
        name="stream_prep",
    )(x)


def _norm_matmul_body(x_ref, ssq_ref, w_ref, o_ref):
    o_ref[...] = (_row_scale(ssq_ref, x_ref.shape[1]) * _dot(x_ref[...], w_ref[...])).astype(o_ref.dtype)


def norm_matmul(xb, ssq, w, tm, tn, out_dtype):
    m, k = xb.shape
    n = w.shape[1]
    return pl.pallas_call(
        _norm_matmul_body,
        grid=(m // tm, n // tn),
        in_specs=[pl.BlockSpec((tm, k), lambda i, j: (i, 0)), pl.BlockSpec((tm, ssq.shape[1]), lambda i, j: (i, 0)),
                  pl.BlockSpec((k, tn), lambda i, j: (0, j))],
        out_specs=pl.BlockSpec((tm, tn), lambda i, j: (i, j)),
        out_shape=jax.ShapeDtypeStruct((m, n), out_dtype),
        compiler_params=_params("parallel", "parallel"),
        name="in_proj",
    )(xb, ssq, w)


def _rwkv_body(r_ref, k_ref, v_ref, wd_ref, ad_ref, gd_ref, mu_ref,
               w0_ref, a0_ref, kk_ref, ka_ref, rk_ref, lnw_ref, lnb_ref,
               wup_ref, aup_ref, gup_ref,
               o_ref,
               s_ref, pr_ref, pk_ref, pv_ref, pwd_ref, pad_ref, pgd_ref):
    c = pl.program_id(1)
    C = RW_CHUNK
    RS = RW_SUB * C
    HD = RW_HEAD_DIM

    @pl.when(c == 0)
    def _():
        s_ref[...] = jnp.zeros_like(s_ref)
        pr_ref[...] = jnp.zeros_like(pr_ref)
        pk_ref[...] = jnp.zeros_like(pk_ref)
        pv_ref[...] = jnp.zeros_like(pv_ref)
        pwd_ref[...] = jnp.zeros_like(pwd_ref)
        pad_ref[...] = jnp.zeros_like(pad_ref)
        pgd_ref[...] = jnp.zeros_like(pgd_ref)

    def shift(x_ref, prev_ref, mu, sl):
        z = x_ref[:, sl]
        first = lax.broadcasted_iota(jnp.int32, z.shape, 0) == 0
        zs = jnp.where(first, prev_ref[0:1, sl], pltpu.roll(z, 1, 0))
        prev_ref[0:1, sl] = z[RS - 1:RS, :]
        return z + (zs - z) * mu

    mu_lora = 3 * RW_WIDTH
    full = slice(None)
    wd = shift(wd_ref, pwd_ref, mu_ref[:, mu_lora:mu_lora + LANE], full)
    ad = shift(ad_ref, pad_ref, mu_ref[:, mu_lora + LANE:mu_lora + 2 * LANE], full)
    gd = shift(gd_ref, pgd_ref, mu_ref[:, mu_lora + 2 * LANE:], full)
    subs = range(RW_SUB)

    def rows(t, s):
        return t[C * s:C * (s + 1)]

    tanh_wd = [rows(jnp.tanh(wd), s).astype(BF16) for s in subs]
    ad_b = [rows(ad, s).astype(BF16) for s in subs]
    sig_gd = [rows(_sigmoid(gd), s).astype(BF16) for s in subs]

    lane_sq = lax.broadcasted_iota(jnp.int32, (LANE, LANE), 1)
    row_sq = lax.broadcasted_iota(jnp.int32, (LANE, LANE), 0)
    same_head = (lane_sq < HD) == (row_sq < HD)
    head_ones = same_head.astype(BF16)
    eye = (lane_sq == row_sq).astype(F32)
    rc = lax.broadcasted_iota(jnp.int32, (C, C), 0)
    cc = lax.broadcasted_iota(jnp.int32, (C, C), 1)
    tril_incl = (cc <= rc).astype(BF16)
    lane_tall = lax.broadcasted_iota(jnp.int32, (2 * C, LANE), 1)
    lane_c = lax.broadcasted_iota(jnp.int32, (C, LANE), 1)
    row_c = lax.broadcasted_iota(jnp.int32, (C, LANE), 0)
    head0_c = lane_c < HD
    strict_lo = lane_c < row_c
    strict_hi = (lane_c >= C) & (lane_c - C < row_c)
    incl_lo = lane_c <= row_c
    incl_hi = (lane_c >= C) & (lane_c - C <= row_c)
    inv_n = 1.0 / HD

    P = range(RW_PAIRS)

    def head_sums(ts):
        his = [t.astype(BF16) for t in ts]
        los = [(t - hi.astype(F32)).astype(BF16) for t, hi in zip(ts, his)]
        out = _dot(jnp.concatenate(his + los, axis=0), head_ones)
        n = len(ts)
        return [out[C * i:C * (i + 1)] + out[C * (n + i):C * (n + i + 1)] for i in range(n)]

    pair_sls = [slice(LANE * i, LANE * (i + 1)) for i in P]

    def each(fn, *lists):
        return [fn(*args) for args in zip(*lists)]

    def items(per_pair):
        return [rows(t, s) for s in subs for t in per_pair]

    r = items([shift(r_ref, pr_ref, mu_ref[:, sl], sl) for sl in pair_sls])
    k = items([shift(k_ref, pk_ref, mu_ref[:, RW_WIDTH + sl.start:RW_WIDTH + sl.stop], sl) for sl in pair_sls])
    v = items([shift(v_ref, pv_ref, mu_ref[:, 2 * RW_WIDTH + sl.start:2 * RW_WIDTH + sl.stop], sl)
               for sl in pair_sls])
    sls = pair_sls * RW_SUB
    sub_of = [s for s in subs for _ in P]

    def log_decay(s, sl):
        x = -(w0_ref[:, sl] + _dot(tanh_wd[s], wup_ref[:, sl]))
        softplus = jnp.maximum(x, 0.0) + jnp.log1p(jnp.exp(-jnp.abs(x)))
        return -jnp.exp(-softplus - 0.5)

    lw = each(log_decay, sub_of, sls)
    a = each(lambda s, sl: _sigmoid(a0_ref[:, sl] + _dot(ad_b[s], aup_ref[:, sl])), sub_of, sls)
    g = each(lambda s, sl: _dot(sig_gd[s], gup_ref[:, sl]), sub_of, sls)

    kkr = each(lambda ki, sl: ki * kk_ref[:, sl], k, sls)
    ksq = head_sums(each(lambda t: t * t, kkr))
    kkn = each(lambda t, ss: t / jnp.maximum(jnp.sqrt(ss), 1e-12), kkr, ksq)
    kmod = each(lambda ki, ai, sl: ki * (1.0 + (ai - 1.0) * ka_ref[:, sl]), k, a, sls)
    beta = each(lambda ai, t: ai * t, a, kkn)

    def running_sum(lwi):
        hi = lwi.astype(BF16)
        both = _dot(tril_incl, jnp.concatenate([hi, (lwi - hi.astype(F32)).astype(BF16)], axis=1))
        return both[:, :LANE] + both[:, LANE:]

    lcum = each(running_sum, lw)
    lend = each(lambda t: t[C - 1:C, :], lcum)
    rh = each(lambda ri, lc: ri * jnp.exp(lc), r, lcum)
    kh = each(lambda t, lc, lwi: t * jnp.exp(lc - lwi), kkn, lcum, lw)
    e_neg = each(lambda lc: jnp.exp(-lc), lcum)
    e_end = each(lambda le, lc: jnp.exp(le - lc), lend, lcum)
    kb = each(lambda t, e: t * e, kmod, e_neg)
    bb = each(lambda t, e: t * e, beta, e_neg)
    kbe = each(lambda t, e: t * e, kmod, e_end)
    bbe = each(lambda t, e: t * e, beta, e_end)

    kr_f = each(lambda x1, x2: jnp.concatenate([x1, x2], axis=0), kh, rh)
    bk = each(lambda x1, x2: jnp.concatenate([x1, x2], axis=0).astype(BF16), bb, kb)
    vb = each(lambda t: t.astype(BF16), v)

    def gram(krf, bki):
        kr2 = jnp.concatenate([jnp.where(lane_tall < HD, krf, 0.0), jnp.where(lane_tall >= HD, krf, 0.0)], axis=0)
        return _dot_nt(kr2.astype(BF16), bki)

    g_all = each(gram, kr_f, bk)

    n_bd = each(lambda ga: jnp.concatenate([jnp.where(strict_lo, -ga[0:C], 0.0),
                                            jnp.where(strict_hi, -pltpu.roll(ga[2 * C:3 * C], C, 1), 0.0)], axis=0),
                g_all)
    t = each(lambda n: eye + n, n_bd)
    pw = each(lambda n: _dot(n.astype(BF16), n.astype(BF16)), n_bd)
    for _ in range(4):
        both = each(lambda ti, pi: _dot(jnp.concatenate([ti, pi], axis=0).astype(BF16), pi.astype(BF16)), t, pw)
        t = each(lambda ti, bi: ti + bi[:LANE], t, both)
        pw = each(lambda bi: bi[LANE:], both)
    t = each(lambda ti, pi: ti + _dot(ti.astype(BF16), pi.astype(BF16)), t, pw)

    def intra_rhs(ga, vbi):
        m1s = jnp.concatenate([jnp.where(strict_hi, ga[0:C], 0.0), jnp.where(strict_hi, ga[2 * C:3 * C], 0.0)], axis=0)
        return _dot(m1s.astype(BF16), jnp.concatenate([vbi, vbi], axis=0))

    q_intra = each(intra_rhs, g_all, vb)

    def m2(gb):
        return jnp.where(incl_lo, -gb, jnp.where(incl_hi, gb, 0.0))

    m2s = each(lambda ga: jnp.concatenate([m2(ga[C:2 * C]), m2(ga[3 * C:4 * C])], axis=0).astype(BF16), g_all)
    kbe_all = each(lambda kbei, bbei: jnp.concatenate([kbei, -bbei], axis=0).astype(BF16), kbe, bbe)
    s_decay = each(jnp.exp, lend)

    state = [s_ref[i] for i in P]
    y = []
    for s in subs:
        of = lambda lst: lst[RW_PAIRS * s:RW_PAIRS * (s + 1)]
        p_all = each(lambda x1, si: _dot_nt(x1.astype(BF16), si.astype(BF16)), of(kr_f), state)
        q_s = each(lambda qi, pa: jnp.where(same_head, qi + jnp.concatenate([pa[:C], pa[:C]], axis=0), 0.0),
                   of(q_intra), p_all)
        u_s = each(lambda ti, qi: _dot(ti.astype(BF16), qi.astype(BF16)), of(t), q_s)
        u = each(lambda us: us[:C] + us[C:], u_s)
        y_s = each(lambda mi, ui, vi: _dot(mi, jnp.concatenate([ui, vi], axis=0).astype(BF16)), of(m2s), u, of(v))
        y += each(lambda pa, ys: pa[C:] + jnp.where(head0_c, ys[:C], ys[C:]), p_all, y_s)
        ds = each(lambda vi, ui, kb_all: _dot_tn(jnp.concatenate([vi, ui], axis=0).astype(BF16), kb_all),
                  of(v), u, of(kbe_all))
        state = each(lambda si, di, dec: si * dec + jnp.where(same_head, di, 0.0), state, ds, of(s_decay))
    for i in P:
        s_ref[i] = state[i]

    d = each(lambda yi, si: yi - si * inv_n, y, head_sums(y))
    var = each(lambda si: si * inv_n, head_sums(each(lambda di: di * di, d)))
    bsum = head_sums(each(lambda ri, ki, sl: ri * ki * rk_ref[:, sl], r, kmod, sls))
    for i, (s, sl) in enumerate(zip(sub_of, sls)):
        yn = d[i] * lax.rsqrt(var[i] + RW_GN_EPS) * lnw_ref[:, sl] + lnb_ref[:, sl]
        o_ref[C * s:C * (s + 1), sl] = ((yn + bsum[i] * v[i]) * g[i]).astype(o_ref.dtype)


def rwkv_mix(p, batch, mu, w0, a0, k_k, k_a, r_k, ln_w, ln_b, w_up, a_up, g_up):
    tp = p.shape[0]
    lp = tp // batch
    C = RW_CHUNK * RW_SUB
    nchunk = lp // C

    def pspec(width, base):
        return pl.BlockSpec((C, width), lambda b, c: (b * nchunk + c, base // width))

    def const(shape):
        return pl.BlockSpec(shape, lambda b, c: (0, 0))

    in_specs = [
        pspec(RW_WIDTH, OFF_RW), pspec(RW_WIDTH, OFF_RW + RW_WIDTH), pspec(RW_WIDTH, OFF_RW + 2 * RW_WIDTH),
        pspec(LANE, OFF_RW_WD), pspec(LANE, OFF_RW_AD), pspec(RW_GATE_LORA, OFF_RW_GD),
        const(mu.shape),
    ] + [const((1, RW_WIDTH))] * 7 + [const(w_up.shape), const(a_up.shape), const(g_up.shape)]
    row = lambda t: t.reshape(1, -1)
    return pl.pallas_call(
        _rwkv_body,
        grid=(batch, nchunk),
        in_specs=in_specs,
        out_specs=pl.BlockSpec((C, RW_WIDTH), lambda b, c: (b * nchunk + c, 0)),
        out_shape=jax.ShapeDtypeStruct((tp, RW_WIDTH), BF16),
        scratch_shapes=[pltpu.VMEM((RW_PAIRS, LANE, LANE), F32)] + [pltpu.VMEM((8, RW_WIDTH), F32)] * 3
        + [pltpu.VMEM((8, LANE), F32)] * 2 + [pltpu.VMEM((8, RW_GATE_LORA), F32)],
        compiler_params=_params("parallel", "arbitrary"),
        name="rwkv7_mix",
    )(p, p, p, p, p, p, mu, row(w0), row(a0), row(k_k), row(k_a), row(r_k), row(ln_w), row(ln_b),
      w_up, a_up, g_up)


def _ret_body(q_ref, k_ref, v_ref, g_ref, cos_ref, sin_ref, o_ref, state_ref):
    c = pl.program_id(1)
    C = RET_CHUNK
    d = RET_HEAD_DIM

    @pl.when(c == 0)
    def _():
        state_ref[...] = jnp.zeros_like(state_ref)

    row = lax.broadcasted_iota(jnp.int32, (C, C), 0).astype(F32)
    col = lax.broadcasted_iota(jnp.int32, (C, C), 1).astype(F32)
    diff = row - col
    causal = diff >= 0
    heads = range(RET_HEADS)
    subs = range(RET_SUB)
    rws = [slice(C * s, C * (s + 1)) for s in subs for _ in heads]
    sls = [slice(d * h, d * (h + 1)) for _ in subs for h in heads]
    lgs = [math.log1p(-(2.0 ** (-5.0 - h))) for _ in subs for h in heads]

    def each(fn, *lists):
        return [fn(*args) for args in zip(*lists)]

    def rope(x_ref, rw, sl):
        x = x_ref[rw, sl]
        return x * cos_ref[rw, :] + pltpu.roll(x, d // 2, 1) * sin_ref[rw, :]

    qb = each(lambda rw, sl: rope(q_ref, rw, sl).astype(BF16), rws, sls)
    k = each(lambda rw, sl: rope(k_ref, rw, sl) * (d ** -0.5), rws, sls)
    kb = each(lambda t: t.astype(BF16), k)
    vb = each(lambda rw, sl: v_ref[rw, sl].astype(BF16), rws, sls)
    s = each(lambda qi, ki, lg: _dot_nt(qi, ki) * jnp.where(causal, jnp.exp(lg * jnp.maximum(diff, 0.0)), 0.0),
             qb, kb, lgs)
    o_intra = each(lambda si, vi: _dot(si.astype(BF16), vi), s, vb)
    kd = each(lambda ki, lg: (ki * jnp.exp(lg * (C - 1.0 - row))).astype(BF16), k, lgs)
    kv = each(_dot_tn, kd, vb)

    state = [state_ref[h] for h in heads]
    o = []
    for sub in subs:
        of = lambda lst: lst[RET_HEADS * sub:RET_HEADS * (sub + 1)]
        o += each(lambda oi, qi, st, lg: oi + _dot(qi, st.astype(BF16)) * jnp.exp(lg * (row + 1.0)),
                  of(o_intra), of(qb), state, of(lgs))
        state = each(lambda st, kvi, lg: st * math.exp(lg * C) + kvi, state, of(kv), of(lgs))
    for h in heads:
        state_ref[h] = state[h]
    o = each(lambda oi: oi * lax.rsqrt(jnp.mean(oi * oi, axis=-1, keepdims=True) + NORM_EPS), o)
    for oi, rw, sl in zip(o, rws, sls):
        g = g_ref[rw, sl]
        o_ref[rw, sl] = (g * _sigmoid(g) * oi).astype(o_ref.dtype)


def retention_mix(p, batch, cos, sin):
    tp = p.shape[0]
    lp = tp // batch
    C = RET_CHUNK * RET_SUB
    nchunk = lp // C
    base = OFF_RET // RET_WIDTH

    def pspec(j):
        return pl.BlockSpec((C, RET_WIDTH), lambda b, c: (b * nchunk + c, base + j))

    tab = pl.BlockSpec((C, RET_HEAD_DIM), lambda b, c: (c, 0))
    return pl.pallas_call(
        _ret_body,
        grid=(batch, nchunk),
        in_specs=[pspec(0), pspec(1), pspec(2), pspec(3), tab, tab],
        out_specs=pl.BlockSpec((C, RET_WIDTH), lambda b, c: (b * nchunk + c, 0)),
        out_shape=jax.ShapeDtypeStruct((tp, RET_WIDTH), BF16),
        scratch_shapes=[pltpu.VMEM((RET_HEADS, RET_HEAD_DIM, RET_HEAD_DIM), F32)],
        compiler_params=_params("parallel", "arbitrary"),
        name="retention_mix",
    )(p, p, p, p, cos, sin)


def _mla_proj_body(qd_ref, kvd_ref, krd_ref, nq_ref, nkv_ref, wuq_ref, wukv_ref, cos_ref, sin_ref,
                   q_out, k_out, v_out):
    cos = cos_ref[...]
    sin = sin_ref[...]

    def rope(x):
        return x * cos + (pltpu.roll(x, MLA_ROPE // 2, 1) + pltpu.roll(x, LANE - MLA_ROPE // 2, 1)) * sin

    def norm(x, g):
        return x * lax.rsqrt(jnp.mean(x * x, axis=-1, keepdims=True) + NORM_EPS) * g

    scale = (MLA_NOPE + MLA_ROPE) ** -0.5 * math.log2(math.e)
    q = _dot(norm(qd_ref[...], nq_ref[...]).astype(BF16), wuq_ref[...]) * scale
    kv = _dot(norm(kvd_ref[...], nkv_ref[...]).astype(BF16), wukv_ref[...])
    kr = rope(krd_ref[...]).astype(k_out.dtype)
    for h in range(MLA_HEADS):
        lo = MLA_QK_PAD * h
        q_out[:, lo:lo + LANE] = q[:, lo:lo + LANE].astype(q_out.dtype)
        q_out[:, lo + LANE:lo + 2 * LANE] = rope(q[:, lo + LANE:lo + 2 * LANE]).astype(q_out.dtype)
        k_out[:, lo:lo + LANE] = kv[:, MLA_NOPE * h:MLA_NOPE * (h + 1)].astype(k_out.dtype)
        k_out[:, lo + LANE:lo + 2 * LANE] = kr
    v_out[...] = kv[:, MLA_HEADS * MLA_NOPE:].astype(v_out.dtype)


def mla_proj(p, batch, norm_q, norm_kv, w_uq, w_ukv, cos, sin, tm):
    tp = p.shape[0]
    lp = tp // batch
    per_seq = lp // tm
    qk_w = MLA_HEADS * MLA_QK_PAD
    const = lambda i: (0, 0)
    return pl.pallas_call(
        _mla_proj_body,
        grid=(tp // tm,),
        in_specs=[
            pl.BlockSpec((tm, MLA_Q_RANK), lambda i: (i, OFF_MLA_Q // MLA_Q_RANK)),
            pl.BlockSpec((tm, MLA_KV_RANK), lambda i: (i, OFF_MLA_KV // MLA_KV_RANK)),
            pl.BlockSpec((tm, LANE), lambda i: (i, OFF_MLA_KR // LANE)),
            pl.BlockSpec((1, MLA_Q_RANK), const),
            pl.BlockSpec((1, MLA_KV_RANK), const),
            pl.BlockSpec((MLA_Q_RANK, qk_w), const),
            pl.BlockSpec((MLA_KV_RANK, MLA_HEADS * (MLA_NOPE + MLA_V)), const),
            pl.BlockSpec((tm, LANE), lambda i: (i % per_seq, 0)),
            pl.BlockSpec((tm, LANE), lambda i: (i % per_seq, 0)),
        ],
        out_specs=[
            pl.BlockSpec((tm, qk_w), lambda i: (i, 0)),
            pl.BlockSpec((tm, qk_w), lambda i: (i, 0)),
            pl.BlockSpec((tm, MLA_WIDTH), lambda i: (i, 0)),
        ],
        out_shape=[
            jax.ShapeDtypeStruct((tp, qk_w), BF16),
            jax.ShapeDtypeStruct((tp, qk_w), BF16),
            jax.ShapeDtypeStruct((tp, MLA_WIDTH), BF16),
        ],
        compiler_params=_params("parallel"),
        name="mla_proj",
    )(p, p, p, norm_q.reshape(1, -1), norm_kv.reshape(1, -1), w_uq, w_ukv, cos, sin)


def _attn_body(q_ref, k_ref, v_ref, o_ref, *, tq):
    i = pl.program_id(2)
    heads = range(ATTN_HEADS_PER_STEP)
    qs = [q_ref[:, MLA_QK_PAD * h:MLA_QK_PAD * (h + 1)] for h in heads]

    def step(off, width, masked, carry):
        ms, ls, accs = carry
        off = pl.multiple_of(off, tq)
        ss = [_dot_nt(qs[h], k_ref[pl.ds(off, width), MLA_QK_PAD * h:MLA_QK_PAD * (h + 1)]) for h in heads]
        if masked:
            row = i * tq + lax.broadcasted_iota(jnp.int32, (tq, width), 0)
            col = off + lax.broadcasted_iota(jnp.int32, (tq, width), 1)
            ss = [jnp.where(col <= row, s, -jnp.inf) for s in ss]
        m_new = [jnp.maximum(ms[h], jnp.max(ss[h], axis=-1, keepdims=True)) for h in heads]
        alpha = [jnp.exp2(ms[h] - m_new[h]) for h in heads]
        ps = [jnp.exp2(ss[h] - m_new[h]) for h in heads]
        ls = [alpha[h] * ls[h] + jnp.sum(ps[h], axis=-1, keepdims=True) for h in heads]
        pv = [_dot(ps[h].astype(BF16), v_ref[pl.ds(off, width), MLA_V * h:MLA_V * (h + 1)]) for h in heads]
        accs = [alpha[h] * accs[h] + pv[h] for h in heads]
        return tuple(m_new), tuple(ls), tuple(accs)

    carry = (tuple(jnp.full((tq, 1), -1e30, F32) for _ in heads),
             tuple(jnp.zeros((tq, 1), F32) for _ in heads),
             tuple(jnp.zeros((tq, MLA_V), F32) for _ in heads))
    n_single = (i + 1) % 2
    n_pairs = (i + 1) // 2
    carry = lax.cond(i == 0, lambda c: step(0, tq, True, c), lambda c: c, carry)
    carry = lax.cond((n_single == 1) & (i > 0), lambda c: step(0, tq, False, c), lambda c: c, carry)
    pair_off = lambda p: (n_single + 2 * p) * tq
    carry = lax.fori_loop(0, n_pairs - 1, lambda p, c: step(pair_off(p), 2 * tq, False, c), carry)
    carry = lax.cond(n_pairs > 0, lambda c: step(pair_off(n_pairs - 1), 2 * tq, True, c), lambda c: c, carry)
    _, ls, accs = carry
    for h in heads:
        o_ref[:, MLA_V * h:MLA_V * (h + 1)] = (accs[h] / ls[h]).astype(o_ref.dtype)


def mla_attention(q, k, v, batch, tq):
    tp = q.shape[0]
    lp = tp // batch
    nq = lp // tq
    hs = ATTN_HEADS_PER_STEP
    return pl.pallas_call(
        functools.partial(_attn_body, tq=tq),
        grid=(batch, MLA_HEADS // hs, nq),
        in_specs=[
            pl.BlockSpec((tq, hs * MLA_QK_PAD), lambda b, h, i: (b * nq + i, h)),
            pl.BlockSpec((lp, hs * MLA_QK_PAD), lambda b, h, i: (b, h)),
            pl.BlockSpec((lp, hs * MLA_V), lambda b, h, i: (b, h)),
        ],
        out_specs=pl.BlockSpec((tq, hs * MLA_V), lambda b, h, i: (b * nq + i, h)),
        out_shape=jax.ShapeDtypeStruct((tp, MLA_WIDTH), BF16),
        compiler_params=_params("parallel", "parallel", "arbitrary"),
        name="mla_attention",
    )(q, k, v)


def _merge_body(ya_ref, yb_ref, yc_ref, wa_ref, wb_ref, wc_ref, ga_ref, gb_ref, gc_ref, o_ref):
    def branch(y_ref, w_ref, g_ref):
        return _sigmoid(g_ref[...]) * _dot(y_ref[...], w_ref[...])

    o_ref[...] = (branch(ya_ref, wa_ref, ga_ref) + branch(yb_ref, wb_ref, gb_ref)
                  + branch(yc_ref, wc_ref, gc_ref)).astype(o_ref.dtype)


def merge_branches(ya, yb, yc, wa, wb, wc, p, tm, tn):
    tp = ya.shape[0]

    def yspec(width):
        return pl.BlockSpec((tm, width), lambda i, j: (i, 0))

    def wspec(width):
        return pl.BlockSpec((width, tn), lambda i, j: (0, j))

    def gspec(branch):
        base = (OFF_GATE + branch * D_MODEL) // tn
        return pl.BlockSpec((tm, tn), lambda i, j: (i, base + j))

    return pl.pallas_call(
        _merge_body,
        grid=(tp // tm, D_MODEL // tn),
        in_specs=[yspec(RW_WIDTH), yspec(RET_WIDTH), yspec(MLA_WIDTH),
                  wspec(RW_WIDTH), wspec(RET_WIDTH), wspec(MLA_WIDTH),
                  gspec(0), gspec(1), gspec(2)],
        out_specs=pl.BlockSpec((tm, tn), lambda i, j: (i, j)),
        out_shape=jax.ShapeDtypeStruct((tp, D_MODEL), BF16),
        compiler_params=_params("parallel", "parallel"),
        name="merge_branches",
    )(ya, yb, yc, wa, wb, wc, p, p, p)


def _resid_body(x_ref, w_ref, h_ref, o_ref, ob_ref, ssq_ref):
    hn = h_ref[...] + _dot(x_ref[...], w_ref[...])
    o_ref[...] = hn
    ob_ref[...] = hn.astype(ob_ref.dtype)
    ssq_ref[...] = _ssq_block(hn)


def resid_matmul(x, w, h, tm, tn):
    m, kdim = x.shape
    n = w.shape[1]
    tile = pl.BlockSpec((tm, tn), lambda i, j: (i, j))
    return pl.pallas_call(
        _resid_body,
        grid=(m // tm, n // tn),
        in_specs=[
            pl.BlockSpec((tm, kdim), lambda i, j: (i, 0)),
            pl.BlockSpec((kdim, tn), lambda i, j: (0, j)),
            tile,
        ],
        out_specs=[tile, tile, pl.BlockSpec((tm, LANE), lambda i, j: (i, j))],
        out_shape=[jax.ShapeDtypeStruct((m, n), F32), jax.ShapeDtypeStruct((m, n), BF16),
                   jax.ShapeDtypeStruct((m, LANE * (n // tn)), F32)],
        compiler_params=_params("parallel", "parallel"),
        name="resid_matmul",
    )(x, w, h)


def _ffn_up_body(x_ref, ssq_ref, wg_ref, wu_ref, o_ref):
    x = x_ref[...]
    r = _row_scale(ssq_ref, x.shape[1])
    hg = r * _dot(x, wg_ref[...])
    hu = r * _dot(x, wu_ref[...])
    o_ref[...] = (hg * _sigmoid(hg) * hu).astype(o_ref.dtype)


def ffn_up(xb, ssq, w_gate_up, tm, tn):
    m, kdim = xb.shape
    hidden = w_gate_up.shape[1] // 2
    nj = hidden // tn
    return pl.pallas_call(
        _ffn_up_body,
        grid=(m // tm, nj),
        in_specs=[
            pl.BlockSpec((tm, kdim), lambda i, j: (i, 0)),
            pl.BlockSpec((tm, ssq.shape[1]), lambda i, j: (i, 0)),
            pl.BlockSpec((kdim, tn), lambda i, j: (0, j)),
            pl.BlockSpec((kdim, tn), lambda i, j: (0, nj + j)),
        ],
        out_specs=pl.BlockSpec((tm, tn), lambda i, j: (i, j)),
        out_shape=jax.ShapeDtypeStruct((m, hidden), BF16),
        compiler_params=_params("parallel", "parallel"),
        name="ffn_up",
    )(xb, ssq, w_gate_up, w_gate_up)


def _pad_cols(w, width):
    return jnp.pad(w, [(0, 0)] * (w.ndim - 1) + [(0, width - w.shape[-1])])


def _pack_w_in(w_in, gain):
    w_in = w_in * gain[..., None]
    rw_cols = 3 * RW_WIDTH + RW_DECAY_LORA + RW_A_LORA + RW_GATE_LORA
    ret_cols = 4 * RET_WIDTH
    o = 0
    rw = w_in[..., o:o + rw_cols]
    o += rw_cols
    ret = w_in[..., o:o + ret_cols]
    o += ret_cols
    qd = w_in[..., o:o + MLA_Q_RANK]
    o += MLA_Q_RANK
    kvd = w_in[..., o:o + MLA_KV_RANK]
    o += MLA_KV_RANK
    krd = w_in[..., o:o + MLA_ROPE]
    o += MLA_ROPE
    gates = w_in[..., o:]
    rkv = rw[..., :3 * RW_WIDTH]
    wd = rw[..., 3 * RW_WIDTH:3 * RW_WIDTH + RW_DECAY_LORA]
    ad = rw[..., 3 * RW_WIDTH + RW_DECAY_LORA:3 * RW_WIDTH + RW_DECAY_LORA + RW_A_LORA]
    gd = rw[..., 3 * RW_WIDTH + RW_DECAY_LORA + RW_A_LORA:]
    def zeros(n):
        return jnp.zeros(w_in.shape[:-1] + (n,), BF16)

    parts = [ret, gates, rkv, wd, zeros(LANE - RW_DECAY_LORA), ad, zeros(LANE - RW_A_LORA), gd, qd, kvd, krd,
             zeros(LANE - MLA_ROPE), zeros(P_COLS - P_COLS_USED)]
    return jnp.concatenate([t.astype(BF16) for t in parts], axis=-1)


def _pack_mu(mu):
    rkv = mu[..., :3 * RW_WIDTH]
    wd = mu[..., 3 * RW_WIDTH:3 * RW_WIDTH + RW_DECAY_LORA]
    ad = mu[..., 3 * RW_WIDTH + RW_DECAY_LORA:3 * RW_WIDTH + RW_DECAY_LORA + RW_A_LORA]
    gd = mu[..., 3 * RW_WIDTH + RW_DECAY_LORA + RW_A_LORA:]
    return jnp.concatenate([rkv, _pad_cols(wd, LANE), _pad_cols(ad, LANE), gd], axis=-1)


def _pad_rows(w, rows):
    return jnp.pad(w, [(0, 0)] * (w.ndim - 2) + [(0, rows - w.shape[-2]), (0, 0)])


def _pack_w_uq(w):
    nl, rank, _ = w.shape
    w = w.reshape(nl, rank, MLA_HEADS, MLA_NOPE + MLA_ROPE)
    w = jnp.pad(w, ((0, 0), (0, 0), (0, 0), (0, MLA_QK_PAD - MLA_NOPE - MLA_ROPE)))
    return w.reshape(nl, rank, MLA_HEADS * MLA_QK_PAD).astype(BF16)


def _pack_w_ukv(w):
    nl, rank, _ = w.shape
    w = w.reshape(nl, rank, MLA_HEADS, 2, MLA_NOPE)
    w = jnp.swapaxes(w, 2, 3)
    return w.reshape(nl, rank, 2 * MLA_HEADS * MLA_NOPE).astype(BF16)


def _rope_tables(lp):
    pos = jnp.arange(lp, dtype=F32)

    def tables(dim):
        inv = ROPE_BASE ** (-jnp.arange(0, dim, 2, dtype=F32) / dim)
        ang = pos[:, None] * inv[None, :]
        return jnp.cos(ang), jnp.sin(ang)

    c, s = tables(RET_HEAD_DIM)
    ret = (jnp.concatenate([c, c], axis=1), jnp.concatenate([-s, s], axis=1))
    c, s = tables(MLA_ROPE)
    z = jnp.zeros((lp, LANE - MLA_ROPE), F32)
    mla = (jnp.concatenate([c, c, z], axis=1), jnp.concatenate([-s, s, z], axis=1))
    return ret, mla


def kernel(x, meta_tokens, norm_mix, w_in, rw_mu, rw_w0, rw_w_up, rw_a0, rw_a_up, rw_g_up, rw_k_k, rw_k_a, rw_r_k, rw_ln_w, rw_ln_b, mla_norm_q, mla_norm_kv, mla_w_uq, mla_w_ukv, w_br_rwkv, w_br_ret, w_br_mla, w_out, norm_ffn, w_gate_up, w_down, final_norm):
    batch, seq, d = x.shape
    depth = w_in.shape[0]
    lp = -(-(N_META + seq) // SEQ_ALIGN) * SEQ_ALIGN
    tp = batch * lp

    meta = jnp.broadcast_to(meta_tokens[None].astype(x.dtype), (batch, N_META, d))
    pad = jnp.zeros((batch, lp - N_META - seq, d), x.dtype)
    h = jnp.concatenate([meta, x, pad], axis=1).reshape(tp, d)

    wp = _pack_w_in(w_in, norm_mix)
    mu = _pack_mu(rw_mu)
    w_up = _pad_rows(rw_w_up, LANE).astype(BF16)
    a_up = _pad_rows(rw_a_up, LANE).astype(BF16)
    g_up = rw_g_up.astype(BF16)
    wuq = _pack_w_uq(mla_w_uq)
    wukv = _pack_w_ukv(mla_w_ukv)
    wa = w_br_rwkv.astype(BF16)
    wb = w_br_ret.astype(BF16)
    wc = w_br_mla.astype(BF16)
    wo = w_out.astype(BF16)
    wgu = (w_gate_up * norm_ffn[..., None]).astype(BF16)
    wdn = w_down.astype(BF16)
    (cos_ret, sin_ret), (cos_mla, sin_mla) = _rope_tables(lp)

    def row_tile(pref):
        return next((t for t in pref if tp % t == 0), SEQ_ALIGN)

    tm = row_tile((768,))
    tm_wide = row_tile((1536, 768))
    tm_seq = 384 if lp % 384 == 0 else SEQ_ALIGN

    hb, ssq = stream_prep(h, tm)
    for l in range(depth):
        p = norm_matmul(hb, ssq, wp[l], tm_wide, P_TILE_N, F32)
        ya = rwkv_mix(p, batch, mu[l:l + 1], rw_w0[l], rw_a0[l], rw_k_k[l], rw_k_a[l], rw_r_k[l],
                      rw_ln_w[l], rw_ln_b[l], w_up[l], a_up[l], g_up[l])
        yb = retention_mix(p, batch, cos_ret, sin_ret)
        q, k, v = mla_proj(p, batch, mla_norm_q[l], mla_norm_kv[l], wuq[l], wukv[l], cos_mla, sin_mla, tm_seq)
        yc = mla_attention(q, k, v, batch, tm_seq)
        merged = merge_branches(ya, yb, yc, wa[l], wb[l], wc[l], p, tm, 1024)
        h, hb, ssq = resid_matmul(merged, wo[l], h, tm_seq, d)
        act = ffn_up(hb, ssq, wgu[l], tm, 512)
        h, hb, ssq = resid_matmul(act, wdn[l], h, tm, 512)
    out = rmsnorm(h, final_norm, tm, F32)
    return out.reshape(batch, lp, d)[:, N_META:N_META + seq]
```

```python
import functools
import math

import jax
import jax.numpy as jnp
from jax import lax
from jax.experimental import pallas as pl
from jax.experimental.pallas import tpu as pltpu

F32 = jnp.float32
BF16 = jnp.bfloat16

D_MODEL = 2048
N_META = 16
NORM_EPS = 1e-6
ROPE_BASE = 10000.0

RW_HEADS = 16
RW_HEAD_DIM = 64
RW_WIDTH = RW_HEADS * RW_HEAD_DIM
RW_DECAY_LORA = 96
RW_A_LORA = 96
RW_GATE_LORA = 256
RW_GN_EPS = RW_HEAD_DIM * 1e-5
RW_CHUNK = 64
RW_SUB = 2
RW_PAIRS = RW_WIDTH // 128

RET_HEADS = 8
RET_HEAD_DIM = 128
RET_WIDTH = RET_HEADS * RET_HEAD_DIM
RET_CHUNK = 128
RET_SUB = 3

MLA_HEADS = 8
MLA_NOPE = 128
MLA_ROPE = 64
MLA_V = 128
MLA_Q_RANK = 512
MLA_KV_RANK = 256
MLA_WIDTH = MLA_HEADS * MLA_V
MLA_QK_PAD = 256
ATTN_HEADS_PER_STEP = 2

FFN_HIDDEN = -(-8 * D_MODEL // (3 * 256)) * 256

LANE = 128
SEQ_ALIGN = 128

OFF_RET = 0
OFF_GATE = OFF_RET + 4 * RET_WIDTH
OFF_RW = OFF_GATE + 3 * D_MODEL
OFF_RW_WD = OFF_RW + 3 * RW_WIDTH
OFF_RW_AD = OFF_RW_WD + LANE
OFF_RW_GD = OFF_RW_AD + LANE
OFF_MLA_Q = OFF_RW_GD + RW_GATE_LORA
OFF_MLA_KV = OFF_MLA_Q + MLA_Q_RANK
OFF_MLA_KR = OFF_MLA_KV + MLA_KV_RANK
P_COLS_USED = OFF_MLA_KR + LANE
P_TILE_N = 512
P_COLS = -(-P_COLS_USED // P_TILE_N) * P_TILE_N

VMEM_LIMIT = 48 * 1024 * 1024


def _params(*sem):
    return pltpu.CompilerParams(dimension_semantics=sem, vmem_limit_bytes=VMEM_LIMIT)


def _sigmoid(x):
    return 1.0 / (1.0 + jnp.exp(-x))


def _dot(a, b):
    return jnp.dot(a, b, preferred_element_type=F32)


def _dot_nt(a, b):
    return lax.dot_general(a, b, (((1,), (1,)), ((), ())), preferred_element_type=F32)


def _dot_tn(a, b):
    return lax.dot_general(a, b, (((0,), (0,)), ((), ())), preferred_element_type=F32)


def _rmsnorm_body(x_ref, g_ref, o_ref):
    x = x_ref[...]
    y = x * lax.rsqrt(jnp.mean(x * x, axis=-1, keepdims=True) + NORM_EPS)
    o_ref[...] = (y * g_ref[...]).astype(o_ref.dtype)


def rmsnorm(x, g, tm, out_dtype):
    m, d = x.shape
    return pl.pallas_call(
        _rmsnorm_body,
        grid=(m // tm,),
        in_specs=[pl.BlockSpec((tm, d), lambda i: (i, 0)), pl.BlockSpec((1, d), lambda i: (0, 0))],
        out_specs=pl.BlockSpec((tm, d), lambda i: (i, 0)),
        out_shape=jax.ShapeDtypeStruct((m, d), out_dtype),
        compiler_params=_params("parallel"),
        name="rmsnorm",
    )(x, g.reshape(1, d))


def _row_scale(ssq_ref, d):
    ssq = ssq_ref[...]
    total = ssq[:, 0:1]
    for j in range(1, ssq.shape[1] // LANE):
        total = total + ssq[:, LANE * j:LANE * j + 1]
    return lax.rsqrt(total * (1.0 / d) + NORM_EPS)


def _ssq_block(x):
    return jnp.broadcast_to(jnp.sum(x * x, axis=-1, keepdims=True), (x.shape[0], LANE))


def _stream_prep_body(x_ref, xb_ref, ssq_ref):
    x = x_ref[...]
    xb_ref[...] = x.astype(xb_ref.dtype)
    ssq_ref[...] = _ssq_block(x)


def stream_prep(x, tm):
    m, d = x.shape
    return pl.pallas_call(
        _stream_prep_body,
        grid=(m // tm,),
        in_specs=[pl.BlockSpec((tm, d), lambda i: (i, 0))],
        out_specs=[pl.BlockSpec((tm, d), lambda i: (i, 0)), pl.BlockSpec((tm, LANE), lambda i: (i, 0))],
        out_shape=[jax.ShapeDtypeStruct((m, d), BF16), jax.ShapeDtypeStruct((m, LANE), F32)],
        compiler_params=_params("parallel"),
        name="stream_prep",
    )(x)


def _norm_matmul_body(x_ref, ssq_ref, w_ref, o_ref):
    o_ref[...] = (_row_scale(ssq_ref, x_ref.shape[1]) * _dot(x_ref[...], w_ref[...])).astype(o_ref.dtype)


def norm_matmul(xb, ssq, w, layer, tm, tn, out_dtype):
    m, k = xb.shape
    n = w.shape[2]
    return pl.pallas_call(
        _norm_matmul_body,
        grid=(m // tm, n // tn),
        in_specs=[pl.BlockSpec((tm, k), lambda i, j: (i, 0)), pl.BlockSpec((tm, ssq.shape[1]), lambda i, j: (i, 0)),
                  pl.BlockSpec((None, k, tn), lambda i, j: (layer, 0, j))],
        out_specs=pl.BlockSpec((tm, tn), lambda i, j: (i, j)),
        out_shape=jax.ShapeDtypeStruct((m, n), out_dtype),
        compiler_params=_params("parallel", "parallel"),
        name="in_proj",
    )(xb, ssq, w)


def _rwkv_body(r_ref, k_ref, v_ref, wd_ref, ad_ref, gd_ref, mu_ref,
               w0_ref, a0_ref, kk_ref, ka_ref, rk_ref, lnw_ref, lnb_ref,
               wup_ref, aup_ref, gup_ref,
               o_ref,
               s_ref, pr_ref, pk_ref, pv_ref, pwd_ref, pad_ref, pgd_ref):
    c = pl.program_id(1)
    C = RW_CHUNK
    RS = RW_SUB * C
    HD = RW_HEAD_DIM

    @pl.when(c == 0)
    def _():
        s_ref[...] = jnp.zeros_like(s_ref)
        pr_ref[...] = jnp.zeros_like(pr_ref)
        pk_ref[...] = jnp.zeros_like(pk_ref)
        pv_ref[...] = jnp.zeros_like(pv_ref)
        pwd_ref[...] = jnp.zeros_like(pwd_ref)
        pad_ref[...] = jnp.zeros_like(pad_ref)
        pgd_ref[...] = jnp.zeros_like(pgd_ref)

    def shift(x_ref, prev_ref, mu, sl):
        z = x_ref[:, sl]
        first = lax.broadcasted_iota(jnp.int32, z.shape, 0) == 0
        zs = jnp.where(first, prev_ref[0:1, sl], pltpu.roll(z, 1, 0))
        prev_ref[0:1, sl] = z[RS - 1:RS, :]
        return z + (zs - z) * mu

    mu_lora = 3 * RW_WIDTH
    full = slice(None)
    wd = shift(wd_ref, pwd_ref, mu_ref[:, mu_lora:mu_lora + LANE], full)
    ad = shift(ad_ref, pad_ref, mu_ref[:, mu_lora + LANE:mu_lora + 2 * LANE], full)
    gd = shift(gd_ref, pgd_ref, mu_ref[:, mu_lora + 2 * LANE:], full)
    subs = range(RW_SUB)

    def rows(t, s):
        return t[C * s:C * (s + 1)]

    tanh_wd = [rows(jnp.tanh(wd), s).astype(BF16) for s in subs]
    ad_b = [rows(ad, s).astype(BF16) for s in subs]
    sig_gd = [rows(_sigmoid(gd), s).astype(BF16) for s in subs]

    lane_sq = lax.broadcasted_iota(jnp.int32, (LANE, LANE), 1)
    row_sq = lax.broadcasted_iota(jnp.int32, (LANE, LANE), 0)
    same_head = (lane_sq < HD) == (row_sq < HD)
    head_ones = same_head.astype(BF16)
    eye = (lane_sq == row_sq).astype(F32)
    rc = lax.broadcasted_iota(jnp.int32, (C, C), 0)
    cc = lax.broadcasted_iota(jnp.int32, (C, C), 1)
    tril_incl = (cc <= rc).astype(BF16)
    lane_tall = lax.broadcasted_iota(jnp.int32, (2 * C, LANE), 1)
    lane_c = lax.broadcasted_iota(jnp.int32, (C, LANE), 1)
    row_c = lax.broadcasted_iota(jnp.int32, (C, LANE), 0)
    head0_c = lane_c < HD
    strict_lo = lane_c < row_c
    strict_hi = (lane_c >= C) & (lane_c - C < row_c)
    incl_lo = lane_c <= row_c
    incl_hi = (lane_c >= C) & (lane_c - C <= row_c)
    inv_n = 1.0 / HD

    P = range(RW_PAIRS)

    def head_sums(ts):
        his = [t.astype(BF16) for t in ts]
        los = [(t - hi.astype(F32)).astype(BF16) for t, hi in zip(ts, his)]
        out = _dot(jnp.concatenate(his + los, axis=0), head_ones)
        n = len(ts)
        return [out[C * i:C * (i + 1)] + out[C * (n + i):C * (n + i + 1)] for i in range(n)]

    pair_sls = [slice(LANE * i, LANE * (i + 1)) for i in P]

    def each(fn, *lists):
        return [fn(*args) for args in zip(*lists)]

    def items(per_pair):
        return [rows(t, s) for s in subs for t in per_pair]

    r = items([shift(r_ref, pr_ref, mu_ref[:, sl], sl) for sl in pair_sls])
    k = items([shift(k_ref, pk_ref, mu_ref[:, RW_WIDTH + sl.start:RW_WIDTH + sl.stop], sl) for sl in pair_sls])
    v = items([shift(v_ref, pv_ref, mu_ref[:, 2 * RW_WIDTH + sl.start:2 * RW_WIDTH + sl.stop], sl)
               for sl in pair_sls])
    sls = pair_sls * RW_SUB
    sub_of = [s for s in subs for _ in P]

    def log_decay(s, sl):
        x = -(w0_ref[:, sl] + _dot(tanh_wd[s], wup_ref[:, sl]))
        softplus = jnp.maximum(x, 0.0) + jnp.log1p(jnp.exp(-jnp.abs(x)))
        return -jnp.exp(-softplus - 0.5)

    lw = each(log_decay, sub_of, sls)
    a = each(lambda s, sl: _sigmoid(a0_ref[:, sl] + _dot(ad_b[s], aup_ref[:, sl])), sub_of, sls)
    g = each(lambda s, sl: _dot(sig_gd[s], gup_ref[:, sl]), sub_of, sls)

    kkr = each(lambda ki, sl: ki * kk_ref[:, sl], k, sls)
    ksq = head_sums(each(lambda t: t * t, kkr))
    kkn = each(lambda t, ss: t / jnp.maximum(jnp.sqrt(ss), 1e-12), kkr, ksq)
    kmod = each(lambda ki, ai, sl: ki * (1.0 + (ai - 1.0) * ka_ref[:, sl]), k, a, sls)
    beta = each(lambda ai, t: ai * t, a, kkn)

    def running_sum(lwi):
        hi = lwi.astype(BF16)
        both = _dot(tril_incl, jnp.concatenate([hi, (lwi - hi.astype(F32)).astype(BF16)], axis=1))
        return both[:, :LANE] + both[:, LANE:]

    lcum = each(running_sum, lw)
    lend = each(lambda t: t[C - 1:C, :], lcum)
    rh = each(lambda ri, lc: ri * jnp.exp(lc), r, lcum)
    kh = each(lambda t, lc, lwi: t * jnp.exp(lc - lwi), kkn, lcum, lw)
    e_neg = each(lambda lc: jnp.exp(-lc), lcum)
    e_end = each(lambda le, lc: jnp.exp(le - lc), lend, lcum)
    kb = each(lambda t, e: t * e, kmod, e_neg)
    bb = each(lambda t, e: t * e, beta, e_neg)
    kbe = each(lambda t, e: t * e, kmod, e_end)
    bbe = each(lambda t, e: t * e, beta, e_end)

    kr_f = each(lambda x1, x2: jnp.concatenate([x1, x2], axis=0), kh, rh)
    bk = each(lambda x1, x2: jnp.concatenate([x1, x2], axis=0).astype(BF16), bb, kb)
    vb = each(lambda t: t.astype(BF16), v)

    def gram(krf, bki):
        kr2 = jnp.concatenate([jnp.where(lane_tall < HD, krf, 0.0), jnp.where(lane_tall >= HD, krf, 0.0)], axis=0)
        return _dot_nt(kr2.astype(BF16), bki)

    g_all = each(gram, kr_f, bk)

    n_bd = each(lambda ga: jnp.concatenate([jnp.where(strict_lo, -ga[0:C], 0.0),
                                            jnp.where(strict_hi, -pltpu.roll(ga[2 * C:3 * C], C, 1), 0.0)], axis=0),
                g_all)
    t = each(lambda n: eye + n, n_bd)
    pw = each(lambda n: _dot(n.astype(BF16), n.astype(BF16)), n_bd)
    for _ in range(4):
        both = each(lambda ti, pi: _dot(jnp.concatenate([ti, pi], axis=0).astype(BF16), pi.astype(BF16)), t, pw)
        t = each(lambda ti, bi: ti + bi[:LANE], t, both)
        pw = each(lambda bi: bi[LANE:], both)
    t = each(lambda ti, pi: ti + _dot(ti.astype(BF16), pi.astype(BF16)), t, pw)

    def intra_rhs(ga, vbi):
        m1s = jnp.concatenate([jnp.where(strict_hi, ga[0:C], 0.0), jnp.where(strict_hi, ga[2 * C:3 * C], 0.0)], axis=0)
        return _dot(m1s.astype(BF16), jnp.concatenate([vbi, vbi], axis=0))

    q_intra = each(intra_rhs, g_all, vb)

    def m2(gb):
        return jnp.where(incl_lo, -gb, jnp.where(incl_hi, gb, 0.0))

    m2s = each(lambda ga: jnp.concatenate([m2(ga[C:2 * C]), m2(ga[3 * C:4 * C])], axis=0).astype(BF16), g_all)
    kbe_all = each(lambda kbei, bbei: jnp.concatenate([kbei, -bbei], axis=0).astype(BF16), kbe, bbe)
    s_decay = each(jnp.exp, lend)

    state = [s_ref[i] for i in P]
    y = []
    for s in subs:
        of = lambda lst: lst[RW_PAIRS * s:RW_PAIRS * (s + 1)]
        p_all = each(lambda x1, si: _dot_nt(x1.astype(BF16), si.astype(BF16)), of(kr_f), state)
        q_s = each(lambda qi, pa: jnp.where(same_head, qi + jnp.concatenate([pa[:C], pa[:C]], axis=0), 0.0),
                   of(q_intra), p_all)
        u_s = each(lambda ti, qi: _dot(ti.astype(BF16), qi.astype(BF16)), of(t), q_s)
        u = each(lambda us: us[:C] + us[C:], u_s)
        y_s = each(lambda mi, ui, vi: _dot(mi, jnp.concatenate([ui, vi], axis=0).astype(BF16)), of(m2s), u, of(v))
        y += each(lambda pa, ys: pa[C:] + jnp.where(head0_c, ys[:C], ys[C:]), p_all, y_s)
        ds = each(lambda vi, ui, kb_all: _dot_tn(jnp.concatenate([vi, ui], axis=0).astype(BF16), kb_all),
                  of(v), u, of(kbe_all))
        state = each(lambda si, di, dec: si * dec + jnp.where(same_head, di, 0.0), state, ds, of(s_decay))
    for i in P:
        s_ref[i] = state[i]

    d = each(lambda yi, si: yi - si * inv_n, y, head_sums(y))
    var = each(lambda si: si * inv_n, head_sums(each(lambda di: di * di, d)))
    bsum = head_sums(each(lambda ri, ki, sl: ri * ki * rk_ref[:, sl], r, kmod, sls))
    for i, (s, sl) in enumerate(zip(sub_of, sls)):
        yn = d[i] * lax.rsqrt(var[i] + RW_GN_EPS) * lnw_ref[:, sl] + lnb_ref[:, sl]
        o_ref[C * s:C * (s + 1), sl] = ((yn + bsum[i] * v[i]) * g[i]).astype(o_ref.dtype)


def rwkv_mix(p, batch, mu, w0, a0, k_k, k_a, r_k, ln_w, ln_b, w_up, a_up, g_up):
    tp = p.shape[0]
    lp = tp // batch
    C = RW_CHUNK * RW_SUB
    nchunk = lp // C

    def pspec(width, base):
        return pl.BlockSpec((C, width), lambda b, c: (b * nchunk + c, base // width))

    def const(shape):
        return pl.BlockSpec(shape, lambda b, c: (0, 0))

    in_specs = [
        pspec(RW_WIDTH, OFF_RW), pspec(RW_WIDTH, OFF_RW + RW_WIDTH), pspec(RW_WIDTH, OFF_RW + 2 * RW_WIDTH),
        pspec(LANE, OFF_RW_WD), pspec(LANE, OFF_RW_AD), pspec(RW_GATE_LORA, OFF_RW_GD),
        const(mu.shape),
    ] + [const((1, RW_WIDTH))] * 7 + [const(w_up.shape), const(a_up.shape), const(g_up.shape)]
    row = lambda t: t.reshape(1, -1)
    return pl.pallas_call(
        _rwkv_body,
        grid=(batch, nchunk),
        in_specs=in_specs,
        out_specs=pl.BlockSpec((C, RW_WIDTH), lambda b, c: (b * nchunk + c, 0)),
        out_shape=jax.ShapeDtypeStruct((tp, RW_WIDTH), BF16),
        scratch_shapes=[pltpu.VMEM((RW_PAIRS, LANE, LANE), F32)] + [pltpu.VMEM((8, RW_WIDTH), F32)] * 3
        + [pltpu.VMEM((8, LANE), F32)] * 2 + [pltpu.VMEM((8, RW_GATE_LORA), F32)],
        compiler_params=_params("parallel", "arbitrary"),
        name="rwkv7_mix",
    )(p, p, p, p, p, p, mu, row(w0), row(a0), row(k_k), row(k_a), row(r_k), row(ln_w), row(ln_b),
      w_up, a_up, g_up)


def _ret_body(q_ref, k_ref, v_ref, g_ref, cos_ref, sin_ref, o_ref, state_ref):
    c = pl.program_id(1)
    C = RET_CHUNK
    d = RET_HEAD_DIM

    @pl.when(c == 0)
    def _():
        state_ref[...] = jnp.zeros_like(state_ref)

    row = lax.broadcasted_iota(jnp.int32, (C, C), 0).astype(F32)
    col = lax.broadcasted_iota(jnp.int32, (C, C), 1).astype(F32)
    diff = row - col
    causal = diff >= 0
    heads = range(RET_HEADS)
    subs = range(RET_SUB)
    rws = [slice(C * s, C * (s + 1)) for s in subs for _ in heads]
    sls = [slice(d * h, d * (h + 1)) for _ in subs for h in heads]
    lgs = [math.log1p(-(2.0 ** (-5.0 - h))) for _ in subs for h in heads]

    def each(fn, *lists):
        return [fn(*args) for args in zip(*lists)]

    def rope(x_ref, rw, sl):
        x = x_ref[rw, sl]
        return x * cos_ref[rw, :] + pltpu.roll(x, d // 2, 1) * sin_ref[rw, :]

    qb = each(lambda rw, sl: rope(q_ref, rw, sl).astype(BF16), rws, sls)
    k = each(lambda rw, sl: rope(k_ref, rw, sl) * (d ** -0.5), rws, sls)
    kb = each(lambda t: t.astype(BF16), k)
    vb = each(lambda rw, sl: v_ref[rw, sl].astype(BF16), rws, sls)
    s = each(lambda qi, ki, lg: _dot_nt(qi, ki) * jnp.where(causal, jnp.exp(lg * jnp.maximum(diff, 0.0)), 0.0),
             qb, kb, lgs)
    o_intra = each(lambda si, vi: _dot(si.astype(BF16), vi), s, vb)
    kd = each(lambda ki, lg: (ki * jnp.exp(lg * (C - 1.0 - row))).astype(BF16), k, lgs)
    kv = each(_dot_tn, kd, vb)

    state = [state_ref[h] for h in heads]
    o = []
    for sub in subs:
        of = lambda lst: lst[RET_HEADS * sub:RET_HEADS * (sub + 1)]
        o += each(lambda oi, qi, st, lg: oi + _dot(qi, st.astype(BF16)) * jnp.exp(lg * (row + 1.0)),
                  of(o_intra), of(qb), state, of(lgs))
        state = each(lambda st, kvi, lg: st * math.exp(lg * C) + kvi, state, of(kv), of(lgs))
    for h in heads:
        state_ref[h] = state[h]
    o = each(lambda oi: oi * lax.rsqrt(jnp.mean(oi * oi, axis=-1, keepdims=True) + NORM_EPS), o)
    for oi, rw, sl in zip(o, rws, sls):
        g = g_ref[rw, sl]
        o_ref[rw, sl] = (g * _sigmoid(g) * oi).astype(o_ref.dtype)


def retention_mix(p, batch, cos, sin):
    tp = p.shape[0]
    lp = tp // batch
    C = RET_CHUNK * RET_SUB
    nchunk = lp // C
    base = OFF_RET // RET_WIDTH

    def pspec(j):
        return pl.BlockSpec((C, RET_WIDTH), lambda b, c: (b * nchunk + c, base + j))

    tab = pl.BlockSpec((C, RET_HEAD_DIM), lambda b, c: (c, 0))
    return pl.pallas_call(
        _ret_body,
        grid=(batch, nchunk),
        in_specs=[pspec(0), pspec(1), pspec(2), pspec(3), tab, tab],
        out_specs=pl.BlockSpec((C, RET_WIDTH), lambda b, c: (b * nchunk + c, 0)),
        out_shape=jax.ShapeDtypeStruct((tp, RET_WIDTH), BF16),
        scratch_shapes=[pltpu.VMEM((RET_HEADS, RET_HEAD_DIM, RET_HEAD_DIM), F32)],
        compiler_params=_params("parallel", "arbitrary"),
        name="retention_mix",
    )(p, p, p, p, cos, sin)


def _mla_proj_body(qd_ref, kvd_ref, krd_ref, nq_ref, nkv_ref, wuq_ref, wukv_ref, cos_ref, sin_ref,
                   q_out, k_out, v_out):
    cos = cos_ref[...]
    sin = sin_ref[...]

    def rope(x):
        return x * cos + (pltpu.roll(x, MLA_ROPE // 2, 1) + pltpu.roll(x, LANE - MLA_ROPE // 2, 1)) * sin

    def norm(x, g):
        return x * lax.rsqrt(jnp.mean(x * x, axis=-1, keepdims=True) + NORM_EPS) * g

    scale = (MLA_NOPE + MLA_ROPE) ** -0.5 * math.log2(math.e)
    q = _dot(norm(qd_ref[...], nq_ref[...]).astype(BF16), wuq_ref[...]) * scale
    kv = _dot(norm(kvd_ref[...], nkv_ref[...]).astype(BF16), wukv_ref[...])
    kr = rope(krd_ref[...]).astype(k_out.dtype)
    for h in range(MLA_HEADS):
        lo = MLA_QK_PAD * h
        q_out[:, lo:lo + LANE] = q[:, lo:lo + LANE].astype(q_out.dtype)
        q_out[:, lo + LANE:lo + 2 * LANE] = rope(q[:, lo + LANE:lo + 2 * LANE]).astype(q_out.dtype)
        k_out[:, lo:lo + LANE] = kv[:, MLA_NOPE * h:MLA_NOPE * (h + 1)].astype(k_out.dtype)
        k_out[:, lo + LANE:lo + 2 * LANE] = kr
    v_out[...] = kv[:, MLA_HEADS * MLA_NOPE:].astype(v_out.dtype)


def mla_proj(p, batch, norm_q, norm_kv, w_uq, w_ukv, cos, sin, tm):
    tp = p.shape[0]
    lp = tp // batch
    per_seq = lp // tm
    qk_w = MLA_HEADS * MLA_QK_PAD
    const = lambda i: (0, 0)
    return pl.pallas_call(
        _mla_proj_body,
        grid=(tp // tm,),
        in_specs=[
            pl.BlockSpec((tm, MLA_Q_RANK), lambda i: (i, OFF_MLA_Q // MLA_Q_RANK)),
            pl.BlockSpec((tm, MLA_KV_RANK), lambda i: (i, OFF_MLA_KV // MLA_KV_RANK)),
            pl.BlockSpec((tm, LANE), lambda i: (i, OFF_MLA_KR // LANE)),
            pl.BlockSpec((1, MLA_Q_RANK), const),
            pl.BlockSpec((1, MLA_KV_RANK), const),
            pl.BlockSpec((MLA_Q_RANK, qk_w), const),
            pl.BlockSpec((MLA_KV_RANK, MLA_HEADS * (MLA_NOPE + MLA_V)), const),
            pl.BlockSpec((tm, LANE), lambda i: (i % per_seq, 0)),
            pl.BlockSpec((tm, LANE), lambda i: (i % per_seq, 0)),
        ],
        out_specs=[
            pl.BlockSpec((tm, qk_w), lambda i: (i, 0)),
            pl.BlockSpec((tm, qk_w), lambda i: (i, 0)),
            pl.BlockSpec((tm, MLA_WIDTH), lambda i: (i, 0)),
        ],
        out_shape=[
            jax.ShapeDtypeStruct((tp, qk_w), BF16),
            jax.ShapeDtypeStruct((tp, qk_w), BF16),
            jax.ShapeDtypeStruct((tp, MLA_WIDTH), BF16),
        ],
        compiler_params=_params("parallel"),
        name="mla_proj",
    )(p, p, p, norm_q.reshape(1, -1), norm_kv.reshape(1, -1), w_uq, w_ukv, cos, sin)


def _attn_body(q_ref, k_ref, v_ref, o_ref, *, tq):
    i = pl.program_id(2)
    heads = range(ATTN_HEADS_PER_STEP)
    qs = [q_ref[:, MLA_QK_PAD * h:MLA_QK_PAD * (h + 1)] for h in heads]

    def step(off, width, masked, carry):
        ms, ls, accs = carry
        off = pl.multiple_of(off, tq)
        ss = [_dot_nt(qs[h], k_ref[pl.ds(off, width), MLA_QK_PAD * h:MLA_QK_PAD * (h + 1)]) for h in heads]
        if masked:
            row = i * tq + lax.broadcasted_iota(jnp.int32, (tq, width), 0)
            col = off + lax.broadcasted_iota(jnp.int32, (tq, width), 1)
            ss = [jnp.where(col <= row, s, -jnp.inf) for s in ss]
        m_new = [jnp.maximum(ms[h], jnp.max(ss[h], axis=-1, keepdims=True)) for h in heads]
        alpha = [jnp.exp2(ms[h] - m_new[h]) for h in heads]
        ps = [jnp.exp2(ss[h] - m_new[h]) for h in heads]
        ls = [alpha[h] * ls[h] + jnp.sum(ps[h], axis=-1, keepdims=True) for h in heads]
        pv = [_dot(ps[h].astype(BF16), v_ref[pl.ds(off, width), MLA_V * h:MLA_V * (h + 1)]) for h in heads]
        accs = [alpha[h] * accs[h] + pv[h] for h in heads]
        return tuple(m_new), tuple(ls), tuple(accs)

    carry = (tuple(jnp.full((tq, 1), -1e30, F32) for _ in heads),
             tuple(jnp.zeros((tq, 1), F32) for _ in heads),
             tuple(jnp.zeros((tq, MLA_V), F32) for _ in heads))
    n_single = (i + 1) % 2
    n_pairs = (i + 1) // 2
    carry = lax.cond(i == 0, lambda c: step(0, tq, True, c), lambda c: c, carry)
    carry = lax.cond((n_single == 1) & (i > 0), lambda c: step(0, tq, False, c), lambda c: c, carry)
    pair_off = lambda p: (n_single + 2 * p) * tq
    carry = lax.fori_loop(0, n_pairs - 1, lambda p, c: step(pair_off(p), 2 * tq, False, c), carry)
    carry = lax.cond(n_pairs > 0, lambda c: step(pair_off(n_pairs - 1), 2 * tq, True, c), lambda c: c, carry)
    _, ls, accs = carry
    for h in heads:
        o_ref[:, MLA_V * h:MLA_V * (h + 1)] = (accs[h] / ls[h]).astype(o_ref.dtype)


def mla_attention(q, k, v, batch, tq):
    tp = q.shape[0]
    lp = tp // batch
    nq = lp // tq
    hs = ATTN_HEADS_PER_STEP
    return pl.pallas_call(
        functools.partial(_attn_body, tq=tq),
        grid=(batch, MLA_HEADS // hs, nq),
        in_specs=[
            pl.BlockSpec((tq, hs * MLA_QK_PAD), lambda b, h, i: (b * nq + i, h)),
            pl.BlockSpec((lp, hs * MLA_QK_PAD), lambda b, h, i: (b, h)),
            pl.BlockSpec((lp, hs * MLA_V), lambda b, h, i: (b, h)),
        ],
        out_specs=pl.BlockSpec((tq, hs * MLA_V), lambda b, h, i: (b * nq + i, h)),
        out_shape=jax.ShapeDtypeStruct((tp, MLA_WIDTH), BF16),
        compiler_params=_params("parallel", "parallel", "arbitrary"),
        name="mla_attention",
    )(q, k, v)


def _merge_body(ya_ref, yb_ref, yc_ref, wa_ref, wb_ref, wc_ref, ga_ref, gb_ref, gc_ref, o_ref):
    def branch(y_ref, w_ref, g_ref):
        return _sigmoid(g_ref[...]) * _dot(y_ref[...], w_ref[...])

    o_ref[...] = (branch(ya_ref, wa_ref, ga_ref) + branch(yb_ref, wb_ref, gb_ref)
                  + branch(yc_ref, wc_ref, gc_ref)).astype(o_ref.dtype)


def merge_branches(ya, yb, yc, wa, wb, wc, layer, p, tm, tn):
    tp = ya.shape[0]

    def yspec(width):
        return pl.BlockSpec((tm, width), lambda i, j: (i, 0))

    def wspec(width):
        return pl.BlockSpec((None, width, tn), lambda i, j: (layer, 0, j))

    def gspec(branch):
        base = (OFF_GATE + branch * D_MODEL) // tn
        return pl.BlockSpec((tm, tn), lambda i, j: (i, base + j))

    return pl.pallas_call(
        _merge_body,
        grid=(tp // tm, D_MODEL // tn),
        in_specs=[yspec(RW_WIDTH), yspec(RET_WIDTH), yspec(MLA_WIDTH),
                  wspec(RW_WIDTH), wspec(RET_WIDTH), wspec(MLA_WIDTH),
                  gspec(0), gspec(1), gspec(2)],
        out_specs=pl.BlockSpec((tm, tn), lambda i, j: (i, j)),
        out_shape=jax.ShapeDtypeStruct((tp, D_MODEL), BF16),
        compiler_params=_params("parallel", "parallel"),
        name="merge_branches",
    )(ya, yb, yc, wa, wb, wc, p, p, p)


def _resid_body(x_ref, w_ref, h_ref, o_ref, ob_ref, ssq_ref):
    hn = h_ref[...] + _dot(x_ref[...], w_ref[...])
    o_ref[...] = hn
    ob_ref[...] = hn.astype(ob_ref.dtype)
    ssq_ref[...] = _ssq_block(hn)


def resid_matmul(x, w, layer, h, tm, tn):
    m, kdim = x.shape
    n = w.shape[2]
    tile = pl.BlockSpec((tm, tn), lambda i, j: (i, j))
    return pl.pallas_call(
        _resid_body,
        grid=(m // tm, n // tn),
        in_specs=[
            pl.BlockSpec((tm, kdim), lambda i, j: (i, 0)),
            pl.BlockSpec((None, kdim, tn), lambda i, j: (layer, 0, j)),
            tile,
        ],
        out_specs=[tile, tile, pl.BlockSpec((tm, LANE), lambda i, j: (i, j))],
        out_shape=[jax.ShapeDtypeStruct((m, n), F32), jax.ShapeDtypeStruct((m, n), BF16),
                   jax.ShapeDtypeStruct((m, LANE * (n // tn)), F32)],
        compiler_params=_params("parallel", "parallel"),
        name="resid_matmul",
    )(x, w, h)


def _ffn_up_body(x_ref, ssq_ref, wg_ref, wu_ref, o_ref):
    x = x_ref[...]
    r = _row_scale(ssq_ref, x.shape[1])
    hg = r * _dot(x, wg_ref[...])
    hu = r * _dot(x, wu_ref[...])
    o_ref[...] = (hg * _sigmoid(hg) * hu).astype(o_ref.dtype)


def ffn_up(xb, ssq, w_gate_up, layer, tm, tn):
    m, kdim = xb.shape
    hidden = w_gate_up.shape[2] // 2
    nj = hidden // tn
    return pl.pallas_call(
        _ffn_up_body,
        grid=(m // tm, nj),
        in_specs=[
            pl.BlockSpec((tm, kdim), lambda i, j: (i, 0)),
            pl.BlockSpec((tm, ssq.shape[1]), lambda i, j: (i, 0)),
            pl.BlockSpec((None, kdim, tn), lambda i, j: (layer, 0, j)),
            pl.BlockSpec((None, kdim, tn), lambda i, j: (layer, 0, nj + j)),
        ],
        out_specs=pl.BlockSpec((tm, tn), lambda i, j: (i, j)),
        out_shape=jax.ShapeDtypeStruct((m, hidden), BF16),
        compiler_params=_params("parallel", "parallel"),
        name="ffn_up",
    )(xb, ssq, w_gate_up, w_gate_up)


def _pad_cols(w, width):
    return jnp.pad(w, [(0, 0)] * (w.ndim - 1) + [(0, width - w.shape[-1])])


def _pack_w_in(w_in, gain):
    w_in = w_in * gain[..., None]
    rw_cols = 3 * RW_WIDTH + RW_DECAY_LORA + RW_A_LORA + RW_GATE_LORA
    ret_cols = 4 * RET_WIDTH
    o = 0
    rw = w_in[..., o:o + rw_cols]
    o += rw_cols
    ret = w_in[..., o:o + ret_cols]
    o += ret_cols
    qd = w_in[..., o:o + MLA_Q_RANK]
    o += MLA_Q_RANK
    kvd = w_in[..., o:o + MLA_KV_RANK]
    o += MLA_KV_RANK
    krd = w_in[..., o:o + MLA_ROPE]
    o += MLA_ROPE
    gates = w_in[..., o:]
    rkv = rw[..., :3 * RW_WIDTH]
    wd = rw[..., 3 * RW_WIDTH:3 * RW_WIDTH + RW_DECAY_LORA]
    ad = rw[..., 3 * RW_WIDTH + RW_DECAY_LORA:3 * RW_WIDTH + RW_DECAY_LORA + RW_A_LORA]
    gd = rw[..., 3 * RW_WIDTH + RW_DECAY_LORA + RW_A_LORA:]
    packed = jnp.zeros(w_in.shape[:-1] + (P_COLS,), BF16)
    placed = [(OFF_RET, ret), (OFF_GATE, gates), (OFF_RW, rkv), (OFF_RW_WD, wd), (OFF_RW_AD, ad), (OFF_RW_GD, gd),
              (OFF_MLA_Q, qd), (OFF_MLA_KV, kvd), (OFF_MLA_KR, krd)]
    for off, part in placed:
        packed = packed.at[..., off:off + part.shape[-1]].set(part.astype(BF16))
    return packed


def _pack_mu(mu):
    rkv = mu[..., :3 * RW_WIDTH]
    wd = mu[..., 3 * RW_WIDTH:3 * RW_WIDTH + RW_DECAY_LORA]
    ad = mu[..., 3 * RW_WIDTH + RW_DECAY_LORA:3 * RW_WIDTH + RW_DECAY_LORA + RW_A_LORA]
    gd = mu[..., 3 * RW_WIDTH + RW_DECAY_LORA + RW_A_LORA:]
    return jnp.concatenate([rkv, _pad_cols(wd, LANE), _pad_cols(ad, LANE), gd], axis=-1)


def _pad_rows(w, rows):
    return jnp.pad(w, [(0, 0)] * (w.ndim - 2) + [(0, rows - w.shape[-2]), (0, 0)])


def _pack_w_uq(w):
    nl, rank, _ = w.shape
    w = w.reshape(nl, rank, MLA_HEADS, MLA_NOPE + MLA_ROPE)
    w = jnp.pad(w, ((0, 0), (0, 0), (0, 0), (0, MLA_QK_PAD - MLA_NOPE - MLA_ROPE)))
    return w.reshape(nl, rank, MLA_HEADS * MLA_QK_PAD).astype(BF16)


def _pack_w_ukv(w):
    nl, rank, _ = w.shape
    w = w.reshape(nl, rank, MLA_HEADS, 2, MLA_NOPE)
    w = jnp.swapaxes(w, 2, 3)
    return w.reshape(nl, rank, 2 * MLA_HEADS * MLA_NOPE).astype(BF16)


def _rope_tables(lp):
    pos = jnp.arange(lp, dtype=F32)

    def tables(dim):
        inv = ROPE_BASE ** (-jnp.arange(0, dim, 2, dtype=F32) / dim)
        ang = pos[:, None] * inv[None, :]
        return jnp.cos(ang), jnp.sin(ang)

    c, s = tables(RET_HEAD_DIM)
    ret = (jnp.concatenate([c, c], axis=1), jnp.concatenate([-s, s], axis=1))
    c, s = tables(MLA_ROPE)
    z = jnp.zeros((lp, LANE - MLA_ROPE), F32)
    mla = (jnp.concatenate([c, c, z], axis=1), jnp.concatenate([-s, s, z], axis=1))
    return ret, mla


def kernel(x, meta_tokens, norm_mix, w_in, rw_mu, rw_w0, rw_w_up, rw_a0, rw_a_up, rw_g_up, rw_k_k, rw_k_a, rw_r_k, rw_ln_w, rw_ln_b, mla_norm_q, mla_norm_kv, mla_w_uq, mla_w_ukv, w_br_rwkv, w_br_ret, w_br_mla, w_out, norm_ffn, w_gate_up, w_down, final_norm):
    batch, seq, d = x.shape
    depth = w_in.shape[0]
    lp = -(-(N_META + seq) // SEQ_ALIGN) * SEQ_ALIGN
    tp = batch * lp

    meta = jnp.broadcast_to(meta_tokens[None].astype(x.dtype), (batch, N_META, d))
    pad = jnp.zeros((batch, lp - N_META - seq, d), x.dtype)
    h = jnp.concatenate([meta, x, pad], axis=1).reshape(tp, d)

    wp = _pack_w_in(w_in, norm_mix)
    mu = _pack_mu(rw_mu)
    w_up = _pad_rows(rw_w_up, LANE).astype(BF16)
    a_up = _pad_rows(rw_a_up, LANE).astype(BF16)
    g_up = rw_g_up.astype(BF16)
    wuq = _pack_w_uq(mla_w_uq)
    wukv = _pack_w_ukv(mla_w_ukv)
    wa = w_br_rwkv.astype(BF16)
    wb = w_br_ret.astype(BF16)
    wc = w_br_mla.astype(BF16)
    wo = w_out.astype(BF16)
    wgu = (w_gate_up * norm_ffn[..., None]).astype(BF16)
    wdn = w_down.astype(BF16)
    (cos_ret, sin_ret), (cos_mla, sin_mla) = _rope_tables(lp)

    def row_tile(pref):
        return next((t for t in pref if tp % t == 0), SEQ_ALIGN)

    tm = row_tile((768,))
    tm_wide = row_tile((1536, 768))
    tm_seq = 384 if lp % 384 == 0 else SEQ_ALIGN

    hb, ssq = stream_prep(h, tm)
    for l in range(depth):
        p = norm_matmul(hb, ssq, wp, l, tm_wide, P_TILE_N, F32)
        ya = rwkv_mix(p, batch, mu[l:l + 1], rw_w0[l], rw_a0[l], rw_k_k[l], rw_k_a[l], rw_r_k[l],
                      rw_ln_w[l], rw_ln_b[l], w_up[l], a_up[l], g_up[l])
        yb = retention_mix(p, batch, cos_ret, sin_ret)
        q, k, v = mla_proj(p, batch, mla_norm_q[l], mla_norm_kv[l], wuq[l], wukv[l], cos_mla, sin_mla, tm_seq)
        yc = mla_attention(q, k, v, batch, tm_seq)
        merged = merge_branches(ya, yb, yc, wa, wb, wc, l, p, tm, 1024)
        h, hb, ssq = resid_matmul(merged, wo, l, h, tm_seq, d)
        act = ffn_up(hb, ssq, wgu, l, tm, 512)
        h, hb, ssq = resid_matmul(act, wdn, l, h, tm, 512)
    out = rmsnorm(h, final_norm, tm, F32)
    return out.reshape(batch, lp, d)[:, N_META:N_META + seq]
```

```python
import functools
import math

import jax
import jax.numpy as jnp
from jax import lax
from jax.experimental import pallas as pl
from jax.experimental.pallas import tpu as pltpu

F32 = jnp.float32
BF16 = jnp.bfloat16

D_MODEL = 2048
N_META = 16
NORM_EPS = 1e-6
ROPE_BASE = 10000.0

RW_HEADS = 16
RW_HEAD_DIM = 64
RW_WIDTH = RW_HEADS * RW_HEAD_DIM
RW_DECAY_LORA = 96
RW_A_LORA = 96
RW_GATE_LORA = 256
RW_GN_EPS = RW_HEAD_DIM * 1e-5
RW_CHUNK = 64
RW_SUB = 2
RW_PAIRS = RW_WIDTH // 128

RET_HEADS = 8
RET_HEAD_DIM = 128
RET_WIDTH = RET_HEADS * RET_HEAD_DIM
RET_CHUNK = 128
RET_SUB = 3

MLA_HEADS = 8
MLA_NOPE = 128
MLA_ROPE = 64
MLA_V = 128
MLA_Q_RANK = 512
MLA_KV_RANK = 256
MLA_WIDTH = MLA_HEADS * MLA_V
MLA_QK_PAD = 256
ATTN_HEADS_PER_STEP = 2

FFN_HIDDEN = -(-8 * D_MODEL // (3 * 256)) * 256

LANE = 128
SEQ_ALIGN = 128

OFF_RET = 0
OFF_GATE = OFF_RET + 4 * RET_WIDTH
OFF_RW = OFF_GATE + 3 * D_MODEL
OFF_RW_WD = OFF_RW + 3 * RW_WIDTH
OFF_RW_AD = OFF_RW_WD + LANE
OFF_RW_GD = OFF_RW_AD + LANE
OFF_MLA_Q = OFF_RW_GD + RW_GATE_LORA
OFF_MLA_KV = OFF_MLA_Q + MLA_Q_RANK
OFF_MLA_KR = OFF_MLA_KV + MLA_KV_RANK
P_COLS_USED = OFF_MLA_KR + LANE
P_TILE_N = 512
P_COLS = -(-P_COLS_USED // P_TILE_N) * P_TILE_N

VMEM_LIMIT = 48 * 1024 * 1024


def _params(*sem):
    return pltpu.CompilerParams(dimension_semantics=sem, vmem_limit_bytes=VMEM_LIMIT)


def _sigmoid(x):
    return 1.0 / (1.0 + jnp.exp(-x))


def _dot(a, b):
    return jnp.dot(a, b, preferred_element_type=F32)


def _dot_nt(a, b):
    return lax.dot_general(a, b, (((1,), (1,)), ((), ())), preferred_element_type=F32)


def _dot_tn(a, b):
    return lax.dot_general(a, b, (((0,), (0,)), ((), ())), preferred_element_type=F32)


def _rmsnorm_body(x_ref, g_ref, o_ref):
    x = x_ref[...]
    y = x * lax.rsqrt(jnp.mean(x * x, axis=-1, keepdims=True) + NORM_EPS)
    o_ref[...] = (y * g_ref[...]).astype(o_ref.dtype)


def rmsnorm(x, g, tm, out_dtype):
    m, d = x.shape
    return pl.pallas_call(
        _rmsnorm_body,
        grid=(m // tm,),
        in_specs=[pl.BlockSpec((tm, d), lambda i: (i, 0)), pl.BlockSpec((1, d), lambda i: (0, 0))],
        out_specs=pl.BlockSpec((tm, d), lambda i: (i, 0)),
        out_shape=jax.ShapeDtypeStruct((m, d), out_dtype),
        compiler_params=_params("parallel"),
        name="rmsnorm",
    )(x, g.reshape(1, d))


def _row_scale(ssq_ref, d):
    ssq = ssq_ref[...]
    total = ssq[:, 0:1]
    for j in range(1, ssq.shape[1] // LANE):
        total = total + ssq[:, LANE * j:LANE * j + 1]
    return lax.rsqrt(total * (1.0 / d) + NORM_EPS)


def _ssq_block(x):
    return jnp.broadcast_to(jnp.sum(x * x, axis=-1, keepdims=True), (x.shape[0], LANE))


def _stream_prep_body(x_ref, xb_ref, ssq_ref):
    x = x_ref[...]
    xb_ref[...] = x.astype(xb_ref.dtype)
    ssq_ref[...] = _ssq_block(x)


def stream_prep(x, tm):
    m, d = x.shape
    return pl.pallas_call(
        _stream_prep_body,
        grid=(m // tm,),
        in_specs=[pl.BlockSpec((tm, d), lambda i: (i, 0))],
        out_specs=[pl.BlockSpec((tm, d), lambda i: (i, 0)), pl.BlockSpec((tm, LANE), lambda i: (i, 0))],
        out_shape=[jax.ShapeDtypeStruct((m, d), BF16), jax.ShapeDtypeStruct((m, LANE), F32)],
        compiler_params=_params("parallel"),
        name="stream_prep",
    )(x)


def _norm_matmul_body(x_ref, ssq_ref, w_ref, o_ref):
    o_ref[...] = (_row_scale(ssq_ref, x_ref.shape[1]) * _dot(x_ref[...], w_ref[...])).astype(o_ref.dtype)


def norm_matmul(xb, ssq, w, layer, tm, tn, out_dtype):
    m, k = xb.shape
    n = w.shape[2]
    return pl.pallas_call(
        _norm_matmul_body,
        grid=(m // tm, n // tn),
        in_specs=[pl.BlockSpec((tm, k), lambda i, j: (i, 0)), pl.BlockSpec((tm, ssq.shape[1]), lambda i, j: (i, 0)),
                  pl.BlockSpec((None, k, tn), lambda i, j: (layer, 0, j))],
        out_specs=pl.BlockSpec((tm, tn), lambda i, j: (i, j)),
        out_shape=jax.ShapeDtypeStruct((m, n), out_dtype),
        compiler_params=_params("parallel", "parallel"),
        name="in_proj",
    )(xb, ssq, w)


def _rwkv_body(r_ref, k_ref, v_ref, wd_ref, ad_ref, gd_ref, mu_ref,
               w0_ref, a0_ref, kk_ref, ka_ref, rk_ref, lnw_ref, lnb_ref,
               wup_ref, aup_ref, gup_ref,
               o_ref,
               s_ref, pr_ref, pk_ref, pv_ref, pwd_ref, pad_ref, pgd_ref):
    c = pl.program_id(1)
    C = RW_CHUNK
    RS = RW_SUB * C
    HD = RW_HEAD_DIM

    @pl.when(c == 0)
    def _():
        s_ref[...] = jnp.zeros_like(s_ref)
        pr_ref[...] = jnp.zeros_like(pr_ref)
        pk_ref[...] = jnp.zeros_like(pk_ref)
        pv_ref[...] = jnp.zeros_like(pv_ref)
        pwd_ref[...] = jnp.zeros_like(pwd_ref)
        pad_ref[...] = jnp.zeros_like(pad_ref)
        pgd_ref[...] = jnp.zeros_like(pgd_ref)

    def shift(x_ref, prev_ref, mu, sl):
        z = x_ref[:, sl]
        first = lax.broadcasted_iota(jnp.int32, z.shape, 0) == 0
        zs = jnp.where(first, prev_ref[0:1, sl], pltpu.roll(z, 1, 0))
        prev_ref[0:1, sl] = z[RS - 1:RS, :]
        return z + (zs - z) * mu

    mu_lora = 3 * RW_WIDTH
    full = slice(None)
    wd = shift(wd_ref, pwd_ref, mu_ref[:, mu_lora:mu_lora + LANE], full)
    ad = shift(ad_ref, pad_ref, mu_ref[:, mu_lora + LANE:mu_lora + 2 * LANE], full)
    gd = shift(gd_ref, pgd_ref, mu_ref[:, mu_lora + 2 * LANE:], full)
    subs = range(RW_SUB)

    def rows(t, s):
        return t[C * s:C * (s + 1)]

    tanh_wd = [rows(jnp.tanh(wd), s).astype(BF16) for s in subs]
    ad_b = [rows(ad, s).astype(BF16) for s in subs]
    sig_gd = [rows(_sigmoid(gd), s).astype(BF16) for s in subs]

    lane_sq = lax.broadcasted_iota(jnp.int32, (LANE, LANE), 1)
    row_sq = lax.broadcasted_iota(jnp.int32, (LANE, LANE), 0)
    same_head = (lane_sq < HD) == (row_sq < HD)
    head_ones = same_head.astype(BF16)
    eye = (lane_sq == row_sq).astype(F32)
    rc = lax.broadcasted_iota(jnp.int32, (C, C), 0)
    cc = lax.broadcasted_iota(jnp.int32, (C, C), 1)
    tril_incl = (cc <= rc).astype(BF16)
    lane_tall = lax.broadcasted_iota(jnp.int32, (2 * C, LANE), 1)
    lane_c = lax.broadcasted_iota(jnp.int32, (C, LANE), 1)
    row_c = lax.broadcasted_iota(jnp.int32, (C, LANE), 0)
    head0_c = lane_c < HD
    strict_lo = lane_c < row_c
    strict_hi = (lane_c >= C) & (lane_c - C < row_c)
    incl_lo = lane_c <= row_c
    incl_hi = (lane_c >= C) & (lane_c - C <= row_c)
    inv_n = 1.0 / HD

    P = range(RW_PAIRS)

    def head_sums(ts):
        his = [t.astype(BF16) for t in ts]
        los = [(t - hi.astype(F32)).astype(BF16) for t, hi in zip(ts, his)]
        out = _dot(jnp.concatenate(his + los, axis=0), head_ones)
        n = len(ts)
        return [out[C * i:C * (i + 1)] + out[C * (n + i):C * (n + i + 1)] for i in range(n)]

    pair_sls = [slice(LANE * i, LANE * (i + 1)) for i in P]

    def each(fn, *lists):
        return [fn(*args) for args in zip(*lists)]

    def items(per_pair):
        return [rows(t, s) for s in subs for t in per_pair]

    r = items([shift(r_ref, pr_ref, mu_ref[:, sl], sl) for sl in pair_sls])
    k = items([shift(k_ref, pk_ref, mu_ref[:, RW_WIDTH + sl.start:RW_WIDTH + sl.stop], sl) for sl in pair_sls])
    v = items([shift(v_ref, pv_ref, mu_ref[:, 2 * RW_WIDTH + sl.start:2 * RW_WIDTH + sl.stop], sl)
               for sl in pair_sls])
    sls = pair_sls * RW_SUB
    sub_of = [s for s in subs for _ in P]

    def log_decay(s, sl):
        x = -(w0_ref[:, sl] + _dot(tanh_wd[s], wup_ref[:, sl]))
        softplus = jnp.maximum(x, 0.0) + jnp.log1p(jnp.exp(-jnp.abs(x)))
        return -jnp.exp(-softplus - 0.5)

    lw = each(log_decay, sub_of, sls)
    a = each(lambda s, sl: _sigmoid(a0_ref[:, sl] + _dot(ad_b[s], aup_ref[:, sl])), sub_of, sls)
    g = each(lambda s, sl: _dot(sig_gd[s], gup_ref[:, sl]), sub_of, sls)

    kkr = each(lambda ki, sl: ki * kk_ref[:, sl], k, sls)
    ksq = head_sums(each(lambda t: t * t, kkr))
    kkn = each(lambda t, ss: t / jnp.maximum(jnp.sqrt(ss), 1e-12), kkr, ksq)
    kmod = each(lambda ki, ai, sl: ki * (1.0 + (ai - 1.0) * ka_ref[:, sl]), k, a, sls)
    beta = each(lambda ai, t: ai * t, a, kkn)

    def running_sum(lwi):
        hi = lwi.astype(BF16)
        both = _dot(tril_incl, jnp.concatenate([hi, (lwi - hi.astype(F32)).astype(BF16)], axis=1))
        return both[:, :LANE] + both[:, LANE:]

    lcum = each(running_sum, lw)
    lend = each(lambda t: t[C - 1:C, :], lcum)
    rh = each(lambda ri, lc: ri * jnp.exp(lc), r, lcum)
    kh = each(lambda t, lc, lwi: t * jnp.exp(lc - lwi), kkn, lcum, lw)
    e_neg = each(lambda lc: jnp.exp(-lc), lcum)
    e_end = each(lambda le, lc: jnp.exp(le - lc), lend, lcum)
    kb = each(lambda t, e: t * e, kmod, e_neg)
    bb = each(lambda t, e: t * e, beta, e_neg)
    kbe = each(lambda t, e: t * e, kmod, e_end)
    bbe = each(lambda t, e: t * e, beta, e_end)

    kr_f = each(lambda x1, x2: jnp.concatenate([x1, x2], axis=0), kh, rh)
    bk = each(lambda x1, x2: jnp.concatenate([x1, x2], axis=0).astype(BF16), bb, kb)
    vb = each(lambda t: t.astype(BF16), v)

    def gram(krf, bki):
        kr2 = jnp.concatenate([jnp.where(lane_tall < HD, krf, 0.0), jnp.where(lane_tall >= HD, krf, 0.0)], axis=0)
        return _dot_nt(kr2.astype(BF16), bki)

    g_all = each(gram, kr_f, bk)

    n_bd = each(lambda ga: jnp.concatenate([jnp.where(strict_lo, -ga[0:C], 0.0),
                                            jnp.where(strict_hi, -pltpu.roll(ga[2 * C:3 * C], C, 1), 0.0)], axis=0),
                g_all)
    t = each(lambda n: eye + n, n_bd)
    pw = each(lambda n: _dot(n.astype(BF16), n.astype(BF16)), n_bd)
    for _ in range(4):
        both = each(lambda ti, pi: _dot(jnp.concatenate([ti, pi], axis=0).astype(BF16), pi.astype(BF16)), t, pw)
        t = each(lambda ti, bi: ti + bi[:LANE], t, both)
        pw = each(lambda bi: bi[LANE:], both)
    t = each(lambda ti, pi: ti + _dot(ti.astype(BF16), pi.astype(BF16)), t, pw)

    def intra_rhs(ga, vbi):
        m1s = jnp.concatenate([jnp.where(strict_hi, ga[0:C], 0.0), jnp.where(strict_hi, ga[2 * C:3 * C], 0.0)], axis=0)
        return _dot(m1s.astype(BF16), jnp.concatenate([vbi, vbi], axis=0))

    q_intra = each(intra_rhs, g_all, vb)

    def m2(gb):
        return jnp.where(incl_lo, -gb, jnp.where(incl_hi, gb, 0.0))

    m2s = each(lambda ga: jnp.concatenate([m2(ga[C:2 * C]), m2(ga[3 * C:4 * C])], axis=0).astype(BF16), g_all)
    kbe_all = each(lambda kbei, bbei: jnp.concatenate([kbei, -bbei], axis=0).astype(BF16), kbe, bbe)
    s_decay = each(jnp.exp, lend)

    state = [s_ref[i] for i in P]
    y = []
    for s in subs:
        of = lambda lst: lst[RW_PAIRS * s:RW_PAIRS * (s + 1)]
        p_all = each(lambda x1, si: _dot_nt(x1.astype(BF16), si.astype(BF16)), of(kr_f), state)
        q_s = each(lambda qi, pa: jnp.where(same_head, qi + jnp.concatenate([pa[:C], pa[:C]], axis=0), 0.0),
                   of(q_intra), p_all)
        u_s = each(lambda ti, qi: _dot(ti.astype(BF16), qi.astype(BF16)), of(t), q_s)
        u = each(lambda us: us[:C] + us[C:], u_s)
        y_s = each(lambda mi, ui, vi: _dot(mi, jnp.concatenate([ui, vi], axis=0).astype(BF16)), of(m2s), u, of(v))
        y += each(lambda pa, ys: pa[C:] + jnp.where(head0_c, ys[:C], ys[C:]), p_all, y_s)
        ds = each(lambda vi, ui, kb_all: _dot_tn(jnp.concatenate([vi, ui], axis=0).astype(BF16), kb_all),
                  of(v), u, of(kbe_all))
        state = each(lambda si, di, dec: si * dec + jnp.where(same_head, di, 0.0), state, ds, of(s_decay))
    for i in P:
        s_ref[i] = state[i]

    d = each(lambda yi, si: yi - si * inv_n, y, head_sums(y))
    var = each(lambda si: si * inv_n, head_sums(each(lambda di: di * di, d)))
    bsum = head_sums(each(lambda ri, ki, sl: ri * ki * rk_ref[:, sl], r, kmod, sls))
    for i, (s, sl) in enumerate(zip(sub_of, sls)):
        yn = d[i] * lax.rsqrt(var[i] + RW_GN_EPS) * lnw_ref[:, sl] + lnb_ref[:, sl]
        o_ref[C * s:C * (s + 1), sl] = ((yn + bsum[i] * v[i]) * g[i]).astype(o_ref.dtype)


def rwkv_mix(p, batch, mu, w0, a0, k_k, k_a, r_k, ln_w, ln_b, w_up, a_up, g_up):
    tp = p.shape[0]
    lp = tp // batch
    C = RW_CHUNK * RW_SUB
    nchunk = lp // C

    def pspec(width, base):
        return pl.BlockSpec((C, width), lambda b, c: (b * nchunk + c, base // width))

    def const(shape):
        return pl.BlockSpec(shape, lambda b, c: (0, 0))

    in_specs = [
        pspec(RW_WIDTH, OFF_RW), pspec(RW_WIDTH, OFF_RW + RW_WIDTH), pspec(RW_WIDTH, OFF_RW + 2 * RW_WIDTH),
        pspec(LANE, OFF_RW_WD), pspec(LANE, OFF_RW_AD), pspec(RW_GATE_LORA, OFF_RW_GD),
        const(mu.shape),
    ] + [const((1, RW_WIDTH))] * 7 + [const(w_up.shape), const(a_up.shape), const(g_up.shape)]
    row = lambda t: t.reshape(1, -1)
    return pl.pallas_call(
        _rwkv_body,
        grid=(batch, nchunk),
        in_specs=in_specs,
        out_specs=pl.BlockSpec((C, RW_WIDTH), lambda b, c: (b * nchunk + c, 0)),
        out_shape=jax.ShapeDtypeStruct((tp, RW_WIDTH), BF16),
        scratch_shapes=[pltpu.VMEM((RW_PAIRS, LANE, LANE), F32)] + [pltpu.VMEM((8, RW_WIDTH), F32)] * 3
        + [pltpu.VMEM((8, LANE), F32)] * 2 + [pltpu.VMEM((8, RW_GATE_LORA), F32)],
        compiler_params=_params("parallel", "arbitrary"),
        name="rwkv7_mix",
    )(p, p, p, p, p, p, mu, row(w0), row(a0), row(k_k), row(k_a), row(r_k), row(ln_w), row(ln_b),
      w_up, a_up, g_up)


def _ret_body(q_ref, k_ref, v_ref, g_ref, cos_ref, sin_ref, o_ref, state_ref):
    c = pl.program_id(1)
    C = RET_CHUNK
    d = RET_HEAD_DIM

    @pl.when(c == 0)
    def _():
        state_ref[...] = jnp.zeros_like(state_ref)

    row = lax.broadcasted_iota(jnp.int32, (C, C), 0).astype(F32)
    col = lax.broadcasted_iota(jnp.int32, (C, C), 1).astype(F32)
    diff = row - col
    causal = diff >= 0
    heads = range(RET_HEADS)
    subs = range(RET_SUB)
    rws = [slice(C * s, C * (s + 1)) for s in subs for _ in heads]
    sls = [slice(d * h, d * (h + 1)) for _ in subs for h in heads]
    lgs = [math.log1p(-(2.0 ** (-5.0 - h))) for _ in subs for h in heads]

    def each(fn, *lists):
        return [fn(*args) for args in zip(*lists)]

    def rope(x_ref, rw, sl):
        x = x_ref[rw, sl]
        return x * cos_ref[rw, :] + pltpu.roll(x, d // 2, 1) * sin_ref[rw, :]

    qb = each(lambda rw, sl: rope(q_ref, rw, sl).astype(BF16), rws, sls)
    k = each(lambda rw, sl: rope(k_ref, rw, sl) * (d ** -0.5), rws, sls)
    kb = each(lambda t: t.astype(BF16), k)
    vb = each(lambda rw, sl: v_ref[rw, sl].astype(BF16), rws, sls)
    s = each(lambda qi, ki, lg: _dot_nt(qi, ki) * jnp.where(causal, jnp.exp(lg * jnp.maximum(diff, 0.0)), 0.0),
             qb, kb, lgs)
    o_intra = each(lambda si, vi: _dot(si.astype(BF16), vi), s, vb)
    kd = each(lambda ki, lg: (ki * jnp.exp(lg * (C - 1.0 - row))).astype(BF16), k, lgs)
    kv = each(_dot_tn, kd, vb)

    state = [state_ref[h] for h in heads]
    o = []
    for sub in subs:
        of = lambda lst: lst[RET_HEADS * sub:RET_HEADS * (sub + 1)]
        o += each(lambda oi, qi, st, lg: oi + _dot(qi, st.astype(BF16)) * jnp.exp(lg * (row + 1.0)),
                  of(o_intra), of(qb), state, of(lgs))
        state = each(lambda st, kvi, lg: st * math.exp(lg * C) + kvi, state, of(kv), of(lgs))
    for h in heads:
        state_ref[h] = state[h]
    o = each(lambda oi: oi * lax.rsqrt(jnp.mean(oi * oi, axis=-1, keepdims=True) + NORM_EPS), o)
    for oi, rw, sl in zip(o, rws, sls):
        g = g_ref[rw, sl]
        o_ref[rw, sl] = (g * _sigmoid(g) * oi).astype(o_ref.dtype)


def retention_mix(p, batch, cos, sin):
    tp = p.shape[0]
    lp = tp // batch
    C = RET_CHUNK * RET_SUB
    nchunk = lp // C
    base = OFF_RET // RET_WIDTH

    def pspec(j):
        return pl.BlockSpec((C, RET_WIDTH), lambda b, c: (b * nchunk + c, base + j))

    tab = pl.BlockSpec((C, RET_HEAD_DIM), lambda b, c: (c, 0))
    return pl.pallas_call(
        _ret_body,
        grid=(batch, nchunk),
        in_specs=[pspec(0), pspec(1), pspec(2), pspec(3), tab, tab],
        out_specs=pl.BlockSpec((C, RET_WIDTH), lambda b, c: (b * nchunk + c, 0)),
        out_shape=jax.ShapeDtypeStruct((tp, RET_WIDTH), BF16),
        scratch_shapes=[pltpu.VMEM((RET_HEADS, RET_HEAD_DIM, RET_HEAD_DIM), F32)],
        compiler_params=_params("parallel", "arbitrary"),
        name="retention_mix",
    )(p, p, p, p, cos, sin)


def _mla_proj_body(qd_ref, kvd_ref, krd_ref, nq_ref, nkv_ref, wuq_ref, wukv_ref, cos_ref, sin_ref,
                   q_out, k_out, v_out):
    cos = cos_ref[...]
    sin = sin_ref[...]

    def rope(x):
        return x * cos + (pltpu.roll(x, MLA_ROPE // 2, 1) + pltpu.roll(x, LANE - MLA_ROPE // 2, 1)) * sin

    def norm(x, g):
        return x * lax.rsqrt(jnp.mean(x * x, axis=-1, keepdims=True) + NORM_EPS) * g

    scale = (MLA_NOPE + MLA_ROPE) ** -0.5 * math.log2(math.e)
    q = _dot(norm(qd_ref[...], nq_ref[...]).astype(BF16), wuq_ref[...]) * scale
    kv = _dot(norm(kvd_ref[...], nkv_ref[...]).astype(BF16), wukv_ref[...])
    kr = rope(krd_ref[...]).astype(k_out.dtype)
    for h in range(MLA_HEADS):
        lo = MLA_QK_PAD * h
        q_out[:, lo:lo + LANE] = q[:, lo:lo + LANE].astype(q_out.dtype)
        q_out[:, lo + LANE:lo + 2 * LANE] = rope(q[:, lo + LANE:lo + 2 * LANE]).astype(q_out.dtype)
        k_out[:, lo:lo + LANE] = kv[:, MLA_NOPE * h:MLA_NOPE * (h + 1)].astype(k_out.dtype)
        k_out[:, lo + LANE:lo + 2 * LANE] = kr
    v_out[...] = kv[:, MLA_HEADS * MLA_NOPE:].astype(v_out.dtype)


def mla_proj(p, batch, norm_q, norm_kv, w_uq, w_ukv, cos, sin, tm):
    tp = p.shape[0]
    lp = tp // batch
    per_seq = lp // tm
    qk_w = MLA_HEADS * MLA_QK_PAD
    const = lambda i: (0, 0)
    return pl.pallas_call(
        _mla_proj_body,
        grid=(tp // tm,),
        in_specs=[
            pl.BlockSpec((tm, MLA_Q_RANK), lambda i: (i, OFF_MLA_Q // MLA_Q_RANK)),
            pl.BlockSpec((tm, MLA_KV_RANK), lambda i: (i, OFF_MLA_KV // MLA_KV_RANK)),
            pl.BlockSpec((tm, LANE), lambda i: (i, OFF_MLA_KR // LANE)),
            pl.BlockSpec((1, MLA_Q_RANK), const),
            pl.BlockSpec((1, MLA_KV_RANK), const),
            pl.BlockSpec((MLA_Q_RANK, qk_w), const),
            pl.BlockSpec((MLA_KV_RANK, MLA_HEADS * (MLA_NOPE + MLA_V)), const),
            pl.BlockSpec((tm, LANE), lambda i: (i % per_seq, 0)),
            pl.BlockSpec((tm, LANE), lambda i: (i % per_seq, 0)),
        ],
        out_specs=[
            pl.BlockSpec((tm, qk_w), lambda i: (i, 0)),
            pl.BlockSpec((tm, qk_w), lambda i: (i, 0)),
            pl.BlockSpec((tm, MLA_WIDTH), lambda i: (i, 0)),
        ],
        out_shape=[
            jax.ShapeDtypeStruct((tp, qk_w), BF16),
            jax.ShapeDtypeStruct((tp, qk_w), BF16),
            jax.ShapeDtypeStruct((tp, MLA_WIDTH), BF16),
        ],
        compiler_params=_params("parallel"),
        name="mla_proj",
    )(p, p, p, norm_q.reshape(1, -1), norm_kv.reshape(1, -1), w_uq, w_ukv, cos, sin)


def _attn_body(q_ref, k_ref, v_ref, o_ref, *, tq):
    i = pl.program_id(2)
    heads = range(ATTN_HEADS_PER_STEP)
    qs = [q_ref[:, MLA_QK_PAD * h:MLA_QK_PAD * (h + 1)] for h in heads]

    def step(off, width, masked, carry):
        ms, ls, accs = carry
        off = pl.multiple_of(off, tq)
        ss = [_dot_nt(qs[h], k_ref[pl.ds(off, width), MLA_QK_PAD * h:MLA_QK_PAD * (h + 1)]) for h in heads]
        if masked:
            row = i * tq + lax.broadcasted_iota(jnp.int32, (tq, width), 0)
            col = off + lax.broadcasted_iota(jnp.int32, (tq, width), 1)
            ss = [jnp.where(col <= row, s, -jnp.inf) for s in ss]
        m_new = [jnp.maximum(ms[h], jnp.max(ss[h], axis=-1, keepdims=True)) for h in heads]
        alpha = [jnp.exp2(ms[h] - m_new[h]) for h in heads]
        ps = [jnp.exp2(ss[h] - m_new[h]) for h in heads]
        ls = [alpha[h] * ls[h] + jnp.sum(ps[h], axis=-1, keepdims=True) for h in heads]
        pv = [_dot(ps[h].astype(BF16), v_ref[pl.ds(off, width), MLA_V * h:MLA_V * (h + 1)]) for h in heads]
        accs = [alpha[h] * accs[h] + pv[h] for h in heads]
        return tuple(m_new), tuple(ls), tuple(accs)

    carry = (tuple(jnp.full((tq, 1), -1e30, F32) for _ in heads),
             tuple(jnp.zeros((tq, 1), F32) for _ in heads),
             tuple(jnp.zeros((tq, MLA_V), F32) for _ in heads))
    n_single = (i + 1) % 2
    n_pairs = (i + 1) // 2
    carry = lax.cond(i == 0, lambda c: step(0, tq, True, c), lambda c: c, carry)
    carry = lax.cond((n_single == 1) & (i > 0), lambda c: step(0, tq, False, c), lambda c: c, carry)
    pair_off = lambda p: (n_single + 2 * p) * tq
    carry = lax.fori_loop(0, n_pairs - 1, lambda p, c: step(pair_off(p), 2 * tq, False, c), carry)
    carry = lax.cond(n_pairs > 0, lambda c: step(pair_off(n_pairs - 1), 2 * tq, True, c), lambda c: c, carry)
    _, ls, accs = carry
    for h in heads:
        o_ref[:, MLA_V * h:MLA_V * (h + 1)] = (accs[h] / ls[h]).astype(o_ref.dtype)


def mla_attention(q, k, v, batch, tq):
    tp = q.shape[0]
    lp = tp // batch
    nq = lp // tq
    hs = ATTN_HEADS_PER_STEP
    return pl.pallas_call(
        functools.partial(_attn_body, tq=tq),
        grid=(batch, MLA_HEADS // hs, nq),
        in_specs=[
            pl.BlockSpec((tq, hs * MLA_QK_PAD), lambda b, h, i: (b * nq + i, h)),
            pl.BlockSpec((lp, hs * MLA_QK_PAD), lambda b, h, i: (b, h)),
            pl.BlockSpec((lp, hs * MLA_V), lambda b, h, i: (b, h)),
        ],
        out_specs=pl.BlockSpec((tq, hs * MLA_V), lambda b, h, i: (b * nq + i, h)),
        out_shape=jax.ShapeDtypeStruct((tp, MLA_WIDTH), BF16),
        compiler_params=_params("parallel", "parallel", "arbitrary"),
        name="mla_attention",
    )(q, k, v)


def _merge_body(ya_ref, yb_ref, yc_ref, wa_ref, wb_ref, wc_ref, ga_ref, gb_ref, gc_ref, o_ref):
    def branch(y_ref, w_ref, g_ref):
        return _sigmoid(g_ref[...]) * _dot(y_ref[...], w_ref[...])

    o_ref[...] = (branch(ya_ref, wa_ref, ga_ref) + branch(yb_ref, wb_ref, gb_ref)
                  + branch(yc_ref, wc_ref, gc_ref)).astype(o_ref.dtype)


def merge_branches(ya, yb, yc, wa, wb, wc, layer, p, tm, tn):
    tp = ya.shape[0]

    def yspec(width):
        return pl.BlockSpec((tm, width), lambda i, j: (i, 0))

    def wspec(width):
        return pl.BlockSpec((None, width, tn), lambda i, j: (layer, 0, j))

    def gspec(branch):
        base = (OFF_GATE + branch * D_MODEL) // tn
        return pl.BlockSpec((tm, tn), lambda i, j: (i, base + j))

    return pl.pallas_call(
        _merge_body,
        grid=(tp // tm, D_MODEL // tn),
        in_specs=[yspec(RW_WIDTH), yspec(RET_WIDTH), yspec(MLA_WIDTH),
                  wspec(RW_WIDTH), wspec(RET_WIDTH), wspec(MLA_WIDTH),
                  gspec(0), gspec(1), gspec(2)],
        out_specs=pl.BlockSpec((tm, tn), lambda i, j: (i, j)),
        out_shape=jax.ShapeDtypeStruct((tp, D_MODEL), BF16),
        compiler_params=_params("parallel", "parallel"),
        name="merge_branches",
    )(ya, yb, yc, wa, wb, wc, p, p, p)


def _resid_body(x_ref, w_ref, h_ref, o_ref, ob_ref, ssq_ref):
    hn = h_ref[...] + _dot(x_ref[...], w_ref[...])
    o_ref[...] = hn
    ob_ref[...] = hn.astype(ob_ref.dtype)
    ssq_ref[...] = _ssq_block(hn)


def resid_matmul(x, w, layer, h, tm, tn):
    m, kdim = x.shape
    n = w.shape[2]
    tile = pl.BlockSpec((tm, tn), lambda i, j: (i, j))
    return pl.pallas_call(
        _resid_body,
        grid=(m // tm, n // tn),
        in_specs=[
            pl.BlockSpec((tm, kdim), lambda i, j: (i, 0)),
            pl.BlockSpec((None, kdim, tn), lambda i, j: (layer, 0, j)),
            tile,
        ],
        out_specs=[tile, tile, pl.BlockSpec((tm, LANE), lambda i, j: (i, j))],
        out_shape=[jax.ShapeDtypeStruct((m, n), F32), jax.ShapeDtypeStruct((m, n), BF16),
                   jax.ShapeDtypeStruct((m, LANE * (n // tn)), F32)],
        compiler_params=_params("parallel", "parallel"),
        name="resid_matmul",
    )(x, w, h)


def _ffn_up_body(x_ref, ssq_ref, wg_ref, wu_ref, o_ref):
    x = x_ref[...]
    r = _row_scale(ssq_ref, x.shape[1])
    hg = r * _dot(x, wg_ref[...])
    hu = r * _dot(x, wu_ref[...])
    o_ref[...] = (hg * _sigmoid(hg) * hu).astype(o_ref.dtype)


def ffn_up(xb, ssq, w_gate_up, layer, tm, tn):
    m, kdim = xb.shape
    hidden = w_gate_up.shape[2] // 2
    nj = hidden // tn
    return pl.pallas_call(
        _ffn_up_body,
        grid=(m // tm, nj),
        in_specs=[
            pl.BlockSpec((tm, kdim), lambda i, j: (i, 0)),
            pl.BlockSpec((tm, ssq.shape[1]), lambda i, j: (i, 0)),
            pl.BlockSpec((None, kdim, tn), lambda i, j: (layer, 0, j)),
            pl.BlockSpec((None, kdim, tn), lambda i, j: (layer, 0, nj + j)),
        ],
        out_specs=pl.BlockSpec((tm, tn), lambda i, j: (i, j)),
        out_shape=jax.ShapeDtypeStruct((m, hidden), BF16),
        compiler_params=_params("parallel", "parallel"),
        name="ffn_up",
    )(xb, ssq, w_gate_up, w_gate_up)


def _pad_cols(w, width):
    return jnp.pad(w, [(0, 0)] * (w.ndim - 1) + [(0, width - w.shape[-1])])


def _w_in_pieces():
    widths = [3 * RW_WIDTH, RW_DECAY_LORA, RW_A_LORA, RW_GATE_LORA, 4 * RET_WIDTH, MLA_Q_RANK, MLA_KV_RANK, MLA_ROPE,
              3 * D_MODEL]
    dsts = [OFF_RW, OFF_RW_WD, OFF_RW_AD, OFF_RW_GD, OFF_RET, OFF_MLA_Q, OFF_MLA_KV, OFF_MLA_KR, OFF_GATE]
    pieces, src = [], 0
    for dst, width in zip(dsts, widths):
        pieces.append((dst, src, width))
        src += width
    return pieces


def _pack_w_in_body(w_ref, g_ref, o_ref):
    g = g_ref[...]
    covered = 0
    for dst, src, width in sorted(_w_in_pieces()):
        if dst > covered:
            o_ref[:, covered:dst] = jnp.zeros((o_ref.shape[0], dst - covered), o_ref.dtype)
        o_ref[:, dst:dst + width] = (w_ref[:, src:src + width] * g).astype(o_ref.dtype)
        covered = dst + width
    o_ref[:, covered:] = jnp.zeros((o_ref.shape[0], o_ref.shape[1] - covered), o_ref.dtype)


def _pack_w_in(w_in, gain, tr=128):
    nl, rows, cols = w_in.shape
    return pl.pallas_call(
        _pack_w_in_body,
        grid=(nl, rows // tr),
        in_specs=[pl.BlockSpec((None, tr, cols), lambda l, i: (l, i, 0)),
                  pl.BlockSpec((None, tr, 1), lambda l, i: (l, i, 0))],
        out_specs=pl.BlockSpec((None, tr, P_COLS), lambda l, i: (l, i, 0)),
        out_shape=jax.ShapeDtypeStruct((nl, rows, P_COLS), BF16),
        compiler_params=_params("parallel", "parallel"),
        name="pack_w_in",
    )(w_in, gain[..., None])


def _pack_mu(mu):
    rkv = mu[..., :3 * RW_WIDTH]
    wd = mu[..., 3 * RW_WIDTH:3 * RW_WIDTH + RW_DECAY_LORA]
    ad = mu[..., 3 * RW_WIDTH + RW_DECAY_LORA:3 * RW_WIDTH + RW_DECAY_LORA + RW_A_LORA]
    gd = mu[..., 3 * RW_WIDTH + RW_DECAY_LORA + RW_A_LORA:]
    return jnp.concatenate([rkv, _pad_cols(wd, LANE), _pad_cols(ad, LANE), gd], axis=-1)


def _pad_rows(w, rows):
    return jnp.pad(w, [(0, 0)] * (w.ndim - 2) + [(0, rows - w.shape[-2]), (0, 0)])


def _pack_w_uq(w):
    nl, rank, _ = w.shape
    w = w.reshape(nl, rank, MLA_HEADS, MLA_NOPE + MLA_ROPE)
    w = jnp.pad(w, ((0, 0), (0, 0), (0, 0), (0, MLA_QK_PAD - MLA_NOPE - MLA_ROPE)))
    return w.reshape(nl, rank, MLA_HEADS * MLA_QK_PAD).astype(BF16)


def _pack_w_ukv(w):
    nl, rank, _ = w.shape
    w = w.reshape(nl, rank, MLA_HEADS, 2, MLA_NOPE)
    w = jnp.swapaxes(w, 2, 3)
    return w.reshape(nl, rank, 2 * MLA_HEADS * MLA_NOPE).astype(BF16)


def _rope_tables(lp):
    pos = jnp.arange(lp, dtype=F32)

    def tables(dim):
        inv = ROPE_BASE ** (-jnp.arange(0, dim, 2, dtype=F32) / dim)
        ang = pos[:, None] * inv[None, :]
        return jnp.cos(ang), jnp.sin(ang)

    c, s = tables(RET_HEAD_DIM)
    ret = (jnp.concatenate([c, c], axis=1), jnp.concatenate([-s, s], axis=1))
    c, s = tables(MLA_ROPE)
    z = jnp.zeros((lp, LANE - MLA_ROPE), F32)
    mla = (jnp.concatenate([c, c, z], axis=1), jnp.concatenate([-s, s, z], axis=1))
    return ret, mla


def kernel(x, meta_tokens, norm_mix, w_in, rw_mu, rw_w0, rw_w_up, rw_a0, rw_a_up, rw_g_up, rw_k_k, rw_k_a, rw_r_k, rw_ln_w, rw_ln_b, mla_norm_q, mla_norm_kv, mla_w_uq, mla_w_ukv, w_br_rwkv, w_br_ret, w_br_mla, w_out, norm_ffn, w_gate_up, w_down, final_norm):
    batch, seq, d = x.shape
    depth = w_in.shape[0]
    lp = -(-(N_META + seq) // SEQ_ALIGN) * SEQ_ALIGN
    tp = batch * lp

    meta = jnp.broadcast_to(meta_tokens[None].astype(x.dtype), (batch, N_META, d))
    pad = jnp.zeros((batch, lp - N_META - seq, d), x.dtype)
    h = jnp.concatenate([meta, x, pad], axis=1).reshape(tp, d)

    wp = _pack_w_in(w_in, norm_mix)
    mu = _pack_mu(rw_mu)
    w_up = _pad_rows(rw_w_up, LANE).astype(BF16)
    a_up = _pad_rows(rw_a_up, LANE).astype(BF16)
    g_up = rw_g_up.astype(BF16)
    wuq = _pack_w_uq(mla_w_uq)
    wukv = _pack_w_ukv(mla_w_ukv)
    wa = w_br_rwkv.astype(BF16)
    wb = w_br_ret.astype(BF16)
    wc = w_br_mla.astype(BF16)
    wo = w_out.astype(BF16)
    wgu = (w_gate_up * norm_ffn[..., None]).astype(BF16)
    wdn = w_down.astype(BF16)
    (cos_ret, sin_ret), (cos_mla, sin_mla) = _rope_tables(lp)

    def row_tile(pref):
        return next((t for t in pref if tp % t == 0), SEQ_ALIGN)

    tm = row_tile((768,))
    tm_wide = row_tile((1536, 768))
    tm_seq = 384 if lp % 384 == 0 else SEQ_ALIGN

    hb, ssq = stream_prep(h, tm)
    for l in range(depth):
        p = norm_matmul(hb, ssq, wp, l, tm_wide, P_TILE_N, F32)
        ya = rwkv_mix(p, batch, mu[l:l + 1], rw_w0[l], rw_a0[l], rw_k_k[l], rw_k_a[l], rw_r_k[l],
                      rw_ln_w[l], rw_ln_b[l], w_up[l], a_up[l], g_up[l])
        yb = retention_mix(p, batch, cos_ret, sin_ret)
        q, k, v = mla_proj(p, batch, mla_norm_q[l], mla_norm_kv[l], wuq[l], wukv[l], cos_mla, sin_mla, tm_seq)
        yc = mla_attention(q, k, v, batch, tm_seq)
        merged = merge_branches(ya, yb, yc, wa, wb, wc, l, p, tm, 1024)
        h, hb, ssq = resid_matmul(merged, wo, l, h, tm_seq, d)
        act = ffn_up(hb, ssq, wgu, l, tm, 512)
        h, hb, ssq = resid_matmul(act, wdn, l, h, tm, 512)
    out = rmsnorm(h, final_norm, tm, F32)
    return out.reshape(batch, lp, d)[:, N_META:N_META + seq]
```

```python
import functools
import math

import jax
import jax.numpy as jnp
from jax import lax
from jax.experimental import pallas as pl
from jax.experimental.pallas import tpu as pltpu

F32 = jnp.float32
BF16 = jnp.bfloat16

D_MODEL = 2048
N_META = 16
NORM_EPS = 1e-6
ROPE_BASE = 10000.0

RW_HEADS = 16
RW_HEAD_DIM = 64
RW_WIDTH = RW_HEADS * RW_HEAD_DIM
RW_DECAY_LORA = 96
RW_A_LORA = 96
RW_GATE_LORA = 256
RW_GN_EPS = RW_HEAD_DIM * 1e-5
RW_CHUNK = 64
RW_SUB = 2
RW_PAIRS = RW_WIDTH // 128

RET_HEADS = 8
RET_HEAD_DIM = 128
RET_WIDTH = RET_HEADS * RET_HEAD_DIM
RET_CHUNK = 128
RET_SUB = 3

MLA_HEADS = 8
MLA_NOPE = 128
MLA_ROPE = 64
MLA_V = 128
MLA_Q_RANK = 512
MLA_KV_RANK = 256
MLA_WIDTH = MLA_HEADS * MLA_V
MLA_QK_PAD = 256
ATTN_HEADS_PER_STEP = 2

FFN_HIDDEN = -(-8 * D_MODEL // (3 * 256)) * 256

LANE = 128
SEQ_ALIGN = 128

OFF_RET = 0
OFF_GATE = OFF_RET + 4 * RET_WIDTH
OFF_RW = OFF_GATE + 3 * D_MODEL
OFF_RW_WD = OFF_RW + 3 * RW_WIDTH
OFF_RW_AD = OFF_RW_WD + LANE
OFF_RW_GD = OFF_RW_AD + LANE
OFF_MLA_Q = OFF_RW_GD + RW_GATE_LORA
OFF_MLA_KV = OFF_MLA_Q + MLA_Q_RANK
OFF_MLA_KR = OFF_MLA_KV + MLA_KV_RANK
P_COLS_USED = OFF_MLA_KR + LANE
P_TILE_N = 512
P_COLS = -(-P_COLS_USED // P_TILE_N) * P_TILE_N

VMEM_LIMIT = 48 * 1024 * 1024


def _params(*sem):
    return pltpu.CompilerParams(dimension_semantics=sem, vmem_limit_bytes=VMEM_LIMIT)


def _sigmoid(x):
    return 1.0 / (1.0 + jnp.exp(-x))


def _dot(a, b):
    return jnp.dot(a, b, preferred_element_type=F32)


def _dot_nt(a, b):
    return lax.dot_general(a, b, (((1,), (1,)), ((), ())), preferred_element_type=F32)


def _dot_tn(a, b):
    return lax.dot_general(a, b, (((0,), (0,)), ((), ())), preferred_element_type=F32)


def _rmsnorm_body(x_ref, g_ref, o_ref):
    x = x_ref[...]
    y = x * lax.rsqrt(jnp.mean(x * x, axis=-1, keepdims=True) + NORM_EPS)
    o_ref[...] = (y * g_ref[...]).astype(o_ref.dtype)


def final_rmsnorm(x, g, batch, first, count, tm):
    m, d = x.shape
    rows_per_batch = m // batch
    assert rows_per_batch % 8 == 0 and first % 8 == 0 and tm % 8 == 0
    tiles = count // tm
    return pl.pallas_call(
        _rmsnorm_body,
        grid=(batch, tiles),
        in_specs=[pl.BlockSpec((pl.Element(tm), pl.Element(d)),
                               lambda b, t: (pl.multiple_of(b * rows_per_batch + first + t * tm, 8), 0)),
                  pl.BlockSpec((1, d), lambda b, t: (0, 0))],
        out_specs=pl.BlockSpec((tm, d), lambda b, t: (b * tiles + t, 0)),
        out_shape=jax.ShapeDtypeStruct((batch * count, d), x.dtype),
        compiler_params=_params("parallel", "parallel"),
        name="final_rmsnorm",
    )(x, g.reshape(1, d))


def _row_scale(ssq_ref, d):
    ssq = ssq_ref[...]
    total = ssq[:, 0:1]
    for j in range(1, ssq.shape[1] // LANE):
        total = total + ssq[:, LANE * j:LANE * j + 1]
    return lax.rsqrt(total * (1.0 / d) + NORM_EPS)


def _ssq_block(x):
    return jnp.broadcast_to(jnp.sum(x * x, axis=-1, keepdims=True), (x.shape[0], LANE))


def _stream_prep_body(x_ref, xb_ref, ssq_ref):
    x = x_ref[...]
    xb_ref[...] = x.astype(xb_ref.dtype)
    ssq_ref[...] = _ssq_block(x)


def stream_prep(x, tm):
    m, d = x.shape
    return pl.pallas_call(
        _stream_prep_body,
        grid=(m // tm,),
        in_specs=[pl.BlockSpec((tm, d), lambda i: (i, 0))],
        out_specs=[pl.BlockSpec((tm, d), lambda i: (i, 0)), pl.BlockSpec((tm, LANE), lambda i: (i, 0))],
        out_shape=[jax.ShapeDtypeStruct((m, d), BF16), jax.ShapeDtypeStruct((m, LANE), F32)],
        compiler_params=_params("parallel"),
        name="stream_prep",
    )(x)


def _norm_matmul_body(x_ref, ssq_ref, w_ref, o_ref):
    o_ref[...] = (_row_scale(ssq_ref, x_ref.shape[1]) * _dot(x_ref[...], w_ref[...])).astype(o_ref.dtype)


def norm_matmul(xb, ssq, w, layer, tm, tn, out_dtype):
    m, k = xb.shape
    n = w.shape[2]
    return pl.pallas_call(
        _norm_matmul_body,
        grid=(m // tm, n // tn),
        in_specs=[pl.BlockSpec((tm, k), lambda i, j: (i, 0)), pl.BlockSpec((tm, ssq.shape[1]), lambda i, j: (i, 0)),
                  pl.BlockSpec((None, k, tn), lambda i, j: (layer, 0, j))],
        out_specs=pl.BlockSpec((tm, tn), lambda i, j: (i, j)),
        out_shape=jax.ShapeDtypeStruct((m, n), out_dtype),
        compiler_params=_params("parallel", "parallel"),
        name="in_proj",
    )(xb, ssq, w)


def _rwkv_body(r_ref, k_ref, v_ref, wd_ref, ad_ref, gd_ref, mu_ref,
               w0_ref, a0_ref, kk_ref, ka_ref, rk_ref, lnw_ref, lnb_ref,
               wup_ref, aup_ref, gup_ref,
               o_ref,
               s_ref, pr_ref, pk_ref, pv_ref, pwd_ref, pad_ref, pgd_ref):
    c = pl.program_id(1)
    C = RW_CHUNK
    RS = RW_SUB * C
    HD = RW_HEAD_DIM

    @pl.when(c == 0)
    def _():
        s_ref[...] = jnp.zeros_like(s_ref)
        pr_ref[...] = jnp.zeros_like(pr_ref)
        pk_ref[...] = jnp.zeros_like(pk_ref)
        pv_ref[...] = jnp.zeros_like(pv_ref)
        pwd_ref[...] = jnp.zeros_like(pwd_ref)
        pad_ref[...] = jnp.zeros_like(pad_ref)
        pgd_ref[...] = jnp.zeros_like(pgd_ref)

    def shift(x_ref, prev_ref, mu, sl):
        z = x_ref[:, sl]
        first = lax.broadcasted_iota(jnp.int32, z.shape, 0) == 0
        zs = jnp.where(first, prev_ref[0:1, sl], pltpu.roll(z, 1, 0))
        prev_ref[0:1, sl] = z[RS - 1:RS, :]
        return z + (zs - z) * mu

    mu_lora = 3 * RW_WIDTH
    full = slice(None)
    wd = shift(wd_ref, pwd_ref, mu_ref[:, mu_lora:mu_lora + LANE], full)
    ad = shift(ad_ref, pad_ref, mu_ref[:, mu_lora + LANE:mu_lora + 2 * LANE], full)
    gd = shift(gd_ref, pgd_ref, mu_ref[:, mu_lora + 2 * LANE:], full)
    subs = range(RW_SUB)

    def rows(t, s):
        return t[C * s:C * (s + 1)]

    tanh_wd = [rows(jnp.tanh(wd), s).astype(BF16) for s in subs]
    ad_b = [rows(ad, s).astype(BF16) for s in subs]
    sig_gd = [rows(_sigmoid(gd), s).astype(BF16) for s in subs]

    lane_sq = lax.broadcasted_iota(jnp.int32, (LANE, LANE), 1)
    row_sq = lax.broadcasted_iota(jnp.int32, (LANE, LANE), 0)
    same_head = (lane_sq < HD) == (row_sq < HD)
    head_ones = same_head.astype(BF16)
    eye = (lane_sq == row_sq).astype(F32)
    rc = lax.broadcasted_iota(jnp.int32, (C, C), 0)
    cc = lax.broadcasted_iota(jnp.int32, (C, C), 1)
    tril_incl = (cc <= rc).astype(BF16)
    lane_tall = lax.broadcasted_iota(jnp.int32, (2 * C, LANE), 1)
    lane_c = lax.broadcasted_iota(jnp.int32, (C, LANE), 1)
    row_c = lax.broadcasted_iota(jnp.int32, (C, LANE), 0)
    head0_c = lane_c < HD
    strict_lo = lane_c < row_c
    strict_hi = (lane_c >= C) & (lane_c - C < row_c)
    incl_lo = lane_c <= row_c
    incl_hi = (lane_c >= C) & (lane_c - C <= row_c)
    inv_n = 1.0 / HD

    P = range(RW_PAIRS)

    def head_sums(ts):
        his = [t.astype(BF16) for t in ts]
        los = [(t - hi.astype(F32)).astype(BF16) for t, hi in zip(ts, his)]
        out = _dot(jnp.concatenate(his + los, axis=0), head_ones)
        n = len(ts)
        return [out[C * i:C * (i + 1)] + out[C * (n + i):C * (n + i + 1)] for i in range(n)]

    pair_sls = [slice(LANE * i, LANE * (i + 1)) for i in P]

    def each(fn, *lists):
        return [fn(*args) for args in zip(*lists)]

    def items(per_pair):
        return [rows(t, s) for s in subs for t in per_pair]

    r = items([shift(r_ref, pr_ref, mu_ref[:, sl], sl) for sl in pair_sls])
    k = items([shift(k_ref, pk_ref, mu_ref[:, RW_WIDTH + sl.start:RW_WIDTH + sl.stop], sl) for sl in pair_sls])
    v = items([shift(v_ref, pv_ref, mu_ref[:, 2 * RW_WIDTH + sl.start:2 * RW_WIDTH + sl.stop], sl)
               for sl in pair_sls])
    sls = pair_sls * RW_SUB
    sub_of = [s for s in subs for _ in P]

    def log_decay(s, sl):
        x = -(w0_ref[:, sl] + _dot(tanh_wd[s], wup_ref[:, sl]))
        softplus = jnp.maximum(x, 0.0) + jnp.log1p(jnp.exp(-jnp.abs(x)))
        return -jnp.exp(-softplus - 0.5)

    lw = each(log_decay, sub_of, sls)
    a = each(lambda s, sl: _sigmoid(a0_ref[:, sl] + _dot(ad_b[s], aup_ref[:, sl])), sub_of, sls)
    g = each(lambda s, sl: _dot(sig_gd[s], gup_ref[:, sl]), sub_of, sls)

    kkr = each(lambda ki, sl: ki * kk_ref[:, sl], k, sls)
    ksq = head_sums(each(lambda t: t * t, kkr))
    kkn = each(lambda t, ss: t / jnp.maximum(jnp.sqrt(ss), 1e-12), kkr, ksq)
    kmod = each(lambda ki, ai, sl: ki * (1.0 + (ai - 1.0) * ka_ref[:, sl]), k, a, sls)
    beta = each(lambda ai, t: ai * t, a, kkn)

    def running_sum(lwi):
        hi = lwi.astype(BF16)
        both = _dot(tril_incl, jnp.concatenate([hi, (lwi - hi.astype(F32)).astype(BF16)], axis=1))
        return both[:, :LANE] + both[:, LANE:]

    lcum = each(running_sum, lw)
    lend = each(lambda t: t[C - 1:C, :], lcum)
    rh = each(lambda ri, lc: ri * jnp.exp(lc), r, lcum)
    kh = each(lambda t, lc, lwi: t * jnp.exp(lc - lwi), kkn, lcum, lw)
    e_neg = each(lambda lc: jnp.exp(-lc), lcum)
    e_end = each(lambda le, lc: jnp.exp(le - lc), lend, lcum)
    kb = each(lambda t, e: t * e, kmod, e_neg)
    bb = each(lambda t, e: t * e, beta, e_neg)
    kbe = each(lambda t, e: t * e, kmod, e_end)
    bbe = each(lambda t, e: t * e, beta, e_end)

    kr_f = each(lambda x1, x2: jnp.concatenate([x1, x2], axis=0), kh, rh)
    bk = each(lambda x1, x2: jnp.concatenate([x1, x2], axis=0).astype(BF16), bb, kb)
    vb = each(lambda t: t.astype(BF16), v)

    def gram(krf, bki):
        kr2 = jnp.concatenate([jnp.where(lane_tall < HD, krf, 0.0), jnp.where(lane_tall >= HD, krf, 0.0)], axis=0)
        return _dot_nt(kr2.astype(BF16), bki)

    g_all = each(gram, kr_f, bk)

    n_bd = each(lambda ga: jnp.concatenate([jnp.where(strict_lo, -ga[0:C], 0.0),
                                            jnp.where(strict_hi, -pltpu.roll(ga[2 * C:3 * C], C, 1), 0.0)], axis=0),
                g_all)
    t = each(lambda n: eye + n, n_bd)
    pw = each(lambda n: _dot(n.astype(BF16), n.astype(BF16)), n_bd)
    for _ in range(4):
        both = each(lambda ti, pi: _dot(jnp.concatenate([ti, pi], axis=0).astype(BF16), pi.astype(BF16)), t, pw)
        t = each(lambda ti, bi: ti + bi[:LANE], t, both)
        pw = each(lambda bi: bi[LANE:], both)
    t = each(lambda ti, pi: ti + _dot(ti.astype(BF16), pi.astype(BF16)), t, pw)

    def intra_rhs(ga, vbi):
        m1s = jnp.concatenate([jnp.where(strict_hi, ga[0:C], 0.0), jnp.where(strict_hi, ga[2 * C:3 * C], 0.0)], axis=0)
        return _dot(m1s.astype(BF16), jnp.concatenate([vbi, vbi], axis=0))

    q_intra = each(intra_rhs, g_all, vb)

    def m2(gb):
        return jnp.where(incl_lo, -gb, jnp.where(incl_hi, gb, 0.0))

    m2s = each(lambda ga: jnp.concatenate([m2(ga[C:2 * C]), m2(ga[3 * C:4 * C])], axis=0).astype(BF16), g_all)
    kbe_all = each(lambda kbei, bbei: jnp.concatenate([kbei, -bbei], axis=0).astype(BF16), kbe, bbe)
    s_decay = each(jnp.exp, lend)

    state = [s_ref[i] for i in P]
    y = []
    for s in subs:
        of = lambda lst: lst[RW_PAIRS * s:RW_PAIRS * (s + 1)]
        p_all = each(lambda x1, si: _dot_nt(x1.astype(BF16), si.astype(BF16)), of(kr_f), state)
        q_s = each(lambda qi, pa: jnp.where(same_head, qi + jnp.concatenate([pa[:C], pa[:C]], axis=0), 0.0),
                   of(q_intra), p_all)
        u_s = each(lambda ti, qi: _dot(ti.astype(BF16), qi.astype(BF16)), of(t), q_s)
        u = each(lambda us: us[:C] + us[C:], u_s)
        y_s = each(lambda mi, ui, vi: _dot(mi, jnp.concatenate([ui, vi], axis=0).astype(BF16)), of(m2s), u, of(v))
        y += each(lambda pa, ys: pa[C:] + jnp.where(head0_c, ys[:C], ys[C:]), p_all, y_s)
        ds = each(lambda vi, ui, kb_all: _dot_tn(jnp.concatenate([vi, ui], axis=0).astype(BF16), kb_all),
                  of(v), u, of(kbe_all))
        state = each(lambda si, di, dec: si * dec + jnp.where(same_head, di, 0.0), state, ds, of(s_decay))
    for i in P:
        s_ref[i] = state[i]

    d = each(lambda yi, si: yi - si * inv_n, y, head_sums(y))
    var = each(lambda si: si * inv_n, head_sums(each(lambda di: di * di, d)))
    bsum = head_sums(each(lambda ri, ki, sl: ri * ki * rk_ref[:, sl], r, kmod, sls))
    for i, (s, sl) in enumerate(zip(sub_of, sls)):
        yn = d[i] * lax.rsqrt(var[i] + RW_GN_EPS) * lnw_ref[:, sl] + lnb_ref[:, sl]
        o_ref[C * s:C * (s + 1), sl] = ((yn + bsum[i] * v[i]) * g[i]).astype(o_ref.dtype)


def rwkv_mix(p, batch, mu, w0, a0, k_k, k_a, r_k, ln_w, ln_b, w_up, a_up, g_up):
    tp = p.shape[0]
    lp = tp // batch
    C = RW_CHUNK * RW_SUB
    nchunk = lp // C

    def pspec(width, base):
        return pl.BlockSpec((C, width), lambda b, c: (b * nchunk + c, base // width))

    def const(shape):
        return pl.BlockSpec(shape, lambda b, c: (0, 0))

    in_specs = [
        pspec(RW_WIDTH, OFF_RW), pspec(RW_WIDTH, OFF_RW + RW_WIDTH), pspec(RW_WIDTH, OFF_RW + 2 * RW_WIDTH),
        pspec(LANE, OFF_RW_WD), pspec(LANE, OFF_RW_AD), pspec(RW_GATE_LORA, OFF_RW_GD),
        const(mu.shape),
    ] + [const((1, RW_WIDTH))] * 7 + [const(w_up.shape), const(a_up.shape), const(g_up.shape)]
    row = lambda t: t.reshape(1, -1)
    return pl.pallas_call(
        _rwkv_body,
        grid=(batch, nchunk),
        in_specs=in_specs,
        out_specs=pl.BlockSpec((C, RW_WIDTH), lambda b, c: (b * nchunk + c, 0)),
        out_shape=jax.ShapeDtypeStruct((tp, RW_WIDTH), BF16),
        scratch_shapes=[pltpu.VMEM((RW_PAIRS, LANE, LANE), F32)] + [pltpu.VMEM((8, RW_WIDTH), F32)] * 3
        + [pltpu.VMEM((8, LANE), F32)] * 2 + [pltpu.VMEM((8, RW_GATE_LORA), F32)],
        compiler_params=_params("parallel", "arbitrary"),
        name="rwkv7_mix",
    )(p, p, p, p, p, p, mu, row(w0), row(a0), row(k_k), row(k_a), row(r_k), row(ln_w), row(ln_b),
      w_up, a_up, g_up)


def _ret_body(q_ref, k_ref, v_ref, g_ref, cos_ref, sin_ref, o_ref, state_ref):
    c = pl.program_id(1)
    C = RET_CHUNK
    d = RET_HEAD_DIM

    @pl.when(c == 0)
    def _():
        state_ref[...] = jnp.zeros_like(state_ref)

    row = lax.broadcasted_iota(jnp.int32, (C, C), 0).astype(F32)
    col = lax.broadcasted_iota(jnp.int32, (C, C), 1).astype(F32)
    diff = row - col
    causal = diff >= 0
    heads = range(RET_HEADS)
    subs = range(RET_SUB)
    rws = [slice(C * s, C * (s + 1)) for s in subs for _ in heads]
    sls = [slice(d * h, d * (h + 1)) for _ in subs for h in heads]
    lgs = [math.log1p(-(2.0 ** (-5.0 - h))) for _ in subs for h in heads]

    def each(fn, *lists):
        return [fn(*args) for args in zip(*lists)]

    def rope(x_ref, rw, sl):
        x = x_ref[rw, sl]
        return x * cos_ref[rw, :] + pltpu.roll(x, d // 2, 1) * sin_ref[rw, :]

    qb = each(lambda rw, sl: rope(q_ref, rw, sl).astype(BF16), rws, sls)
    k = each(lambda rw, sl: rope(k_ref, rw, sl) * (d ** -0.5), rws, sls)
    kb = each(lambda t: t.astype(BF16), k)
    vb = each(lambda rw, sl: v_ref[rw, sl].astype(BF16), rws, sls)
    s = each(lambda qi, ki, lg: _dot_nt(qi, ki) * jnp.where(causal, jnp.exp(lg * jnp.maximum(diff, 0.0)), 0.0),
             qb, kb, lgs)
    o_intra = each(lambda si, vi: _dot(si.astype(BF16), vi), s, vb)
    kd = each(lambda ki, lg: (ki * jnp.exp(lg * (C - 1.0 - row))).astype(BF16), k, lgs)
    kv = each(_dot_tn, kd, vb)

    state = [state_ref[h] for h in heads]
    o = []
    for sub in subs:
        of = lambda lst: lst[RET_HEADS * sub:RET_HEADS * (sub + 1)]
        o += each(lambda oi, qi, st, lg: oi + _dot(qi, st.astype(BF16)) * jnp.exp(lg * (row + 1.0)),
                  of(o_intra), of(qb), state, of(lgs))
        state = each(lambda st, kvi, lg: st * math.exp(lg * C) + kvi, state, of(kv), of(lgs))
    for h in heads:
        state_ref[h] = state[h]
    o = each(lambda oi: oi * lax.rsqrt(jnp.mean(oi * oi, axis=-1, keepdims=True) + NORM_EPS), o)
    for oi, rw, sl in zip(o, rws, sls):
        g = g_ref[rw, sl]
        o_ref[rw, sl] = (g * _sigmoid(g) * oi).astype(o_ref.dtype)


def retention_mix(p, batch, cos, sin):
    tp = p.shape[0]
    lp = tp // batch
    C = RET_CHUNK * RET_SUB
    nchunk = lp // C
    base = OFF_RET // RET_WIDTH

    def pspec(j):
        return pl.BlockSpec((C, RET_WIDTH), lambda b, c: (b * nchunk + c, base + j))

    tab = pl.BlockSpec((C, RET_HEAD_DIM), lambda b, c: (c, 0))
    return pl.pallas_call(
        _ret_body,
        grid=(batch, nchunk),
        in_specs=[pspec(0), pspec(1), pspec(2), pspec(3), tab, tab],
        out_specs=pl.BlockSpec((C, RET_WIDTH), lambda b, c: (b * nchunk + c, 0)),
        out_shape=jax.ShapeDtypeStruct((tp, RET_WIDTH), BF16),
        scratch_shapes=[pltpu.VMEM((RET_HEADS, RET_HEAD_DIM, RET_HEAD_DIM), F32)],
        compiler_params=_params("parallel", "arbitrary"),
        name="retention_mix",
    )(p, p, p, p, cos, sin)


def _mla_proj_body(qd_ref, kvd_ref, krd_ref, nq_ref, nkv_ref, wuq_ref, wukv_ref, cos_ref, sin_ref,
                   q_out, k_out, v_out):
    cos = cos_ref[...]
    sin = sin_ref[...]

    def rope(x):
        return x * cos + (pltpu.roll(x, MLA_ROPE // 2, 1) + pltpu.roll(x, LANE - MLA_ROPE // 2, 1)) * sin

    def norm(x, g):
        return x * lax.rsqrt(jnp.mean(x * x, axis=-1, keepdims=True) + NORM_EPS) * g

    scale = (MLA_NOPE + MLA_ROPE) ** -0.5 * math.log2(math.e)
    q = _dot(norm(qd_ref[...], nq_ref[...]).astype(BF16), wuq_ref[...]) * scale
    kv = _dot(norm(kvd_ref[...], nkv_ref[...]).astype(BF16), wukv_ref[...])
    kr = rope(krd_ref[...]).astype(k_out.dtype)
    for h in range(MLA_HEADS):
        lo = MLA_QK_PAD * h
        q_out[:, lo:lo + LANE] = q[:, lo:lo + LANE].astype(q_out.dtype)
        q_out[:, lo + LANE:lo + 2 * LANE] = rope(q[:, lo + LANE:lo + 2 * LANE]).astype(q_out.dtype)
        k_out[:, lo:lo + LANE] = kv[:, MLA_NOPE * h:MLA_NOPE * (h + 1)].astype(k_out.dtype)
        k_out[:, lo + LANE:lo + 2 * LANE] = kr
    v_out[...] = kv[:, MLA_HEADS * MLA_NOPE:].astype(v_out.dtype)


def mla_proj(p, batch, norm_q, norm_kv, w_uq, w_ukv, cos, sin, tm):
    tp = p.shape[0]
    lp = tp // batch
    per_seq = lp // tm
    qk_w = MLA_HEADS * MLA_QK_PAD
    const = lambda i: (0, 0)
    return pl.pallas_call(
        _mla_proj_body,
        grid=(tp // tm,),
        in_specs=[
            pl.BlockSpec((tm, MLA_Q_RANK), lambda i: (i, OFF_MLA_Q // MLA_Q_RANK)),
            pl.BlockSpec((tm, MLA_KV_RANK), lambda i: (i, OFF_MLA_KV // MLA_KV_RANK)),
            pl.BlockSpec((tm, LANE), lambda i: (i, OFF_MLA_KR // LANE)),
            pl.BlockSpec((1, MLA_Q_RANK), const),
            pl.BlockSpec((1, MLA_KV_RANK), const),
            pl.BlockSpec((MLA_Q_RANK, qk_w), const),
            pl.BlockSpec((MLA_KV_RANK, MLA_HEADS * (MLA_NOPE + MLA_V)), const),
            pl.BlockSpec((tm, LANE), lambda i: (i % per_seq, 0)),
            pl.BlockSpec((tm, LANE), lambda i: (i % per_seq, 0)),
        ],
        out_specs=[
            pl.BlockSpec((tm, qk_w), lambda i: (i, 0)),
            pl.BlockSpec((tm, qk_w), lambda i: (i, 0)),
            pl.BlockSpec((tm, MLA_WIDTH), lambda i: (i, 0)),
        ],
        out_shape=[
            jax.ShapeDtypeStruct((tp, qk_w), BF16),
            jax.ShapeDtypeStruct((tp, qk_w), BF16),
            jax.ShapeDtypeStruct((tp, MLA_WIDTH), BF16),
        ],
        compiler_params=_params("parallel"),
        name="mla_proj",
    )(p, p, p, norm_q.reshape(1, -1), norm_kv.reshape(1, -1), w_uq, w_ukv, cos, sin)


def _attn_body(q_ref, k_ref, v_ref, o_ref, *, tq):
    i = pl.program_id(2)
    heads = range(ATTN_HEADS_PER_STEP)
    qs = [q_ref[:, MLA_QK_PAD * h:MLA_QK_PAD * (h + 1)] for h in heads]

    def step(off, width, masked, carry):
        ms, ls, accs = carry
        off = pl.multiple_of(off, tq)
        ss = [_dot_nt(qs[h], k_ref[pl.ds(off, width), MLA_QK_PAD * h:MLA_QK_PAD * (h + 1)]) for h in heads]
        if masked:
            row = i * tq + lax.broadcasted_iota(jnp.int32, (tq, width), 0)
            col = off + lax.broadcasted_iota(jnp.int32, (tq, width), 1)
            ss = [jnp.where(col <= row, s, -jnp.inf) for s in ss]
        m_new = [jnp.maximum(ms[h], jnp.max(ss[h], axis=-1, keepdims=True)) for h in heads]
        alpha = [jnp.exp2(ms[h] - m_new[h]) for h in heads]
        ps = [jnp.exp2(ss[h] - m_new[h]) for h in heads]
        ls = [alpha[h] * ls[h] + jnp.sum(ps[h], axis=-1, keepdims=True) for h in heads]
        pv = [_dot(ps[h].astype(BF16), v_ref[pl.ds(off, width), MLA_V * h:MLA_V * (h + 1)]) for h in heads]
        accs = [alpha[h] * accs[h] + pv[h] for h in heads]
        return tuple(m_new), tuple(ls), tuple(accs)

    carry = (tuple(jnp.full((tq, 1), -1e30, F32) for _ in heads),
             tuple(jnp.zeros((tq, 1), F32) for _ in heads),
             tuple(jnp.zeros((tq, MLA_V), F32) for _ in heads))
    n_single = (i + 1) % 2
    n_pairs = (i + 1) // 2
    carry = lax.cond(i == 0, lambda c: step(0, tq, True, c), lambda c: c, carry)
    carry = lax.cond((n_single == 1) & (i > 0), lambda c: step(0, tq, False, c), lambda c: c, carry)
    pair_off = lambda p: (n_single + 2 * p) * tq
    carry = lax.fori_loop(0, n_pairs - 1, lambda p, c: step(pair_off(p), 2 * tq, False, c), carry)
    carry = lax.cond(n_pairs > 0, lambda c: step(pair_off(n_pairs - 1), 2 * tq, True, c), lambda c: c, carry)
    _, ls, accs = carry
    for h in heads:
        o_ref[:, MLA_V * h:MLA_V * (h + 1)] = (accs[h] / ls[h]).astype(o_ref.dtype)


def mla_attention(q, k, v, batch, tq):
    tp = q.shape[0]
    lp = tp // batch
    nq = lp // tq
    hs = ATTN_HEADS_PER_STEP
    return pl.pallas_call(
        functools.partial(_attn_body, tq=tq),
        grid=(batch, MLA_HEADS // hs, nq),
        in_specs=[
            pl.BlockSpec((tq, hs * MLA_QK_PAD), lambda b, h, i: (b * nq + i, h)),
            pl.BlockSpec((lp, hs * MLA_QK_PAD), lambda b, h, i: (b, h)),
            pl.BlockSpec((lp, hs * MLA_V), lambda b, h, i: (b, h)),
        ],
        out_specs=pl.BlockSpec((tq, hs * MLA_V), lambda b, h, i: (b * nq + i, h)),
        out_shape=jax.ShapeDtypeStruct((tp, MLA_WIDTH), BF16),
        compiler_params=_params("parallel", "parallel", "arbitrary"),
        name="mla_attention",
    )(q, k, v)


def _merge_body(ya_ref, yb_ref, yc_ref, wa_ref, wb_ref, wc_ref, ga_ref, gb_ref, gc_ref, o_ref):
    def branch(y_ref, w_ref, g_ref):
        return _sigmoid(g_ref[...]) * _dot(y_ref[...], w_ref[...])

    o_ref[...] = (branch(ya_ref, wa_ref, ga_ref) + branch(yb_ref, wb_ref, gb_ref)
                  + branch(yc_ref, wc_ref, gc_ref)).astype(o_ref.dtype)


def merge_branches(ya, yb, yc, wa, wb, wc, layer, p, tm, tn):
    tp = ya.shape[0]

    def yspec(width):
        return pl.BlockSpec((tm, width), lambda i, j: (i, 0))

    def wspec(width):
        return pl.BlockSpec((None, width, tn), lambda i, j: (layer, 0, j))

    def gspec(branch):
        base = (OFF_GATE + branch * D_MODEL) // tn
        return pl.BlockSpec((tm, tn), lambda i, j: (i, base + j))

    return pl.pallas_call(
        _merge_body,
        grid=(tp // tm, D_MODEL // tn),
        in_specs=[yspec(RW_WIDTH), yspec(RET_WIDTH), yspec(MLA_WIDTH),
                  wspec(RW_WIDTH), wspec(RET_WIDTH), wspec(MLA_WIDTH),
                  gspec(0), gspec(1), gspec(2)],
        out_specs=pl.BlockSpec((tm, tn), lambda i, j: (i, j)),
        out_shape=jax.ShapeDtypeStruct((tp, D_MODEL), BF16),
        compiler_params=_params("parallel", "parallel"),
        name="merge_branches",
    )(ya, yb, yc, wa, wb, wc, p, p, p)


def _resid_body(x_ref, w_ref, h_ref, o_ref, ob_ref, ssq_ref):
    hn = h_ref[...] + _dot(x_ref[...], w_ref[...])
    o_ref[...] = hn
    ob_ref[...] = hn.astype(ob_ref.dtype)
    ssq_ref[...] = _ssq_block(hn)


def resid_matmul(x, w, layer, h, tm, tn):
    m, kdim = x.shape
    n = w.shape[2]
    tile = pl.BlockSpec((tm, tn), lambda i, j: (i, j))
    return pl.pallas_call(
        _resid_body,
        grid=(m // tm, n // tn),
        in_specs=[
            pl.BlockSpec((tm, kdim), lambda i, j: (i, 0)),
            pl.BlockSpec((None, kdim, tn), lambda i, j: (layer, 0, j)),
            tile,
        ],
        out_specs=[tile, tile, pl.BlockSpec((tm, LANE), lambda i, j: (i, j))],
        out_shape=[jax.ShapeDtypeStruct((m, n), F32), jax.ShapeDtypeStruct((m, n), BF16),
                   jax.ShapeDtypeStruct((m, LANE * (n // tn)), F32)],
        compiler_params=_params("parallel", "parallel"),
        name="resid_matmul",
    )(x, w, h)


def _ffn_up_body(x_ref, ssq_ref, wg_ref, wu_ref, o_ref):
    x = x_ref[...]
    r = _row_scale(ssq_ref, x.shape[1])
    hg = r * _dot(x, wg_ref[...])
    hu = r * _dot(x, wu_ref[...])
    o_ref[...] = (hg * _sigmoid(hg) * hu).astype(o_ref.dtype)


def ffn_up(xb, ssq, w_gate_up, layer, tm, tn):
    m, kdim = xb.shape
    hidden = w_gate_up.shape[2] // 2
    nj = hidden // tn
    return pl.pallas_call(
        _ffn_up_body,
        grid=(m // tm, nj),
        in_specs=[
            pl.BlockSpec((tm, kdim), lambda i, j: (i, 0)),
            pl.BlockSpec((tm, ssq.shape[1]), lambda i, j: (i, 0)),
            pl.BlockSpec((None, kdim, tn), lambda i, j: (layer, 0, j)),
            pl.BlockSpec((None, kdim, tn), lambda i, j: (layer, 0, nj + j)),
        ],
        out_specs=pl.BlockSpec((tm, tn), lambda i, j: (i, j)),
        out_shape=jax.ShapeDtypeStruct((m, hidden), BF16),
        compiler_params=_params("parallel", "parallel"),
        name="ffn_up",
    )(xb, ssq, w_gate_up, w_gate_up)


def _pad_cols(w, width):
    return jnp.pad(w, [(0, 0)] * (w.ndim - 1) + [(0, width - w.shape[-1])])


def _w_in_pieces():
    widths = [3 * RW_WIDTH, RW_DECAY_LORA, RW_A_LORA, RW_GATE_LORA, 4 * RET_WIDTH, MLA_Q_RANK, MLA_KV_RANK, MLA_ROPE,
              3 * D_MODEL]
    dsts = [OFF_RW, OFF_RW_WD, OFF_RW_AD, OFF_RW_GD, OFF_RET, OFF_MLA_Q, OFF_MLA_KV, OFF_MLA_KR, OFF_GATE]
    pieces, src = [], 0
    for dst, width in zip(dsts, widths):
        pieces.append((dst, src, width))
        src += width
    return pieces


def _pack_w_in_body(w_ref, g_ref, o_ref):
    g = g_ref[...]
    covered = 0
    for dst, src, width in sorted(_w_in_pieces()):
        if dst > covered:
            o_ref[:, covered:dst] = jnp.zeros((o_ref.shape[0], dst - covered), o_ref.dtype)
        o_ref[:, dst:dst + width] = (w_ref[:, src:src + width] * g).astype(o_ref.dtype)
        covered = dst + width
    o_ref[:, covered:] = jnp.zeros((o_ref.shape[0], o_ref.shape[1] - covered), o_ref.dtype)


def _pack_w_in(w_in, gain, tr=128):
    nl, rows, cols = w_in.shape
    return pl.pallas_call(
        _pack_w_in_body,
        grid=(nl, rows // tr),
        in_specs=[pl.BlockSpec((None, tr, cols), lambda l, i: (l, i, 0)),
                  pl.BlockSpec((None, tr, 1), lambda l, i: (l, i, 0))],
        out_specs=pl.BlockSpec((None, tr, P_COLS), lambda l, i: (l, i, 0)),
        out_shape=jax.ShapeDtypeStruct((nl, rows, P_COLS), BF16),
        compiler_params=_params("parallel", "parallel"),
        name="pack_w_in",
    )(w_in, gain[..., None])


def _pack_mu(mu):
    rkv = mu[..., :3 * RW_WIDTH]
    wd = mu[..., 3 * RW_WIDTH:3 * RW_WIDTH + RW_DECAY_LORA]
    ad = mu[..., 3 * RW_WIDTH + RW_DECAY_LORA:3 * RW_WIDTH + RW_DECAY_LORA + RW_A_LORA]
    gd = mu[..., 3 * RW_WIDTH + RW_DECAY_LORA + RW_A_LORA:]
    return jnp.concatenate([rkv, _pad_cols(wd, LANE), _pad_cols(ad, LANE), gd], axis=-1)


def _pad_rows(w, rows):
    return jnp.pad(w, [(0, 0)] * (w.ndim - 2) + [(0, rows - w.shape[-2]), (0, 0)])


def _pack_w_uq(w):
    nl, rank, _ = w.shape
    w = w.reshape(nl, rank, MLA_HEADS, MLA_NOPE + MLA_ROPE)
    w = jnp.pad(w, ((0, 0), (0, 0), (0, 0), (0, MLA_QK_PAD - MLA_NOPE - MLA_ROPE)))
    return w.reshape(nl, rank, MLA_HEADS * MLA_QK_PAD).astype(BF16)


def _pack_w_ukv(w):
    nl, rank, _ = w.shape
    w = w.reshape(nl, rank, MLA_HEADS, 2, MLA_NOPE)
    w = jnp.swapaxes(w, 2, 3)
    return w.reshape(nl, rank, 2 * MLA_HEADS * MLA_NOPE).astype(BF16)


def _rope_tables(lp):
    pos = jnp.arange(lp, dtype=F32)

    def tables(dim):
        inv = ROPE_BASE ** (-jnp.arange(0, dim, 2, dtype=F32) / dim)
        ang = pos[:, None] * inv[None, :]
        return jnp.cos(ang), jnp.sin(ang)

    c, s = tables(RET_HEAD_DIM)
    ret = (jnp.concatenate([c, c], axis=1), jnp.concatenate([-s, s], axis=1))
    c, s = tables(MLA_ROPE)
    z = jnp.zeros((lp, LANE - MLA_ROPE), F32)
    mla = (jnp.concatenate([c, c, z], axis=1), jnp.concatenate([-s, s, z], axis=1))
    return ret, mla


def kernel(x, meta_tokens, norm_mix, w_in, rw_mu, rw_w0, rw_w_up, rw_a0, rw_a_up, rw_g_up, rw_k_k, rw_k_a, rw_r_k, rw_ln_w, rw_ln_b, mla_norm_q, mla_norm_kv, mla_w_uq, mla_w_ukv, w_br_rwkv, w_br_ret, w_br_mla, w_out, norm_ffn, w_gate_up, w_down, final_norm):
    batch, seq, d = x.shape
    depth = w_in.shape[0]
    lp = -(-(N_META + seq) // SEQ_ALIGN) * SEQ_ALIGN
    tp = batch * lp

    meta = jnp.broadcast_to(meta_tokens[None].astype(x.dtype), (batch, N_META, d))
    pad = jnp.zeros((batch, lp - N_META - seq, d), x.dtype)
    h = jnp.concatenate([meta, x, pad], axis=1).reshape(tp, d)

    wp = _pack_w_in(w_in, norm_mix)
    mu = _pack_mu(rw_mu)
    w_up = _pad_rows(rw_w_up, LANE).astype(BF16)
    a_up = _pad_rows(rw_a_up, LANE).astype(BF16)
    g_up = rw_g_up.astype(BF16)
    wuq = _pack_w_uq(mla_w_uq)
    wukv = _pack_w_ukv(mla_w_ukv)
    wa = w_br_rwkv.astype(BF16)
    wb = w_br_ret.astype(BF16)
    wc = w_br_mla.astype(BF16)
    wo = w_out.astype(BF16)
    wgu = (w_gate_up * norm_ffn[..., None]).astype(BF16)
    wdn = w_down.astype(BF16)
    (cos_ret, sin_ret), (cos_mla, sin_mla) = _rope_tables(lp)

    def row_tile(pref):
        return next((t for t in pref if tp % t == 0), SEQ_ALIGN)

    tm = row_tile((768,))
    tm_wide = row_tile((1536, 768))
    tm_seq = 384 if lp % 384 == 0 else SEQ_ALIGN

    hb, ssq = stream_prep(h, tm)
    for l in range(depth):
        p = norm_matmul(hb, ssq, wp, l, tm_wide, P_TILE_N, F32)
        ya = rwkv_mix(p, batch, mu[l:l + 1], rw_w0[l], rw_a0[l], rw_k_k[l], rw_k_a[l], rw_r_k[l],
                      rw_ln_w[l], rw_ln_b[l], w_up[l], a_up[l], g_up[l])
        yb = retention_mix(p, batch, cos_ret, sin_ret)
        q, k, v = mla_proj(p, batch, mla_norm_q[l], mla_norm_kv[l], wuq[l], wukv[l], cos_mla, sin_mla, tm_seq)
        yc = mla_attention(q, k, v, batch, tm_seq)
        merged = merge_branches(ya, yb, yc, wa, wb, wc, l, p, tm, 1024)
        h, hb, ssq = resid_matmul(merged, wo, l, h, tm_seq, d)
        act = ffn_up(hb, ssq, wgu, l, tm, 512)
        h, hb, ssq = resid_matmul(act, wdn, l, h, tm, 512)
    tm_out = next(t for t in (512, 256, SEQ_ALIGN) if seq % t == 0)
    return final_rmsnorm(h, final_norm, batch, N_META, seq, tm_out).reshape(batch, seq, d)
```

```python
import functools
import math

import jax
import jax.numpy as jnp
from jax import lax
from jax.experimental import pallas as pl
from jax.experimental.pallas import tpu as pltpu

F32 = jnp.float32
BF16 = jnp.bfloat16

D_MODEL = 2048
N_META = 16
NORM_EPS = 1e-6
ROPE_BASE = 10000.0

RW_HEADS = 16
RW_HEAD_DIM = 64
RW_WIDTH = RW_HEADS * RW_HEAD_DIM
RW_DECAY_LORA = 96
RW_A_LORA = 96
RW_GATE_LORA = 256
RW_GN_EPS = RW_HEAD_DIM * 1e-5
RW_CHUNK = 64
RW_SUB = 3
RW_PAIRS = RW_WIDTH // 128

RET_HEADS = 8
RET_HEAD_DIM = 128
RET_WIDTH = RET_HEADS * RET_HEAD_DIM
RET_CHUNK = 128
RET_SUB = 3

MLA_HEADS = 8
MLA_NOPE = 128
MLA_ROPE = 64
MLA_V = 128
MLA_Q_RANK = 512
MLA_KV_RANK = 256
MLA_WIDTH = MLA_HEADS * MLA_V
MLA_QK_PAD = 256
ATTN_HEADS_PER_STEP = 2

FFN_HIDDEN = -(-8 * D_MODEL // (3 * 256)) * 256

LANE = 128
SEQ_ALIGN = 384
assert SEQ_ALIGN % (RW_CHUNK * RW_SUB) == 0 and SEQ_ALIGN % (RET_CHUNK * RET_SUB) == 0 and SEQ_ALIGN % LANE == 0

OFF_RET = 0
OFF_GATE = OFF_RET + 4 * RET_WIDTH
OFF_RW = OFF_GATE + 3 * D_MODEL
OFF_RW_WD = OFF_RW + 3 * RW_WIDTH
OFF_RW_AD = OFF_RW_WD + LANE
OFF_RW_GD = OFF_RW_AD + LANE
OFF_MLA_Q = OFF_RW_GD + RW_GATE_LORA
OFF_MLA_KV = OFF_MLA_Q + MLA_Q_RANK
OFF_MLA_KR = OFF_MLA_KV + MLA_KV_RANK
P_COLS_USED = OFF_MLA_KR + LANE
P_TILE_N = 512
P_COLS = -(-P_COLS_USED // P_TILE_N) * P_TILE_N

VMEM_LIMIT = 48 * 1024 * 1024


def _params(*sem):
    return pltpu.CompilerParams(dimension_semantics=sem, vmem_limit_bytes=VMEM_LIMIT)


def _sigmoid(x):
    return 1.0 / (1.0 + jnp.exp(-x))


def _dot(a, b):
    return jnp.dot(a, b, preferred_element_type=F32)


def _dot_nt(a, b):
    return lax.dot_general(a, b, (((1,), (1,)), ((), ())), preferred_element_type=F32)


def _dot_tn(a, b):
    return lax.dot_general(a, b, (((0,), (0,)), ((), ())), preferred_element_type=F32)


def _rmsnorm_body(x_ref, g_ref, o_ref):
    x = x_ref[...]
    y = x * lax.rsqrt(jnp.mean(x * x, axis=-1, keepdims=True) + NORM_EPS)
    o_ref[...] = (y * g_ref[...]).astype(o_ref.dtype)


def final_rmsnorm(x, g, batch, first, count, tm):
    m, d = x.shape
    rows_per_batch = m // batch
    assert rows_per_batch % 8 == 0 and first % 8 == 0 and tm % 8 == 0
    tiles = count // tm
    return pl.pallas_call(
        _rmsnorm_body,
        grid=(batch, tiles),
        in_specs=[pl.BlockSpec((pl.Element(tm), pl.Element(d)),
                               lambda b, t: (pl.multiple_of(b * rows_per_batch + first + t * tm, 8), 0)),
                  pl.BlockSpec((1, d), lambda b, t: (0, 0))],
        out_specs=pl.BlockSpec((tm, d), lambda b, t: (b * tiles + t, 0)),
        out_shape=jax.ShapeDtypeStruct((batch * count, d), x.dtype),
        compiler_params=_params("parallel", "parallel"),
        name="final_rmsnorm",
    )(x, g.reshape(1, d))


def _row_scale(ssq_ref, d):
    ssq = ssq_ref[...]
    total = ssq[:, 0:1]
    for j in range(1, ssq.shape[1] // LANE):
        total = total + ssq[:, LANE * j:LANE * j + 1]
    return lax.rsqrt(total * (1.0 / d) + NORM_EPS)


def _ssq_block(x):
    return jnp.broadcast_to(jnp.sum(x * x, axis=-1, keepdims=True), (x.shape[0], LANE))


def _stream_prep_body(x_ref, xb_ref, ssq_ref):
    x = x_ref[...]
    xb_ref[...] = x.astype(xb_ref.dtype)
    ssq_ref[...] = _ssq_block(x)


def stream_prep(x, tm):
    m, d = x.shape
    return pl.pallas_call(
        _stream_prep_body,
        grid=(m // tm,),
        in_specs=[pl.BlockSpec((tm, d), lambda i: (i, 0))],
        out_specs=[pl.BlockSpec((tm, d), lambda i: (i, 0)), pl.BlockSpec((tm, LANE), lambda i: (i, 0))],
        out_shape=[jax.ShapeDtypeStruct((m, d), BF16), jax.ShapeDtypeStruct((m, LANE), F32)],
        compiler_params=_params("parallel"),
        name="stream_prep",
    )(x)


def _norm_matmul_body(x_ref, ssq_ref, w_ref, o_ref):
    o_ref[...] = (_row_scale(ssq_ref, x_ref.shape[1]) * _dot(x_ref[...], w_ref[...])).astype(o_ref.dtype)


def norm_matmul(xb, ssq, w, layer, tm, tn, out_dtype):
    m, k = xb.shape
    n = w.shape[2]
    return pl.pallas_call(
        _norm_matmul_body,
        grid=(m // tm, n // tn),
        in_specs=[pl.BlockSpec((tm, k), lambda i, j: (i, 0)), pl.BlockSpec((tm, ssq.shape[1]), lambda i, j: (i, 0)),
                  pl.BlockSpec((None, k, tn), lambda i, j: (layer, 0, j))],
        out_specs=pl.BlockSpec((tm, tn), lambda i, j: (i, j)),
        out_shape=jax.ShapeDtypeStruct((m, n), out_dtype),
        compiler_params=_params("parallel", "parallel"),
        name="in_proj",
    )(xb, ssq, w)


def _rwkv_body(r_ref, k_ref, v_ref, wd_ref, ad_ref, gd_ref, mu_ref,
               w0_ref, a0_ref, kk_ref, ka_ref, rk_ref, lnw_ref, lnb_ref,
               wup_ref, aup_ref, gup_ref,
               o_ref,
               s_ref, pr_ref, pk_ref, pv_ref, pwd_ref, pad_ref, pgd_ref):
    c = pl.program_id(1)
    C = RW_CHUNK
    RS = RW_SUB * C
    HD = RW_HEAD_DIM

    @pl.when(c == 0)
    def _():
        s_ref[...] = jnp.zeros_like(s_ref)
        pr_ref[...] = jnp.zeros_like(pr_ref)
        pk_ref[...] = jnp.zeros_like(pk_ref)
        pv_ref[...] = jnp.zeros_like(pv_ref)
        pwd_ref[...] = jnp.zeros_like(pwd_ref)
        pad_ref[...] = jnp.zeros_like(pad_ref)
        pgd_ref[...] = jnp.zeros_like(pgd_ref)

    def shift(x_ref, prev_ref, mu, sl):
        z = x_ref[:, sl]
        first = lax.broadcasted_iota(jnp.int32, z.shape, 0) == 0
        zs = jnp.where(first, prev_ref[0:1, sl], pltpu.roll(z, 1, 0))
        prev_ref[0:1, sl] = z[RS - 1:RS, :]
        return z + (zs - z) * mu

    mu_lora = 3 * RW_WIDTH
    full = slice(None)
    wd = shift(wd_ref, pwd_ref, mu_ref[:, mu_lora:mu_lora + LANE], full)
    ad = shift(ad_ref, pad_ref, mu_ref[:, mu_lora + LANE:mu_lora + 2 * LANE], full)
    gd = shift(gd_ref, pgd_ref, mu_ref[:, mu_lora + 2 * LANE:], full)
    subs = range(RW_SUB)

    def rows(t, s):
        return t[C * s:C * (s + 1)]

    tanh_wd = [rows(jnp.tanh(wd), s).astype(BF16) for s in subs]
    ad_b = [rows(ad, s).astype(BF16) for s in subs]
    sig_gd = [rows(_sigmoid(gd), s).astype(BF16) for s in subs]

    lane_sq = lax.broadcasted_iota(jnp.int32, (LANE, LANE), 1)
    row_sq = lax.broadcasted_iota(jnp.int32, (LANE, LANE), 0)
    same_head = (lane_sq < HD) == (row_sq < HD)
    head_ones = same_head.astype(BF16)
    eye = (lane_sq == row_sq).astype(F32)
    rc = lax.broadcasted_iota(jnp.int32, (C, C), 0)
    cc = lax.broadcasted_iota(jnp.int32, (C, C), 1)
    tril_incl = (cc <= rc).astype(BF16)
    lane_tall = lax.broadcasted_iota(jnp.int32, (2 * C, LANE), 1)
    lane_c = lax.broadcasted_iota(jnp.int32, (C, LANE), 1)
    row_c = lax.broadcasted_iota(jnp.int32, (C, LANE), 0)
    head0_c = lane_c < HD
    strict_lo = lane_c < row_c
    strict_hi = (lane_c >= C) & (lane_c - C < row_c)
    incl_lo = lane_c <= row_c
    incl_hi = (lane_c >= C) & (lane_c - C <= row_c)
    inv_n = 1.0 / HD

    P = range(RW_PAIRS)

    def head_sums(ts):
        his = [t.astype(BF16) for t in ts]
        los = [(t - hi.astype(F32)).astype(BF16) for t, hi in zip(ts, his)]
        out = _dot(jnp.concatenate(his + los, axis=0), head_ones)
        n = len(ts)
        return [out[C * i:C * (i + 1)] + out[C * (n + i):C * (n + i + 1)] for i in range(n)]

    pair_sls = [slice(LANE * i, LANE * (i + 1)) for i in P]

    def each(fn, *lists):
        return [fn(*args) for args in zip(*lists)]

    def items(per_pair):
        return [rows(t, s) for s in subs for t in per_pair]

    r = items([shift(r_ref, pr_ref, mu_ref[:, sl], sl) for sl in pair_sls])
    k = items([shift(k_ref, pk_ref, mu_ref[:, RW_WIDTH + sl.start:RW_WIDTH + sl.stop], sl) for sl in pair_sls])
    v = items([shift(v_ref, pv_ref, mu_ref[:, 2 * RW_WIDTH + sl.start:2 * RW_WIDTH + sl.stop], sl)
               for sl in pair_sls])
    sls = pair_sls * RW_SUB
    sub_of = [s for s in subs for _ in P]

    def log_decay(s, sl):
        x = -(w0_ref[:, sl] + _dot(tanh_wd[s], wup_ref[:, sl]))
        softplus = jnp.maximum(x, 0.0) + jnp.log1p(jnp.exp(-jnp.abs(x)))
        return -jnp.exp(-softplus - 0.5)

    lw = each(log_decay, sub_of, sls)
    a = each(lambda s, sl: _sigmoid(a0_ref[:, sl] + _dot(ad_b[s], aup_ref[:, sl])), sub_of, sls)
    g = each(lambda s, sl: _dot(sig_gd[s], gup_ref[:, sl]), sub_of, sls)

    kkr = each(lambda ki, sl: ki * kk_ref[:, sl], k, sls)
    ksq = head_sums(each(lambda t: t * t, kkr))
    kkn = each(lambda t, ss: t / jnp.maximum(jnp.sqrt(ss), 1e-12), kkr, ksq)
    kmod = each(lambda ki, ai, sl: ki * (1.0 + (ai - 1.0) * ka_ref[:, sl]), k, a, sls)
    beta = each(lambda ai, t: ai * t, a, kkn)

    def running_sum(lwi):
        hi = lwi.astype(BF16)
        both = _dot(tril_incl, jnp.concatenate([hi, (lwi - hi.astype(F32)).astype(BF16)], axis=1))
        return both[:, :LANE] + both[:, LANE:]

    lcum = each(running_sum, lw)
    lend = each(lambda t: t[C - 1:C, :], lcum)
    rh = each(lambda ri, lc: ri * jnp.exp(lc), r, lcum)
    kh = each(lambda t, lc, lwi: t * jnp.exp(lc - lwi), kkn, lcum, lw)
    e_neg = each(lambda lc: jnp.exp(-lc), lcum)
    e_end = each(lambda le, lc: jnp.exp(le - lc), lend, lcum)
    kb = each(lambda t, e: t * e, kmod, e_neg)
    bb = each(lambda t, e: t * e, beta, e_neg)
    kbe = each(lambda t, e: t * e, kmod, e_end)
    bbe = each(lambda t, e: t * e, beta, e_end)

    kr_f = each(lambda x1, x2: jnp.concatenate([x1, x2], axis=0), kh, rh)
    bk = each(lambda x1, x2: jnp.concatenate([x1, x2], axis=0).astype(BF16), bb, kb)
    vb = each(lambda t: t.astype(BF16), v)

    def gram(krf, bki):
        kr2 = jnp.concatenate([jnp.where(lane_tall < HD, krf, 0.0), jnp.where(lane_tall >= HD, krf, 0.0)], axis=0)
        return _dot_nt(kr2.astype(BF16), bki)

    g_all = each(gram, kr_f, bk)

    n_bd = each(lambda ga: jnp.concatenate([jnp.where(strict_lo, -ga[0:C], 0.0),
                                            jnp.where(strict_hi, -pltpu.roll(ga[2 * C:3 * C], C, 1), 0.0)], axis=0),
                g_all)
    t = each(lambda n: eye + n, n_bd)
    pw = each(lambda n: _dot(n.astype(BF16), n.astype(BF16)), n_bd)
    for _ in range(4):
        both = each(lambda ti, pi: _dot(jnp.concatenate([ti, pi], axis=0).astype(BF16), pi.astype(BF16)), t, pw)
        t = each(lambda ti, bi: ti + bi[:LANE], t, both)
        pw = each(lambda bi: bi[LANE:], both)
    t = each(lambda ti, pi: ti + _dot(ti.astype(BF16), pi.astype(BF16)), t, pw)

    def intra_rhs(ga, vbi):
        m1s = jnp.concatenate([jnp.where(strict_hi, ga[0:C], 0.0), jnp.where(strict_hi, ga[2 * C:3 * C], 0.0)], axis=0)
        return _dot(m1s.astype(BF16), jnp.concatenate([vbi, vbi], axis=0))

    q_intra = each(intra_rhs, g_all, vb)

    def m2(gb):
        return jnp.where(incl_lo, -gb, jnp.where(incl_hi, gb, 0.0))

    m2s = each(lambda ga: jnp.concatenate([m2(ga[C:2 * C]), m2(ga[3 * C:4 * C])], axis=0).astype(BF16), g_all)
    kbe_all = each(lambda kbei, bbei: jnp.concatenate([kbei, -bbei], axis=0).astype(BF16), kbe, bbe)
    s_decay = each(jnp.exp, lend)

    state = [s_ref[i] for i in P]
    y = []
    for s in subs:
        of = lambda lst: lst[RW_PAIRS * s:RW_PAIRS * (s + 1)]
        p_all = each(lambda x1, si: _dot_nt(x1.astype(BF16), si.astype(BF16)), of(kr_f), state)
        q_s = each(lambda qi, pa: jnp.where(same_head, qi + jnp.concatenate([pa[:C], pa[:C]], axis=0), 0.0),
                   of(q_intra), p_all)
        u_s = each(lambda ti, qi: _dot(ti.astype(BF16), qi.astype(BF16)), of(t), q_s)
        u = each(lambda us: us[:C] + us[C:], u_s)
        y_s = each(lambda mi, ui, vi: _dot(mi, jnp.concatenate([ui, vi], axis=0).astype(BF16)), of(m2s), u, of(v))
        y += each(lambda pa, ys: pa[C:] + jnp.where(head0_c, ys[:C], ys[C:]), p_all, y_s)
        ds = each(lambda vi, ui, kb_all: _dot_tn(jnp.concatenate([vi, ui], axis=0).astype(BF16), kb_all),
                  of(v), u, of(kbe_all))
        state = each(lambda si, di, dec: si * dec + jnp.where(same_head, di, 0.0), state, ds, of(s_decay))
    for i in P:
        s_ref[i] = state[i]

    d = each(lambda yi, si: yi - si * inv_n, y, head_sums(y))
    var = each(lambda si: si * inv_n, head_sums(each(lambda di: di * di, d)))
    bsum = head_sums(each(lambda ri, ki, sl: ri * ki * rk_ref[:, sl], r, kmod, sls))
    for i, (s, sl) in enumerate(zip(sub_of, sls)):
        yn = d[i] * lax.rsqrt(var[i] + RW_GN_EPS) * lnw_ref[:, sl] + lnb_ref[:, sl]
        o_ref[C * s:C * (s + 1), sl] = ((yn + bsum[i] * v[i]) * g[i]).astype(o_ref.dtype)


def rwkv_mix(p, batch, mu, w0, a0, k_k, k_a, r_k, ln_w, ln_b, w_up, a_up, g_up):
    tp = p.shape[0]
    lp = tp // batch
    C = RW_CHUNK * RW_SUB
    nchunk = lp // C

    def pspec(width, base):
        return pl.BlockSpec((C, width), lambda b, c: (b * nchunk + c, base // width))

    def const(shape):
        return pl.BlockSpec(shape, lambda b, c: (0, 0))

    in_specs = [
        pspec(RW_WIDTH, OFF_RW), pspec(RW_WIDTH, OFF_RW + RW_WIDTH), pspec(RW_WIDTH, OFF_RW + 2 * RW_WIDTH),
        pspec(LANE, OFF_RW_WD), pspec(LANE, OFF_RW_AD), pspec(RW_GATE_LORA, OFF_RW_GD),
        const(mu.shape),
    ] + [const((1, RW_WIDTH))] * 7 + [const(w_up.shape), const(a_up.shape), const(g_up.shape)]
    row = lambda t: t.reshape(1, -1)
    return pl.pallas_call(
        _rwkv_body,
        grid=(batch, nchunk),
        in_specs=in_specs,
        out_specs=pl.BlockSpec((C, RW_WIDTH), lambda b, c: (b * nchunk + c, 0)),
        out_shape=jax.ShapeDtypeStruct((tp, RW_WIDTH), BF16),
        scratch_shapes=[pltpu.VMEM((RW_PAIRS, LANE, LANE), F32)] + [pltpu.VMEM((8, RW_WIDTH), F32)] * 3
        + [pltpu.VMEM((8, LANE), F32)] * 2 + [pltpu.VMEM((8, RW_GATE_LORA), F32)],
        compiler_params=_params("parallel", "arbitrary"),
        name="rwkv7_mix",
    )(p, p, p, p, p, p, mu, row(w0), row(a0), row(k_k), row(k_a), row(r_k), row(ln_w), row(ln_b),
      w_up, a_up, g_up)


def _ret_body(q_ref, k_ref, v_ref, g_ref, cos_ref, sin_ref, o_ref, state_ref):
    c = pl.program_id(1)
    C = RET_CHUNK
    d = RET_HEAD_DIM

    @pl.when(c == 0)
    def _():
        state_ref[...] = jnp.zeros_like(state_ref)

    row = lax.broadcasted_iota(jnp.int32, (C, C), 0).astype(F32)
    col = lax.broadcasted_iota(jnp.int32, (C, C), 1).astype(F32)
    diff = row - col
    causal = diff >= 0
    heads = range(RET_HEADS)
    subs = range(RET_SUB)
    rws = [slice(C * s, C * (s + 1)) for s in subs for _ in heads]
    sls = [slice(d * h, d * (h + 1)) for _ in subs for h in heads]
    lgs = [math.log1p(-(2.0 ** (-5.0 - h))) for _ in subs for h in heads]

    def each(fn, *lists):
        return [fn(*args) for args in zip(*lists)]

    def rope(x_ref, rw, sl):
        x = x_ref[rw, sl]
        return x * cos_ref[rw, :] + pltpu.roll(x, d // 2, 1) * sin_ref[rw, :]

    qb = each(lambda rw, sl: rope(q_ref, rw, sl).astype(BF16), rws, sls)
    k = each(lambda rw, sl: rope(k_ref, rw, sl) * (d ** -0.5), rws, sls)
    kb = each(lambda t: t.astype(BF16), k)
    vb = each(lambda rw, sl: v_ref[rw, sl].astype(BF16), rws, sls)
    s = each(lambda qi, ki, lg: _dot_nt(qi, ki) * jnp.where(causal, jnp.exp(lg * jnp.maximum(diff, 0.0)), 0.0),
             qb, kb, lgs)
    o_intra = each(lambda si, vi: _dot(si.astype(BF16), vi), s, vb)
    kd = each(lambda ki, lg: (ki * jnp.exp(lg * (C - 1.0 - row))).astype(BF16), k, lgs)
    kv = each(_dot_tn, kd, vb)

    state = [state_ref[h] for h in heads]
    o = []
    for sub in subs:
        of = lambda lst: lst[RET_HEADS * sub:RET_HEADS * (sub + 1)]
        o += each(lambda oi, qi, st, lg: oi + _dot(qi, st.astype(BF16)) * jnp.exp(lg * (row + 1.0)),
                  of(o_intra), of(qb), state, of(lgs))
        state = each(lambda st, kvi, lg: st * math.exp(lg * C) + kvi, state, of(kv), of(lgs))
    for h in heads:
        state_ref[h] = state[h]
    o = each(lambda oi: oi * lax.rsqrt(jnp.mean(oi * oi, axis=-1, keepdims=True) + NORM_EPS), o)
    for oi, rw, sl in zip(o, rws, sls):
        g = g_ref[rw, sl]
        o_ref[rw, sl] = (g * _sigmoid(g) * oi).astype(o_ref.dtype)


def retention_mix(p, batch, cos, sin):
    tp = p.shape[0]
    lp = tp // batch
    C = RET_CHUNK * RET_SUB
    nchunk = lp // C
    base = OFF_RET // RET_WIDTH

    def pspec(j):
        return pl.BlockSpec((C, RET_WIDTH), lambda b, c: (b * nchunk + c, base + j))

    tab = pl.BlockSpec((C, RET_HEAD_DIM), lambda b, c: (c, 0))
    return pl.pallas_call(
        _ret_body,
        grid=(batch, nchunk),
        in_specs=[pspec(0), pspec(1), pspec(2), pspec(3), tab, tab],
        out_specs=pl.BlockSpec((C, RET_WIDTH), lambda b, c: (b * nchunk + c, 0)),
        out_shape=jax.ShapeDtypeStruct((tp, RET_WIDTH), BF16),
        scratch_shapes=[pltpu.VMEM((RET_HEADS, RET_HEAD_DIM, RET_HEAD_DIM), F32)],
        compiler_params=_params("parallel", "arbitrary"),
        name="retention_mix",
    )(p, p, p, p, cos, sin)


def _mla_proj_body(qd_ref, kvd_ref, krd_ref, nq_ref, nkv_ref, wuq_ref, wukv_ref, cos_ref, sin_ref,
                   q_out, k_out, v_out):
    cos = cos_ref[...]
    sin = sin_ref[...]

    def rope(x):
        return x * cos + (pltpu.roll(x, MLA_ROPE // 2, 1) + pltpu.roll(x, LANE - MLA_ROPE // 2, 1)) * sin

    def norm(x, g):
        return x * lax.rsqrt(jnp.mean(x * x, axis=-1, keepdims=True) + NORM_EPS) * g

    scale = (MLA_NOPE + MLA_ROPE) ** -0.5 * math.log2(math.e)
    q = _dot(norm(qd_ref[...], nq_ref[...]).astype(BF16), wuq_ref[...]) * scale
    kv = _dot(norm(kvd_ref[...], nkv_ref[...]).astype(BF16), wukv_ref[...])
    kr = rope(krd_ref[...]).astype(k_out.dtype)
    for h in range(MLA_HEADS):
        lo = MLA_QK_PAD * h
        q_out[:, lo:lo + LANE] = q[:, lo:lo + LANE].astype(q_out.dtype)
        q_out[:, lo + LANE:lo + 2 * LANE] = rope(q[:, lo + LANE:lo + 2 * LANE]).astype(q_out.dtype)
        k_out[:, lo:lo + LANE] = kv[:, MLA_NOPE * h:MLA_NOPE * (h + 1)].astype(k_out.dtype)
        k_out[:, lo + LANE:lo + 2 * LANE] = kr
    v_out[...] = kv[:, MLA_HEADS * MLA_NOPE:].astype(v_out.dtype)


def mla_proj(p, batch, norm_q, norm_kv, w_uq, w_ukv, cos, sin, tm):
    tp = p.shape[0]
    lp = tp // batch
    per_seq = lp // tm
    qk_w = MLA_HEADS * MLA_QK_PAD
    const = lambda i: (0, 0)
    return pl.pallas_call(
        _mla_proj_body,
        grid=(tp // tm,),
        in_specs=[
            pl.BlockSpec((tm, MLA_Q_RANK), lambda i: (i, OFF_MLA_Q // MLA_Q_RANK)),
            pl.BlockSpec((tm, MLA_KV_RANK), lambda i: (i, OFF_MLA_KV // MLA_KV_RANK)),
            pl.BlockSpec((tm, LANE), lambda i: (i, OFF_MLA_KR // LANE)),
            pl.BlockSpec((1, MLA_Q_RANK), const),
            pl.BlockSpec((1, MLA_KV_RANK), const),
            pl.BlockSpec((MLA_Q_RANK, qk_w), const),
            pl.BlockSpec((MLA_KV_RANK, MLA_HEADS * (MLA_NOPE + MLA_V)), const),
            pl.BlockSpec((tm, LANE), lambda i: (i % per_seq, 0)),
            pl.BlockSpec((tm, LANE), lambda i: (i % per_seq, 0)),
        ],
        out_specs=[
            pl.BlockSpec((tm, qk_w), lambda i: (i, 0)),
            pl.BlockSpec((tm, qk_w), lambda i: (i, 0)),
            pl.BlockSpec((tm, MLA_WIDTH), lambda i: (i, 0)),
        ],
        out_shape=[
            jax.ShapeDtypeStruct((tp, qk_w), BF16),
            jax.ShapeDtypeStruct((tp, qk_w), BF16),
            jax.ShapeDtypeStruct((tp, MLA_WIDTH), BF16),
        ],
        compiler_params=_params("parallel"),
        name="mla_proj",
    )(p, p, p, norm_q.reshape(1, -1), norm_kv.reshape(1, -1), w_uq, w_ukv, cos, sin)


def _attn_body(q_ref, k_ref, v_ref, o_ref, *, tq):
    i = pl.program_id(2)
    heads = range(ATTN_HEADS_PER_STEP)
    qs = [q_ref[:, MLA_QK_PAD * h:MLA_QK_PAD * (h + 1)] for h in heads]

    def step(off, width, masked, carry):
        ms, ls, accs = carry
        off = pl.multiple_of(off, tq)
        ss = [_dot_nt(qs[h], k_ref[pl.ds(off, width), MLA_QK_PAD * h:MLA_QK_PAD * (h + 1)]) for h in heads]
        if masked:
            row = i * tq + lax.broadcasted_iota(jnp.int32, (tq, width), 0)
            col = off + lax.broadcasted_iota(jnp.int32, (tq, width), 1)
            ss = [jnp.where(col <= row, s, -jnp.inf) for s in ss]
        m_new = [jnp.maximum(ms[h], jnp.max(ss[h], axis=-1, keepdims=True)) for h in heads]
        alpha = [jnp.exp2(ms[h] - m_new[h]) for h in heads]
        ps = [jnp.exp2(ss[h] - m_new[h]) for h in heads]
        ls = [alpha[h] * ls[h] + jnp.sum(ps[h], axis=-1, keepdims=True) for h in heads]
        pv = [_dot(ps[h].astype(BF16), v_ref[pl.ds(off, width), MLA_V * h:MLA_V * (h + 1)]) for h in heads]
        accs = [alpha[h] * accs[h] + pv[h] for h in heads]
        return tuple(m_new), tuple(ls), tuple(accs)

    carry = (tuple(jnp.full((tq, 1), -1e30, F32) for _ in heads),
             tuple(jnp.zeros((tq, 1), F32) for _ in heads),
             tuple(jnp.zeros((tq, MLA_V), F32) for _ in heads))
    n_single = (i + 1) % 2
    n_pairs = (i + 1) // 2
    carry = lax.cond(i == 0, lambda c: step(0, tq, True, c), lambda c: c, carry)
    carry = lax.cond((n_single == 1) & (i > 0), lambda c: step(0, tq, False, c), lambda c: c, carry)
    pair_off = lambda p: (n_single + 2 * p) * tq
    carry = lax.fori_loop(0, n_pairs - 1, lambda p, c: step(pair_off(p), 2 * tq, False, c), carry)
    carry = lax.cond(n_pairs > 0, lambda c: step(pair_off(n_pairs - 1), 2 * tq, True, c), lambda c: c, carry)
    _, ls, accs = carry
    for h in heads:
        o_ref[:, MLA_V * h:MLA_V * (h + 1)] = (accs[h] / ls[h]).astype(o_ref.dtype)


def mla_attention(q, k, v, batch, tq):
    tp = q.shape[0]
    lp = tp // batch
    nq = lp // tq
    hs = ATTN_HEADS_PER_STEP
    return pl.pallas_call(
        functools.partial(_attn_body, tq=tq),
        grid=(batch, MLA_HEADS // hs, nq),
        in_specs=[
            pl.BlockSpec((tq, hs * MLA_QK_PAD), lambda b, h, i: (b * nq + i, h)),
            pl.BlockSpec((lp, hs * MLA_QK_PAD), lambda b, h, i: (b, h)),
            pl.BlockSpec((lp, hs * MLA_V), lambda b, h, i: (b, h)),
        ],
        out_specs=pl.BlockSpec((tq, hs * MLA_V), lambda b, h, i: (b * nq + i, h)),
        out_shape=jax.ShapeDtypeStruct((tp, MLA_WIDTH), BF16),
        compiler_params=_params("parallel", "parallel", "arbitrary"),
        name="mla_attention",
    )(q, k, v)


def _merge_body(ya_ref, yb_ref, yc_ref, wa_ref, wb_ref, wc_ref, ga_ref, gb_ref, gc_ref, o_ref):
    def branch(y_ref, w_ref, g_ref):
        return _sigmoid(g_ref[...]) * _dot(y_ref[...], w_ref[...])

    o_ref[...] = (branch(ya_ref, wa_ref, ga_ref) + branch(yb_ref, wb_ref, gb_ref)
                  + branch(yc_ref, wc_ref, gc_ref)).astype(o_ref.dtype)


def merge_branches(ya, yb, yc, wa, wb, wc, layer, p, tm, tn):
    tp = ya.shape[0]

    def yspec(width):
        return pl.BlockSpec((tm, width), lambda i, j: (i, 0))

    def wspec(width):
        return pl.BlockSpec((None, width, tn), lambda i, j: (layer, 0, j))

    def gspec(branch):
        base = (OFF_GATE + branch * D_MODEL) // tn
        return pl.BlockSpec((tm, tn), lambda i, j: (i, base + j))

    return pl.pallas_call(
        _merge_body,
        grid=(tp // tm, D_MODEL // tn),
        in_specs=[yspec(RW_WIDTH), yspec(RET_WIDTH), yspec(MLA_WIDTH),
                  wspec(RW_WIDTH), wspec(RET_WIDTH), wspec(MLA_WIDTH),
                  gspec(0), gspec(1), gspec(2)],
        out_specs=pl.BlockSpec((tm, tn), lambda i, j: (i, j)),
        out_shape=jax.ShapeDtypeStruct((tp, D_MODEL), BF16),
        compiler_params=_params("parallel", "parallel"),
        name="merge_branches",
    )(ya, yb, yc, wa, wb, wc, p, p, p)


def _resid_body(x_ref, w_ref, h_ref, o_ref, ob_ref, ssq_ref):
    hn = h_ref[...] + _dot(x_ref[...], w_ref[...])
    o_ref[...] = hn
    ob_ref[...] = hn.astype(ob_ref.dtype)
    ssq_ref[...] = _ssq_block(hn)


def resid_matmul(x, w, layer, h, tm, tn):
    m, kdim = x.shape
    n = w.shape[2]
    tile = pl.BlockSpec((tm, tn), lambda i, j: (i, j))
    return pl.pallas_call(
        _resid_body,
        grid=(m // tm, n // tn),
        in_specs=[
            pl.BlockSpec((tm, kdim), lambda i, j: (i, 0)),
            pl.BlockSpec((None, kdim, tn), lambda i, j: (layer, 0, j)),
            tile,
        ],
        out_specs=[tile, tile, pl.BlockSpec((tm, LANE), lambda i, j: (i, j))],
        out_shape=[jax.ShapeDtypeStruct((m, n), F32), jax.ShapeDtypeStruct((m, n), BF16),
                   jax.ShapeDtypeStruct((m, LANE * (n // tn)), F32)],
        compiler_params=_params("parallel", "parallel"),
        name="resid_matmul",
    )(x, w, h)


def _ffn_up_body(x_ref, ssq_ref, wg_ref, wu_ref, o_ref):
    x = x_ref[...]
    r = _row_scale(ssq_ref, x.shape[1])
    hg = r * _dot(x, wg_ref[...])
    hu = r * _dot(x, wu_ref[...])
    o_ref[...] = (hg * _sigmoid(hg) * hu).astype(o_ref.dtype)


def ffn_up(xb, ssq, w_gate_up, layer, tm, tn):
    m, kdim = xb.shape
    hidden = w_gate_up.shape[2] // 2
    nj = hidden // tn
    return pl.pallas_call(
        _ffn_up_body,
        grid=(m // tm, nj),
        in_specs=[
            pl.BlockSpec((tm, kdim), lambda i, j: (i, 0)),
            pl.BlockSpec((tm, ssq.shape[1]), lambda i, j: (i, 0)),
            pl.BlockSpec((None, kdim, tn), lambda i, j: (layer, 0, j)),
            pl.BlockSpec((None, kdim, tn), lambda i, j: (layer, 0, nj + j)),
        ],
        out_specs=pl.BlockSpec((tm, tn), lambda i, j: (i, j)),
        out_shape=jax.ShapeDtypeStruct((m, hidden), BF16),
        compiler_params=_params("parallel", "parallel"),
        name="ffn_up",
    )(xb, ssq, w_gate_up, w_gate_up)


def _pad_cols(w, width):
    return jnp.pad(w, [(0, 0)] * (w.ndim - 1) + [(0, width - w.shape[-1])])


def _w_in_pieces():
    widths = [3 * RW_WIDTH, RW_DECAY_LORA, RW_A_LORA, RW_GATE_LORA, 4 * RET_WIDTH, MLA_Q_RANK, MLA_KV_RANK, MLA_ROPE,
              3 * D_MODEL]
    dsts = [OFF_RW, OFF_RW_WD, OFF_RW_AD, OFF_RW_GD, OFF_RET, OFF_MLA_Q, OFF_MLA_KV, OFF_MLA_KR, OFF_GATE]
    pieces, src = [], 0
    for dst, width in zip(dsts, widths):
        pieces.append((dst, src, width))
        src += width
    return pieces


def _pack_w_in_body(w_ref, g_ref, o_ref):
    g = g_ref[...]
    covered = 0
    for dst, src, width in sorted(_w_in_pieces()):
        if dst > covered:
            o_ref[:, covered:dst] = jnp.zeros((o_ref.shape[0], dst - covered), o_ref.dtype)
        o_ref[:, dst:dst + width] = (w_ref[:, src:src + width] * g).astype(o_ref.dtype)
        covered = dst + width
    o_ref[:, covered:] = jnp.zeros((o_ref.shape[0], o_ref.shape[1] - covered), o_ref.dtype)


def _pack_w_in(w_in, gain, tr=128):
    nl, rows, cols = w_in.shape
    return pl.pallas_call(
        _pack_w_in_body,
        grid=(nl, rows // tr),
        in_specs=[pl.BlockSpec((None, tr, cols), lambda l, i: (l, i, 0)),
                  pl.BlockSpec((None, tr, 1), lambda l, i: (l, i, 0))],
        out_specs=pl.BlockSpec((None, tr, P_COLS), lambda l, i: (l, i, 0)),
        out_shape=jax.ShapeDtypeStruct((nl, rows, P_COLS), BF16),
        compiler_params=_params("parallel", "parallel"),
        name="pack_w_in",
    )(w_in, gain[..., None])


def _pack_mu(mu):
    rkv = mu[..., :3 * RW_WIDTH]
    wd = mu[..., 3 * RW_WIDTH:3 * RW_WIDTH + RW_DECAY_LORA]
    ad = mu[..., 3 * RW_WIDTH + RW_DECAY_LORA:3 * RW_WIDTH + RW_DECAY_LORA + RW_A_LORA]
    gd = mu[..., 3 * RW_WIDTH + RW_DECAY_LORA + RW_A_LORA:]
    return jnp.concatenate([rkv, _pad_cols(wd, LANE), _pad_cols(ad, LANE), gd], axis=-1)


def _pad_rows(w, rows):
    return jnp.pad(w, [(0, 0)] * (w.ndim - 2) + [(0, rows - w.shape[-2]), (0, 0)])


def _pack_w_uq(w):
    nl, rank, _ = w.shape
    w = w.reshape(nl, rank, MLA_HEADS, MLA_NOPE + MLA_ROPE)
    w = jnp.pad(w, ((0, 0), (0, 0), (0, 0), (0, MLA_QK_PAD - MLA_NOPE - MLA_ROPE)))
    return w.reshape(nl, rank, MLA_HEADS * MLA_QK_PAD).astype(BF16)


def _pack_w_ukv(w):
    nl, rank, _ = w.shape
    w = w.reshape(nl, rank, MLA_HEADS, 2, MLA_NOPE)
    w = jnp.swapaxes(w, 2, 3)
    return w.reshape(nl, rank, 2 * MLA_HEADS * MLA_NOPE).astype(BF16)


def _rope_tables(lp):
    pos = jnp.arange(lp, dtype=F32)

    def tables(dim):
        inv = ROPE_BASE ** (-jnp.arange(0, dim, 2, dtype=F32) / dim)
        ang = pos[:, None] * inv[None, :]
        return jnp.cos(ang), jnp.sin(ang)

    c, s = tables(RET_HEAD_DIM)
    ret = (jnp.concatenate([c, c], axis=1), jnp.concatenate([-s, s], axis=1))
    c, s = tables(MLA_ROPE)
    z = jnp.zeros((lp, LANE - MLA_ROPE), F32)
    mla = (jnp.concatenate([c, c, z], axis=1), jnp.concatenate([-s, s, z], axis=1))
    return ret, mla


def kernel(x, meta_tokens, norm_mix, w_in, rw_mu, rw_w0, rw_w_up, rw_a0, rw_a_up, rw_g_up, rw_k_k, rw_k_a, rw_r_k, rw_ln_w, rw_ln_b, mla_norm_q, mla_norm_kv, mla_w_uq, mla_w_ukv, w_br_rwkv, w_br_ret, w_br_mla, w_out, norm_ffn, w_gate_up, w_down, final_norm):
    batch, seq, d = x.shape
    depth = w_in.shape[0]
    lp = -(-(N_META + seq) // SEQ_ALIGN) * SEQ_ALIGN
    tp = batch * lp

    meta = jnp.broadcast_to(meta_tokens[None].astype(x.dtype), (batch, N_META, d))
    pad = jnp.zeros((batch, lp - N_META - seq, d), x.dtype)
    h = jnp.concatenate([meta, x, pad], axis=1).reshape(tp, d)

    wp = _pack_w_in(w_in, norm_mix)
    mu = _pack_mu(rw_mu)
    w_up = _pad_rows(rw_w_up, LANE).astype(BF16)
    a_up = _pad_rows(rw_a_up, LANE).astype(BF16)
    g_up = rw_g_up.astype(BF16)
    wuq = _pack_w_uq(mla_w_uq)
    wukv = _pack_w_ukv(mla_w_ukv)
    wa = w_br_rwkv.astype(BF16)
    wb = w_br_ret.astype(BF16)
    wc = w_br_mla.astype(BF16)
    wo = w_out.astype(BF16)
    wgu = (w_gate_up * norm_ffn[..., None]).astype(BF16)
    wdn = w_down.astype(BF16)
    (cos_ret, sin_ret), (cos_mla, sin_mla) = _rope_tables(lp)

    def row_tile(pref):
        return next((t for t in pref if tp % t == 0), SEQ_ALIGN)

    tm = row_tile((768,))
    tm_wide = row_tile((1536, 768))
    tm_seq = SEQ_ALIGN

    hb, ssq = stream_prep(h, tm)
    for l in range(depth):
        p = norm_matmul(hb, ssq, wp, l, tm_wide, P_TILE_N, F32)
        ya = rwkv_mix(p, batch, mu[l:l + 1], rw_w0[l], rw_a0[l], rw_k_k[l], rw_k_a[l], rw_r_k[l],
                      rw_ln_w[l], rw_ln_b[l], w_up[l], a_up[l], g_up[l])
        yb = retention_mix(p, batch, cos_ret, sin_ret)
        q, k, v = mla_proj(p, batch, mla_norm_q[l], mla_norm_kv[l], wuq[l], wukv[l], cos_mla, sin_mla, tm_seq)
        yc = mla_attention(q, k, v, batch, tm_seq)
        merged = merge_branches(ya, yb, yc, wa, wb, wc, l, p, tm, 1024)
        h, hb, ssq = resid_matmul(merged, wo, l, h, tm_seq, d)
        act = ffn_up(hb, ssq, wgu, l, tm_wide, 512)
        h, hb, ssq = resid_matmul(act, wdn, l, h, tm, 512)
    tm_out = next(t for t in (512, 256, SEQ_ALIGN) if seq % t == 0)
    return final_rmsnorm(h, final_norm, batch, N_META, seq, tm_out).reshape(batch, seq, d)
```

```python
import functools
import math

import jax
import jax.numpy as jnp
from jax import lax
from jax.experimental import pallas as pl
from jax.experimental.pallas import tpu as pltpu

F32 = jnp.float32
BF16 = jnp.bfloat16

D_MODEL = 2048
N_META = 16
NORM_EPS = 1e-6
ROPE_BASE = 10000.0

RW_HEADS = 16
RW_HEAD_DIM = 64
RW_WIDTH = RW_HEADS * RW_HEAD_DIM
RW_DECAY_LORA = 96
RW_A_LORA = 96
RW_GATE_LORA = 256
RW_GN_EPS = RW_HEAD_DIM * 1e-5
RW_CHUNK = 64
RW_SUB = 3
RW_PAIRS = RW_WIDTH // 128

RET_HEADS = 8
RET_HEAD_DIM = 128
RET_WIDTH = RET_HEADS * RET_HEAD_DIM
RET_CHUNK = 128
RET_SUB = 3

MLA_HEADS = 8
MLA_NOPE = 128
MLA_ROPE = 64
MLA_V = 128
MLA_Q_RANK = 512
MLA_KV_RANK = 256
MLA_WIDTH = MLA_HEADS * MLA_V
MLA_QK_PAD = 256
ATTN_HEADS_PER_STEP = 2

FFN_HIDDEN = -(-8 * D_MODEL // (3 * 256)) * 256

LANE = 128
SEQ_ALIGN = 384
assert SEQ_ALIGN % (RW_CHUNK * RW_SUB) == 0 and SEQ_ALIGN % (RET_CHUNK * RET_SUB) == 0 and SEQ_ALIGN % LANE == 0

OFF_RET = 0
OFF_GATE = OFF_RET + 4 * RET_WIDTH
OFF_RW = OFF_GATE + 3 * D_MODEL
OFF_RW_WD = OFF_RW + 3 * RW_WIDTH
OFF_RW_AD = OFF_RW_WD + LANE
OFF_RW_GD = OFF_RW_AD + LANE
OFF_MLA_Q = OFF_RW_GD + RW_GATE_LORA
OFF_MLA_KV = OFF_MLA_Q + MLA_Q_RANK
OFF_MLA_KR = OFF_MLA_KV + MLA_KV_RANK
P_COLS_USED = OFF_MLA_KR + LANE
P_TILE_N = 512
P_COLS = -(-P_COLS_USED // P_TILE_N) * P_TILE_N
P_SPLIT = OFF_RW
assert P_SPLIT % P_TILE_N == 0

VMEM_LIMIT = 48 * 1024 * 1024


def _params(*sem):
    return pltpu.CompilerParams(dimension_semantics=sem, vmem_limit_bytes=VMEM_LIMIT)


def _sigmoid(x):
    return 1.0 / (1.0 + jnp.exp(-x))


def _dot(a, b):
    return jnp.dot(a, b, preferred_element_type=F32)


def _dot_nt(a, b):
    return lax.dot_general(a, b, (((1,), (1,)), ((), ())), preferred_element_type=F32)


def _dot_tn(a, b):
    return lax.dot_general(a, b, (((0,), (0,)), ((), ())), preferred_element_type=F32)


def _rmsnorm_body(x_ref, g_ref, o_ref):
    x = x_ref[...]
    y = x * lax.rsqrt(jnp.mean(x * x, axis=-1, keepdims=True) + NORM_EPS)
    o_ref[...] = (y * g_ref[...]).astype(o_ref.dtype)


def final_rmsnorm(x, g, batch, first, count, tm):
    m, d = x.shape
    rows_per_batch = m // batch
    assert rows_per_batch % 8 == 0 and first % 8 == 0 and tm % 8 == 0
    tiles = count // tm
    return pl.pallas_call(
        _rmsnorm_body,
        grid=(batch, tiles),
        in_specs=[pl.BlockSpec((pl.Element(tm), pl.Element(d)),
                               lambda b, t: (pl.multiple_of(b * rows_per_batch + first + t * tm, 8), 0)),
                  pl.BlockSpec((1, d), lambda b, t: (0, 0))],
        out_specs=pl.BlockSpec((tm, d), lambda b, t: (b * tiles + t, 0)),
        out_shape=jax.ShapeDtypeStruct((batch * count, d), x.dtype),
        compiler_params=_params("parallel", "parallel"),
        name="final_rmsnorm",
    )(x, g.reshape(1, d))


def _row_scale(ssq_ref, d):
    ssq = ssq_ref[...]
    total = ssq[:, 0:1]
    for j in range(1, ssq.shape[1] // LANE):
        total = total + ssq[:, LANE * j:LANE * j + 1]
    return lax.rsqrt(total * (1.0 / d) + NORM_EPS)


def _ssq_block(x):
    return jnp.broadcast_to(jnp.sum(x * x, axis=-1, keepdims=True), (x.shape[0], LANE))


def _stream_prep_body(x_ref, xb_ref, ssq_ref):
    x = x_ref[...]
    xb_ref[...] = x.astype(xb_ref.dtype)
    ssq_ref[...] = _ssq_block(x)


def stream_prep(x, tm):
    m, d = x.shape
    return pl.pallas_call(
        _stream_prep_body,
        grid=(m // tm,),
        in_specs=[pl.BlockSpec((tm, d), lambda i: (i, 0))],
        out_specs=[pl.BlockSpec((tm, d), lambda i: (i, 0)), pl.BlockSpec((tm, LANE), lambda i: (i, 0))],
        out_shape=[jax.ShapeDtypeStruct((m, d), BF16), jax.ShapeDtypeStruct((m, LANE), F32)],
        compiler_params=_params("parallel"),
        name="stream_prep",
    )(x)


def _norm_matmul_body(x_ref, ssq_ref, w_ref, o_ref):
    o_ref[...] = (_row_scale(ssq_ref, x_ref.shape[1]) * _dot(x_ref[...], w_ref[...])).astype(o_ref.dtype)


def norm_matmul(xb, ssq, w, layer, col0, n, tm, tn, out_dtype):
    m, k = xb.shape
    return pl.pallas_call(
        _norm_matmul_body,
        grid=(m // tm, n // tn),
        in_specs=[pl.BlockSpec((tm, k), lambda i, j: (i, 0)), pl.BlockSpec((tm, ssq.shape[1]), lambda i, j: (i, 0)),
                  pl.BlockSpec((None, k, tn), lambda i, j: (layer, 0, col0 // tn + j))],
        out_specs=pl.BlockSpec((tm, tn), lambda i, j: (i, j)),
        out_shape=jax.ShapeDtypeStruct((m, n), out_dtype),
        compiler_params=_params("parallel", "parallel"),
        name="in_proj",
    )(xb, ssq, w)


def _rwkv_body(r_ref, k_ref, v_ref, wd_ref, ad_ref, gd_ref, mu_ref,
               w0_ref, a0_ref, kk_ref, ka_ref, rk_ref, lnw_ref, lnb_ref,
               wup_ref, aup_ref, gup_ref,
               o_ref,
               s_ref, pr_ref, pk_ref, pv_ref, pwd_ref, pad_ref, pgd_ref):
    c = pl.program_id(1)
    C = RW_CHUNK
    RS = RW_SUB * C
    HD = RW_HEAD_DIM

    @pl.when(c == 0)
    def _():
        s_ref[...] = jnp.zeros_like(s_ref)
        pr_ref[...] = jnp.zeros_like(pr_ref)
        pk_ref[...] = jnp.zeros_like(pk_ref)
        pv_ref[...] = jnp.zeros_like(pv_ref)
        pwd_ref[...] = jnp.zeros_like(pwd_ref)
        pad_ref[...] = jnp.zeros_like(pad_ref)
        pgd_ref[...] = jnp.zeros_like(pgd_ref)

    def shift(x_ref, prev_ref, mu, sl):
        z = x_ref[:, sl]
        first = lax.broadcasted_iota(jnp.int32, z.shape, 0) == 0
        zs = jnp.where(first, prev_ref[0:1, sl], pltpu.roll(z, 1, 0))
        prev_ref[0:1, sl] = z[RS - 1:RS, :]
        return z + (zs - z) * mu

    mu_lora = 3 * RW_WIDTH
    full = slice(None)
    wd = shift(wd_ref, pwd_ref, mu_ref[:, mu_lora:mu_lora + LANE], full)
    ad = shift(ad_ref, pad_ref, mu_ref[:, mu_lora + LANE:mu_lora + 2 * LANE], full)
    gd = shift(gd_ref, pgd_ref, mu_ref[:, mu_lora + 2 * LANE:], full)
    subs = range(RW_SUB)

    def rows(t, s):
        return t[C * s:C * (s + 1)]

    tanh_wd = [rows(jnp.tanh(wd), s).astype(BF16) for s in subs]
    ad_b = [rows(ad, s).astype(BF16) for s in subs]
    sig_gd = [rows(_sigmoid(gd), s).astype(BF16) for s in subs]

    lane_sq = lax.broadcasted_iota(jnp.int32, (LANE, LANE), 1)
    row_sq = lax.broadcasted_iota(jnp.int32, (LANE, LANE), 0)
    same_head = (lane_sq < HD) == (row_sq < HD)
    head_ones = same_head.astype(BF16)
    eye = (lane_sq == row_sq).astype(F32)
    rc = lax.broadcasted_iota(jnp.int32, (C, C), 0)
    cc = lax.broadcasted_iota(jnp.int32, (C, C), 1)
    tril_incl = (cc <= rc).astype(BF16)
    lane_tall = lax.broadcasted_iota(jnp.int32, (2 * C, LANE), 1)
    lane_c = lax.broadcasted_iota(jnp.int32, (C, LANE), 1)
    row_c = lax.broadcasted_iota(jnp.int32, (C, LANE), 0)
    head0_c = lane_c < HD
    strict_lo = lane_c < row_c
    strict_hi = (lane_c >= C) & (lane_c - C < row_c)
    incl_lo = lane_c <= row_c
    incl_hi = (lane_c >= C) & (lane_c - C <= row_c)
    inv_n = 1.0 / HD

    P = range(RW_PAIRS)

    def head_sums(ts):
        his = [t.astype(BF16) for t in ts]
        los = [(t - hi.astype(F32)).astype(BF16) for t, hi in zip(ts, his)]
        out = _dot(jnp.concatenate(his + los, axis=0), head_ones)
        n = len(ts)
        return [out[C * i:C * (i + 1)] + out[C * (n + i):C * (n + i + 1)] for i in range(n)]

    pair_sls = [slice(LANE * i, LANE * (i + 1)) for i in P]

    def each(fn, *lists):
        return [fn(*args) for args in zip(*lists)]

    def items(per_pair):
        return [rows(t, s) for s in subs for t in per_pair]

    r = items([shift(r_ref, pr_ref, mu_ref[:, sl], sl) for sl in pair_sls])
    k = items([shift(k_ref, pk_ref, mu_ref[:, RW_WIDTH + sl.start:RW_WIDTH + sl.stop], sl) for sl in pair_sls])
    v = items([shift(v_ref, pv_ref, mu_ref[:, 2 * RW_WIDTH + sl.start:2 * RW_WIDTH + sl.stop], sl)
               for sl in pair_sls])
    sls = pair_sls * RW_SUB
    sub_of = [s for s in subs for _ in P]

    def log_decay(s, sl):
        x = -(w0_ref[:, sl] + _dot(tanh_wd[s], wup_ref[:, sl]))
        softplus = jnp.maximum(x, 0.0) + jnp.log1p(jnp.exp(-jnp.abs(x)))
        return -jnp.exp(-softplus - 0.5)

    lw = each(log_decay, sub_of, sls)
    a = each(lambda s, sl: _sigmoid(a0_ref[:, sl] + _dot(ad_b[s], aup_ref[:, sl])), sub_of, sls)
    g = each(lambda s, sl: _dot(sig_gd[s], gup_ref[:, sl]), sub_of, sls)

    kkr = each(lambda ki, sl: ki * kk_ref[:, sl], k, sls)
    ksq = head_sums(each(lambda t: t * t, kkr))
    kkn = each(lambda t, ss: t / jnp.maximum(jnp.sqrt(ss), 1e-12), kkr, ksq)
    kmod = each(lambda ki, ai, sl: ki * (1.0 + (ai - 1.0) * ka_ref[:, sl]), k, a, sls)
    beta = each(lambda ai, t: ai * t, a, kkn)

    def running_sum(lwi):
        hi = lwi.astype(BF16)
        both = _dot(tril_incl, jnp.concatenate([hi, (lwi - hi.astype(F32)).astype(BF16)], axis=1))
        return both[:, :LANE] + both[:, LANE:]

    lcum = each(running_sum, lw)
    lend = each(lambda t: t[C - 1:C, :], lcum)
    rh = each(lambda ri, lc: ri * jnp.exp(lc), r, lcum)
    kh = each(lambda t, lc, lwi: t * jnp.exp(lc - lwi), kkn, lcum, lw)
    e_neg = each(lambda lc: jnp.exp(-lc), lcum)
    e_end = each(lambda le, lc: jnp.exp(le - lc), lend, lcum)
    kb = each(lambda t, e: t * e, kmod, e_neg)
    bb = each(lambda t, e: t * e, beta, e_neg)
    kbe = each(lambda t, e: t * e, kmod, e_end)
    bbe = each(lambda t, e: t * e, beta, e_end)

    kr_f = each(lambda x1, x2: jnp.concatenate([x1, x2], axis=0), kh, rh)
    bk = each(lambda x1, x2: jnp.concatenate([x1, x2], axis=0).astype(BF16), bb, kb)
    vb = each(lambda t: t.astype(BF16), v)

    def gram(krf, bki):
        kr2 = jnp.concatenate([jnp.where(lane_tall < HD, krf, 0.0), jnp.where(lane_tall >= HD, krf, 0.0)], axis=0)
        return _dot_nt(kr2.astype(BF16), bki)

    g_all = each(gram, kr_f, bk)

    n_bd = each(lambda ga: jnp.concatenate([jnp.where(strict_lo, -ga[0:C], 0.0),
                                            jnp.where(strict_hi, -pltpu.roll(ga[2 * C:3 * C], C, 1), 0.0)], axis=0),
                g_all)
    t = each(lambda n: eye + n, n_bd)
    pw = each(lambda n: _dot(n.astype(BF16), n.astype(BF16)), n_bd)
    for _ in range(4):
        both = each(lambda ti, pi: _dot(jnp.concatenate([ti, pi], axis=0).astype(BF16), pi.astype(BF16)), t, pw)
        t = each(lambda ti, bi: ti + bi[:LANE], t, both)
        pw = each(lambda bi: bi[LANE:], both)
    t = each(lambda ti, pi: ti + _dot(ti.astype(BF16), pi.astype(BF16)), t, pw)

    def intra_rhs(ga, vbi):
        m1s = jnp.concatenate([jnp.where(strict_hi, ga[0:C], 0.0), jnp.where(strict_hi, ga[2 * C:3 * C], 0.0)], axis=0)
        return _dot(m1s.astype(BF16), jnp.concatenate([vbi, vbi], axis=0))

    q_intra = each(intra_rhs, g_all, vb)

    def m2(gb):
        return jnp.where(incl_lo, -gb, jnp.where(incl_hi, gb, 0.0))

    m2s = each(lambda ga: jnp.concatenate([m2(ga[C:2 * C]), m2(ga[3 * C:4 * C])], axis=0).astype(BF16), g_all)
    kbe_all = each(lambda kbei, bbei: jnp.concatenate([kbei, -bbei], axis=0).astype(BF16), kbe, bbe)
    s_decay = each(jnp.exp, lend)

    state = [s_ref[i] for i in P]
    y = []
    for s in subs:
        of = lambda lst: lst[RW_PAIRS * s:RW_PAIRS * (s + 1)]
        p_all = each(lambda x1, si: _dot_nt(x1.astype(BF16), si.astype(BF16)), of(kr_f), state)
        q_s = each(lambda qi, pa: jnp.where(same_head, qi + jnp.concatenate([pa[:C], pa[:C]], axis=0), 0.0),
                   of(q_intra), p_all)
        u_s = each(lambda ti, qi: _dot(ti.astype(BF16), qi.astype(BF16)), of(t), q_s)
        u = each(lambda us: us[:C] + us[C:], u_s)
        y_s = each(lambda mi, ui, vi: _dot(mi, jnp.concatenate([ui, vi], axis=0).astype(BF16)), of(m2s), u, of(v))
        y += each(lambda pa, ys: pa[C:] + jnp.where(head0_c, ys[:C], ys[C:]), p_all, y_s)
        ds = each(lambda vi, ui, kb_all: _dot_tn(jnp.concatenate([vi, ui], axis=0).astype(BF16), kb_all),
                  of(v), u, of(kbe_all))
        state = each(lambda si, di, dec: si * dec + jnp.where(same_head, di, 0.0), state, ds, of(s_decay))
    for i in P:
        s_ref[i] = state[i]

    d = each(lambda yi, si: yi - si * inv_n, y, head_sums(y))
    var = each(lambda si: si * inv_n, head_sums(each(lambda di: di * di, d)))
    bsum = head_sums(each(lambda ri, ki, sl: ri * ki * rk_ref[:, sl], r, kmod, sls))
    for i, (s, sl) in enumerate(zip(sub_of, sls)):
        yn = d[i] * lax.rsqrt(var[i] + RW_GN_EPS) * lnw_ref[:, sl] + lnb_ref[:, sl]
        o_ref[C * s:C * (s + 1), sl] = ((yn + bsum[i] * v[i]) * g[i]).astype(o_ref.dtype)


def rwkv_mix(p, batch, mu, w0, a0, k_k, k_a, r_k, ln_w, ln_b, w_up, a_up, g_up):
    tp = p.shape[0]
    lp = tp // batch
    C = RW_CHUNK * RW_SUB
    nchunk = lp // C

    def pspec(width, base):
        return pl.BlockSpec((C, width), lambda b, c: (b * nchunk + c, base // width))

    def const(shape):
        return pl.BlockSpec(shape, lambda b, c: (0, 0))

    in_specs = [
        pspec(RW_WIDTH, OFF_RW - P_SPLIT), pspec(RW_WIDTH, OFF_RW - P_SPLIT + RW_WIDTH),
        pspec(RW_WIDTH, OFF_RW - P_SPLIT + 2 * RW_WIDTH),
        pspec(LANE, OFF_RW_WD - P_SPLIT), pspec(LANE, OFF_RW_AD - P_SPLIT), pspec(RW_GATE_LORA, OFF_RW_GD - P_SPLIT),
        const(mu.shape),
    ] + [const((1, RW_WIDTH))] * 7 + [const(w_up.shape), const(a_up.shape), const(g_up.shape)]
    row = lambda t: t.reshape(1, -1)
    return pl.pallas_call(
        _rwkv_body,
        grid=(batch, nchunk),
        in_specs=in_specs,
        out_specs=pl.BlockSpec((C, RW_WIDTH), lambda b, c: (b * nchunk + c, 0)),
        out_shape=jax.ShapeDtypeStruct((tp, RW_WIDTH), BF16),
        scratch_shapes=[pltpu.VMEM((RW_PAIRS, LANE, LANE), F32)] + [pltpu.VMEM((8, RW_WIDTH), F32)] * 3
        + [pltpu.VMEM((8, LANE), F32)] * 2 + [pltpu.VMEM((8, RW_GATE_LORA), F32)],
        compiler_params=_params("parallel", "arbitrary"),
        name="rwkv7_mix",
    )(p, p, p, p, p, p, mu, row(w0), row(a0), row(k_k), row(k_a), row(r_k), row(ln_w), row(ln_b),
      w_up, a_up, g_up)


def _ret_body(q_ref, k_ref, v_ref, g_ref, cos_ref, sin_ref, o_ref, state_ref):
    c = pl.program_id(1)
    C = RET_CHUNK
    d = RET_HEAD_DIM

    @pl.when(c == 0)
    def _():
        state_ref[...] = jnp.zeros_like(state_ref)

    row = lax.broadcasted_iota(jnp.int32, (C, C), 0).astype(F32)
    col = lax.broadcasted_iota(jnp.int32, (C, C), 1).astype(F32)
    diff = row - col
    causal = diff >= 0
    heads = range(RET_HEADS)
    subs = range(RET_SUB)
    rws = [slice(C * s, C * (s + 1)) for s in subs for _ in heads]
    sls = [slice(d * h, d * (h + 1)) for _ in subs for h in heads]
    lgs = [math.log1p(-(2.0 ** (-5.0 - h))) for _ in subs for h in heads]

    def each(fn, *lists):
        return [fn(*args) for args in zip(*lists)]

    def rope(x_ref, rw, sl):
        x = x_ref[rw, sl].astype(F32)
        return x * cos_ref[rw, :] + pltpu.roll(x, d // 2, 1) * sin_ref[rw, :]

    qb = each(lambda rw, sl: rope(q_ref, rw, sl).astype(BF16), rws, sls)
    k = each(lambda rw, sl: rope(k_ref, rw, sl) * (d ** -0.5), rws, sls)
    kb = each(lambda t: t.astype(BF16), k)
    vb = each(lambda rw, sl: v_ref[rw, sl], rws, sls)
    s = each(lambda qi, ki, lg: _dot_nt(qi, ki) * jnp.where(causal, jnp.exp(lg * jnp.maximum(diff, 0.0)), 0.0),
             qb, kb, lgs)
    o_intra = each(lambda si, vi: _dot(si.astype(BF16), vi), s, vb)
    kd = each(lambda ki, lg: (ki * jnp.exp(lg * (C - 1.0 - row))).astype(BF16), k, lgs)
    kv = each(_dot_tn, kd, vb)

    state = [state_ref[h] for h in heads]
    o = []
    for sub in subs:
        of = lambda lst: lst[RET_HEADS * sub:RET_HEADS * (sub + 1)]
        o += each(lambda oi, qi, st, lg: oi + _dot(qi, st.astype(BF16)) * jnp.exp(lg * (row + 1.0)),
                  of(o_intra), of(qb), state, of(lgs))
        state = each(lambda st, kvi, lg: st * math.exp(lg * C) + kvi, state, of(kv), of(lgs))
    for h in heads:
        state_ref[h] = state[h]
    o = each(lambda oi: oi * lax.rsqrt(jnp.mean(oi * oi, axis=-1, keepdims=True) + NORM_EPS), o)
    for oi, rw, sl in zip(o, rws, sls):
        g = g_ref[rw, sl].astype(F32)
        o_ref[rw, sl] = (g * _sigmoid(g) * oi).astype(o_ref.dtype)


def retention_mix(p, batch, cos, sin):
    tp = p.shape[0]
    lp = tp // batch
    C = RET_CHUNK * RET_SUB
    nchunk = lp // C
    base = OFF_RET // RET_WIDTH

    def pspec(j):
        return pl.BlockSpec((C, RET_WIDTH), lambda b, c: (b * nchunk + c, base + j))

    tab = pl.BlockSpec((C, RET_HEAD_DIM), lambda b, c: (c, 0))
    return pl.pallas_call(
        _ret_body,
        grid=(batch, nchunk),
        in_specs=[pspec(0), pspec(1), pspec(2), pspec(3), tab, tab],
        out_specs=pl.BlockSpec((C, RET_WIDTH), lambda b, c: (b * nchunk + c, 0)),
        out_shape=jax.ShapeDtypeStruct((tp, RET_WIDTH), BF16),
        scratch_shapes=[pltpu.VMEM((RET_HEADS, RET_HEAD_DIM, RET_HEAD_DIM), F32)],
        compiler_params=_params("parallel", "arbitrary"),
        name="retention_mix",
    )(p, p, p, p, cos, sin)


def _mla_proj_body(qd_ref, kvd_ref, krd_ref, nq_ref, nkv_ref, wuq_ref, wukv_ref, cos_ref, sin_ref,
                   q_out, k_out, v_out):
    cos = cos_ref[...]
    sin = sin_ref[...]

    def rope(x):
        return x * cos + (pltpu.roll(x, MLA_ROPE // 2, 1) + pltpu.roll(x, LANE - MLA_ROPE // 2, 1)) * sin

    def norm(x, g):
        return x * lax.rsqrt(jnp.mean(x * x, axis=-1, keepdims=True) + NORM_EPS) * g

    scale = (MLA_NOPE + MLA_ROPE) ** -0.5 * math.log2(math.e)
    q = _dot(norm(qd_ref[...], nq_ref[...]).astype(BF16), wuq_ref[...]) * scale
    kv = _dot(norm(kvd_ref[...], nkv_ref[...]).astype(BF16), wukv_ref[...])
    kr = rope(krd_ref[...]).astype(k_out.dtype)
    for h in range(MLA_HEADS):
        lo = MLA_QK_PAD * h
        q_out[:, lo:lo + LANE] = q[:, lo:lo + LANE].astype(q_out.dtype)
        q_out[:, lo + LANE:lo + 2 * LANE] = rope(q[:, lo + LANE:lo + 2 * LANE]).astype(q_out.dtype)
        k_out[:, lo:lo + LANE] = kv[:, MLA_NOPE * h:MLA_NOPE * (h + 1)].astype(k_out.dtype)
        k_out[:, lo + LANE:lo + 2 * LANE] = kr
    v_out[...] = kv[:, MLA_HEADS * MLA_NOPE:].astype(v_out.dtype)


def mla_proj(p, batch, norm_q, norm_kv, w_uq, w_ukv, cos, sin, tm):
    tp = p.shape[0]
    lp = tp // batch
    per_seq = lp // tm
    qk_w = MLA_HEADS * MLA_QK_PAD
    const = lambda i: (0, 0)
    return pl.pallas_call(
        _mla_proj_body,
        grid=(tp // tm,),
        in_specs=[
            pl.BlockSpec((tm, MLA_Q_RANK), lambda i: (i, (OFF_MLA_Q - P_SPLIT) // MLA_Q_RANK)),
            pl.BlockSpec((tm, MLA_KV_RANK), lambda i: (i, (OFF_MLA_KV - P_SPLIT) // MLA_KV_RANK)),
            pl.BlockSpec((tm, LANE), lambda i: (i, (OFF_MLA_KR - P_SPLIT) // LANE)),
            pl.BlockSpec((1, MLA_Q_RANK), const),
            pl.BlockSpec((1, MLA_KV_RANK), const),
            pl.BlockSpec((MLA_Q_RANK, qk_w), const),
            pl.BlockSpec((MLA_KV_RANK, MLA_HEADS * (MLA_NOPE + MLA_V)), const),
            pl.BlockSpec((tm, LANE), lambda i: (i % per_seq, 0)),
            pl.BlockSpec((tm, LANE), lambda i: (i % per_seq, 0)),
        ],
        out_specs=[
            pl.BlockSpec((tm, qk_w), lambda i: (i, 0)),
            pl.BlockSpec((tm, qk_w), lambda i: (i, 0)),
            pl.BlockSpec((tm, MLA_WIDTH), lambda i: (i, 0)),
        ],
        out_shape=[
            jax.ShapeDtypeStruct((tp, qk_w), BF16),
            jax.ShapeDtypeStruct((tp, qk_w), BF16),
            jax.ShapeDtypeStruct((tp, MLA_WIDTH), BF16),
        ],
        compiler_params=_params("parallel"),
        name="mla_proj",
    )(p, p, p, norm_q.reshape(1, -1), norm_kv.reshape(1, -1), w_uq, w_ukv, cos, sin)


def _attn_body(q_ref, k_ref, v_ref, o_ref, *, tq):
    i = pl.program_id(2)
    heads = range(ATTN_HEADS_PER_STEP)
    qs = [q_ref[:, MLA_QK_PAD * h:MLA_QK_PAD * (h + 1)] for h in heads]

    def step(off, width, masked, carry):
        ms, ls, accs = carry
        off = pl.multiple_of(off, tq)
        ss = [_dot_nt(qs[h], k_ref[pl.ds(off, width), MLA_QK_PAD * h:MLA_QK_PAD * (h + 1)]) for h in heads]
        if masked:
            row = i * tq + lax.broadcasted_iota(jnp.int32, (tq, width), 0)
            col = off + lax.broadcasted_iota(jnp.int32, (tq, width), 1)
            ss = [jnp.where(col <= row, s, -jnp.inf) for s in ss]
        m_new = [jnp.maximum(ms[h], jnp.max(ss[h], axis=-1, keepdims=True)) for h in heads]
        alpha = [jnp.exp2(ms[h] - m_new[h]) for h in heads]
        ps = [jnp.exp2(ss[h] - m_new[h]) for h in heads]
        ls = [alpha[h] * ls[h] + jnp.sum(ps[h], axis=-1, keepdims=True) for h in heads]
        pv = [_dot(ps[h].astype(BF16), v_ref[pl.ds(off, width), MLA_V * h:MLA_V * (h + 1)]) for h in heads]
        accs = [alpha[h] * accs[h] + pv[h] for h in heads]
        return tuple(m_new), tuple(ls), tuple(accs)

    carry = (tuple(jnp.full((tq, 1), -1e30, F32) for _ in heads),
             tuple(jnp.zeros((tq, 1), F32) for _ in heads),
             tuple(jnp.zeros((tq, MLA_V), F32) for _ in heads))
    n_single = (i + 1) % 2
    n_pairs = (i + 1) // 2
    carry = lax.cond(i == 0, lambda c: step(0, tq, True, c), lambda c: c, carry)
    carry = lax.cond((n_single == 1) & (i > 0), lambda c: step(0, tq, False, c), lambda c: c, carry)
    pair_off = lambda p: (n_single + 2 * p) * tq
    carry = lax.fori_loop(0, n_pairs - 1, lambda p, c: step(pair_off(p), 2 * tq, False, c), carry)
    carry = lax.cond(n_pairs > 0, lambda c: step(pair_off(n_pairs - 1), 2 * tq, True, c), lambda c: c, carry)
    _, ls, accs = carry
    for h in heads:
        o_ref[:, MLA_V * h:MLA_V * (h + 1)] = (accs[h] / ls[h]).astype(o_ref.dtype)


def mla_attention(q, k, v, batch, tq):
    tp = q.shape[0]
    lp = tp // batch
    nq = lp // tq
    hs = ATTN_HEADS_PER_STEP
    return pl.pallas_call(
        functools.partial(_attn_body, tq=tq),
        grid=(batch, MLA_HEADS // hs, nq),
        in_specs=[
            pl.BlockSpec((tq, hs * MLA_QK_PAD), lambda b, h, i: (b * nq + i, h)),
            pl.BlockSpec((lp, hs * MLA_QK_PAD), lambda b, h, i: (b, h)),
            pl.BlockSpec((lp, hs * MLA_V), lambda b, h, i: (b, h)),
        ],
        out_specs=pl.BlockSpec((tq, hs * MLA_V), lambda b, h, i: (b * nq + i, h)),
        out_shape=jax.ShapeDtypeStruct((tp, MLA_WIDTH), BF16),
        compiler_params=_params("parallel", "parallel", "arbitrary"),
        name="mla_attention",
    )(q, k, v)


def _merge_body(ya_ref, yb_ref, yc_ref, wa_ref, wb_ref, wc_ref, ga_ref, gb_ref, gc_ref, o_ref):
    def branch(y_ref, w_ref, g_ref):
        return _sigmoid(g_ref[...].astype(F32)) * _dot(y_ref[...], w_ref[...])

    o_ref[...] = (branch(ya_ref, wa_ref, ga_ref) + branch(yb_ref, wb_ref, gb_ref)
                  + branch(yc_ref, wc_ref, gc_ref)).astype(o_ref.dtype)


def merge_branches(ya, yb, yc, wa, wb, wc, layer, p, tm, tn):
    tp = ya.shape[0]

    def yspec(width):
        return pl.BlockSpec((tm, width), lambda i, j: (i, 0))

    def wspec(width):
        return pl.BlockSpec((None, width, tn), lambda i, j: (layer, 0, j))

    def gspec(branch):
        base = (OFF_GATE + branch * D_MODEL) // tn
        return pl.BlockSpec((tm, tn), lambda i, j: (i, base + j))

    return pl.pallas_call(
        _merge_body,
        grid=(tp // tm, D_MODEL // tn),
        in_specs=[yspec(RW_WIDTH), yspec(RET_WIDTH), yspec(MLA_WIDTH),
                  wspec(RW_WIDTH), wspec(RET_WIDTH), wspec(MLA_WIDTH),
                  gspec(0), gspec(1), gspec(2)],
        out_specs=pl.BlockSpec((tm, tn), lambda i, j: (i, j)),
        out_shape=jax.ShapeDtypeStruct((tp, D_MODEL), BF16),
        compiler_params=_params("parallel", "parallel"),
        name="merge_branches",
    )(ya, yb, yc, wa, wb, wc, p, p, p)


def _resid_body(x_ref, w_ref, h_ref, o_ref, ob_ref, ssq_ref):
    hn = h_ref[...] + _dot(x_ref[...], w_ref[...])
    o_ref[...] = hn
    ob_ref[...] = hn.astype(ob_ref.dtype)
    ssq_ref[...] = _ssq_block(hn)


def resid_matmul(x, w, layer, h, tm, tn):
    m, kdim = x.shape
    n = w.shape[2]
    tile = pl.BlockSpec((tm, tn), lambda i, j: (i, j))
    return pl.pallas_call(
        _resid_body,
        grid=(m // tm, n // tn),
        in_specs=[
            pl.BlockSpec((tm, kdim), lambda i, j: (i, 0)),
            pl.BlockSpec((None, kdim, tn), lambda i, j: (layer, 0, j)),
            tile,
        ],
        out_specs=[tile, tile, pl.BlockSpec((tm, LANE), lambda i, j: (i, j))],
        out_shape=[jax.ShapeDtypeStruct((m, n), F32), jax.ShapeDtypeStruct((m, n), BF16),
                   jax.ShapeDtypeStruct((m, LANE * (n // tn)), F32)],
        compiler_params=_params("parallel", "parallel"),
        name="resid_matmul",
    )(x, w, h)


def _ffn_up_body(x_ref, ssq_ref, wg_ref, wu_ref, o_ref):
    x = x_ref[...]
    r = _row_scale(ssq_ref, x.shape[1])
    hg = r * _dot(x, wg_ref[...])
    hu = r * _dot(x, wu_ref[...])
    o_ref[...] = (hg * _sigmoid(hg) * hu).astype(o_ref.dtype)


def ffn_up(xb, ssq, w_gate_up, layer, tm, tn):
    m, kdim = xb.shape
    hidden = w_gate_up.shape[2] // 2
    nj = hidden // tn
    return pl.pallas_call(
        _ffn_up_body,
        grid=(m // tm, nj),
        in_specs=[
            pl.BlockSpec((tm, kdim), lambda i, j: (i, 0)),
            pl.BlockSpec((tm, ssq.shape[1]), lambda i, j: (i, 0)),
            pl.BlockSpec((None, kdim, tn), lambda i, j: (layer, 0, j)),
            pl.BlockSpec((None, kdim, tn), lambda i, j: (layer, 0, nj + j)),
        ],
        out_specs=pl.BlockSpec((tm, tn), lambda i, j: (i, j)),
        out_shape=jax.ShapeDtypeStruct((m, hidden), BF16),
        compiler_params=_params("parallel", "parallel"),
        name="ffn_up",
    )(xb, ssq, w_gate_up, w_gate_up)


def _pad_cols(w, width):
    return jnp.pad(w, [(0, 0)] * (w.ndim - 1) + [(0, width - w.shape[-1])])


def _w_in_pieces():
    widths = [3 * RW_WIDTH, RW_DECAY_LORA, RW_A_LORA, RW_GATE_LORA, 4 * RET_WIDTH, MLA_Q_RANK, MLA_KV_RANK, MLA_ROPE,
              3 * D_MODEL]
    dsts = [OFF_RW, OFF_RW_WD, OFF_RW_AD, OFF_RW_GD, OFF_RET, OFF_MLA_Q, OFF_MLA_KV, OFF_MLA_KR, OFF_GATE]
    pieces, src = [], 0
    for dst, width in zip(dsts, widths):
        pieces.append((dst, src, width))
        src += width
    return pieces


def _pack_w_in_body(w_ref, g_ref, o_ref):
    g = g_ref[...]
    covered = 0
    for dst, src, width in sorted(_w_in_pieces()):
        if dst > covered:
            o_ref[:, covered:dst] = jnp.zeros((o_ref.shape[0], dst - covered), o_ref.dtype)
        o_ref[:, dst:dst + width] = (w_ref[:, src:src + width] * g).astype(o_ref.dtype)
        covered = dst + width
    o_ref[:, covered:] = jnp.zeros((o_ref.shape[0], o_ref.shape[1] - covered), o_ref.dtype)


def _pack_w_in(w_in, gain, tr=128):
    nl, rows, cols = w_in.shape
    return pl.pallas_call(
        _pack_w_in_body,
        grid=(nl, rows // tr),
        in_specs=[pl.BlockSpec((None, tr, cols), lambda l, i: (l, i, 0)),
                  pl.BlockSpec((None, tr, 1), lambda l, i: (l, i, 0))],
        out_specs=pl.BlockSpec((None, tr, P_COLS), lambda l, i: (l, i, 0)),
        out_shape=jax.ShapeDtypeStruct((nl, rows, P_COLS), BF16),
        compiler_params=_params("parallel", "parallel"),
        name="pack_w_in",
    )(w_in, gain[..., None])


def _pack_mu(mu):
    rkv = mu[..., :3 * RW_WIDTH]
    wd = mu[..., 3 * RW_WIDTH:3 * RW_WIDTH + RW_DECAY_LORA]
    ad = mu[..., 3 * RW_WIDTH + RW_DECAY_LORA:3 * RW_WIDTH + RW_DECAY_LORA + RW_A_LORA]
    gd = mu[..., 3 * RW_WIDTH + RW_DECAY_LORA + RW_A_LORA:]
    return jnp.concatenate([rkv, _pad_cols(wd, LANE), _pad_cols(ad, LANE), gd], axis=-1)


def _pad_rows(w, rows):
    return jnp.pad(w, [(0, 0)] * (w.ndim - 2) + [(0, rows - w.shape[-2]), (0, 0)])


def _pack_w_uq(w):
    nl, rank, _ = w.shape
    w = w.reshape(nl, rank, MLA_HEADS, MLA_NOPE + MLA_ROPE)
    w = jnp.pad(w, ((0, 0), (0, 0), (0, 0), (0, MLA_QK_PAD - MLA_NOPE - MLA_ROPE)))
    return w.reshape(nl, rank, MLA_HEADS * MLA_QK_PAD).astype(BF16)


def _pack_w_ukv(w):
    nl, rank, _ = w.shape
    w = w.reshape(nl, rank, MLA_HEADS, 2, MLA_NOPE)
    w = jnp.swapaxes(w, 2, 3)
    return w.reshape(nl, rank, 2 * MLA_HEADS * MLA_NOPE).astype(BF16)


def _rope_tables(lp):
    pos = jnp.arange(lp, dtype=F32)

    def tables(dim):
        inv = ROPE_BASE ** (-jnp.arange(0, dim, 2, dtype=F32) / dim)
        ang = pos[:, None] * inv[None, :]
        return jnp.cos(ang), jnp.sin(ang)

    c, s = tables(RET_HEAD_DIM)
    ret = (jnp.concatenate([c, c], axis=1), jnp.concatenate([-s, s], axis=1))
    c, s = tables(MLA_ROPE)
    z = jnp.zeros((lp, LANE - MLA_ROPE), F32)
    mla = (jnp.concatenate([c, c, z], axis=1), jnp.concatenate([-s, s, z], axis=1))
    return ret, mla


def kernel(x, meta_tokens, norm_mix, w_in, rw_mu, rw_w0, rw_w_up, rw_a0, rw_a_up, rw_g_up, rw_k_k, rw_k_a, rw_r_k, rw_ln_w, rw_ln_b, mla_norm_q, mla_norm_kv, mla_w_uq, mla_w_ukv, w_br_rwkv, w_br_ret, w_br_mla, w_out, norm_ffn, w_gate_up, w_down, final_norm):
    batch, seq, d = x.shape
    depth = w_in.shape[0]
    lp = -(-(N_META + seq) // SEQ_ALIGN) * SEQ_ALIGN
    tp = batch * lp

    meta = jnp.broadcast_to(meta_tokens[None].astype(x.dtype), (batch, N_META, d))
    pad = jnp.zeros((batch, lp - N_META - seq, d), x.dtype)
    h = jnp.concatenate([meta, x, pad], axis=1).reshape(tp, d)

    wp = _pack_w_in(w_in, norm_mix)
    mu = _pack_mu(rw_mu)
    w_up = _pad_rows(rw_w_up, LANE).astype(BF16)
    a_up = _pad_rows(rw_a_up, LANE).astype(BF16)
    g_up = rw_g_up.astype(BF16)
    wuq = _pack_w_uq(mla_w_uq)
    wukv = _pack_w_ukv(mla_w_ukv)
    wa = w_br_rwkv.astype(BF16)
    wb = w_br_ret.astype(BF16)
    wc = w_br_mla.astype(BF16)
    wo = w_out.astype(BF16)
    wgu = (w_gate_up * norm_ffn[..., None]).astype(BF16)
    wdn = w_down.astype(BF16)
    (cos_ret, sin_ret), (cos_mla, sin_mla) = _rope_tables(lp)

    def row_tile(pref):
        return next((t for t in pref if tp % t == 0), SEQ_ALIGN)

    tm = row_tile((768,))
    tm_wide = row_tile((1536, 768))
    tm_seq = SEQ_ALIGN

    hb, ssq = stream_prep(h, tm)
    for l in range(depth):
        p_lo = norm_matmul(hb, ssq, wp, l, 0, P_SPLIT, tm_wide, P_TILE_N, BF16)
        p = norm_matmul(hb, ssq, wp, l, P_SPLIT, P_COLS - P_SPLIT, tm_wide, P_TILE_N, F32)
        ya = rwkv_mix(p, batch, mu[l:l + 1], rw_w0[l], rw_a0[l], rw_k_k[l], rw_k_a[l], rw_r_k[l],
                      rw_ln_w[l], rw_ln_b[l], w_up[l], a_up[l], g_up[l])
        yb = retention_mix(p_lo, batch, cos_ret, sin_ret)
        q, k, v = mla_proj(p, batch, mla_norm_q[l], mla_norm_kv[l], wuq[l], wukv[l], cos_mla, sin_mla, tm_seq)
        yc = mla_attention(q, k, v, batch, tm_seq)
        merged = merge_branches(ya, yb, yc, wa, wb, wc, l, p_lo, tm, 1024)
        h, hb, ssq = resid_matmul(merged, wo, l, h, tm_seq, d)
        act = ffn_up(hb, ssq, wgu, l, tm_wide, 512)
        h, hb, ssq = resid_matmul(act, wdn, l, h, tm, 512)
    tm_out = next(t for t in (512, 256, SEQ_ALIGN) if seq % t == 0)
    return final_rmsnorm(h, final_norm, batch, N_META, seq, tm_out).reshape(batch, seq, d)
```

```python
import functools
import math

import jax
import jax.numpy as jnp
from jax import lax
from jax.experimental import pallas as pl
from jax.experimental.pallas import tpu as pltpu

F32 = jnp.float32
BF16 = jnp.bfloat16

D_MODEL = 2048
N_META = 16
NORM_EPS = 1e-6
ROPE_BASE = 10000.0

RW_HEADS = 16
RW_HEAD_DIM = 64
RW_WIDTH = RW_HEADS * RW_HEAD_DIM
RW_DECAY_LORA = 96
RW_A_LORA = 96
RW_GATE_LORA = 256
RW_GN_EPS = RW_HEAD_DIM * 1e-5
RW_CHUNK = 64
RW_SUB = 3
RW_PAIRS = RW_WIDTH // 128

RET_HEADS = 8
RET_HEAD_DIM = 128
RET_WIDTH = RET_HEADS * RET_HEAD_DIM
RET_CHUNK = 128
RET_SUB = 3

MLA_HEADS = 8
MLA_NOPE = 128
MLA_ROPE = 64
MLA_V = 128
MLA_Q_RANK = 512
MLA_KV_RANK = 256
MLA_WIDTH = MLA_HEADS * MLA_V
MLA_QK_PAD = 256
ATTN_HEADS_PER_STEP = 2

FFN_HIDDEN = -(-8 * D_MODEL // (3 * 256)) * 256

LANE = 128
SEQ_ALIGN = 384
assert SEQ_ALIGN % (RW_CHUNK * RW_SUB) == 0 and SEQ_ALIGN % (RET_CHUNK * RET_SUB) == 0 and SEQ_ALIGN % LANE == 0

OFF_RET = 0
OFF_GATE = OFF_RET + 4 * RET_WIDTH
OFF_RW = OFF_GATE + 3 * D_MODEL
OFF_RW_WD = OFF_RW + 3 * RW_WIDTH
OFF_RW_AD = OFF_RW_WD + LANE
OFF_RW_GD = OFF_RW_AD + LANE
OFF_MLA_Q = OFF_RW_GD + RW_GATE_LORA
OFF_MLA_KV = OFF_MLA_Q + MLA_Q_RANK
OFF_MLA_KR = OFF_MLA_KV + MLA_KV_RANK
P_COLS_USED = OFF_MLA_KR + LANE
P_TILE_N = 512
P_COLS = -(-P_COLS_USED // P_TILE_N) * P_TILE_N
P_SPLIT = OFF_RW
assert P_SPLIT % (2 * P_TILE_N) == 0

VMEM_LIMIT = 48 * 1024 * 1024


def _params(*sem):
    return pltpu.CompilerParams(dimension_semantics=sem, vmem_limit_bytes=VMEM_LIMIT)


def _sigmoid(x):
    return 1.0 / (1.0 + jnp.exp(-x))


def _dot(a, b):
    return jnp.dot(a, b, preferred_element_type=F32)


def _dot_nt(a, b):
    return lax.dot_general(a, b, (((1,), (1,)), ((), ())), preferred_element_type=F32)


def _dot_tn(a, b):
    return lax.dot_general(a, b, (((0,), (0,)), ((), ())), preferred_element_type=F32)


def _rmsnorm_body(x_ref, g_ref, o_ref):
    x = x_ref[...]
    y = x * lax.rsqrt(jnp.mean(x * x, axis=-1, keepdims=True) + NORM_EPS)
    o_ref[...] = (y * g_ref[...]).astype(o_ref.dtype)


def final_rmsnorm(x, g, batch, first, count, tm):
    m, d = x.shape
    rows_per_batch = m // batch
    assert rows_per_batch % 8 == 0 and first % 8 == 0 and tm % 8 == 0
    tiles = count // tm
    return pl.pallas_call(
        _rmsnorm_body,
        grid=(batch, tiles),
        in_specs=[pl.BlockSpec((pl.Element(tm), pl.Element(d)),
                               lambda b, t: (pl.multiple_of(b * rows_per_batch + first + t * tm, 8), 0)),
                  pl.BlockSpec((1, d), lambda b, t: (0, 0))],
        out_specs=pl.BlockSpec((tm, d), lambda b, t: (b * tiles + t, 0)),
        out_shape=jax.ShapeDtypeStruct((batch * count, d), x.dtype),
        compiler_params=_params("parallel", "parallel"),
        name="final_rmsnorm",
    )(x, g.reshape(1, d))


def _row_scale(ssq_ref, d):
    ssq = ssq_ref[...]
    total = ssq[:, 0:1]
    for j in range(1, ssq.shape[1] // LANE):
        total = total + ssq[:, LANE * j:LANE * j + 1]
    return lax.rsqrt(total * (1.0 / d) + NORM_EPS)


def _ssq_block(x):
    return jnp.broadcast_to(jnp.sum(x * x, axis=-1, keepdims=True), (x.shape[0], LANE))


def _stream_prep_body(x_ref, xb_ref, ssq_ref):
    x = x_ref[...]
    xb_ref[...] = x.astype(xb_ref.dtype)
    ssq_ref[...] = _ssq_block(x)


def stream_prep(x, tm):
    m, d = x.shape
    return pl.pallas_call(
        _stream_prep_body,
        grid=(m // tm,),
        in_specs=[pl.BlockSpec((tm, d), lambda i: (i, 0))],
        out_specs=[pl.BlockSpec((tm, d), lambda i: (i, 0)), pl.BlockSpec((tm, LANE), lambda i: (i, 0))],
        out_shape=[jax.ShapeDtypeStruct((m, d), BF16), jax.ShapeDtypeStruct((m, LANE), F32)],
        compiler_params=_params("parallel"),
        name="stream_prep",
    )(x)


def _norm_matmul_body(x_ref, ssq_ref, w_ref, o_ref):
    o_ref[...] = (_row_scale(ssq_ref, x_ref.shape[1]) * _dot(x_ref[...], w_ref[...])).astype(o_ref.dtype)


def norm_matmul(xb, ssq, w, layer, col0, n, tm, tn, out_dtype):
    m, k = xb.shape
    return pl.pallas_call(
        _norm_matmul_body,
        grid=(m // tm, n // tn),
        in_specs=[pl.BlockSpec((tm, k), lambda i, j: (i, 0)), pl.BlockSpec((tm, ssq.shape[1]), lambda i, j: (i, 0)),
                  pl.BlockSpec((None, k, tn), lambda i, j: (layer, 0, col0 // tn + j))],
        out_specs=pl.BlockSpec((tm, tn), lambda i, j: (i, j)),
        out_shape=jax.ShapeDtypeStruct((m, n), out_dtype),
        compiler_params=_params("parallel", "parallel"),
        name="in_proj",
    )(xb, ssq, w)


def _rwkv_body(r_ref, k_ref, v_ref, wd_ref, ad_ref, gd_ref, mu_ref,
               w0_ref, a0_ref, kk_ref, ka_ref, rk_ref, lnw_ref, lnb_ref,
               wup_ref, aup_ref, gup_ref,
               o_ref,
               s_ref, pr_ref, pk_ref, pv_ref, pwd_ref, pad_ref, pgd_ref):
    c = pl.program_id(1)
    C = RW_CHUNK
    RS = RW_SUB * C
    HD = RW_HEAD_DIM

    @pl.when(c == 0)
    def _():
        s_ref[...] = jnp.zeros_like(s_ref)
        pr_ref[...] = jnp.zeros_like(pr_ref)
        pk_ref[...] = jnp.zeros_like(pk_ref)
        pv_ref[...] = jnp.zeros_like(pv_ref)
        pwd_ref[...] = jnp.zeros_like(pwd_ref)
        pad_ref[...] = jnp.zeros_like(pad_ref)
        pgd_ref[...] = jnp.zeros_like(pgd_ref)

    def shift(x_ref, prev_ref, mu, sl):
        z = x_ref[:, sl]
        first = lax.broadcasted_iota(jnp.int32, z.shape, 0) == 0
        zs = jnp.where(first, prev_ref[0:1, sl], pltpu.roll(z, 1, 0))
        prev_ref[0:1, sl] = z[RS - 1:RS, :]
        return z + (zs - z) * mu

    mu_lora = 3 * RW_WIDTH
    full = slice(None)
    wd = shift(wd_ref, pwd_ref, mu_ref[:, mu_lora:mu_lora + LANE], full)
    ad = shift(ad_ref, pad_ref, mu_ref[:, mu_lora + LANE:mu_lora + 2 * LANE], full)
    gd = shift(gd_ref, pgd_ref, mu_ref[:, mu_lora + 2 * LANE:], full)
    subs = range(RW_SUB)

    def rows(t, s):
        return t[C * s:C * (s + 1)]

    tanh_wd = [rows(jnp.tanh(wd), s).astype(BF16) for s in subs]
    ad_b = [rows(ad, s).astype(BF16) for s in subs]
    sig_gd = [rows(_sigmoid(gd), s).astype(BF16) for s in subs]

    lane_sq = lax.broadcasted_iota(jnp.int32, (LANE, LANE), 1)
    row_sq = lax.broadcasted_iota(jnp.int32, (LANE, LANE), 0)
    same_head = (lane_sq < HD) == (row_sq < HD)
    head_ones = same_head.astype(BF16)
    eye = (lane_sq == row_sq).astype(F32)
    rc = lax.broadcasted_iota(jnp.int32, (C, C), 0)
    cc = lax.broadcasted_iota(jnp.int32, (C, C), 1)
    tril_incl = (cc <= rc).astype(BF16)
    lane_tall = lax.broadcasted_iota(jnp.int32, (2 * C, LANE), 1)
    lane_c = lax.broadcasted_iota(jnp.int32, (C, LANE), 1)
    row_c = lax.broadcasted_iota(jnp.int32, (C, LANE), 0)
    head0_c = lane_c < HD
    strict_lo = lane_c < row_c
    strict_hi = (lane_c >= C) & (lane_c - C < row_c)
    incl_lo = lane_c <= row_c
    incl_hi = (lane_c >= C) & (lane_c - C <= row_c)
    inv_n = 1.0 / HD

    P = range(RW_PAIRS)

    def head_sums(ts):
        out = _dot(jnp.concatenate([t.astype(BF16) for t in ts], axis=0), head_ones)
        return [out[C * i:C * (i + 1)] for i in range(len(ts))]

    pair_sls = [slice(LANE * i, LANE * (i + 1)) for i in P]

    def each(fn, *lists):
        return [fn(*args) for args in zip(*lists)]

    def items(per_pair):
        return [rows(t, s) for s in subs for t in per_pair]

    r = items([shift(r_ref, pr_ref, mu_ref[:, sl], sl) for sl in pair_sls])
    k = items([shift(k_ref, pk_ref, mu_ref[:, RW_WIDTH + sl.start:RW_WIDTH + sl.stop], sl) for sl in pair_sls])
    v = items([shift(v_ref, pv_ref, mu_ref[:, 2 * RW_WIDTH + sl.start:2 * RW_WIDTH + sl.stop], sl)
               for sl in pair_sls])
    sls = pair_sls * RW_SUB
    sub_of = [s for s in subs for _ in P]

    def log_decay(s, sl):
        x = -(w0_ref[:, sl] + _dot(tanh_wd[s], wup_ref[:, sl]))
        softplus = jnp.maximum(x, 0.0) + jnp.log1p(jnp.exp(-jnp.abs(x)))
        return -jnp.exp(-softplus - 0.5)

    lw = each(log_decay, sub_of, sls)
    a = each(lambda s, sl: _sigmoid(a0_ref[:, sl] + _dot(ad_b[s], aup_ref[:, sl])), sub_of, sls)
    g = each(lambda s, sl: _dot(sig_gd[s], gup_ref[:, sl]), sub_of, sls)

    kkr = each(lambda ki, sl: ki * kk_ref[:, sl], k, sls)
    ksq = head_sums(each(lambda t: t * t, kkr))
    kkn = each(lambda t, ss: t / jnp.maximum(jnp.sqrt(ss), 1e-12), kkr, ksq)
    kmod = each(lambda ki, ai, sl: ki * (1.0 + (ai - 1.0) * ka_ref[:, sl]), k, a, sls)
    beta = each(lambda ai, t: ai * t, a, kkn)

    def running_sum(lwi):
        hi = lwi.astype(BF16)
        both = _dot(tril_incl, jnp.concatenate([hi, (lwi - hi.astype(F32)).astype(BF16)], axis=1))
        return both[:, :LANE] + both[:, LANE:]

    lcum = each(running_sum, lw)
    lend = each(lambda t: t[C - 1:C, :], lcum)
    rh = each(lambda ri, lc: ri * jnp.exp(lc), r, lcum)
    kh = each(lambda t, lc, lwi: t * jnp.exp(lc - lwi), kkn, lcum, lw)
    e_neg = each(lambda lc: jnp.exp(-lc), lcum)
    e_end = each(lambda le, lc: jnp.exp(le - lc), lend, lcum)
    kb = each(lambda t, e: t * e, kmod, e_neg)
    bb = each(lambda t, e: t * e, beta, e_neg)
    kbe = each(lambda t, e: t * e, kmod, e_end)
    bbe = each(lambda t, e: t * e, beta, e_end)

    kr_f = each(lambda x1, x2: jnp.concatenate([x1, x2], axis=0), kh, rh)
    bk = each(lambda x1, x2: jnp.concatenate([x1, x2], axis=0).astype(BF16), bb, kb)
    vb = each(lambda t: t.astype(BF16), v)

    def gram(krf, bki):
        kr2 = jnp.concatenate([jnp.where(lane_tall < HD, krf, 0.0), jnp.where(lane_tall >= HD, krf, 0.0)], axis=0)
        return _dot_nt(kr2.astype(BF16), bki)

    g_all = each(gram, kr_f, bk)

    n_bd = each(lambda ga: jnp.concatenate([jnp.where(strict_lo, -ga[0:C], 0.0),
                                            jnp.where(strict_hi, -pltpu.roll(ga[2 * C:3 * C], C, 1), 0.0)], axis=0),
                g_all)
    t = each(lambda n: eye + n, n_bd)
    pw = each(lambda n: _dot(n.astype(BF16), n.astype(BF16)), n_bd)
    for _ in range(4):
        both = each(lambda ti, pi: _dot(jnp.concatenate([ti, pi], axis=0).astype(BF16), pi.astype(BF16)), t, pw)
        t = each(lambda ti, bi: ti + bi[:LANE], t, both)
        pw = each(lambda bi: bi[LANE:], both)
    t = each(lambda ti, pi: ti + _dot(ti.astype(BF16), pi.astype(BF16)), t, pw)

    def intra_rhs(ga, vbi):
        m1s = jnp.concatenate([jnp.where(strict_hi, ga[0:C], 0.0), jnp.where(strict_hi, ga[2 * C:3 * C], 0.0)], axis=0)
        return _dot(m1s.astype(BF16), jnp.concatenate([vbi, vbi], axis=0))

    q_intra = each(intra_rhs, g_all, vb)

    def m2(gb):
        return jnp.where(incl_lo, -gb, jnp.where(incl_hi, gb, 0.0))

    m2s = each(lambda ga: jnp.concatenate([m2(ga[C:2 * C]), m2(ga[3 * C:4 * C])], axis=0).astype(BF16), g_all)
    kbe_all = each(lambda kbei, bbei: jnp.concatenate([kbei, -bbei], axis=0).astype(BF16), kbe, bbe)
    s_decay = each(jnp.exp, lend)

    state = [s_ref[i] for i in P]
    y = []
    for s in subs:
        of = lambda lst: lst[RW_PAIRS * s:RW_PAIRS * (s + 1)]
        p_all = each(lambda x1, si: _dot_nt(x1.astype(BF16), si.astype(BF16)), of(kr_f), state)
        q_s = each(lambda qi, pa: jnp.where(same_head, qi + jnp.concatenate([pa[:C], pa[:C]], axis=0), 0.0),
                   of(q_intra), p_all)
        u_s = each(lambda ti, qi: _dot(ti.astype(BF16), qi.astype(BF16)), of(t), q_s)
        u = each(lambda us: us[:C] + us[C:], u_s)
        y_s = each(lambda mi, ui, vi: _dot(mi, jnp.concatenate([ui, vi], axis=0).astype(BF16)), of(m2s), u, of(v))
        y += each(lambda pa, ys: pa[C:] + jnp.where(head0_c, ys[:C], ys[C:]), p_all, y_s)
        ds = each(lambda vi, ui, kb_all: _dot_tn(jnp.concatenate([vi, ui], axis=0).astype(BF16), kb_all),
                  of(v), u, of(kbe_all))
        state = each(lambda si, di, dec: si * dec + jnp.where(same_head, di, 0.0), state, ds, of(s_decay))
    for i in P:
        s_ref[i] = state[i]

    d = each(lambda yi, si: yi - si * inv_n, y, head_sums(y))
    var = each(lambda si: si * inv_n, head_sums(each(lambda di: di * di, d)))
    bsum = head_sums(each(lambda ri, ki, sl: ri * ki * rk_ref[:, sl], r, kmod, sls))
    for i, (s, sl) in enumerate(zip(sub_of, sls)):
        yn = d[i] * lax.rsqrt(var[i] + RW_GN_EPS) * lnw_ref[:, sl] + lnb_ref[:, sl]
        o_ref[C * s:C * (s + 1), sl] = ((yn + bsum[i] * v[i]) * g[i]).astype(o_ref.dtype)


def rwkv_mix(p, batch, mu, w0, a0, k_k, k_a, r_k, ln_w, ln_b, w_up, a_up, g_up):
    tp = p.shape[0]
    lp = tp // batch
    C = RW_CHUNK * RW_SUB
    nchunk = lp // C

    def pspec(width, base):
        return pl.BlockSpec((C, width), lambda b, c: (b * nchunk + c, base // width))

    def const(shape):
        return pl.BlockSpec(shape, lambda b, c: (0, 0))

    in_specs = [
        pspec(RW_WIDTH, OFF_RW - P_SPLIT), pspec(RW_WIDTH, OFF_RW - P_SPLIT + RW_WIDTH),
        pspec(RW_WIDTH, OFF_RW - P_SPLIT + 2 * RW_WIDTH),
        pspec(LANE, OFF_RW_WD - P_SPLIT), pspec(LANE, OFF_RW_AD - P_SPLIT), pspec(RW_GATE_LORA, OFF_RW_GD - P_SPLIT),
        const(mu.shape),
    ] + [const((1, RW_WIDTH))] * 7 + [const(w_up.shape), const(a_up.shape), const(g_up.shape)]
    row = lambda t: t.reshape(1, -1)
    return pl.pallas_call(
        _rwkv_body,
        grid=(batch, nchunk),
        in_specs=in_specs,
        out_specs=pl.BlockSpec((C, RW_WIDTH), lambda b, c: (b * nchunk + c, 0)),
        out_shape=jax.ShapeDtypeStruct((tp, RW_WIDTH), BF16),
        scratch_shapes=[pltpu.VMEM((RW_PAIRS, LANE, LANE), F32)] + [pltpu.VMEM((8, RW_WIDTH), F32)] * 3
        + [pltpu.VMEM((8, LANE), F32)] * 2 + [pltpu.VMEM((8, RW_GATE_LORA), F32)],
        compiler_params=_params("parallel", "arbitrary"),
        name="rwkv7_mix",
    )(p, p, p, p, p, p, mu, row(w0), row(a0), row(k_k), row(k_a), row(r_k), row(ln_w), row(ln_b),
      w_up, a_up, g_up)


def _ret_body(q_ref, k_ref, v_ref, g_ref, cos_ref, sin_ref, o_ref, state_ref):
    c = pl.program_id(1)
    C = RET_CHUNK
    d = RET_HEAD_DIM

    @pl.when(c == 0)
    def _():
        state_ref[...] = jnp.zeros_like(state_ref)

    row = lax.broadcasted_iota(jnp.int32, (C, C), 0).astype(F32)
    col = lax.broadcasted_iota(jnp.int32, (C, C), 1).astype(F32)
    diff = row - col
    causal = diff >= 0
    heads = range(RET_HEADS)
    subs = range(RET_SUB)
    rws = [slice(C * s, C * (s + 1)) for s in subs for _ in heads]
    sls = [slice(d * h, d * (h + 1)) for _ in subs for h in heads]
    lgs = [math.log1p(-(2.0 ** (-5.0 - h))) for _ in subs for h in heads]

    def each(fn, *lists):
        return [fn(*args) for args in zip(*lists)]

    def rope(x_ref, rw, sl):
        x = x_ref[rw, sl].astype(F32)
        return x * cos_ref[rw, :] + pltpu.roll(x, d // 2, 1) * sin_ref[rw, :]

    qb = each(lambda rw, sl: rope(q_ref, rw, sl).astype(BF16), rws, sls)
    k = each(lambda rw, sl: rope(k_ref, rw, sl) * (d ** -0.5), rws, sls)
    kb = each(lambda t: t.astype(BF16), k)
    vb = each(lambda rw, sl: v_ref[rw, sl], rws, sls)
    s = each(lambda qi, ki, lg: _dot_nt(qi, ki) * jnp.where(causal, jnp.exp(lg * jnp.maximum(diff, 0.0)), 0.0),
             qb, kb, lgs)
    o_intra = each(lambda si, vi: _dot(si.astype(BF16), vi), s, vb)
    kd = each(lambda ki, lg: (ki * jnp.exp(lg * (C - 1.0 - row))).astype(BF16), k, lgs)
    kv = each(_dot_tn, kd, vb)

    state = [state_ref[h] for h in heads]
    o = []
    for sub in subs:
        of = lambda lst: lst[RET_HEADS * sub:RET_HEADS * (sub + 1)]
        o += each(lambda oi, qi, st, lg: oi + _dot(qi, st.astype(BF16)) * jnp.exp(lg * (row + 1.0)),
                  of(o_intra), of(qb), state, of(lgs))
        state = each(lambda st, kvi, lg: st * math.exp(lg * C) + kvi, state, of(kv), of(lgs))
    for h in heads:
        state_ref[h] = state[h]
    o = each(lambda oi: oi * lax.rsqrt(jnp.mean(oi * oi, axis=-1, keepdims=True) + NORM_EPS), o)
    for oi, rw, sl in zip(o, rws, sls):
        g = g_ref[rw, sl].astype(F32)
        o_ref[rw, sl] = (g * _sigmoid(g) * oi).astype(o_ref.dtype)


def retention_mix(p, batch, cos, sin):
    tp = p.shape[0]
    lp = tp // batch
    C = RET_CHUNK * RET_SUB
    nchunk = lp // C
    base = OFF_RET // RET_WIDTH

    def pspec(j):
        return pl.BlockSpec((C, RET_WIDTH), lambda b, c: (b * nchunk + c, base + j))

    tab = pl.BlockSpec((C, RET_HEAD_DIM), lambda b, c: (c, 0))
    return pl.pallas_call(
        _ret_body,
        grid=(batch, nchunk),
        in_specs=[pspec(0), pspec(1), pspec(2), pspec(3), tab, tab],
        out_specs=pl.BlockSpec((C, RET_WIDTH), lambda b, c: (b * nchunk + c, 0)),
        out_shape=jax.ShapeDtypeStruct((tp, RET_WIDTH), BF16),
        scratch_shapes=[pltpu.VMEM((RET_HEADS, RET_HEAD_DIM, RET_HEAD_DIM), F32)],
        compiler_params=_params("parallel", "arbitrary"),
        name="retention_mix",
    )(p, p, p, p, cos, sin)


def _mla_proj_body(qd_ref, kvd_ref, krd_ref, nq_ref, nkv_ref, wuq_ref, wukv_ref, cos_ref, sin_ref,
                   q_out, k_out, v_out):
    cos = cos_ref[...]
    sin = sin_ref[...]

    def rope(x):
        return x * cos + (pltpu.roll(x, MLA_ROPE // 2, 1) + pltpu.roll(x, LANE - MLA_ROPE // 2, 1)) * sin

    def norm(x, g):
        return x * lax.rsqrt(jnp.mean(x * x, axis=-1, keepdims=True) + NORM_EPS) * g

    scale = (MLA_NOPE + MLA_ROPE) ** -0.5 * math.log2(math.e)
    q = _dot(norm(qd_ref[...], nq_ref[...]).astype(BF16), wuq_ref[...]) * scale
    kv = _dot(norm(kvd_ref[...], nkv_ref[...]).astype(BF16), wukv_ref[...])
    kr = rope(krd_ref[...]).astype(k_out.dtype)
    for h in range(MLA_HEADS):
        lo = MLA_QK_PAD * h
        q_out[:, lo:lo + LANE] = q[:, lo:lo + LANE].astype(q_out.dtype)
        q_out[:, lo + LANE:lo + 2 * LANE] = rope(q[:, lo + LANE:lo + 2 * LANE]).astype(q_out.dtype)
        k_out[:, lo:lo + LANE] = kv[:, MLA_NOPE * h:MLA_NOPE * (h + 1)].astype(k_out.dtype)
        k_out[:, lo + LANE:lo + 2 * LANE] = kr
    v_out[...] = kv[:, MLA_HEADS * MLA_NOPE:].astype(v_out.dtype)


def mla_proj(p, batch, norm_q, norm_kv, w_uq, w_ukv, cos, sin, tm):
    tp = p.shape[0]
    lp = tp // batch
    per_seq = lp // tm
    qk_w = MLA_HEADS * MLA_QK_PAD
    const = lambda i: (0, 0)
    return pl.pallas_call(
        _mla_proj_body,
        grid=(tp // tm,),
        in_specs=[
            pl.BlockSpec((tm, MLA_Q_RANK), lambda i: (i, (OFF_MLA_Q - P_SPLIT) // MLA_Q_RANK)),
            pl.BlockSpec((tm, MLA_KV_RANK), lambda i: (i, (OFF_MLA_KV - P_SPLIT) // MLA_KV_RANK)),
            pl.BlockSpec((tm, LANE), lambda i: (i, (OFF_MLA_KR - P_SPLIT) // LANE)),
            pl.BlockSpec((1, MLA_Q_RANK), const),
            pl.BlockSpec((1, MLA_KV_RANK), const),
            pl.BlockSpec((MLA_Q_RANK, qk_w), const),
            pl.BlockSpec((MLA_KV_RANK, MLA_HEADS * (MLA_NOPE + MLA_V)), const),
            pl.BlockSpec((tm, LANE), lambda i: (i % per_seq, 0)),
            pl.BlockSpec((tm, LANE), lambda i: (i % per_seq, 0)),
        ],
        out_specs=[
            pl.BlockSpec((tm, qk_w), lambda i: (i, 0)),
            pl.BlockSpec((tm, qk_w), lambda i: (i, 0)),
            pl.BlockSpec((tm, MLA_WIDTH), lambda i: (i, 0)),
        ],
        out_shape=[
            jax.ShapeDtypeStruct((tp, qk_w), BF16),
            jax.ShapeDtypeStruct((tp, qk_w), BF16),
            jax.ShapeDtypeStruct((tp, MLA_WIDTH), BF16),
        ],
        compiler_params=_params("parallel"),
        name="mla_proj",
    )(p, p, p, norm_q.reshape(1, -1), norm_kv.reshape(1, -1), w_uq, w_ukv, cos, sin)


def _attn_body(q_ref, k_ref, v_ref, o_ref, *, tq):
    i = pl.program_id(2)
    heads = range(ATTN_HEADS_PER_STEP)
    qs = [q_ref[:, MLA_QK_PAD * h:MLA_QK_PAD * (h + 1)] for h in heads]

    def step(off, width, masked, carry):
        ms, ls, accs = carry
        off = pl.multiple_of(off, tq)
        ss = [_dot_nt(qs[h], k_ref[pl.ds(off, width), MLA_QK_PAD * h:MLA_QK_PAD * (h + 1)]) for h in heads]
        if masked:
            row = i * tq + lax.broadcasted_iota(jnp.int32, (tq, width), 0)
            col = off + lax.broadcasted_iota(jnp.int32, (tq, width), 1)
            ss = [jnp.where(col <= row, s, -jnp.inf) for s in ss]
        m_new = [jnp.maximum(ms[h], jnp.max(ss[h], axis=-1, keepdims=True)) for h in heads]
        alpha = [jnp.exp2(ms[h] - m_new[h]) for h in heads]
        ps = [jnp.exp2(ss[h] - m_new[h]) for h in heads]
        ls = [alpha[h] * ls[h] + jnp.sum(ps[h], axis=-1, keepdims=True) for h in heads]
        pv = [_dot(ps[h].astype(BF16), v_ref[pl.ds(off, width), MLA_V * h:MLA_V * (h + 1)]) for h in heads]
        accs = [alpha[h] * accs[h] + pv[h] for h in heads]
        return tuple(m_new), tuple(ls), tuple(accs)

    carry = (tuple(jnp.full((tq, 1), -1e30, F32) for _ in heads),
             tuple(jnp.zeros((tq, 1), F32) for _ in heads),
             tuple(jnp.zeros((tq, MLA_V), F32) for _ in heads))
    n_single = (i + 1) % 2
    n_pairs = (i + 1) // 2
    carry = lax.cond(i == 0, lambda c: step(0, tq, True, c), lambda c: c, carry)
    carry = lax.cond((n_single == 1) & (i > 0), lambda c: step(0, tq, False, c), lambda c: c, carry)
    pair_off = lambda p: (n_single + 2 * p) * tq
    carry = lax.fori_loop(0, n_pairs - 1, lambda p, c: step(pair_off(p), 2 * tq, False, c), carry)
    carry = lax.cond(n_pairs > 0, lambda c: step(pair_off(n_pairs - 1), 2 * tq, True, c), lambda c: c, carry)
    _, ls, accs = carry
    for h in heads:
        o_ref[:, MLA_V * h:MLA_V * (h + 1)] = (accs[h] / ls[h]).astype(o_ref.dtype)


def mla_attention(q, k, v, batch, tq):
    tp = q.shape[0]
    lp = tp // batch
    nq = lp // tq
    hs = ATTN_HEADS_PER_STEP
    return pl.pallas_call(
        functools.partial(_attn_body, tq=tq),
        grid=(batch, MLA_HEADS // hs, nq),
        in_specs=[
            pl.BlockSpec((tq, hs * MLA_QK_PAD), lambda b, h, i: (b * nq + i, h)),
            pl.BlockSpec((lp, hs * MLA_QK_PAD), lambda b, h, i: (b, h)),
            pl.BlockSpec((lp, hs * MLA_V), lambda b, h, i: (b, h)),
        ],
        out_specs=pl.BlockSpec((tq, hs * MLA_V), lambda b, h, i: (b * nq + i, h)),
        out_shape=jax.ShapeDtypeStruct((tp, MLA_WIDTH), BF16),
        compiler_params=_params("parallel", "parallel", "arbitrary"),
        name="mla_attention",
    )(q, k, v)


def _merge_body(ya_ref, yb_ref, yc_ref, wa_ref, wb_ref, wc_ref, ga_ref, gb_ref, gc_ref, o_ref):
    def branch(y_ref, w_ref, g_ref):
        return _sigmoid(g_ref[...].astype(F32)) * _dot(y_ref[...], w_ref[...])

    o_ref[...] = (branch(ya_ref, wa_ref, ga_ref) + branch(yb_ref, wb_ref, gb_ref)
                  + branch(yc_ref, wc_ref, gc_ref)).astype(o_ref.dtype)


def merge_branches(ya, yb, yc, wa, wb, wc, layer, p, tm, tn):
    tp = ya.shape[0]

    def yspec(width):
        return pl.BlockSpec((tm, width), lambda i, j: (i, 0))

    def wspec(width):
        return pl.BlockSpec((None, width, tn), lambda i, j: (layer, 0, j))

    def gspec(branch):
        base = (OFF_GATE + branch * D_MODEL) // tn
        return pl.BlockSpec((tm, tn), lambda i, j: (i, base + j))

    return pl.pallas_call(
        _merge_body,
        grid=(tp // tm, D_MODEL // tn),
        in_specs=[yspec(RW_WIDTH), yspec(RET_WIDTH), yspec(MLA_WIDTH),
                  wspec(RW_WIDTH), wspec(RET_WIDTH), wspec(MLA_WIDTH),
                  gspec(0), gspec(1), gspec(2)],
        out_specs=pl.BlockSpec((tm, tn), lambda i, j: (i, j)),
        out_shape=jax.ShapeDtypeStruct((tp, D_MODEL), BF16),
        compiler_params=_params("parallel", "parallel"),
        name="merge_branches",
    )(ya, yb, yc, wa, wb, wc, p, p, p)


def _resid_body(x_ref, w_ref, h_ref, o_ref, ob_ref, ssq_ref):
    hn = h_ref[...] + _dot(x_ref[...], w_ref[...])
    o_ref[...] = hn
    ob_ref[...] = hn.astype(ob_ref.dtype)
    ssq_ref[...] = _ssq_block(hn)


def resid_matmul(x, w, layer, h, tm, tn):
    m, kdim = x.shape
    n = w.shape[2]
    tile = pl.BlockSpec((tm, tn), lambda i, j: (i, j))
    return pl.pallas_call(
        _resid_body,
        grid=(m // tm, n // tn),
        in_specs=[
            pl.BlockSpec((tm, kdim), lambda i, j: (i, 0)),
            pl.BlockSpec((None, kdim, tn), lambda i, j: (layer, 0, j)),
            tile,
        ],
        out_specs=[tile, tile, pl.BlockSpec((tm, LANE), lambda i, j: (i, j))],
        out_shape=[jax.ShapeDtypeStruct((m, n), F32), jax.ShapeDtypeStruct((m, n), BF16),
                   jax.ShapeDtypeStruct((m, LANE * (n // tn)), F32)],
        compiler_params=_params("parallel", "parallel"),
        name="resid_matmul",
    )(x, w, h)


def _ffn_up_body(x_ref, ssq_ref, wg_ref, wu_ref, o_ref):
    x = x_ref[...]
    r = _row_scale(ssq_ref, x.shape[1])
    hg = r * _dot(x, wg_ref[...])
    hu = r * _dot(x, wu_ref[...])
    o_ref[...] = (hg * _sigmoid(hg) * hu).astype(o_ref.dtype)


def ffn_up(xb, ssq, w_gate_up, layer, tm, tn):
    m, kdim = xb.shape
    hidden = w_gate_up.shape[2] // 2
    nj = hidden // tn
    return pl.pallas_call(
        _ffn_up_body,
        grid=(m // tm, nj),
        in_specs=[
            pl.BlockSpec((tm, kdim), lambda i, j: (i, 0)),
            pl.BlockSpec((tm, ssq.shape[1]), lambda i, j: (i, 0)),
            pl.BlockSpec((None, kdim, tn), lambda i, j: (layer, 0, j)),
            pl.BlockSpec((None, kdim, tn), lambda i, j: (layer, 0, nj + j)),
        ],
        out_specs=pl.BlockSpec((tm, tn), lambda i, j: (i, j)),
        out_shape=jax.ShapeDtypeStruct((m, hidden), BF16),
        compiler_params=_params("parallel", "parallel"),
        name="ffn_up",
    )(xb, ssq, w_gate_up, w_gate_up)


def _pad_cols(w, width):
    return jnp.pad(w, [(0, 0)] * (w.ndim - 1) + [(0, width - w.shape[-1])])


def _w_in_pieces():
    widths = [3 * RW_WIDTH, RW_DECAY_LORA, RW_A_LORA, RW_GATE_LORA, 4 * RET_WIDTH, MLA_Q_RANK, MLA_KV_RANK, MLA_ROPE,
              3 * D_MODEL]
    dsts = [OFF_RW, OFF_RW_WD, OFF_RW_AD, OFF_RW_GD, OFF_RET, OFF_MLA_Q, OFF_MLA_KV, OFF_MLA_KR, OFF_GATE]
    pieces, src = [], 0
    for dst, width in zip(dsts, widths):
        pieces.append((dst, src, width))
        src += width
    return pieces


def _pack_w_in_body(w_ref, g_ref, o_ref):
    g = g_ref[...]
    covered = 0
    for dst, src, width in sorted(_w_in_pieces()):
        if dst > covered:
            o_ref[:, covered:dst] = jnp.zeros((o_ref.shape[0], dst - covered), o_ref.dtype)
        o_ref[:, dst:dst + width] = (w_ref[:, src:src + width] * g).astype(o_ref.dtype)
        covered = dst + width
    o_ref[:, covered:] = jnp.zeros((o_ref.shape[0], o_ref.shape[1] - covered), o_ref.dtype)


def _pack_w_in(w_in, gain, tr=128):
    nl, rows, cols = w_in.shape
    return pl.pallas_call(
        _pack_w_in_body,
        grid=(nl, rows // tr),
        in_specs=[pl.BlockSpec((None, tr, cols), lambda l, i: (l, i, 0)),
                  pl.BlockSpec((None, tr, 1), lambda l, i: (l, i, 0))],
        out_specs=pl.BlockSpec((None, tr, P_COLS), lambda l, i: (l, i, 0)),
        out_shape=jax.ShapeDtypeStruct((nl, rows, P_COLS), BF16),
        compiler_params=_params("parallel", "parallel"),
        name="pack_w_in",
    )(w_in, gain[..., None])


def _pack_mu(mu):
    rkv = mu[..., :3 * RW_WIDTH]
    wd = mu[..., 3 * RW_WIDTH:3 * RW_WIDTH + RW_DECAY_LORA]
    ad = mu[..., 3 * RW_WIDTH + RW_DECAY_LORA:3 * RW_WIDTH + RW_DECAY_LORA + RW_A_LORA]
    gd = mu[..., 3 * RW_WIDTH + RW_DECAY_LORA + RW_A_LORA:]
    return jnp.concatenate([rkv, _pad_cols(wd, LANE), _pad_cols(ad, LANE), gd], axis=-1)


def _pad_rows(w, rows):
    return jnp.pad(w, [(0, 0)] * (w.ndim - 2) + [(0, rows - w.shape[-2]), (0, 0)])


def _pack_w_uq(w):
    nl, rank, _ = w.shape
    w = w.reshape(nl, rank, MLA_HEADS, MLA_NOPE + MLA_ROPE)
    w = jnp.pad(w, ((0, 0), (0, 0), (0, 0), (0, MLA_QK_PAD - MLA_NOPE - MLA_ROPE)))
    return w.reshape(nl, rank, MLA_HEADS * MLA_QK_PAD).astype(BF16)


def _pack_w_ukv(w):
    nl, rank, _ = w.shape
    w = w.reshape(nl, rank, MLA_HEADS, 2, MLA_NOPE)
    w = jnp.swapaxes(w, 2, 3)
    return w.reshape(nl, rank, 2 * MLA_HEADS * MLA_NOPE).astype(BF16)


def _rope_tables(lp):
    pos = jnp.arange(lp, dtype=F32)

    def tables(dim):
        inv = ROPE_BASE ** (-jnp.arange(0, dim, 2, dtype=F32) / dim)
        ang = pos[:, None] * inv[None, :]
        return jnp.cos(ang), jnp.sin(ang)

    c, s = tables(RET_HEAD_DIM)
    ret = (jnp.concatenate([c, c], axis=1), jnp.concatenate([-s, s], axis=1))
    c, s = tables(MLA_ROPE)
    z = jnp.zeros((lp, LANE - MLA_ROPE), F32)
    mla = (jnp.concatenate([c, c, z], axis=1), jnp.concatenate([-s, s, z], axis=1))
    return ret, mla


def kernel(x, meta_tokens, norm_mix, w_in, rw_mu, rw_w0, rw_w_up, rw_a0, rw_a_up, rw_g_up, rw_k_k, rw_k_a, rw_r_k, rw_ln_w, rw_ln_b, mla_norm_q, mla_norm_kv, mla_w_uq, mla_w_ukv, w_br_rwkv, w_br_ret, w_br_mla, w_out, norm_ffn, w_gate_up, w_down, final_norm):
    batch, seq, d = x.shape
    depth = w_in.shape[0]
    lp = -(-(N_META + seq) // SEQ_ALIGN) * SEQ_ALIGN
    tp = batch * lp

    meta = jnp.broadcast_to(meta_tokens[None].astype(x.dtype), (batch, N_META, d))
    pad = jnp.zeros((batch, lp - N_META - seq, d), x.dtype)
    h = jnp.concatenate([meta, x, pad], axis=1).reshape(tp, d)

    wp = _pack_w_in(w_in, norm_mix)
    mu = _pack_mu(rw_mu)
    w_up = _pad_rows(rw_w_up, LANE).astype(BF16)
    a_up = _pad_rows(rw_a_up, LANE).astype(BF16)
    g_up = rw_g_up.astype(BF16)
    wuq = _pack_w_uq(mla_w_uq)
    wukv = _pack_w_ukv(mla_w_ukv)
    wa = w_br_rwkv.astype(BF16)
    wb = w_br_ret.astype(BF16)
    wc = w_br_mla.astype(BF16)
    wo = w_out.astype(BF16)
    wgu = (w_gate_up * norm_ffn[..., None]).astype(BF16)
    wdn = w_down.astype(BF16)
    (cos_ret, sin_ret), (cos_mla, sin_mla) = _rope_tables(lp)

    def row_tile(pref):
        return next((t for t in pref if tp % t == 0), SEQ_ALIGN)

    tm = row_tile((768,))
    tm_wide = row_tile((1536, 768))
    tm_seq = SEQ_ALIGN

    hb, ssq = stream_prep(h, tm)
    for l in range(depth):
        p_lo = norm_matmul(hb, ssq, wp, l, 0, P_SPLIT, tm_wide, 2 * P_TILE_N, BF16)
        p = norm_matmul(hb, ssq, wp, l, P_SPLIT, P_COLS - P_SPLIT, tm_wide, P_TILE_N, F32)
        ya = rwkv_mix(p, batch, mu[l:l + 1], rw_w0[l], rw_a0[l], rw_k_k[l], rw_k_a[l], rw_r_k[l],
                      rw_ln_w[l], rw_ln_b[l], w_up[l], a_up[l], g_up[l])
        yb = retention_mix(p_lo, batch, cos_ret, sin_ret)
        q, k, v = mla_proj(p, batch, mla_norm_q[l], mla_norm_kv[l], wuq[l], wukv[l], cos_mla, sin_mla, tm_seq)
        yc = mla_attention(q, k, v, batch, tm_seq)
        merged = merge_branches(ya, yb, yc, wa, wb, wc, l, p_lo, tm, 1024)
        h, hb, ssq = resid_matmul(merged, wo, l, h, tm_seq, d)
        act = ffn_up(hb, ssq, wgu, l, tm_wide, 512)
        h, hb, ssq = resid_matmul(act, wdn, l, h, tm, 512)
    tm_out = next(t for t in (512, 256, SEQ_ALIGN) if seq % t == 0)
    return final_rmsnorm(h, final_norm, batch, N_META, seq, tm_out).reshape(batch, seq, d)
```

```python
import functools
import math

import jax
import jax.numpy as jnp
from jax import lax
from jax.experimental import pallas as pl
from jax.experimental.pallas import tpu as pltpu

F32 = jnp.float32
BF16 = jnp.bfloat16

D_MODEL = 2048
N_META = 16
NORM_EPS = 1e-6
ROPE_BASE = 10000.0

RW_HEADS = 16
RW_HEAD_DIM = 64
RW_WIDTH = RW_HEADS * RW_HEAD_DIM
RW_DECAY_LORA = 96
RW_A_LORA = 96
RW_GATE_LORA = 256
RW_GN_EPS = RW_HEAD_DIM * 1e-5
RW_CHUNK = 64
RW_SUB = 3
RW_PAIRS = RW_WIDTH // 128

RET_HEADS = 8
RET_HEAD_DIM = 128
RET_WIDTH = RET_HEADS * RET_HEAD_DIM
RET_CHUNK = 128
RET_SUB = 3

MLA_HEADS = 8
MLA_NOPE = 128
MLA_ROPE = 64
MLA_V = 128
MLA_Q_RANK = 512
MLA_KV_RANK = 256
MLA_WIDTH = MLA_HEADS * MLA_V
MLA_QK_PAD = 256
ATTN_HEADS_PER_STEP = 2

FFN_HIDDEN = -(-8 * D_MODEL // (3 * 256)) * 256

LANE = 128
SEQ_ALIGN = 384
assert SEQ_ALIGN % (RW_CHUNK * RW_SUB) == 0 and SEQ_ALIGN % (RET_CHUNK * RET_SUB) == 0 and SEQ_ALIGN % LANE == 0

OFF_RET = 0
OFF_GATE = OFF_RET + 4 * RET_WIDTH
OFF_RW = OFF_GATE + 3 * D_MODEL
OFF_RW_WD = OFF_RW + 3 * RW_WIDTH
OFF_RW_AD = OFF_RW_WD + LANE
OFF_RW_GD = OFF_RW_AD + LANE
OFF_MLA_Q = OFF_RW_GD + RW_GATE_LORA
OFF_MLA_KV = OFF_MLA_Q + MLA_Q_RANK
OFF_MLA_KR = OFF_MLA_KV + MLA_KV_RANK
P_COLS_USED = OFF_MLA_KR + LANE
P_TILE_N = 512
P_COLS = -(-P_COLS_USED // P_TILE_N) * P_TILE_N
P_SPLIT = OFF_RW
assert P_SPLIT % (2 * P_TILE_N) == 0

VMEM_LIMIT = 48 * 1024 * 1024


def _params(*sem):
    return pltpu.CompilerParams(dimension_semantics=sem, vmem_limit_bytes=VMEM_LIMIT)


def _sigmoid(x):
    return 1.0 / (1.0 + jnp.exp(-x))


def _dot(a, b):
    return jnp.dot(a, b, preferred_element_type=F32)


def _dot_nt(a, b):
    return lax.dot_general(a, b, (((1,), (1,)), ((), ())), preferred_element_type=F32)


def _dot_tn(a, b):
    return lax.dot_general(a, b, (((0,), (0,)), ((), ())), preferred_element_type=F32)


def _rmsnorm_body(x_ref, g_ref, o_ref):
    x = x_ref[...]
    y = x * lax.rsqrt(jnp.mean(x * x, axis=-1, keepdims=True) + NORM_EPS)
    o_ref[...] = (y * g_ref[...]).astype(o_ref.dtype)


def final_rmsnorm(x, g, batch, first, count, tm):
    m, d = x.shape
    rows_per_batch = m // batch
    assert rows_per_batch % 8 == 0 and first % 8 == 0 and tm % 8 == 0
    tiles = count // tm
    return pl.pallas_call(
        _rmsnorm_body,
        grid=(batch, tiles),
        in_specs=[pl.BlockSpec((pl.Element(tm), pl.Element(d)),
                               lambda b, t: (pl.multiple_of(b * rows_per_batch + first + t * tm, 8), 0)),
                  pl.BlockSpec((1, d), lambda b, t: (0, 0))],
        out_specs=pl.BlockSpec((tm, d), lambda b, t: (b * tiles + t, 0)),
        out_shape=jax.ShapeDtypeStruct((batch * count, d), x.dtype),
        compiler_params=_params("parallel", "parallel"),
        name="final_rmsnorm",
    )(x, g.reshape(1, d))


def _row_scale(ssq_ref, d):
    ssq = ssq_ref[...]
    total = ssq[:, 0:1]
    for j in range(1, ssq.shape[1] // LANE):
        total = total + ssq[:, LANE * j:LANE * j + 1]
    return lax.rsqrt(total * (1.0 / d) + NORM_EPS)


def _ssq_block(x):
    return jnp.broadcast_to(jnp.sum(x * x, axis=-1, keepdims=True), (x.shape[0], LANE))


def _stream_prep_body(x_ref, xb_ref, ssq_ref):
    x = x_ref[...]
    xb_ref[...] = x.astype(xb_ref.dtype)
    ssq_ref[...] = _ssq_block(x)


def stream_prep(x, tm):
    m, d = x.shape
    return pl.pallas_call(
        _stream_prep_body,
        grid=(m // tm,),
        in_specs=[pl.BlockSpec((tm, d), lambda i: (i, 0))],
        out_specs=[pl.BlockSpec((tm, d), lambda i: (i, 0)), pl.BlockSpec((tm, LANE), lambda i: (i, 0))],
        out_shape=[jax.ShapeDtypeStruct((m, d), BF16), jax.ShapeDtypeStruct((m, LANE), F32)],
        compiler_params=_params("parallel"),
        name="stream_prep",
    )(x)


def _norm_matmul_body(x_ref, ssq_ref, w_ref, o_ref):
    o_ref[...] = (_row_scale(ssq_ref, x_ref.shape[1]) * _dot(x_ref[...], w_ref[...])).astype(o_ref.dtype)


def norm_matmul(xb, ssq, w, layer, col0, n, tm, tn, out_dtype):
    m, k = xb.shape
    assert col0 % tn == 0 and n % tn == 0 and m % tm == 0
    return pl.pallas_call(
        _norm_matmul_body,
        grid=(m // tm, n // tn),
        in_specs=[pl.BlockSpec((tm, k), lambda i, j: (i, 0)), pl.BlockSpec((tm, ssq.shape[1]), lambda i, j: (i, 0)),
                  pl.BlockSpec((None, k, tn), lambda i, j: (layer, 0, col0 // tn + j))],
        out_specs=pl.BlockSpec((tm, tn), lambda i, j: (i, j)),
        out_shape=jax.ShapeDtypeStruct((m, n), out_dtype),
        compiler_params=_params("parallel", "parallel"),
        name="in_proj",
    )(xb, ssq, w)


def _rwkv_body(r_ref, k_ref, v_ref, wd_ref, ad_ref, gd_ref, mu_ref,
               w0_ref, a0_ref, kk_ref, ka_ref, rk_ref, lnw_ref, lnb_ref,
               wup_ref, aup_ref, gup_ref,
               o_ref,
               s_ref, pr_ref, pk_ref, pv_ref, pwd_ref, pad_ref, pgd_ref):
    c = pl.program_id(1)
    C = RW_CHUNK
    RS = RW_SUB * C
    HD = RW_HEAD_DIM

    @pl.when(c == 0)
    def _():
        s_ref[...] = jnp.zeros_like(s_ref)
        pr_ref[...] = jnp.zeros_like(pr_ref)
        pk_ref[...] = jnp.zeros_like(pk_ref)
        pv_ref[...] = jnp.zeros_like(pv_ref)
        pwd_ref[...] = jnp.zeros_like(pwd_ref)
        pad_ref[...] = jnp.zeros_like(pad_ref)
        pgd_ref[...] = jnp.zeros_like(pgd_ref)

    def shift(x_ref, prev_ref, mu, sl):
        z = x_ref[:, sl]
        first = lax.broadcasted_iota(jnp.int32, z.shape, 0) == 0
        zs = jnp.where(first, prev_ref[0:1, sl], pltpu.roll(z, 1, 0))
        prev_ref[0:1, sl] = z[RS - 1:RS, :]
        return z + (zs - z) * mu

    mu_lora = 3 * RW_WIDTH
    full = slice(None)
    wd = shift(wd_ref, pwd_ref, mu_ref[:, mu_lora:mu_lora + LANE], full)
    ad = shift(ad_ref, pad_ref, mu_ref[:, mu_lora + LANE:mu_lora + 2 * LANE], full)
    gd = shift(gd_ref, pgd_ref, mu_ref[:, mu_lora + 2 * LANE:], full)
    subs = range(RW_SUB)

    def rows(t, s):
        return t[C * s:C * (s + 1)]

    tanh_wd = [rows(jnp.tanh(wd), s).astype(BF16) for s in subs]
    ad_b = [rows(ad, s).astype(BF16) for s in subs]
    sig_gd = [rows(_sigmoid(gd), s).astype(BF16) for s in subs]

    lane_sq = lax.broadcasted_iota(jnp.int32, (LANE, LANE), 1)
    row_sq = lax.broadcasted_iota(jnp.int32, (LANE, LANE), 0)
    same_head = (lane_sq < HD) == (row_sq < HD)
    head_ones = same_head.astype(BF16)
    eye = (lane_sq == row_sq).astype(F32)
    rc = lax.broadcasted_iota(jnp.int32, (C, C), 0)
    cc = lax.broadcasted_iota(jnp.int32, (C, C), 1)
    tril_incl = (cc <= rc).astype(BF16)
    lane_tall = lax.broadcasted_iota(jnp.int32, (2 * C, LANE), 1)
    lane_c = lax.broadcasted_iota(jnp.int32, (C, LANE), 1)
    row_c = lax.broadcasted_iota(jnp.int32, (C, LANE), 0)
    head0_c = lane_c < HD
    strict_lo = lane_c < row_c
    strict_hi = (lane_c >= C) & (lane_c - C < row_c)
    incl_lo = lane_c <= row_c
    incl_hi = (lane_c >= C) & (lane_c - C <= row_c)
    inv_n = 1.0 / HD

    P = range(RW_PAIRS)

    def head_sums(ts):
        out = _dot(jnp.concatenate([t.astype(BF16) for t in ts], axis=0), head_ones)
        return [out[C * i:C * (i + 1)] for i in range(len(ts))]

    pair_sls = [slice(LANE * i, LANE * (i + 1)) for i in P]

    def each(fn, *lists):
        return [fn(*args) for args in zip(*lists)]

    def items(per_pair):
        return [rows(t, s) for s in subs for t in per_pair]

    r = items([shift(r_ref, pr_ref, mu_ref[:, sl], sl) for sl in pair_sls])
    k = items([shift(k_ref, pk_ref, mu_ref[:, RW_WIDTH + sl.start:RW_WIDTH + sl.stop], sl) for sl in pair_sls])
    v = items([shift(v_ref, pv_ref, mu_ref[:, 2 * RW_WIDTH + sl.start:2 * RW_WIDTH + sl.stop], sl)
               for sl in pair_sls])
    sls = pair_sls * RW_SUB
    sub_of = [s for s in subs for _ in P]

    def log_decay(s, sl):
        x = -(w0_ref[:, sl] + _dot(tanh_wd[s], wup_ref[:, sl]))
        softplus = jnp.maximum(x, 0.0) + jnp.log1p(jnp.exp(-jnp.abs(x)))
        return -jnp.exp(-softplus - 0.5)

    lw = each(log_decay, sub_of, sls)
    a = each(lambda s, sl: _sigmoid(a0_ref[:, sl] + _dot(ad_b[s], aup_ref[:, sl])), sub_of, sls)
    g = each(lambda s, sl: _dot(sig_gd[s], gup_ref[:, sl]), sub_of, sls)

    kkr = each(lambda ki, sl: ki * kk_ref[:, sl], k, sls)
    ksq = head_sums(each(lambda t: t * t, kkr))
    kkn = each(lambda t, ss: t / jnp.maximum(jnp.sqrt(ss), 1e-12), kkr, ksq)
    kmod = each(lambda ki, ai, sl: ki * (1.0 + (ai - 1.0) * ka_ref[:, sl]), k, a, sls)
    beta = each(lambda ai, t: ai * t, a, kkn)

    def running_sum(lwi):
        hi = lwi.astype(BF16)
        both = _dot(tril_incl, jnp.concatenate([hi, (lwi - hi.astype(F32)).astype(BF16)], axis=1))
        return both[:, :LANE] + both[:, LANE:]

    lcum = each(running_sum, lw)
    lend = each(lambda t: t[C - 1:C, :], lcum)
    rh = each(lambda ri, lc: ri * jnp.exp(lc), r, lcum)
    kh = each(lambda t, lc, lwi: t * jnp.exp(lc - lwi), kkn, lcum, lw)
    e_neg = each(lambda lc: jnp.exp(-lc), lcum)
    e_end = each(lambda le, lc: jnp.exp(le - lc), lend, lcum)
    kb = each(lambda t, e: t * e, kmod, e_neg)
    bb = each(lambda t, e: t * e, beta, e_neg)
    kbe = each(lambda t, e: t * e, kmod, e_end)
    bbe = each(lambda t, e: t * e, beta, e_end)

    kr_f = each(lambda x1, x2: jnp.concatenate([x1, x2], axis=0), kh, rh)
    bk = each(lambda x1, x2: jnp.concatenate([x1, x2], axis=0).astype(BF16), bb, kb)
    vb = each(lambda t: t.astype(BF16), v)

    def gram(krf, bki):
        kr2 = jnp.concatenate([jnp.where(lane_tall < HD, krf, 0.0), jnp.where(lane_tall >= HD, krf, 0.0)], axis=0)
        return _dot_nt(kr2.astype(BF16), bki)

    g_all = each(gram, kr_f, bk)

    n_bd = each(lambda ga: jnp.concatenate([jnp.where(strict_lo, -ga[0:C], 0.0),
                                            jnp.where(strict_hi, -pltpu.roll(ga[2 * C:3 * C], C, 1), 0.0)], axis=0),
                g_all)
    t = each(lambda n: eye + n, n_bd)
    pw = each(lambda n: _dot(n.astype(BF16), n.astype(BF16)), n_bd)
    for _ in range(4):
        both = each(lambda ti, pi: _dot(jnp.concatenate([ti, pi], axis=0).astype(BF16), pi.astype(BF16)), t, pw)
        t = each(lambda ti, bi: ti + bi[:LANE], t, both)
        pw = each(lambda bi: bi[LANE:], both)
    t = each(lambda ti, pi: ti + _dot(ti.astype(BF16), pi.astype(BF16)), t, pw)

    def intra_rhs(ga, vbi):
        m1s = jnp.concatenate([jnp.where(strict_hi, ga[0:C], 0.0), jnp.where(strict_hi, ga[2 * C:3 * C], 0.0)], axis=0)
        return _dot(m1s.astype(BF16), jnp.concatenate([vbi, vbi], axis=0))

    q_intra = each(intra_rhs, g_all, vb)

    def m2(gb):
        return jnp.where(incl_lo, -gb, jnp.where(incl_hi, gb, 0.0))

    m2s = each(lambda ga: jnp.concatenate([m2(ga[C:2 * C]), m2(ga[3 * C:4 * C])], axis=0).astype(BF16), g_all)
    kbe_all = each(lambda kbei, bbei: jnp.concatenate([kbei, -bbei], axis=0).astype(BF16), kbe, bbe)
    s_decay = each(jnp.exp, lend)

    state = [s_ref[i] for i in P]
    y = []
    for s in subs:
        of = lambda lst: lst[RW_PAIRS * s:RW_PAIRS * (s + 1)]
        p_all = each(lambda x1, si: _dot_nt(x1.astype(BF16), si.astype(BF16)), of(kr_f), state)
        q_s = each(lambda qi, pa: jnp.where(same_head, qi + jnp.concatenate([pa[:C], pa[:C]], axis=0), 0.0),
                   of(q_intra), p_all)
        u_s = each(lambda ti, qi: _dot(ti.astype(BF16), qi.astype(BF16)), of(t), q_s)
        u = each(lambda us: us[:C] + us[C:], u_s)
        y_s = each(lambda mi, ui, vi: _dot(mi, jnp.concatenate([ui, vi], axis=0).astype(BF16)), of(m2s), u, of(v))
        y += each(lambda pa, ys: pa[C:] + jnp.where(head0_c, ys[:C], ys[C:]), p_all, y_s)
        ds = each(lambda vi, ui, kb_all: _dot_tn(jnp.concatenate([vi, ui], axis=0).astype(BF16), kb_all),
                  of(v), u, of(kbe_all))
        state = each(lambda si, di, dec: si * dec + jnp.where(same_head, di, 0.0), state, ds, of(s_decay))
    for i in P:
        s_ref[i] = state[i]

    d = each(lambda yi, si: yi - si * inv_n, y, head_sums(y))
    var = each(lambda si: si * inv_n, head_sums(each(lambda di: di * di, d)))
    bsum = head_sums(each(lambda ri, ki, sl: ri * ki * rk_ref[:, sl], r, kmod, sls))
    for i, (s, sl) in enumerate(zip(sub_of, sls)):
        yn = d[i] * lax.rsqrt(var[i] + RW_GN_EPS) * lnw_ref[:, sl] + lnb_ref[:, sl]
        o_ref[C * s:C * (s + 1), sl] = ((yn + bsum[i] * v[i]) * g[i]).astype(o_ref.dtype)


def rwkv_mix(p, batch, mu, w0, a0, k_k, k_a, r_k, ln_w, ln_b, w_up, a_up, g_up):
    tp = p.shape[0]
    lp = tp // batch
    C = RW_CHUNK * RW_SUB
    nchunk = lp // C

    def pspec(width, base):
        return pl.BlockSpec((C, width), lambda b, c: (b * nchunk + c, base // width))

    def const(shape):
        return pl.BlockSpec(shape, lambda b, c: (0, 0))

    in_specs = [
        pspec(RW_WIDTH, OFF_RW - P_SPLIT), pspec(RW_WIDTH, OFF_RW - P_SPLIT + RW_WIDTH),
        pspec(RW_WIDTH, OFF_RW - P_SPLIT + 2 * RW_WIDTH),
        pspec(LANE, OFF_RW_WD - P_SPLIT), pspec(LANE, OFF_RW_AD - P_SPLIT), pspec(RW_GATE_LORA, OFF_RW_GD - P_SPLIT),
        const(mu.shape),
    ] + [const((1, RW_WIDTH))] * 7 + [const(w_up.shape), const(a_up.shape), const(g_up.shape)]
    row = lambda t: t.reshape(1, -1)
    return pl.pallas_call(
        _rwkv_body,
        grid=(batch, nchunk),
        in_specs=in_specs,
        out_specs=pl.BlockSpec((C, RW_WIDTH), lambda b, c: (b * nchunk + c, 0)),
        out_shape=jax.ShapeDtypeStruct((tp, RW_WIDTH), BF16),
        scratch_shapes=[pltpu.VMEM((RW_PAIRS, LANE, LANE), F32)] + [pltpu.VMEM((8, RW_WIDTH), F32)] * 3
        + [pltpu.VMEM((8, LANE), F32)] * 2 + [pltpu.VMEM((8, RW_GATE_LORA), F32)],
        compiler_params=_params("parallel", "arbitrary"),
        name="rwkv7_mix",
    )(p, p, p, p, p, p, mu, row(w0), row(a0), row(k_k), row(k_a), row(r_k), row(ln_w), row(ln_b),
      w_up, a_up, g_up)


def _ret_body(q_ref, k_ref, v_ref, g_ref, cos_ref, sin_ref, o_ref, state_ref):
    c = pl.program_id(1)
    C = RET_CHUNK
    d = RET_HEAD_DIM

    @pl.when(c == 0)
    def _():
        state_ref[...] = jnp.zeros_like(state_ref)

    row = lax.broadcasted_iota(jnp.int32, (C, C), 0).astype(F32)
    col = lax.broadcasted_iota(jnp.int32, (C, C), 1).astype(F32)
    diff = row - col
    causal = diff >= 0
    heads = range(RET_HEADS)
    subs = range(RET_SUB)
    rws = [slice(C * s, C * (s + 1)) for s in subs for _ in heads]
    sls = [slice(d * h, d * (h + 1)) for _ in subs for h in heads]
    lgs = [math.log1p(-(2.0 ** (-5.0 - h))) for _ in subs for h in heads]

    def each(fn, *lists):
        return [fn(*args) for args in zip(*lists)]

    def rope(x_ref, rw, sl):
        x = x_ref[rw, sl].astype(F32)
        return x * cos_ref[rw, :] + pltpu.roll(x, d // 2, 1) * sin_ref[rw, :]

    qb = each(lambda rw, sl: rope(q_ref, rw, sl).astype(BF16), rws, sls)
    k = each(lambda rw, sl: rope(k_ref, rw, sl) * (d ** -0.5), rws, sls)
    kb = each(lambda t: t.astype(BF16), k)
    vb = each(lambda rw, sl: v_ref[rw, sl], rws, sls)
    s = each(lambda qi, ki, lg: _dot_nt(qi, ki) * jnp.where(causal, jnp.exp(lg * jnp.maximum(diff, 0.0)), 0.0),
             qb, kb, lgs)
    o_intra = each(lambda si, vi: _dot(si.astype(BF16), vi), s, vb)
    kd = each(lambda ki, lg: (ki * jnp.exp(lg * (C - 1.0 - row))).astype(BF16), k, lgs)
    kv = each(_dot_tn, kd, vb)

    state = [state_ref[h] for h in heads]
    o = []
    for sub in subs:
        of = lambda lst: lst[RET_HEADS * sub:RET_HEADS * (sub + 1)]
        o += each(lambda oi, qi, st, lg: oi + _dot(qi, st.astype(BF16)) * jnp.exp(lg * (row + 1.0)),
                  of(o_intra), of(qb), state, of(lgs))
        state = each(lambda st, kvi, lg: st * math.exp(lg * C) + kvi, state, of(kv), of(lgs))
    for h in heads:
        state_ref[h] = state[h]
    o = each(lambda oi: oi * lax.rsqrt(jnp.mean(oi * oi, axis=-1, keepdims=True) + NORM_EPS), o)
    for oi, rw, sl in zip(o, rws, sls):
        g = g_ref[rw, sl].astype(F32)
        o_ref[rw, sl] = (g * _sigmoid(g) * oi).astype(o_ref.dtype)


def retention_mix(p, batch, cos, sin):
    tp = p.shape[0]
    lp = tp // batch
    C = RET_CHUNK * RET_SUB
    nchunk = lp // C
    base = OFF_RET // RET_WIDTH

    def pspec(j):
        return pl.BlockSpec((C, RET_WIDTH), lambda b, c: (b * nchunk + c, base + j))

    tab = pl.BlockSpec((C, RET_HEAD_DIM), lambda b, c: (c, 0))
    return pl.pallas_call(
        _ret_body,
        grid=(batch, nchunk),
        in_specs=[pspec(0), pspec(1), pspec(2), pspec(3), tab, tab],
        out_specs=pl.BlockSpec((C, RET_WIDTH), lambda b, c: (b * nchunk + c, 0)),
        out_shape=jax.ShapeDtypeStruct((tp, RET_WIDTH), BF16),
        scratch_shapes=[pltpu.VMEM((RET_HEADS, RET_HEAD_DIM, RET_HEAD_DIM), F32)],
        compiler_params=_params("parallel", "arbitrary"),
        name="retention_mix",
    )(p, p, p, p, cos, sin)


def _mla_proj_body(qd_ref, kvd_ref, krd_ref, nq_ref, nkv_ref, wuq_ref, wukv_ref, cos_ref, sin_ref,
                   q_out, k_out, v_out):
    cos = cos_ref[...]
    sin = sin_ref[...]

    def rope(x):
        return x * cos + (pltpu.roll(x, MLA_ROPE // 2, 1) + pltpu.roll(x, LANE - MLA_ROPE // 2, 1)) * sin

    def norm(x, g):
        return x * lax.rsqrt(jnp.mean(x * x, axis=-1, keepdims=True) + NORM_EPS) * g

    scale = (MLA_NOPE + MLA_ROPE) ** -0.5 * math.log2(math.e)
    q = _dot(norm(qd_ref[...], nq_ref[...]).astype(BF16), wuq_ref[...]) * scale
    kv = _dot(norm(kvd_ref[...], nkv_ref[...]).astype(BF16), wukv_ref[...])
    kr = rope(krd_ref[...]).astype(k_out.dtype)
    for h in range(MLA_HEADS):
        lo = MLA_QK_PAD * h
        q_out[:, lo:lo + LANE] = q[:, lo:lo + LANE].astype(q_out.dtype)
        q_out[:, lo + LANE:lo + 2 * LANE] = rope(q[:, lo + LANE:lo + 2 * LANE]).astype(q_out.dtype)
        k_out[:, lo:lo + LANE] = kv[:, MLA_NOPE * h:MLA_NOPE * (h + 1)].astype(k_out.dtype)
        k_out[:, lo + LANE:lo + 2 * LANE] = kr
    v_out[...] = kv[:, MLA_HEADS * MLA_NOPE:].astype(v_out.dtype)


def mla_proj(p, batch, norm_q, norm_kv, w_uq, w_ukv, cos, sin, tm):
    tp = p.shape[0]
    lp = tp // batch
    per_seq = lp // tm
    qk_w = MLA_HEADS * MLA_QK_PAD
    const = lambda i: (0, 0)
    return pl.pallas_call(
        _mla_proj_body,
        grid=(tp // tm,),
        in_specs=[
            pl.BlockSpec((tm, MLA_Q_RANK), lambda i: (i, (OFF_MLA_Q - P_SPLIT) // MLA_Q_RANK)),
            pl.BlockSpec((tm, MLA_KV_RANK), lambda i: (i, (OFF_MLA_KV - P_SPLIT) // MLA_KV_RANK)),
            pl.BlockSpec((tm, LANE), lambda i: (i, (OFF_MLA_KR - P_SPLIT) // LANE)),
            pl.BlockSpec((1, MLA_Q_RANK), const),
            pl.BlockSpec((1, MLA_KV_RANK), const),
            pl.BlockSpec((MLA_Q_RANK, qk_w), const),
            pl.BlockSpec((MLA_KV_RANK, MLA_HEADS * (MLA_NOPE + MLA_V)), const),
            pl.BlockSpec((tm, LANE), lambda i: (i % per_seq, 0)),
            pl.BlockSpec((tm, LANE), lambda i: (i % per_seq, 0)),
        ],
        out_specs=[
            pl.BlockSpec((tm, qk_w), lambda i: (i, 0)),
            pl.BlockSpec((tm, qk_w), lambda i: (i, 0)),
            pl.BlockSpec((tm, MLA_WIDTH), lambda i: (i, 0)),
        ],
        out_shape=[
            jax.ShapeDtypeStruct((tp, qk_w), BF16),
            jax.ShapeDtypeStruct((tp, qk_w), BF16),
            jax.ShapeDtypeStruct((tp, MLA_WIDTH), BF16),
        ],
        compiler_params=_params("parallel"),
        name="mla_proj",
    )(p, p, p, norm_q.reshape(1, -1), norm_kv.reshape(1, -1), w_uq, w_ukv, cos, sin)


def _attn_body(q_ref, k_ref, v_ref, o_ref, *, tq):
    i = pl.program_id(2)
    heads = range(ATTN_HEADS_PER_STEP)
    qs = [q_ref[:, MLA_QK_PAD * h:MLA_QK_PAD * (h + 1)] for h in heads]

    def step(off, width, masked, carry):
        ms, accs = carry
        off = pl.multiple_of(off, tq)
        ss = [_dot_nt(qs[h], k_ref[pl.ds(off, width), MLA_QK_PAD * h:MLA_QK_PAD * (h + 1)]) for h in heads]
        if masked:
            row = i * tq + lax.broadcasted_iota(jnp.int32, (tq, width), 0)
            col = off + lax.broadcasted_iota(jnp.int32, (tq, width), 1)
            ss = [jnp.where(col <= row, s, -jnp.inf) for s in ss]
        m_new = [jnp.maximum(ms[h], jnp.max(ss[h], axis=-1, keepdims=True)) for h in heads]
        alpha = [jnp.exp2(ms[h] - m_new[h]) for h in heads]
        ps = [jnp.exp2(ss[h] - m_new[h]).astype(BF16) for h in heads]
        ones = jnp.ones((width, LANE), BF16)
        pv = [_dot(ps[h], jnp.concatenate([v_ref[pl.ds(off, width), MLA_V * h:MLA_V * (h + 1)], ones], axis=1))
              for h in heads]
        accs = [alpha[h] * accs[h] + pv[h] for h in heads]
        return tuple(m_new), tuple(accs)

    carry = (tuple(jnp.full((tq, 1), -1e30, F32) for _ in heads),
             tuple(jnp.zeros((tq, MLA_V + LANE), F32) for _ in heads))
    n_single = (i + 1) % 2
    n_pairs = (i + 1) // 2
    carry = lax.cond(i == 0, lambda c: step(0, tq, True, c), lambda c: c, carry)
    carry = lax.cond((n_single == 1) & (i > 0), lambda c: step(0, tq, False, c), lambda c: c, carry)
    pair_off = lambda p: (n_single + 2 * p) * tq
    carry = lax.fori_loop(0, n_pairs - 1, lambda p, c: step(pair_off(p), 2 * tq, False, c), carry)
    carry = lax.cond(n_pairs > 0, lambda c: step(pair_off(n_pairs - 1), 2 * tq, True, c), lambda c: c, carry)
    _, accs = carry
    for h in heads:
        o_ref[:, MLA_V * h:MLA_V * (h + 1)] = (accs[h][:, :MLA_V] / accs[h][:, MLA_V:]).astype(o_ref.dtype)


def mla_attention(q, k, v, batch, tq):
    tp = q.shape[0]
    lp = tp // batch
    nq = lp // tq
    hs = ATTN_HEADS_PER_STEP
    return pl.pallas_call(
        functools.partial(_attn_body, tq=tq),
        grid=(batch, MLA_HEADS // hs, nq),
        in_specs=[
            pl.BlockSpec((tq, hs * MLA_QK_PAD), lambda b, h, i: (b * nq + i, h)),
            pl.BlockSpec((lp, hs * MLA_QK_PAD), lambda b, h, i: (b, h)),
            pl.BlockSpec((lp, hs * MLA_V), lambda b, h, i: (b, h)),
        ],
        out_specs=pl.BlockSpec((tq, hs * MLA_V), lambda b, h, i: (b * nq + i, h)),
        out_shape=jax.ShapeDtypeStruct((tp, MLA_WIDTH), BF16),
        compiler_params=_params("parallel", "parallel", "arbitrary"),
        name="mla_attention",
    )(q, k, v)


def _merge_body(ya_ref, yb_ref, yc_ref, wa_ref, wb_ref, wc_ref, ga_ref, gb_ref, gc_ref, o_ref):
    def branch(y_ref, w_ref, g_ref):
        return _sigmoid(g_ref[...].astype(F32)) * _dot(y_ref[...], w_ref[...])

    o_ref[...] = (branch(ya_ref, wa_ref, ga_ref) + branch(yb_ref, wb_ref, gb_ref)
                  + branch(yc_ref, wc_ref, gc_ref)).astype(o_ref.dtype)


def merge_branches(ya, yb, yc, wa, wb, wc, layer, p, tm, tn):
    tp = ya.shape[0]

    def yspec(width):
        return pl.BlockSpec((tm, width), lambda i, j: (i, 0))

    def wspec(width):
        return pl.BlockSpec((None, width, tn), lambda i, j: (layer, 0, j))

    def gspec(branch):
        base = (OFF_GATE + branch * D_MODEL) // tn
        return pl.BlockSpec((tm, tn), lambda i, j: (i, base + j))

    return pl.pallas_call(
        _merge_body,
        grid=(tp // tm, D_MODEL // tn),
        in_specs=[yspec(RW_WIDTH), yspec(RET_WIDTH), yspec(MLA_WIDTH),
                  wspec(RW_WIDTH), wspec(RET_WIDTH), wspec(MLA_WIDTH),
                  gspec(0), gspec(1), gspec(2)],
        out_specs=pl.BlockSpec((tm, tn), lambda i, j: (i, j)),
        out_shape=jax.ShapeDtypeStruct((tp, D_MODEL), BF16),
        compiler_params=_params("parallel", "parallel"),
        name="merge_branches",
    )(ya, yb, yc, wa, wb, wc, p, p, p)


def _resid_body(x_ref, w_ref, h_ref, o_ref, ob_ref, ssq_ref):
    hn = h_ref[...] + _dot(x_ref[...], w_ref[...])
    o_ref[...] = hn
    ob_ref[...] = hn.astype(ob_ref.dtype)
    ssq_ref[...] = _ssq_block(hn)


def resid_matmul(x, w, layer, h, tm, tn):
    m, kdim = x.shape
    n = w.shape[2]
    tile = pl.BlockSpec((tm, tn), lambda i, j: (i, j))
    return pl.pallas_call(
        _resid_body,
        grid=(m // tm, n // tn),
        in_specs=[
            pl.BlockSpec((tm, kdim), lambda i, j: (i, 0)),
            pl.BlockSpec((None, kdim, tn), lambda i, j: (layer, 0, j)),
            tile,
        ],
        out_specs=[tile, tile, pl.BlockSpec((tm, LANE), lambda i, j: (i, j))],
        out_shape=[jax.ShapeDtypeStruct((m, n), F32), jax.ShapeDtypeStruct((m, n), BF16),
                   jax.ShapeDtypeStruct((m, LANE * (n // tn)), F32)],
        compiler_params=_params("parallel", "parallel"),
        name="resid_matmul",
    )(x, w, h)


def _ffn_up_body(x_ref, ssq_ref, wg_ref, wu_ref, o_ref):
    x = x_ref[...]
    r = _row_scale(ssq_ref, x.shape[1])
    hg = r * _dot(x, wg_ref[...])
    hu = r * _dot(x, wu_ref[...])
    o_ref[...] = (hg * _sigmoid(hg) * hu).astype(o_ref.dtype)


def ffn_up(xb, ssq, w_gate_up, layer, tm, tn):
    m, kdim = xb.shape
    hidden = w_gate_up.shape[2] // 2
    nj = hidden // tn
    return pl.pallas_call(
        _ffn_up_body,
        grid=(m // tm, nj),
        in_specs=[
            pl.BlockSpec((tm, kdim), lambda i, j: (i, 0)),
            pl.BlockSpec((tm, ssq.shape[1]), lambda i, j: (i, 0)),
            pl.BlockSpec((None, kdim, tn), lambda i, j: (layer, 0, j)),
            pl.BlockSpec((None, kdim, tn), lambda i, j: (layer, 0, nj + j)),
        ],
        out_specs=pl.BlockSpec((tm, tn), lambda i, j: (i, j)),
        out_shape=jax.ShapeDtypeStruct((m, hidden), BF16),
        compiler_params=_params("parallel", "parallel"),
        name="ffn_up",
    )(xb, ssq, w_gate_up, w_gate_up)


def _pad_cols(w, width):
    return jnp.pad(w, [(0, 0)] * (w.ndim - 1) + [(0, width - w.shape[-1])])


def _w_in_pieces():
    widths = [3 * RW_WIDTH, RW_DECAY_LORA, RW_A_LORA, RW_GATE_LORA, 4 * RET_WIDTH, MLA_Q_RANK, MLA_KV_RANK, MLA_ROPE,
              3 * D_MODEL]
    dsts = [OFF_RW, OFF_RW_WD, OFF_RW_AD, OFF_RW_GD, OFF_RET, OFF_MLA_Q, OFF_MLA_KV, OFF_MLA_KR, OFF_GATE]
    pieces, src = [], 0
    for dst, width in zip(dsts, widths):
        pieces.append((dst, src, width))
        src += width
    return pieces


def _pack_w_in_body(w_ref, g_ref, o_ref):
    g = g_ref[...]
    covered = 0
    for dst, src, width in sorted(_w_in_pieces()):
        if dst > covered:
            o_ref[:, covered:dst] = jnp.zeros((o_ref.shape[0], dst - covered), o_ref.dtype)
        o_ref[:, dst:dst + width] = (w_ref[:, src:src + width] * g).astype(o_ref.dtype)
        covered = dst + width
    o_ref[:, covered:] = jnp.zeros((o_ref.shape[0], o_ref.shape[1] - covered), o_ref.dtype)


def _pack_w_in(w_in, gain, tr=128):
    nl, rows, cols = w_in.shape
    return pl.pallas_call(
        _pack_w_in_body,
        grid=(nl, rows // tr),
        in_specs=[pl.BlockSpec((None, tr, cols), lambda l, i: (l, i, 0)),
                  pl.BlockSpec((None, tr, 1), lambda l, i: (l, i, 0))],
        out_specs=pl.BlockSpec((None, tr, P_COLS), lambda l, i: (l, i, 0)),
        out_shape=jax.ShapeDtypeStruct((nl, rows, P_COLS), BF16),
        compiler_params=_params("parallel", "parallel"),
        name="pack_w_in",
    )(w_in, gain[..., None])


def _pack_mu(mu):
    rkv = mu[..., :3 * RW_WIDTH]
    wd = mu[..., 3 * RW_WIDTH:3 * RW_WIDTH + RW_DECAY_LORA]
    ad = mu[..., 3 * RW_WIDTH + RW_DECAY_LORA:3 * RW_WIDTH + RW_DECAY_LORA + RW_A_LORA]
    gd = mu[..., 3 * RW_WIDTH + RW_DECAY_LORA + RW_A_LORA:]
    return jnp.concatenate([rkv, _pad_cols(wd, LANE), _pad_cols(ad, LANE), gd], axis=-1)


def _pad_rows(w, rows):
    return jnp.pad(w, [(0, 0)] * (w.ndim - 2) + [(0, rows - w.shape[-2]), (0, 0)])


def _pack_w_uq(w):
    nl, rank, _ = w.shape
    w = w.reshape(nl, rank, MLA_HEADS, MLA_NOPE + MLA_ROPE)
    w = jnp.pad(w, ((0, 0), (0, 0), (0, 0), (0, MLA_QK_PAD - MLA_NOPE - MLA_ROPE)))
    return w.reshape(nl, rank, MLA_HEADS * MLA_QK_PAD).astype(BF16)


def _pack_w_ukv(w):
    nl, rank, _ = w.shape
    w = w.reshape(nl, rank, MLA_HEADS, 2, MLA_NOPE)
    w = jnp.swapaxes(w, 2, 3)
    return w.reshape(nl, rank, 2 * MLA_HEADS * MLA_NOPE).astype(BF16)


def _rope_tables(lp):
    pos = jnp.arange(lp, dtype=F32)

    def tables(dim):
        inv = ROPE_BASE ** (-jnp.arange(0, dim, 2, dtype=F32) / dim)
        ang = pos[:, None] * inv[None, :]
        return jnp.cos(ang), jnp.sin(ang)

    c, s = tables(RET_HEAD_DIM)
    ret = (jnp.concatenate([c, c], axis=1), jnp.concatenate([-s, s], axis=1))
    c, s = tables(MLA_ROPE)
    z = jnp.zeros((lp, LANE - MLA_ROPE), F32)
    mla = (jnp.concatenate([c, c, z], axis=1), jnp.concatenate([-s, s, z], axis=1))
    return ret, mla


def kernel(x, meta_tokens, norm_mix, w_in, rw_mu, rw_w0, rw_w_up, rw_a0, rw_a_up, rw_g_up, rw_k_k, rw_k_a, rw_r_k, rw_ln_w, rw_ln_b, mla_norm_q, mla_norm_kv, mla_w_uq, mla_w_ukv, w_br_rwkv, w_br_ret, w_br_mla, w_out, norm_ffn, w_gate_up, w_down, final_norm):
    batch, seq, d = x.shape
    depth = w_in.shape[0]
    lp = -(-(N_META + seq) // SEQ_ALIGN) * SEQ_ALIGN
    tp = batch * lp

    meta = jnp.broadcast_to(meta_tokens[None].astype(x.dtype), (batch, N_META, d))
    pad = jnp.zeros((batch, lp - N_META - seq, d), x.dtype)
    h = jnp.concatenate([meta, x, pad], axis=1).reshape(tp, d)

    wp = _pack_w_in(w_in, norm_mix)
    mu = _pack_mu(rw_mu)
    w_up = _pad_rows(rw_w_up, LANE).astype(BF16)
    a_up = _pad_rows(rw_a_up, LANE).astype(BF16)
    g_up = rw_g_up.astype(BF16)
    wuq = _pack_w_uq(mla_w_uq)
    wukv = _pack_w_ukv(mla_w_ukv)
    wa = w_br_rwkv.astype(BF16)
    wb = w_br_ret.astype(BF16)
    wc = w_br_mla.astype(BF16)
    wo = w_out.astype(BF16)
    wgu = (w_gate_up * norm_ffn[..., None]).astype(BF16)
    wdn = w_down.astype(BF16)
    (cos_ret, sin_ret), (cos_mla, sin_mla) = _rope_tables(lp)

    def row_tile(pref):
        return next((t for t in pref if tp % t == 0), SEQ_ALIGN)

    tm = row_tile((768,))
    tm_wide = row_tile((1536, 768))
    tm_seq = SEQ_ALIGN

    hb, ssq = stream_prep(h, tm)
    for l in range(depth):
        p_lo = norm_matmul(hb, ssq, wp, l, 0, P_SPLIT, tm_wide, 2 * P_TILE_N, BF16)
        p = norm_matmul(hb, ssq, wp, l, P_SPLIT, P_COLS - P_SPLIT, tm_wide, P_TILE_N, F32)
        ya = rwkv_mix(p, batch, mu[l:l + 1], rw_w0[l], rw_a0[l], rw_k_k[l], rw_k_a[l], rw_r_k[l],
                      rw_ln_w[l], rw_ln_b[l], w_up[l], a_up[l], g_up[l])
        yb = retention_mix(p_lo, batch, cos_ret, sin_ret)
        q, k, v = mla_proj(p, batch, mla_norm_q[l], mla_norm_kv[l], wuq[l], wukv[l], cos_mla, sin_mla, tm_seq)
        yc = mla_attention(q, k, v, batch, tm_seq)
        merged = merge_branches(ya, yb, yc, wa, wb, wc, l, p_lo, tm, 1024)
        h, hb, ssq = resid_matmul(merged, wo, l, h, tm_seq, d)
        act = ffn_up(hb, ssq, wgu, l, tm_wide, 512)
        h, hb, ssq = resid_matmul(act, wdn, l, h, tm, 512)
    tm_out = next(t for t in (512, 256, LANE) if seq % t == 0)
    return final_rmsnorm(h, final_norm, batch, N_META, seq, tm_out).reshape(batch, seq, d)
```

```python
import functools
import math

import jax
import jax.numpy as jnp
from jax import lax
from jax.experimental import pallas as pl
from jax.experimental.pallas import tpu as pltpu

F32 = jnp.float32
BF16 = jnp.bfloat16

D_MODEL = 2048
N_META = 16
NORM_EPS = 1e-6
ROPE_BASE = 10000.0

RW_HEADS = 16
RW_HEAD_DIM = 64
RW_WIDTH = RW_HEADS * RW_HEAD_DIM
RW_DECAY_LORA = 96
RW_A_LORA = 96
RW_GATE_LORA = 256
RW_GN_EPS = RW_HEAD_DIM * 1e-5
RW_CHUNK = 64
RW_SUB = 3
RW_PAIRS = RW_WIDTH // 128

RET_HEADS = 8
RET_HEAD_DIM = 128
RET_WIDTH = RET_HEADS * RET_HEAD_DIM
RET_CHUNK = 128
RET_SUB = 3

MLA_HEADS = 8
MLA_NOPE = 128
MLA_ROPE = 64
MLA_V = 128
MLA_Q_RANK = 512
MLA_KV_RANK = 256
MLA_WIDTH = MLA_HEADS * MLA_V
MLA_QK_PAD = 256
ATTN_HEADS_PER_STEP = 2

FFN_HIDDEN = -(-8 * D_MODEL // (3 * 256)) * 256

LANE = 128
SEQ_ALIGN = 384
assert SEQ_ALIGN % (RW_CHUNK * RW_SUB) == 0 and SEQ_ALIGN % (RET_CHUNK * RET_SUB) == 0 and SEQ_ALIGN % LANE == 0

OFF_RET = 0
OFF_GATE = OFF_RET + 4 * RET_WIDTH
OFF_RW = OFF_GATE + 3 * D_MODEL
OFF_RW_WD = OFF_RW + 3 * RW_WIDTH
OFF_RW_AD = OFF_RW_WD + LANE
OFF_RW_GD = OFF_RW_AD + LANE
OFF_MLA_Q = OFF_RW_GD + RW_GATE_LORA
OFF_MLA_KV = OFF_MLA_Q + MLA_Q_RANK
OFF_MLA_KR = OFF_MLA_KV + MLA_KV_RANK
P_COLS_USED = OFF_MLA_KR + LANE
P_TILE_N = 512
P_COLS = -(-P_COLS_USED // P_TILE_N) * P_TILE_N
P_SPLIT = OFF_RW
assert P_SPLIT % (2 * P_TILE_N) == 0
assert (P_COLS - P_SPLIT) % (3 * P_TILE_N) == 0

VMEM_LIMIT = 48 * 1024 * 1024


def _params(*sem):
    return pltpu.CompilerParams(dimension_semantics=sem, vmem_limit_bytes=VMEM_LIMIT)


def _sigmoid(x):
    return 1.0 / (1.0 + jnp.exp(-x))


def _dot(a, b):
    return jnp.dot(a, b, preferred_element_type=F32)


def _dot_nt(a, b):
    return lax.dot_general(a, b, (((1,), (1,)), ((), ())), preferred_element_type=F32)


def _dot_tn(a, b):
    return lax.dot_general(a, b, (((0,), (0,)), ((), ())), preferred_element_type=F32)


def _rmsnorm_body(x_ref, g_ref, o_ref):
    x = x_ref[...]
    y = x * lax.rsqrt(jnp.mean(x * x, axis=-1, keepdims=True) + NORM_EPS)
    o_ref[...] = (y * g_ref[...]).astype(o_ref.dtype)


def final_rmsnorm(x, g, batch, first, count, tm):
    m, d = x.shape
    rows_per_batch = m // batch
    assert rows_per_batch % 8 == 0 and first % 8 == 0 and tm % 8 == 0
    tiles = count // tm
    return pl.pallas_call(
        _rmsnorm_body,
        grid=(batch, tiles),
        in_specs=[pl.BlockSpec((pl.Element(tm), pl.Element(d)),
                               lambda b, t: (pl.multiple_of(b * rows_per_batch + first + t * tm, 8), 0)),
                  pl.BlockSpec((1, d), lambda b, t: (0, 0))],
        out_specs=pl.BlockSpec((tm, d), lambda b, t: (b * tiles + t, 0)),
        out_shape=jax.ShapeDtypeStruct((batch * count, d), x.dtype),
        compiler_params=_params("parallel", "parallel"),
        name="final_rmsnorm",
    )(x, g.reshape(1, d))


def _row_scale(ssq_ref, d):
    ssq = ssq_ref[...]
    total = ssq[:, 0:1]
    for j in range(1, ssq.shape[1] // LANE):
        total = total + ssq[:, LANE * j:LANE * j + 1]
    return lax.rsqrt(total * (1.0 / d) + NORM_EPS)


def _ssq_block(x):
    return jnp.broadcast_to(jnp.sum(x * x, axis=-1, keepdims=True), (x.shape[0], LANE))


def _stream_prep_body(x_ref, xb_ref, ssq_ref):
    x = x_ref[...]
    xb_ref[...] = x.astype(xb_ref.dtype)
    ssq_ref[...] = _ssq_block(x)


def stream_prep(x, tm):
    m, d = x.shape
    return pl.pallas_call(
        _stream_prep_body,
        grid=(m // tm,),
        in_specs=[pl.BlockSpec((tm, d), lambda i: (i, 0))],
        out_specs=[pl.BlockSpec((tm, d), lambda i: (i, 0)), pl.BlockSpec((tm, LANE), lambda i: (i, 0))],
        out_shape=[jax.ShapeDtypeStruct((m, d), BF16), jax.ShapeDtypeStruct((m, LANE), F32)],
        compiler_params=_params("parallel"),
        name="stream_prep",
    )(x)


def _norm_matmul_body(x_ref, ssq_ref, w_ref, o_ref):
    o_ref[...] = (_row_scale(ssq_ref, x_ref.shape[1]) * _dot(x_ref[...], w_ref[...])).astype(o_ref.dtype)


def norm_matmul(xb, ssq, w, layer, tm, tn, out_dtype):
    m, k = xb.shape
    n = w.shape[2]
    assert n % tn == 0 and m % tm == 0
    return pl.pallas_call(
        _norm_matmul_body,
        grid=(m // tm, n // tn),
        in_specs=[pl.BlockSpec((tm, k), lambda i, j: (i, 0)), pl.BlockSpec((tm, ssq.shape[1]), lambda i, j: (i, 0)),
                  pl.BlockSpec((None, k, tn), lambda i, j: (layer, 0, j))],
        out_specs=pl.BlockSpec((tm, tn), lambda i, j: (i, j)),
        out_shape=jax.ShapeDtypeStruct((m, n), out_dtype),
        compiler_params=_params("parallel", "parallel"),
        name="in_proj",
    )(xb, ssq, w)


def _rwkv_body(r_ref, k_ref, v_ref, wd_ref, ad_ref, gd_ref, mu_ref,
               w0_ref, a0_ref, kk_ref, ka_ref, rk_ref, lnw_ref, lnb_ref,
               wup_ref, aup_ref, gup_ref,
               o_ref,
               s_ref, pr_ref, pk_ref, pv_ref, pwd_ref, pad_ref, pgd_ref):
    c = pl.program_id(1)
    C = RW_CHUNK
    RS = RW_SUB * C
    HD = RW_HEAD_DIM

    @pl.when(c == 0)
    def _():
        s_ref[...] = jnp.zeros_like(s_ref)
        pr_ref[...] = jnp.zeros_like(pr_ref)
        pk_ref[...] = jnp.zeros_like(pk_ref)
        pv_ref[...] = jnp.zeros_like(pv_ref)
        pwd_ref[...] = jnp.zeros_like(pwd_ref)
        pad_ref[...] = jnp.zeros_like(pad_ref)
        pgd_ref[...] = jnp.zeros_like(pgd_ref)

    def shift(x_ref, prev_ref, mu, sl):
        z = x_ref[:, sl]
        first = lax.broadcasted_iota(jnp.int32, z.shape, 0) == 0
        zs = jnp.where(first, prev_ref[0:1, sl], pltpu.roll(z, 1, 0))
        prev_ref[0:1, sl] = z[RS - 1:RS, :]
        return z + (zs - z) * mu

    mu_lora = 3 * RW_WIDTH
    full = slice(None)
    wd = shift(wd_ref, pwd_ref, mu_ref[:, mu_lora:mu_lora + LANE], full)
    ad = shift(ad_ref, pad_ref, mu_ref[:, mu_lora + LANE:mu_lora + 2 * LANE], full)
    gd = shift(gd_ref, pgd_ref, mu_ref[:, mu_lora + 2 * LANE:], full)
    subs = range(RW_SUB)

    def rows(t, s):
        return t[C * s:C * (s + 1)]

    tanh_wd = [rows(jnp.tanh(wd), s).astype(BF16) for s in subs]
    ad_b = [rows(ad, s).astype(BF16) for s in subs]
    sig_gd = [rows(_sigmoid(gd), s).astype(BF16) for s in subs]

    lane_sq = lax.broadcasted_iota(jnp.int32, (LANE, LANE), 1)
    row_sq = lax.broadcasted_iota(jnp.int32, (LANE, LANE), 0)
    same_head = (lane_sq < HD) == (row_sq < HD)
    head_ones = same_head.astype(BF16)
    eye = (lane_sq == row_sq).astype(F32)
    rc = lax.broadcasted_iota(jnp.int32, (C, C), 0)
    cc = lax.broadcasted_iota(jnp.int32, (C, C), 1)
    tril_incl = (cc <= rc).astype(BF16)
    lane_tall = lax.broadcasted_iota(jnp.int32, (2 * C, LANE), 1)
    lane_c = lax.broadcasted_iota(jnp.int32, (C, LANE), 1)
    row_c = lax.broadcasted_iota(jnp.int32, (C, LANE), 0)
    head0_c = lane_c < HD
    strict_lo = lane_c < row_c
    strict_hi = (lane_c >= C) & (lane_c - C < row_c)
    incl_lo = lane_c <= row_c
    incl_hi = (lane_c >= C) & (lane_c - C <= row_c)
    inv_n = 1.0 / HD

    P = range(RW_PAIRS)

    def head_sums(ts):
        out = _dot(jnp.concatenate([t.astype(BF16) for t in ts], axis=0), head_ones)
        return [out[C * i:C * (i + 1)] for i in range(len(ts))]

    pair_sls = [slice(LANE * i, LANE * (i + 1)) for i in P]

    def each(fn, *lists):
        return [fn(*args) for args in zip(*lists)]

    def items(per_pair):
        return [rows(t, s) for s in subs for t in per_pair]

    r = items([shift(r_ref, pr_ref, mu_ref[:, sl], sl) for sl in pair_sls])
    k = items([shift(k_ref, pk_ref, mu_ref[:, RW_WIDTH + sl.start:RW_WIDTH + sl.stop], sl) for sl in pair_sls])
    v = items([shift(v_ref, pv_ref, mu_ref[:, 2 * RW_WIDTH + sl.start:2 * RW_WIDTH + sl.stop], sl)
               for sl in pair_sls])
    sls = pair_sls * RW_SUB
    sub_of = [s for s in subs for _ in P]

    def log_decay(s, sl):
        x = -(w0_ref[:, sl] + _dot(tanh_wd[s], wup_ref[:, sl]))
        softplus = jnp.maximum(x, 0.0) + jnp.log1p(jnp.exp(-jnp.abs(x)))
        return -jnp.exp(-softplus - 0.5)

    lw = each(log_decay, sub_of, sls)
    a = each(lambda s, sl: _sigmoid(a0_ref[:, sl] + _dot(ad_b[s], aup_ref[:, sl])), sub_of, sls)
    g = each(lambda s, sl: _dot(sig_gd[s], gup_ref[:, sl]), sub_of, sls)

    kkr = each(lambda ki, sl: ki * kk_ref[:, sl], k, sls)
    ksq = head_sums(each(lambda t: t * t, kkr))
    kkn = each(lambda t, ss: t / jnp.maximum(jnp.sqrt(ss), 1e-12), kkr, ksq)
    kmod = each(lambda ki, ai, sl: ki * (1.0 + (ai - 1.0) * ka_ref[:, sl]), k, a, sls)
    beta = each(lambda ai, t: ai * t, a, kkn)

    def running_sum(lwi):
        hi = lwi.astype(BF16)
        both = _dot(tril_incl, jnp.concatenate([hi, (lwi - hi.astype(F32)).astype(BF16)], axis=1))
        return both[:, :LANE] + both[:, LANE:]

    lcum = each(running_sum, lw)
    lend = each(lambda t: t[C - 1:C, :], lcum)
    rh = each(lambda ri, lc: ri * jnp.exp(lc), r, lcum)
    kh = each(lambda t, lc, lwi: t * jnp.exp(lc - lwi), kkn, lcum, lw)
    e_neg = each(lambda lc: jnp.exp(-lc), lcum)
    e_end = each(lambda le, lc: jnp.exp(le - lc), lend, lcum)
    kb = each(lambda t, e: t * e, kmod, e_neg)
    bb = each(lambda t, e: t * e, beta, e_neg)
    kbe = each(lambda t, e: t * e, kmod, e_end)
    bbe = each(lambda t, e: t * e, beta, e_end)

    kr_f = each(lambda x1, x2: jnp.concatenate([x1, x2], axis=0), kh, rh)
    bk = each(lambda x1, x2: jnp.concatenate([x1, x2], axis=0).astype(BF16), bb, kb)
    vb = each(lambda t: t.astype(BF16), v)

    def gram(krf, bki):
        kr2 = jnp.concatenate([jnp.where(lane_tall < HD, krf, 0.0), jnp.where(lane_tall >= HD, krf, 0.0)], axis=0)
        return _dot_nt(kr2.astype(BF16), bki)

    g_all = each(gram, kr_f, bk)

    n_bd = each(lambda ga: jnp.concatenate([jnp.where(strict_lo, -ga[0:C], 0.0),
                                            jnp.where(strict_hi, -pltpu.roll(ga[2 * C:3 * C], C, 1), 0.0)], axis=0),
                g_all)
    t = each(lambda n: eye + n, n_bd)
    pw = each(lambda n: _dot(n.astype(BF16), n.astype(BF16)), n_bd)
    for _ in range(4):
        both = each(lambda ti, pi: _dot(jnp.concatenate([ti, pi], axis=0).astype(BF16), pi.astype(BF16)), t, pw)
        t = each(lambda ti, bi: ti + bi[:LANE], t, both)
        pw = each(lambda bi: bi[LANE:], both)
    t = each(lambda ti, pi: ti + _dot(ti.astype(BF16), pi.astype(BF16)), t, pw)

    def intra_rhs(ga, vbi):
        m1s = jnp.concatenate([jnp.where(strict_hi, ga[0:C], 0.0), jnp.where(strict_hi, ga[2 * C:3 * C], 0.0)], axis=0)
        return _dot(m1s.astype(BF16), jnp.concatenate([vbi, vbi], axis=0))

    q_intra = each(intra_rhs, g_all, vb)

    def m2(gb):
        return jnp.where(incl_lo, -gb, jnp.where(incl_hi, gb, 0.0))

    m2s = each(lambda ga: jnp.concatenate([m2(ga[C:2 * C]), m2(ga[3 * C:4 * C])], axis=0).astype(BF16), g_all)
    kbe_all = each(lambda kbei, bbei: jnp.concatenate([kbei, -bbei], axis=0).astype(BF16), kbe, bbe)
    s_decay = each(jnp.exp, lend)

    state = [s_ref[i] for i in P]
    y = []
    for s in subs:
        of = lambda lst: lst[RW_PAIRS * s:RW_PAIRS * (s + 1)]
        p_all = each(lambda x1, si: _dot_nt(x1.astype(BF16), si.astype(BF16)), of(kr_f), state)
        q_s = each(lambda qi, pa: jnp.where(same_head, qi + jnp.concatenate([pa[:C], pa[:C]], axis=0), 0.0),
                   of(q_intra), p_all)
        u_s = each(lambda ti, qi: _dot(ti.astype(BF16), qi.astype(BF16)), of(t), q_s)
        u = each(lambda us: us[:C] + us[C:], u_s)
        y_s = each(lambda mi, ui, vi: _dot(mi, jnp.concatenate([ui, vi], axis=0).astype(BF16)), of(m2s), u, of(v))
        y += each(lambda pa, ys: pa[C:] + jnp.where(head0_c, ys[:C], ys[C:]), p_all, y_s)
        ds = each(lambda vi, ui, kb_all: _dot_tn(jnp.concatenate([vi, ui], axis=0).astype(BF16), kb_all),
                  of(v), u, of(kbe_all))
        state = each(lambda si, di, dec: si * dec + jnp.where(same_head, di, 0.0), state, ds, of(s_decay))
    for i in P:
        s_ref[i] = state[i]

    d = each(lambda yi, si: yi - si * inv_n, y, head_sums(y))
    var = each(lambda si: si * inv_n, head_sums(each(lambda di: di * di, d)))
    bsum = head_sums(each(lambda ri, ki, sl: ri * ki * rk_ref[:, sl], r, kmod, sls))
    for i, (s, sl) in enumerate(zip(sub_of, sls)):
        yn = d[i] * lax.rsqrt(var[i] + RW_GN_EPS) * lnw_ref[:, sl] + lnb_ref[:, sl]
        o_ref[C * s:C * (s + 1), sl] = ((yn + bsum[i] * v[i]) * g[i]).astype(o_ref.dtype)


def rwkv_mix(p, batch, mu, w0, a0, k_k, k_a, r_k, ln_w, ln_b, w_up, a_up, g_up):
    tp = p.shape[0]
    lp = tp // batch
    C = RW_CHUNK * RW_SUB
    nchunk = lp // C

    def pspec(width, base):
        return pl.BlockSpec((C, width), lambda b, c: (b * nchunk + c, base // width))

    def const(shape):
        return pl.BlockSpec(shape, lambda b, c: (0, 0))

    in_specs = [
        pspec(RW_WIDTH, OFF_RW - P_SPLIT), pspec(RW_WIDTH, OFF_RW - P_SPLIT + RW_WIDTH),
        pspec(RW_WIDTH, OFF_RW - P_SPLIT + 2 * RW_WIDTH),
        pspec(LANE, OFF_RW_WD - P_SPLIT), pspec(LANE, OFF_RW_AD - P_SPLIT), pspec(RW_GATE_LORA, OFF_RW_GD - P_SPLIT),
        const(mu.shape),
    ] + [const((1, RW_WIDTH))] * 7 + [const(w_up.shape), const(a_up.shape), const(g_up.shape)]
    row = lambda t: t.reshape(1, -1)
    return pl.pallas_call(
        _rwkv_body,
        grid=(batch, nchunk),
        in_specs=in_specs,
        out_specs=pl.BlockSpec((C, RW_WIDTH), lambda b, c: (b * nchunk + c, 0)),
        out_shape=jax.ShapeDtypeStruct((tp, RW_WIDTH), BF16),
        scratch_shapes=[pltpu.VMEM((RW_PAIRS, LANE, LANE), F32)] + [pltpu.VMEM((8, RW_WIDTH), F32)] * 3
        + [pltpu.VMEM((8, LANE), F32)] * 2 + [pltpu.VMEM((8, RW_GATE_LORA), F32)],
        compiler_params=_params("parallel", "arbitrary"),
        name="rwkv7_mix",
    )(p, p, p, p, p, p, mu, row(w0), row(a0), row(k_k), row(k_a), row(r_k), row(ln_w), row(ln_b),
      w_up, a_up, g_up)


def _ret_body(q_ref, k_ref, v_ref, g_ref, cos_ref, sin_ref, o_ref, state_ref):
    c = pl.program_id(1)
    C = RET_CHUNK
    d = RET_HEAD_DIM

    @pl.when(c == 0)
    def _():
        state_ref[...] = jnp.zeros_like(state_ref)

    row = lax.broadcasted_iota(jnp.int32, (C, C), 0).astype(F32)
    col = lax.broadcasted_iota(jnp.int32, (C, C), 1).astype(F32)
    diff = row - col
    causal = diff >= 0
    heads = range(RET_HEADS)
    subs = range(RET_SUB)
    rws = [slice(C * s, C * (s + 1)) for s in subs for _ in heads]
    sls = [slice(d * h, d * (h + 1)) for _ in subs for h in heads]
    lgs = [math.log1p(-(2.0 ** (-5.0 - h))) for _ in subs for h in heads]

    def each(fn, *lists):
        return [fn(*args) for args in zip(*lists)]

    def rope(x_ref, rw, sl):
        x = x_ref[rw, sl].astype(F32)
        return x * cos_ref[rw, :] + pltpu.roll(x, d // 2, 1) * sin_ref[rw, :]

    qb = each(lambda rw, sl: rope(q_ref, rw, sl).astype(BF16), rws, sls)
    k = each(lambda rw, sl: rope(k_ref, rw, sl) * (d ** -0.5), rws, sls)
    kb = each(lambda t: t.astype(BF16), k)
    vb = each(lambda rw, sl: v_ref[rw, sl], rws, sls)
    s = each(lambda qi, ki, lg: _dot_nt(qi, ki) * jnp.where(causal, jnp.exp(lg * jnp.maximum(diff, 0.0)), 0.0),
             qb, kb, lgs)
    o_intra = each(lambda si, vi: _dot(si.astype(BF16), vi), s, vb)
    kd = each(lambda ki, lg: (ki * jnp.exp(lg * (C - 1.0 - row))).astype(BF16), k, lgs)
    kv = each(_dot_tn, kd, vb)

    state = [state_ref[h] for h in heads]
    o = []
    for sub in subs:
        of = lambda lst: lst[RET_HEADS * sub:RET_HEADS * (sub + 1)]
        o += each(lambda oi, qi, st, lg: oi + _dot(qi, st.astype(BF16)) * jnp.exp(lg * (row + 1.0)),
                  of(o_intra), of(qb), state, of(lgs))
        state = each(lambda st, kvi, lg: st * math.exp(lg * C) + kvi, state, of(kv), of(lgs))
    for h in heads:
        state_ref[h] = state[h]
    o = each(lambda oi: oi * lax.rsqrt(jnp.mean(oi * oi, axis=-1, keepdims=True) + NORM_EPS), o)
    for oi, rw, sl in zip(o, rws, sls):
        g = g_ref[rw, sl].astype(F32)
        o_ref[rw, sl] = (g * _sigmoid(g) * oi).astype(o_ref.dtype)


def retention_mix(p, batch, cos, sin):
    tp = p.shape[0]
    lp = tp // batch
    C = RET_CHUNK * RET_SUB
    nchunk = lp // C
    base = OFF_RET // RET_WIDTH

    def pspec(j):
        return pl.BlockSpec((C, RET_WIDTH), lambda b, c: (b * nchunk + c, base + j))

    tab = pl.BlockSpec((C, RET_HEAD_DIM), lambda b, c: (c, 0))
    return pl.pallas_call(
        _ret_body,
        grid=(batch, nchunk),
        in_specs=[pspec(0), pspec(1), pspec(2), pspec(3), tab, tab],
        out_specs=pl.BlockSpec((C, RET_WIDTH), lambda b, c: (b * nchunk + c, 0)),
        out_shape=jax.ShapeDtypeStruct((tp, RET_WIDTH), BF16),
        scratch_shapes=[pltpu.VMEM((RET_HEADS, RET_HEAD_DIM, RET_HEAD_DIM), F32)],
        compiler_params=_params("parallel", "arbitrary"),
        name="retention_mix",
    )(p, p, p, p, cos, sin)


def _mla_proj_body(qd_ref, kvd_ref, krd_ref, nq_ref, nkv_ref, wuq_ref, wukv_ref, cos_ref, sin_ref,
                   q_out, k_out, v_out):
    cos = cos_ref[...]
    sin = sin_ref[...]

    def rope(x):
        return x * cos + (pltpu.roll(x, MLA_ROPE // 2, 1) + pltpu.roll(x, LANE - MLA_ROPE // 2, 1)) * sin

    def norm(x, g):
        return x * lax.rsqrt(jnp.mean(x * x, axis=-1, keepdims=True) + NORM_EPS) * g

    scale = (MLA_NOPE + MLA_ROPE) ** -0.5 * math.log2(math.e)
    q = _dot(norm(qd_ref[...], nq_ref[...]).astype(BF16), wuq_ref[...]) * scale
    kv = _dot(norm(kvd_ref[...], nkv_ref[...]).astype(BF16), wukv_ref[...])
    kr = rope(krd_ref[...]).astype(k_out.dtype)
    for h in range(MLA_HEADS):
        lo = MLA_QK_PAD * h
        q_out[:, lo:lo + LANE] = q[:, lo:lo + LANE].astype(q_out.dtype)
        q_out[:, lo + LANE:lo + 2 * LANE] = rope(q[:, lo + LANE:lo + 2 * LANE]).astype(q_out.dtype)
        k_out[:, lo:lo + LANE] = kv[:, MLA_NOPE * h:MLA_NOPE * (h + 1)].astype(k_out.dtype)
        k_out[:, lo + LANE:lo + 2 * LANE] = kr
    v_out[...] = kv[:, MLA_HEADS * MLA_NOPE:].astype(v_out.dtype)


def mla_proj(p, batch, norm_q, norm_kv, w_uq, w_ukv, cos, sin, tm):
    tp = p.shape[0]
    lp = tp // batch
    per_seq = lp // tm
    qk_w = MLA_HEADS * MLA_QK_PAD
    const = lambda i: (0, 0)
    return pl.pallas_call(
        _mla_proj_body,
        grid=(tp // tm,),
        in_specs=[
            pl.BlockSpec((tm, MLA_Q_RANK), lambda i: (i, (OFF_MLA_Q - P_SPLIT) // MLA_Q_RANK)),
            pl.BlockSpec((tm, MLA_KV_RANK), lambda i: (i, (OFF_MLA_KV - P_SPLIT) // MLA_KV_RANK)),
            pl.BlockSpec((tm, LANE), lambda i: (i, (OFF_MLA_KR - P_SPLIT) // LANE)),
            pl.BlockSpec((1, MLA_Q_RANK), const),
            pl.BlockSpec((1, MLA_KV_RANK), const),
            pl.BlockSpec((MLA_Q_RANK, qk_w), const),
            pl.BlockSpec((MLA_KV_RANK, MLA_HEADS * (MLA_NOPE + MLA_V)), const),
            pl.BlockSpec((tm, LANE), lambda i: (i % per_seq, 0)),
            pl.BlockSpec((tm, LANE), lambda i: (i % per_seq, 0)),
        ],
        out_specs=[
            pl.BlockSpec((tm, qk_w), lambda i: (i, 0)),
            pl.BlockSpec((tm, qk_w), lambda i: (i, 0)),
            pl.BlockSpec((tm, MLA_WIDTH), lambda i: (i, 0)),
        ],
        out_shape=[
            jax.ShapeDtypeStruct((tp, qk_w), BF16),
            jax.ShapeDtypeStruct((tp, qk_w), BF16),
            jax.ShapeDtypeStruct((tp, MLA_WIDTH), BF16),
        ],
        compiler_params=_params("parallel"),
        name="mla_proj",
    )(p, p, p, norm_q.reshape(1, -1), norm_kv.reshape(1, -1), w_uq, w_ukv, cos, sin)


def _attn_body(q_ref, k_ref, v_ref, o_ref, *, tq):
    i = pl.program_id(2)
    heads = range(ATTN_HEADS_PER_STEP)
    qs = [q_ref[:, MLA_QK_PAD * h:MLA_QK_PAD * (h + 1)] for h in heads]

    def step(off, width, masked, carry):
        ms, ls, accs = carry
        off = pl.multiple_of(off, tq)
        ss = [_dot_nt(qs[h], k_ref[pl.ds(off, width), MLA_QK_PAD * h:MLA_QK_PAD * (h + 1)]) for h in heads]
        if masked:
            row = i * tq + lax.broadcasted_iota(jnp.int32, (tq, width), 0)
            col = off + lax.broadcasted_iota(jnp.int32, (tq, width), 1)
            ss = [jnp.where(col <= row, s, -jnp.inf) for s in ss]
        m_new = [jnp.maximum(ms[h], jnp.max(ss[h], axis=-1, keepdims=True)) for h in heads]
        alpha = [jnp.exp2(ms[h] - m_new[h]) for h in heads]
        ps = [jnp.exp2(ss[h] - m_new[h]) for h in heads]
        ls = [alpha[h] * ls[h] + jnp.sum(ps[h], axis=-1, keepdims=True) for h in heads]
        pv = [_dot(ps[h].astype(BF16), v_ref[pl.ds(off, width), MLA_V * h:MLA_V * (h + 1)]) for h in heads]
        accs = [alpha[h] * accs[h] + pv[h] for h in heads]
        return tuple(m_new), tuple(ls), tuple(accs)

    carry = (tuple(jnp.full((tq, 1), -1e30, F32) for _ in heads),
             tuple(jnp.zeros((tq, 1), F32) for _ in heads),
             tuple(jnp.zeros((tq, MLA_V), F32) for _ in heads))
    n_single = (i + 1) % 2
    n_pairs = (i + 1) // 2
    carry = lax.cond(i == 0, lambda c: step(0, tq, True, c), lambda c: c, carry)
    carry = lax.cond((n_single == 1) & (i > 0), lambda c: step(0, tq, False, c), lambda c: c, carry)
    pair_off = lambda p: (n_single + 2 * p) * tq
    carry = lax.fori_loop(0, n_pairs - 1, lambda p, c: step(pair_off(p), 2 * tq, False, c), carry)
    carry = lax.cond(n_pairs > 0, lambda c: step(pair_off(n_pairs - 1), 2 * tq, True, c), lambda c: c, carry)
    _, ls, accs = carry
    for h in heads:
        o_ref[:, MLA_V * h:MLA_V * (h + 1)] = (accs[h] / ls[h]).astype(o_ref.dtype)


def mla_attention(q, k, v, batch, tq):
    tp = q.shape[0]
    lp = tp // batch
    nq = lp // tq
    hs = ATTN_HEADS_PER_STEP
    return pl.pallas_call(
        functools.partial(_attn_body, tq=tq),
        grid=(batch, MLA_HEADS // hs, nq),
        in_specs=[
            pl.BlockSpec((tq, hs * MLA_QK_PAD), lambda b, h, i: (b * nq + i, h)),
            pl.BlockSpec((lp, hs * MLA_QK_PAD), lambda b, h, i: (b, h)),
            pl.BlockSpec((lp, hs * MLA_V), lambda b, h, i: (b, h)),
        ],
        out_specs=pl.BlockSpec((tq, hs * MLA_V), lambda b, h, i: (b * nq + i, h)),
        out_shape=jax.ShapeDtypeStruct((tp, MLA_WIDTH), BF16),
        compiler_params=_params("parallel", "parallel", "arbitrary"),
        name="mla_attention",
    )(q, k, v)


def _merge_body(ya_ref, yb_ref, yc_ref, wa_ref, wb_ref, wc_ref, ga_ref, gb_ref, gc_ref, o_ref):
    def branch(y_ref, w_ref, g_ref):
        return _sigmoid(g_ref[...].astype(F32)) * _dot(y_ref[...], w_ref[...])

    o_ref[...] = (branch(ya_ref, wa_ref, ga_ref) + branch(yb_ref, wb_ref, gb_ref)
                  + branch(yc_ref, wc_ref, gc_ref)).astype(o_ref.dtype)


def merge_branches(ya, yb, yc, wa, wb, wc, layer, p, tm, tn):
    tp = ya.shape[0]

    def yspec(width):
        return pl.BlockSpec((tm, width), lambda i, j: (i, 0))

    def wspec(width):
        return pl.BlockSpec((None, width, tn), lambda i, j: (layer, 0, j))

    def gspec(branch):
        base = (OFF_GATE + branch * D_MODEL) // tn
        return pl.BlockSpec((tm, tn), lambda i, j: (i, base + j))

    return pl.pallas_call(
        _merge_body,
        grid=(tp // tm, D_MODEL // tn),
        in_specs=[yspec(RW_WIDTH), yspec(RET_WIDTH), yspec(MLA_WIDTH),
                  wspec(RW_WIDTH), wspec(RET_WIDTH), wspec(MLA_WIDTH),
                  gspec(0), gspec(1), gspec(2)],
        out_specs=pl.BlockSpec((tm, tn), lambda i, j: (i, j)),
        out_shape=jax.ShapeDtypeStruct((tp, D_MODEL), BF16),
        compiler_params=_params("parallel", "parallel"),
        name="merge_branches",
    )(ya, yb, yc, wa, wb, wc, p, p, p)


def _resid_body(x_ref, w_ref, h_ref, o_ref, ob_ref, ssq_ref):
    hn = h_ref[...] + _dot(x_ref[...], w_ref[...])
    o_ref[...] = hn
    ob_ref[...] = hn.astype(ob_ref.dtype)
    ssq_ref[...] = _ssq_block(hn)


def resid_matmul(x, w, layer, h, tm, tn):
    m, kdim = x.shape
    n = w.shape[2]
    tile = pl.BlockSpec((tm, tn), lambda i, j: (i, j))
    return pl.pallas_call(
        _resid_body,
        grid=(m // tm, n // tn),
        in_specs=[
            pl.BlockSpec((tm, kdim), lambda i, j: (i, 0)),
            pl.BlockSpec((None, kdim, tn), lambda i, j: (layer, 0, j)),
            tile,
        ],
        out_specs=[tile, tile, pl.BlockSpec((tm, LANE), lambda i, j: (i, j))],
        out_shape=[jax.ShapeDtypeStruct((m, n), F32), jax.ShapeDtypeStruct((m, n), BF16),
                   jax.ShapeDtypeStruct((m, LANE * (n // tn)), F32)],
        compiler_params=_params("parallel", "parallel"),
        name="resid_matmul",
    )(x, w, h)


def _ffn_up_body(x_ref, ssq_ref, wg_ref, wu_ref, o_ref):
    x = x_ref[...]
    r = _row_scale(ssq_ref, x.shape[1])
    hg = r * _dot(x, wg_ref[...])
    hu = r * _dot(x, wu_ref[...])
    o_ref[...] = (hg * _sigmoid(hg) * hu).astype(o_ref.dtype)


def ffn_up(xb, ssq, w_gate_up, layer, tm, tn):
    m, kdim = xb.shape
    hidden = w_gate_up.shape[2] // 2
    nj = hidden // tn
    return pl.pallas_call(
        _ffn_up_body,
        grid=(m // tm, nj),
        in_specs=[
            pl.BlockSpec((tm, kdim), lambda i, j: (i, 0)),
            pl.BlockSpec((tm, ssq.shape[1]), lambda i, j: (i, 0)),
            pl.BlockSpec((None, kdim, tn), lambda i, j: (layer, 0, j)),
            pl.BlockSpec((None, kdim, tn), lambda i, j: (layer, 0, nj + j)),
        ],
        out_specs=pl.BlockSpec((tm, tn), lambda i, j: (i, j)),
        out_shape=jax.ShapeDtypeStruct((m, hidden), BF16),
        compiler_params=_params("parallel", "parallel"),
        name="ffn_up",
    )(xb, ssq, w_gate_up, w_gate_up)


def _pad_cols(w, width):
    return jnp.pad(w, [(0, 0)] * (w.ndim - 1) + [(0, width - w.shape[-1])])


def _w_in_pieces():
    widths = [3 * RW_WIDTH, RW_DECAY_LORA, RW_A_LORA, RW_GATE_LORA, 4 * RET_WIDTH, MLA_Q_RANK, MLA_KV_RANK, MLA_ROPE,
              3 * D_MODEL]
    dsts = [OFF_RW, OFF_RW_WD, OFF_RW_AD, OFF_RW_GD, OFF_RET, OFF_MLA_Q, OFF_MLA_KV, OFF_MLA_KR, OFF_GATE]
    pieces, src = [], 0
    for dst, width in zip(dsts, widths):
        pieces.append((dst, src, width))
        src += width
    return pieces


def _pack_w_in_body(w_ref, g_ref, lo_ref, hi_ref):
    g = g_ref[...]
    for o_ref, base in ((lo_ref, 0), (hi_ref, P_SPLIT)):
        covered = 0
        for dst, src, width in sorted(_w_in_pieces()):
            dst -= base
            if not 0 <= dst < o_ref.shape[1]:
                continue
            if dst > covered:
                o_ref[:, covered:dst] = jnp.zeros((o_ref.shape[0], dst - covered), o_ref.dtype)
            o_ref[:, dst:dst + width] = (w_ref[:, src:src + width] * g).astype(o_ref.dtype)
            covered = dst + width
        if covered < o_ref.shape[1]:
            o_ref[:, covered:] = jnp.zeros((o_ref.shape[0], o_ref.shape[1] - covered), o_ref.dtype)


def _pack_w_in(w_in, gain, tr=128):
    nl, rows, cols = w_in.shape
    widths = (P_SPLIT, P_COLS - P_SPLIT)
    return pl.pallas_call(
        _pack_w_in_body,
        grid=(nl, rows // tr),
        in_specs=[pl.BlockSpec((None, tr, cols), lambda l, i: (l, i, 0)),
                  pl.BlockSpec((None, tr, 1), lambda l, i: (l, i, 0))],
        out_specs=[pl.BlockSpec((None, tr, n), lambda l, i: (l, i, 0)) for n in widths],
        out_shape=[jax.ShapeDtypeStruct((nl, rows, n), BF16) for n in widths],
        compiler_params=_params("parallel", "parallel"),
        name="pack_w_in",
    )(w_in, gain[..., None])


def _pack_mu(mu):
    rkv = mu[..., :3 * RW_WIDTH]
    wd = mu[..., 3 * RW_WIDTH:3 * RW_WIDTH + RW_DECAY_LORA]
    ad = mu[..., 3 * RW_WIDTH + RW_DECAY_LORA:3 * RW_WIDTH + RW_DECAY_LORA + RW_A_LORA]
    gd = mu[..., 3 * RW_WIDTH + RW_DECAY_LORA + RW_A_LORA:]
    return jnp.concatenate([rkv, _pad_cols(wd, LANE), _pad_cols(ad, LANE), gd], axis=-1)


def _pad_rows(w, rows):
    return jnp.pad(w, [(0, 0)] * (w.ndim - 2) + [(0, rows - w.shape[-2]), (0, 0)])


def _pack_w_uq(w):
    nl, rank, _ = w.shape
    w = w.reshape(nl, rank, MLA_HEADS, MLA_NOPE + MLA_ROPE)
    w = jnp.pad(w, ((0, 0), (0, 0), (0, 0), (0, MLA_QK_PAD - MLA_NOPE - MLA_ROPE)))
    return w.reshape(nl, rank, MLA_HEADS * MLA_QK_PAD).astype(BF16)


def _pack_w_ukv(w):
    nl, rank, _ = w.shape
    w = w.reshape(nl, rank, MLA_HEADS, 2, MLA_NOPE)
    w = jnp.swapaxes(w, 2, 3)
    return w.reshape(nl, rank, 2 * MLA_HEADS * MLA_NOPE).astype(BF16)


def _rope_tables(lp):
    pos = jnp.arange(lp, dtype=F32)

    def tables(dim):
        inv = ROPE_BASE ** (-jnp.arange(0, dim, 2, dtype=F32) / dim)
        ang = pos[:, None] * inv[None, :]
        return jnp.cos(ang), jnp.sin(ang)

    c, s = tables(RET_HEAD_DIM)
    ret = (jnp.concatenate([c, c], axis=1), jnp.concatenate([-s, s], axis=1))
    c, s = tables(MLA_ROPE)
    z = jnp.zeros((lp, LANE - MLA_ROPE), F32)
    mla = (jnp.concatenate([c, c, z], axis=1), jnp.concatenate([-s, s, z], axis=1))
    return ret, mla


def kernel(x, meta_tokens, norm_mix, w_in, rw_mu, rw_w0, rw_w_up, rw_a0, rw_a_up, rw_g_up, rw_k_k, rw_k_a, rw_r_k, rw_ln_w, rw_ln_b, mla_norm_q, mla_norm_kv, mla_w_uq, mla_w_ukv, w_br_rwkv, w_br_ret, w_br_mla, w_out, norm_ffn, w_gate_up, w_down, final_norm):
    batch, seq, d = x.shape
    depth = w_in.shape[0]
    lp = -(-(N_META + seq) // SEQ_ALIGN) * SEQ_ALIGN
    tp = batch * lp

    meta = jnp.broadcast_to(meta_tokens[None].astype(x.dtype), (batch, N_META, d))
    pad = jnp.zeros((batch, lp - N_META - seq, d), x.dtype)
    h = jnp.concatenate([meta, x, pad], axis=1).reshape(tp, d)

    wp_lo, wp_hi = _pack_w_in(w_in, norm_mix)
    mu = _pack_mu(rw_mu)
    w_up = _pad_rows(rw_w_up, LANE).astype(BF16)
    a_up = _pad_rows(rw_a_up, LANE).astype(BF16)
    g_up = rw_g_up.astype(BF16)
    wuq = _pack_w_uq(mla_w_uq)
    wukv = _pack_w_ukv(mla_w_ukv)
    wa = w_br_rwkv.astype(BF16)
    wb = w_br_ret.astype(BF16)
    wc = w_br_mla.astype(BF16)
    wo = w_out.astype(BF16)
    wgu = (w_gate_up * norm_ffn[..., None]).astype(BF16)
    wdn = w_down.astype(BF16)
    (cos_ret, sin_ret), (cos_mla, sin_mla) = _rope_tables(lp)

    def row_tile(pref):
        return next((t for t in pref if tp % t == 0), SEQ_ALIGN)

    tm = row_tile((768,))
    tm_wide = row_tile((1536, 768))
    tm_seq = SEQ_ALIGN

    hb, ssq = stream_prep(h, tm)
    for l in range(depth):
        p_lo = norm_matmul(hb, ssq, wp_lo, l, tm_wide, 2 * P_TILE_N, BF16)
        p = norm_matmul(hb, ssq, wp_hi, l, tm, 3 * P_TILE_N, F32)
        ya = rwkv_mix(p, batch, mu[l:l + 1], rw_w0[l], rw_a0[l], rw_k_k[l], rw_k_a[l], rw_r_k[l],
                      rw_ln_w[l], rw_ln_b[l], w_up[l], a_up[l], g_up[l])
        yb = retention_mix(p_lo, batch, cos_ret, sin_ret)
        q, k, v = mla_proj(p, batch, mla_norm_q[l], mla_norm_kv[l], wuq[l], wukv[l], cos_mla, sin_mla, tm_seq)
        yc = mla_attention(q, k, v, batch, tm_seq)
        merged = merge_branches(ya, yb, yc, wa, wb, wc, l, p_lo, tm, 1024)
        h, hb, ssq = resid_matmul(merged, wo, l, h, tm_seq, d)
        act = ffn_up(hb, ssq, wgu, l, tm_wide, 512)
        h, hb, ssq = resid_matmul(act, wdn, l, h, tm, 512)
    tm_out = next(t for t in (512, 256, LANE) if seq % t == 0)
    return final_rmsnorm(h, final_norm, batch, N_META, seq, tm_out).reshape(batch, seq, d)
```

```python
import functools
import math

import jax
import jax.numpy as jnp
from jax import lax
from jax.experimental import pallas as pl
from jax.experimental.pallas import tpu as pltpu

F32 = jnp.float32
BF16 = jnp.bfloat16

D_MODEL = 2048
N_META = 16
NORM_EPS = 1e-6
ROPE_BASE = 10000.0

RW_HEADS = 16
RW_HEAD_DIM = 64
RW_WIDTH = RW_HEADS * RW_HEAD_DIM
RW_DECAY_LORA = 96
RW_A_LORA = 96
RW_GATE_LORA = 256
RW_GN_EPS = RW_HEAD_DIM * 1e-5
RW_CHUNK = 64
RW_SUB = 3
RW_PAIRS = RW_WIDTH // 128

RET_HEADS = 8
RET_HEAD_DIM = 128
RET_WIDTH = RET_HEADS * RET_HEAD_DIM
RET_CHUNK = 128
RET_SUB = 3

MLA_HEADS = 8
MLA_NOPE = 128
MLA_ROPE = 64
MLA_V = 128
MLA_Q_RANK = 512
MLA_KV_RANK = 256
MLA_WIDTH = MLA_HEADS * MLA_V
MLA_QK_PAD = 256
ATTN_HEADS_PER_STEP = 2

FFN_HIDDEN = -(-8 * D_MODEL // (3 * 256)) * 256

LANE = 128
SEQ_ALIGN = 384
assert SEQ_ALIGN % (RW_CHUNK * RW_SUB) == 0 and SEQ_ALIGN % (RET_CHUNK * RET_SUB) == 0 and SEQ_ALIGN % LANE == 0

OFF_RET = 0
OFF_GATE = OFF_RET + 4 * RET_WIDTH
OFF_RW = OFF_GATE + 3 * D_MODEL
OFF_RW_WD = OFF_RW + 3 * RW_WIDTH
OFF_RW_AD = OFF_RW_WD + LANE
OFF_RW_GD = OFF_RW_AD + LANE
OFF_MLA_Q = OFF_RW_GD + RW_GATE_LORA
OFF_MLA_KV = OFF_MLA_Q + MLA_Q_RANK
OFF_MLA_KR = OFF_MLA_KV + MLA_KV_RANK
P_COLS_USED = OFF_MLA_KR + LANE
P_TILE_N = 512
P_COLS = -(-P_COLS_USED // P_TILE_N) * P_TILE_N
P_SPLIT = OFF_RW
assert P_SPLIT % (2 * P_TILE_N) == 0
assert (P_COLS - P_SPLIT) % (3 * P_TILE_N) == 0

VMEM_LIMIT = 48 * 1024 * 1024


def _params(*sem):
    return pltpu.CompilerParams(dimension_semantics=sem, vmem_limit_bytes=VMEM_LIMIT)


def _sigmoid(x):
    return 1.0 / (1.0 + jnp.exp(-x))


def _dot(a, b):
    return jnp.dot(a, b, preferred_element_type=F32)


def _dot_nt(a, b):
    return lax.dot_general(a, b, (((1,), (1,)), ((), ())), preferred_element_type=F32)


def _dot_tn(a, b):
    return lax.dot_general(a, b, (((0,), (0,)), ((), ())), preferred_element_type=F32)


def _rmsnorm_body(x_ref, g_ref, o_ref):
    x = x_ref[...]
    y = x * lax.rsqrt(jnp.mean(x * x, axis=-1, keepdims=True) + NORM_EPS)
    o_ref[...] = (y * g_ref[...]).astype(o_ref.dtype)


def final_rmsnorm(x, g, batch, first, count, tm):
    m, d = x.shape
    rows_per_batch = m // batch
    assert rows_per_batch % 8 == 0 and first % 8 == 0 and tm % 8 == 0
    tiles = count // tm
    return pl.pallas_call(
        _rmsnorm_body,
        grid=(batch, tiles),
        in_specs=[pl.BlockSpec((pl.Element(tm), pl.Element(d)),
                               lambda b, t: (pl.multiple_of(b * rows_per_batch + first + t * tm, 8), 0)),
                  pl.BlockSpec((1, d), lambda b, t: (0, 0))],
        out_specs=pl.BlockSpec((tm, d), lambda b, t: (b * tiles + t, 0)),
        out_shape=jax.ShapeDtypeStruct((batch * count, d), x.dtype),
        compiler_params=_params("parallel", "parallel"),
        name="final_rmsnorm",
    )(x, g.reshape(1, d))


def _row_scale(ssq_ref, d):
    ssq = ssq_ref[...]
    total = ssq[:, 0:1]
    for j in range(1, ssq.shape[1] // LANE):
        total = total + ssq[:, LANE * j:LANE * j + 1]
    return lax.rsqrt(total * (1.0 / d) + NORM_EPS)


def _ssq_block(x):
    return jnp.broadcast_to(jnp.sum(x * x, axis=-1, keepdims=True), (x.shape[0], LANE))


def _stream_init_body(x_ref, meta_ref, h_ref, hb_ref, ssq_ref, *, tiles, pad_rows):
    j = pl.program_id(1)
    tm, d = h_ref.shape

    def emit(rows):
        h_ref[...] = rows
        hb_ref[...] = rows.astype(hb_ref.dtype)
        ssq_ref[...] = _ssq_block(rows)

    @pl.when(j == 0)
    def _():
        emit(jnp.concatenate([meta_ref[...], x_ref[0:tm - N_META, :]], axis=0))

    @pl.when((j > 0) & (j < tiles - 1))
    def _():
        emit(x_ref[...])

    @pl.when(j == tiles - 1)
    def _():
        emit(jnp.concatenate([x_ref[pad_rows:, :], jnp.zeros((pad_rows, d), F32)], axis=0))


def stream_init(x, meta_tokens, lp, tm):
    batch, seq, d = x.shape
    tiles = lp // tm
    pad_rows = lp - N_META - seq
    assert tiles >= 2 and seq >= tm and 0 <= pad_rows < tm and pad_rows % 8 == 0 and N_META % 8 == 0 and seq % 8 == 0
    tile = pl.BlockSpec((tm, d), lambda b, j: (b * tiles + j, 0))
    return pl.pallas_call(
        functools.partial(_stream_init_body, tiles=tiles, pad_rows=pad_rows),
        grid=(batch, tiles),
        in_specs=[pl.BlockSpec((pl.Element(tm), pl.Element(d)),
                               lambda b, j: (pl.multiple_of(b * seq + jnp.clip(j * tm - N_META, 0, seq - tm), 8), 0)),
                  pl.BlockSpec((N_META, d), lambda b, j: (0, 0))],
        out_specs=[tile, tile, pl.BlockSpec((tm, LANE), lambda b, j: (b * tiles + j, 0))],
        out_shape=[jax.ShapeDtypeStruct((batch * lp, d), F32), jax.ShapeDtypeStruct((batch * lp, d), BF16),
                   jax.ShapeDtypeStruct((batch * lp, LANE), F32)],
        compiler_params=_params("parallel", "parallel"),
        name="stream_init",
    )(x.reshape(batch * seq, d), meta_tokens.astype(x.dtype))


def _norm_matmul_body(x_ref, ssq_ref, w_ref, o_ref):
    o_ref[...] = (_row_scale(ssq_ref, x_ref.shape[1]) * _dot(x_ref[...], w_ref[...])).astype(o_ref.dtype)


def norm_matmul(xb, ssq, w, layer, tm, tn, out_dtype):
    m, k = xb.shape
    n = w.shape[2]
    assert n % tn == 0 and m % tm == 0
    return pl.pallas_call(
        _norm_matmul_body,
        grid=(m // tm, n // tn),
        in_specs=[pl.BlockSpec((tm, k), lambda i, j: (i, 0)), pl.BlockSpec((tm, ssq.shape[1]), lambda i, j: (i, 0)),
                  pl.BlockSpec((None, k, tn), lambda i, j: (layer, 0, j))],
        out_specs=pl.BlockSpec((tm, tn), lambda i, j: (i, j)),
        out_shape=jax.ShapeDtypeStruct((m, n), out_dtype),
        compiler_params=_params("parallel", "parallel"),
        name="in_proj",
    )(xb, ssq, w)


def _rwkv_body(r_ref, k_ref, v_ref, wd_ref, ad_ref, gd_ref, mu_ref,
               w0_ref, a0_ref, kk_ref, ka_ref, rk_ref, lnw_ref, lnb_ref,
               wup_ref, aup_ref, gup_ref,
               o_ref,
               s_ref, pr_ref, pk_ref, pv_ref, pwd_ref, pad_ref, pgd_ref):
    c = pl.program_id(1)
    C = RW_CHUNK
    RS = RW_SUB * C
    HD = RW_HEAD_DIM

    @pl.when(c == 0)
    def _():
        s_ref[...] = jnp.zeros_like(s_ref)
        pr_ref[...] = jnp.zeros_like(pr_ref)
        pk_ref[...] = jnp.zeros_like(pk_ref)
        pv_ref[...] = jnp.zeros_like(pv_ref)
        pwd_ref[...] = jnp.zeros_like(pwd_ref)
        pad_ref[...] = jnp.zeros_like(pad_ref)
        pgd_ref[...] = jnp.zeros_like(pgd_ref)

    def shift(x_ref, prev_ref, mu, sl):
        z = x_ref[:, sl]
        first = lax.broadcasted_iota(jnp.int32, z.shape, 0) == 0
        zs = jnp.where(first, prev_ref[0:1, sl], pltpu.roll(z, 1, 0))
        prev_ref[0:1, sl] = z[RS - 1:RS, :]
        return z + (zs - z) * mu

    mu_lora = 3 * RW_WIDTH
    full = slice(None)
    wd = shift(wd_ref, pwd_ref, mu_ref[:, mu_lora:mu_lora + LANE], full)
    ad = shift(ad_ref, pad_ref, mu_ref[:, mu_lora + LANE:mu_lora + 2 * LANE], full)
    gd = shift(gd_ref, pgd_ref, mu_ref[:, mu_lora + 2 * LANE:], full)
    subs = range(RW_SUB)

    def rows(t, s):
        return t[C * s:C * (s + 1)]

    tanh_wd = [rows(jnp.tanh(wd), s).astype(BF16) for s in subs]
    ad_b = [rows(ad, s).astype(BF16) for s in subs]
    sig_gd = [rows(_sigmoid(gd), s).astype(BF16) for s in subs]

    lane_sq = lax.broadcasted_iota(jnp.int32, (LANE, LANE), 1)
    row_sq = lax.broadcasted_iota(jnp.int32, (LANE, LANE), 0)
    same_head = (lane_sq < HD) == (row_sq < HD)
    head_ones = same_head.astype(BF16)
    eye = (lane_sq == row_sq).astype(F32)
    rc = lax.broadcasted_iota(jnp.int32, (C, C), 0)
    cc = lax.broadcasted_iota(jnp.int32, (C, C), 1)
    tril_incl = (cc <= rc).astype(BF16)
    lane_tall = lax.broadcasted_iota(jnp.int32, (2 * C, LANE), 1)
    lane_c = lax.broadcasted_iota(jnp.int32, (C, LANE), 1)
    row_c = lax.broadcasted_iota(jnp.int32, (C, LANE), 0)
    head0_c = lane_c < HD
    strict_lo = lane_c < row_c
    strict_hi = (lane_c >= C) & (lane_c - C < row_c)
    incl_lo = lane_c <= row_c
    incl_hi = (lane_c >= C) & (lane_c - C <= row_c)
    inv_n = 1.0 / HD

    P = range(RW_PAIRS)

    def head_sums(ts):
        out = _dot(jnp.concatenate([t.astype(BF16) for t in ts], axis=0), head_ones)
        return [out[C * i:C * (i + 1)] for i in range(len(ts))]

    pair_sls = [slice(LANE * i, LANE * (i + 1)) for i in P]

    def each(fn, *lists):
        return [fn(*args) for args in zip(*lists)]

    def items(per_pair):
        return [rows(t, s) for s in subs for t in per_pair]

    r = items([shift(r_ref, pr_ref, mu_ref[:, sl], sl) for sl in pair_sls])
    k = items([shift(k_ref, pk_ref, mu_ref[:, RW_WIDTH + sl.start:RW_WIDTH + sl.stop], sl) for sl in pair_sls])
    v = items([shift(v_ref, pv_ref, mu_ref[:, 2 * RW_WIDTH + sl.start:2 * RW_WIDTH + sl.stop], sl)
               for sl in pair_sls])
    sls = pair_sls * RW_SUB
    sub_of = [s for s in subs for _ in P]

    def log_decay(s, sl):
        x = -(w0_ref[:, sl] + _dot(tanh_wd[s], wup_ref[:, sl]))
        softplus = jnp.maximum(x, 0.0) + jnp.log1p(jnp.exp(-jnp.abs(x)))
        return -jnp.exp(-softplus - 0.5)

    lw = each(log_decay, sub_of, sls)
    a = each(lambda s, sl: _sigmoid(a0_ref[:, sl] + _dot(ad_b[s], aup_ref[:, sl])), sub_of, sls)
    g = each(lambda s, sl: _dot(sig_gd[s], gup_ref[:, sl]), sub_of, sls)

    kkr = each(lambda ki, sl: ki * kk_ref[:, sl], k, sls)
    ksq = head_sums(each(lambda t: t * t, kkr))
    kkn = each(lambda t, ss: t / jnp.maximum(jnp.sqrt(ss), 1e-12), kkr, ksq)
    kmod = each(lambda ki, ai, sl: ki * (1.0 + (ai - 1.0) * ka_ref[:, sl]), k, a, sls)
    beta = each(lambda ai, t: ai * t, a, kkn)

    def running_sum(lwi):
        hi = lwi.astype(BF16)
        both = _dot(tril_incl, jnp.concatenate([hi, (lwi - hi.astype(F32)).astype(BF16)], axis=1))
        return both[:, :LANE] + both[:, LANE:]

    lcum = each(running_sum, lw)
    lend = each(lambda t: t[C - 1:C, :], lcum)
    rh = each(lambda ri, lc: ri * jnp.exp(lc), r, lcum)
    kh = each(lambda t, lc, lwi: t * jnp.exp(lc - lwi), kkn, lcum, lw)
    e_neg = each(lambda lc: jnp.exp(-lc), lcum)
    e_end = each(lambda le, lc: jnp.exp(le - lc), lend, lcum)
    kb = each(lambda t, e: t * e, kmod, e_neg)
    bb = each(lambda t, e: t * e, beta, e_neg)
    kbe = each(lambda t, e: t * e, kmod, e_end)
    bbe = each(lambda t, e: t * e, beta, e_end)

    kr_f = each(lambda x1, x2: jnp.concatenate([x1, x2], axis=0), kh, rh)
    bk = each(lambda x1, x2: jnp.concatenate([x1, x2], axis=0).astype(BF16), bb, kb)
    vb = each(lambda t: t.astype(BF16), v)

    def gram(krf, bki):
        kr2 = jnp.concatenate([jnp.where(lane_tall < HD, krf, 0.0), jnp.where(lane_tall >= HD, krf, 0.0)], axis=0)
        return _dot_nt(kr2.astype(BF16), bki)

    g_all = each(gram, kr_f, bk)

    n_bd = each(lambda ga: jnp.concatenate([jnp.where(strict_lo, -ga[0:C], 0.0),
                                            jnp.where(strict_hi, -pltpu.roll(ga[2 * C:3 * C], C, 1), 0.0)], axis=0),
                g_all)
    t = each(lambda n: eye + n, n_bd)
    pw = each(lambda n: _dot(n.astype(BF16), n.astype(BF16)), n_bd)
    for _ in range(4):
        both = each(lambda ti, pi: _dot(jnp.concatenate([ti, pi], axis=0).astype(BF16), pi.astype(BF16)), t, pw)
        t = each(lambda ti, bi: ti + bi[:LANE], t, both)
        pw = each(lambda bi: bi[LANE:], both)
    t = each(lambda ti, pi: ti + _dot(ti.astype(BF16), pi.astype(BF16)), t, pw)

    def intra_rhs(ga, vbi):
        m1s = jnp.concatenate([jnp.where(strict_hi, ga[0:C], 0.0), jnp.where(strict_hi, ga[2 * C:3 * C], 0.0)], axis=0)
        return _dot(m1s.astype(BF16), jnp.concatenate([vbi, vbi], axis=0))

    q_intra = each(intra_rhs, g_all, vb)

    def m2(gb):
        return jnp.where(incl_lo, -gb, jnp.where(incl_hi, gb, 0.0))

    m2s = each(lambda ga: jnp.concatenate([m2(ga[C:2 * C]), m2(ga[3 * C:4 * C])], axis=0).astype(BF16), g_all)
    kbe_all = each(lambda kbei, bbei: jnp.concatenate([kbei, -bbei], axis=0).astype(BF16), kbe, bbe)
    s_decay = each(jnp.exp, lend)

    state = [s_ref[i] for i in P]
    y = []
    for s in subs:
        of = lambda lst: lst[RW_PAIRS * s:RW_PAIRS * (s + 1)]
        p_all = each(lambda x1, si: _dot_nt(x1.astype(BF16), si.astype(BF16)), of(kr_f), state)
        q_s = each(lambda qi, pa: jnp.where(same_head, qi + jnp.concatenate([pa[:C], pa[:C]], axis=0), 0.0),
                   of(q_intra), p_all)
        u_s = each(lambda ti, qi: _dot(ti.astype(BF16), qi.astype(BF16)), of(t), q_s)
        u = each(lambda us: us[:C] + us[C:], u_s)
        y_s = each(lambda mi, ui, vi: _dot(mi, jnp.concatenate([ui, vi], axis=0).astype(BF16)), of(m2s), u, of(v))
        y += each(lambda pa, ys: pa[C:] + jnp.where(head0_c, ys[:C], ys[C:]), p_all, y_s)
        ds = each(lambda vi, ui, kb_all: _dot_tn(jnp.concatenate([vi, ui], axis=0).astype(BF16), kb_all),
                  of(v), u, of(kbe_all))
        state = each(lambda si, di, dec: si * dec + jnp.where(same_head, di, 0.0), state, ds, of(s_decay))
    for i in P:
        s_ref[i] = state[i]

    d = each(lambda yi, si: yi - si * inv_n, y, head_sums(y))
    var = each(lambda si: si * inv_n, head_sums(each(lambda di: di * di, d)))
    bsum = head_sums(each(lambda ri, ki, sl: ri * ki * rk_ref[:, sl], r, kmod, sls))
    for i, (s, sl) in enumerate(zip(sub_of, sls)):
        yn = d[i] * lax.rsqrt(var[i] + RW_GN_EPS) * lnw_ref[:, sl] + lnb_ref[:, sl]
        o_ref[C * s:C * (s + 1), sl] = ((yn + bsum[i] * v[i]) * g[i]).astype(o_ref.dtype)


def rwkv_mix(p, batch, mu, w0, a0, k_k, k_a, r_k, ln_w, ln_b, w_up, a_up, g_up):
    tp = p.shape[0]
    lp = tp // batch
    C = RW_CHUNK * RW_SUB
    nchunk = lp // C

    def pspec(width, base):
        return pl.BlockSpec((C, width), lambda b, c: (b * nchunk + c, base // width))

    def const(shape):
        return pl.BlockSpec(shape, lambda b, c: (0, 0))

    in_specs = [
        pspec(RW_WIDTH, OFF_RW - P_SPLIT), pspec(RW_WIDTH, OFF_RW - P_SPLIT + RW_WIDTH),
        pspec(RW_WIDTH, OFF_RW - P_SPLIT + 2 * RW_WIDTH),
        pspec(LANE, OFF_RW_WD - P_SPLIT), pspec(LANE, OFF_RW_AD - P_SPLIT), pspec(RW_GATE_LORA, OFF_RW_GD - P_SPLIT),
        const(mu.shape),
    ] + [const((1, RW_WIDTH))] * 7 + [const(w_up.shape), const(a_up.shape), const(g_up.shape)]
    row = lambda t: t.reshape(1, -1)
    return pl.pallas_call(
        _rwkv_body,
        grid=(batch, nchunk),
        in_specs=in_specs,
        out_specs=pl.BlockSpec((C, RW_WIDTH), lambda b, c: (b * nchunk + c, 0)),
        out_shape=jax.ShapeDtypeStruct((tp, RW_WIDTH), BF16),
        scratch_shapes=[pltpu.VMEM((RW_PAIRS, LANE, LANE), F32)] + [pltpu.VMEM((8, RW_WIDTH), F32)] * 3
        + [pltpu.VMEM((8, LANE), F32)] * 2 + [pltpu.VMEM((8, RW_GATE_LORA), F32)],
        compiler_params=_params("parallel", "arbitrary"),
        name="rwkv7_mix",
    )(p, p, p, p, p, p, mu, row(w0), row(a0), row(k_k), row(k_a), row(r_k), row(ln_w), row(ln_b),
      w_up, a_up, g_up)


def _ret_body(q_ref, k_ref, v_ref, g_ref, cos_ref, sin_ref, o_ref, state_ref):
    c = pl.program_id(1)
    C = RET_CHUNK
    d = RET_HEAD_DIM

    @pl.when(c == 0)
    def _():
        state_ref[...] = jnp.zeros_like(state_ref)

    row = lax.broadcasted_iota(jnp.int32, (C, C), 0).astype(F32)
    col = lax.broadcasted_iota(jnp.int32, (C, C), 1).astype(F32)
    diff = row - col
    causal = diff >= 0
    heads = range(RET_HEADS)
    subs = range(RET_SUB)
    rws = [slice(C * s, C * (s + 1)) for s in subs for _ in heads]
    sls = [slice(d * h, d * (h + 1)) for _ in subs for h in heads]
    lgs = [math.log1p(-(2.0 ** (-5.0 - h))) for _ in subs for h in heads]

    def each(fn, *lists):
        return [fn(*args) for args in zip(*lists)]

    def rope(x_ref, rw, sl):
        x = x_ref[rw, sl].astype(F32)
        return x * cos_ref[rw, :] + pltpu.roll(x, d // 2, 1) * sin_ref[rw, :]

    qb = each(lambda rw, sl: rope(q_ref, rw, sl).astype(BF16), rws, sls)
    k = each(lambda rw, sl: rope(k_ref, rw, sl) * (d ** -0.5), rws, sls)
    kb = each(lambda t: t.astype(BF16), k)
    vb = each(lambda rw, sl: v_ref[rw, sl], rws, sls)
    s = each(lambda qi, ki, lg: _dot_nt(qi, ki) * jnp.where(causal, jnp.exp(lg * jnp.maximum(diff, 0.0)), 0.0),
             qb, kb, lgs)
    o_intra = each(lambda si, vi: _dot(si.astype(BF16), vi), s, vb)
    kd = each(lambda ki, lg: (ki * jnp.exp(lg * (C - 1.0 - row))).astype(BF16), k, lgs)
    kv = each(_dot_tn, kd, vb)

    state = [state_ref[h] for h in heads]
    o = []
    for sub in subs:
        of = lambda lst: lst[RET_HEADS * sub:RET_HEADS * (sub + 1)]
        o += each(lambda oi, qi, st, lg: oi + _dot(qi, st.astype(BF16)) * jnp.exp(lg * (row + 1.0)),
                  of(o_intra), of(qb), state, of(lgs))
        state = each(lambda st, kvi, lg: st * math.exp(lg * C) + kvi, state, of(kv), of(lgs))
    for h in heads:
        state_ref[h] = state[h]
    o = each(lambda oi: oi * lax.rsqrt(jnp.mean(oi * oi, axis=-1, keepdims=True) + NORM_EPS), o)
    for oi, rw, sl in zip(o, rws, sls):
        g = g_ref[rw, sl].astype(F32)
        o_ref[rw, sl] = (g * _sigmoid(g) * oi).astype(o_ref.dtype)


def retention_mix(p, batch, cos, sin):
    tp = p.shape[0]
    lp = tp // batch
    C = RET_CHUNK * RET_SUB
    nchunk = lp // C
    base = OFF_RET // RET_WIDTH

    def pspec(j):
        return pl.BlockSpec((C, RET_WIDTH), lambda b, c: (b * nchunk + c, base + j))

    tab = pl.BlockSpec((C, RET_HEAD_DIM), lambda b, c: (c, 0))
    return pl.pallas_call(
        _ret_body,
        grid=(batch, nchunk),
        in_specs=[pspec(0), pspec(1), pspec(2), pspec(3), tab, tab],
        out_specs=pl.BlockSpec((C, RET_WIDTH), lambda b, c: (b * nchunk + c, 0)),
        out_shape=jax.ShapeDtypeStruct((tp, RET_WIDTH), BF16),
        scratch_shapes=[pltpu.VMEM((RET_HEADS, RET_HEAD_DIM, RET_HEAD_DIM), F32)],
        compiler_params=_params("parallel", "arbitrary"),
        name="retention_mix",
    )(p, p, p, p, cos, sin)


def _mla_proj_body(qd_ref, kvd_ref, krd_ref, nq_ref, nkv_ref, wuq_ref, wukv_ref, cos_ref, sin_ref,
                   q_out, k_out, v_out):
    cos = cos_ref[...]
    sin = sin_ref[...]

    def rope(x):
        return x * cos + (pltpu.roll(x, MLA_ROPE // 2, 1) + pltpu.roll(x, LANE - MLA_ROPE // 2, 1)) * sin

    def norm(x, g):
        return x * lax.rsqrt(jnp.mean(x * x, axis=-1, keepdims=True) + NORM_EPS) * g

    scale = (MLA_NOPE + MLA_ROPE) ** -0.5 * math.log2(math.e)
    q = _dot(norm(qd_ref[...], nq_ref[...]).astype(BF16), wuq_ref[...]) * scale
    kv = _dot(norm(kvd_ref[...], nkv_ref[...]).astype(BF16), wukv_ref[...])
    kr = rope(krd_ref[...]).astype(k_out.dtype)
    for h in range(MLA_HEADS):
        lo = MLA_QK_PAD * h
        q_out[:, lo:lo + LANE] = q[:, lo:lo + LANE].astype(q_out.dtype)
        q_out[:, lo + LANE:lo + 2 * LANE] = rope(q[:, lo + LANE:lo + 2 * LANE]).astype(q_out.dtype)
        k_out[:, lo:lo + LANE] = kv[:, MLA_NOPE * h:MLA_NOPE * (h + 1)].astype(k_out.dtype)
        k_out[:, lo + LANE:lo + 2 * LANE] = kr
    v_out[...] = kv[:, MLA_HEADS * MLA_NOPE:].astype(v_out.dtype)


def mla_proj(p, batch, norm_q, norm_kv, w_uq, w_ukv, cos, sin, tm):
    tp = p.shape[0]
    lp = tp // batch
    per_seq = lp // tm
    qk_w = MLA_HEADS * MLA_QK_PAD
    const = lambda i: (0, 0)
    return pl.pallas_call(
        _mla_proj_body,
        grid=(tp // tm,),
        in_specs=[
            pl.BlockSpec((tm, MLA_Q_RANK), lambda i: (i, (OFF_MLA_Q - P_SPLIT) // MLA_Q_RANK)),
            pl.BlockSpec((tm, MLA_KV_RANK), lambda i: (i, (OFF_MLA_KV - P_SPLIT) // MLA_KV_RANK)),
            pl.BlockSpec((tm, LANE), lambda i: (i, (OFF_MLA_KR - P_SPLIT) // LANE)),
            pl.BlockSpec((1, MLA_Q_RANK), const),
            pl.BlockSpec((1, MLA_KV_RANK), const),
            pl.BlockSpec((MLA_Q_RANK, qk_w), const),
            pl.BlockSpec((MLA_KV_RANK, MLA_HEADS * (MLA_NOPE + MLA_V)), const),
            pl.BlockSpec((tm, LANE), lambda i: (i % per_seq, 0)),
            pl.BlockSpec((tm, LANE), lambda i: (i % per_seq, 0)),
        ],
        out_specs=[
            pl.BlockSpec((tm, qk_w), lambda i: (i, 0)),
            pl.BlockSpec((tm, qk_w), lambda i: (i, 0)),
            pl.BlockSpec((tm, MLA_WIDTH), lambda i: (i, 0)),
        ],
        out_shape=[
            jax.ShapeDtypeStruct((tp, qk_w), BF16),
            jax.ShapeDtypeStruct((tp, qk_w), BF16),
            jax.ShapeDtypeStruct((tp, MLA_WIDTH), BF16),
        ],
        compiler_params=_params("parallel"),
        name="mla_proj",
    )(p, p, p, norm_q.reshape(1, -1), norm_kv.reshape(1, -1), w_uq, w_ukv, cos, sin)


def _attn_body(q_ref, k_ref, v_ref, o_ref, *, tq):
    i = pl.program_id(2)
    heads = range(ATTN_HEADS_PER_STEP)
    qs = [q_ref[:, MLA_QK_PAD * h:MLA_QK_PAD * (h + 1)] for h in heads]

    def step(off, width, masked, carry):
        ms, ls, accs = carry
        off = pl.multiple_of(off, tq)
        ss = [_dot_nt(qs[h], k_ref[pl.ds(off, width), MLA_QK_PAD * h:MLA_QK_PAD * (h + 1)]) for h in heads]
        if masked:
            row = i * tq + lax.broadcasted_iota(jnp.int32, (tq, width), 0)
            col = off + lax.broadcasted_iota(jnp.int32, (tq, width), 1)
            ss = [jnp.where(col <= row, s, -jnp.inf) for s in ss]
        m_new = [jnp.maximum(ms[h], jnp.max(ss[h], axis=-1, keepdims=True)) for h in heads]
        alpha = [jnp.exp2(ms[h] - m_new[h]) for h in heads]
        ps = [jnp.exp2(ss[h] - m_new[h]) for h in heads]
        ls = [alpha[h] * ls[h] + jnp.sum(ps[h], axis=-1, keepdims=True) for h in heads]
        pv = [_dot(ps[h].astype(BF16), v_ref[pl.ds(off, width), MLA_V * h:MLA_V * (h + 1)]) for h in heads]
        accs = [alpha[h] * accs[h] + pv[h] for h in heads]
        return tuple(m_new), tuple(ls), tuple(accs)

    carry = (tuple(jnp.full((tq, 1), -1e30, F32) for _ in heads),
             tuple(jnp.zeros((tq, 1), F32) for _ in heads),
             tuple(jnp.zeros((tq, MLA_V), F32) for _ in heads))
    n_single = (i + 1) % 2
    n_pairs = (i + 1) // 2
    carry = lax.cond(i == 0, lambda c: step(0, tq, True, c), lambda c: c, carry)
    carry = lax.cond((n_single == 1) & (i > 0), lambda c: step(0, tq, False, c), lambda c: c, carry)
    pair_off = lambda p: (n_single + 2 * p) * tq
    carry = lax.fori_loop(0, n_pairs - 1, lambda p, c: step(pair_off(p), 2 * tq, False, c), carry)
    carry = lax.cond(n_pairs > 0, lambda c: step(pair_off(n_pairs - 1), 2 * tq, True, c), lambda c: c, carry)
    _, ls, accs = carry
    for h in heads:
        o_ref[:, MLA_V * h:MLA_V * (h + 1)] = (accs[h] / ls[h]).astype(o_ref.dtype)


def mla_attention(q, k, v, batch, tq):
    tp = q.shape[0]
    lp = tp // batch
    nq = lp // tq
    hs = ATTN_HEADS_PER_STEP
    return pl.pallas_call(
        functools.partial(_attn_body, tq=tq),
        grid=(batch, MLA_HEADS // hs, nq),
        in_specs=[
            pl.BlockSpec((tq, hs * MLA_QK_PAD), lambda b, h, i: (b * nq + i, h)),
            pl.BlockSpec((lp, hs * MLA_QK_PAD), lambda b, h, i: (b, h)),
            pl.BlockSpec((lp, hs * MLA_V), lambda b, h, i: (b, h)),
        ],
        out_specs=pl.BlockSpec((tq, hs * MLA_V), lambda b, h, i: (b * nq + i, h)),
        out_shape=jax.ShapeDtypeStruct((tp, MLA_WIDTH), BF16),
        compiler_params=_params("parallel", "parallel", "arbitrary"),
        name="mla_attention",
    )(q, k, v)


def _merge_body(ya_ref, yb_ref, yc_ref, wa_ref, wb_ref, wc_ref, ga_ref, gb_ref, gc_ref, o_ref):
    def branch(y_ref, w_ref, g_ref):
        return _sigmoid(g_ref[...].astype(F32)) * _dot(y_ref[...], w_ref[...])

    o_ref[...] = (branch(ya_ref, wa_ref, ga_ref) + branch(yb_ref, wb_ref, gb_ref)
                  + branch(yc_ref, wc_ref, gc_ref)).astype(o_ref.dtype)


def merge_branches(ya, yb, yc, wa, wb, wc, layer, p, tm, tn):
    tp = ya.shape[0]

    def yspec(width):
        return pl.BlockSpec((tm, width), lambda i, j: (i, 0))

    def wspec(width):
        return pl.BlockSpec((None, width, tn), lambda i, j: (layer, 0, j))

    def gspec(branch):
        base = (OFF_GATE + branch * D_MODEL) // tn
        return pl.BlockSpec((tm, tn), lambda i, j: (i, base + j))

    return pl.pallas_call(
        _merge_body,
        grid=(tp // tm, D_MODEL // tn),
        in_specs=[yspec(RW_WIDTH), yspec(RET_WIDTH), yspec(MLA_WIDTH),
                  wspec(RW_WIDTH), wspec(RET_WIDTH), wspec(MLA_WIDTH),
                  gspec(0), gspec(1), gspec(2)],
        out_specs=pl.BlockSpec((tm, tn), lambda i, j: (i, j)),
        out_shape=jax.ShapeDtypeStruct((tp, D_MODEL), BF16),
        compiler_params=_params("parallel", "parallel"),
        name="merge_branches",
    )(ya, yb, yc, wa, wb, wc, p, p, p)


def _resid_body(x_ref, w_ref, h_ref, o_ref, ob_ref, ssq_ref):
    hn = h_ref[...] + _dot(x_ref[...], w_ref[...])
    o_ref[...] = hn
    ob_ref[...] = hn.astype(ob_ref.dtype)
    ssq_ref[...] = _ssq_block(hn)


def resid_matmul(x, w, layer, h, tm, tn):
    m, kdim = x.shape
    n = w.shape[2]
    tile = pl.BlockSpec((tm, tn), lambda i, j: (i, j))
    return pl.pallas_call(
        _resid_body,
        grid=(m // tm, n // tn),
        in_specs=[
            pl.BlockSpec((tm, kdim), lambda i, j: (i, 0)),
            pl.BlockSpec((None, kdim, tn), lambda i, j: (layer, 0, j)),
            tile,
        ],
        out_specs=[tile, tile, pl.BlockSpec((tm, LANE), lambda i, j: (i, j))],
        out_shape=[jax.ShapeDtypeStruct((m, n), F32), jax.ShapeDtypeStruct((m, n), BF16),
                   jax.ShapeDtypeStruct((m, LANE * (n // tn)), F32)],
        compiler_params=_params("parallel", "parallel"),
        name="resid_matmul",
    )(x, w, h)


def _ffn_up_body(x_ref, ssq_ref, wg_ref, wu_ref, o_ref):
    x = x_ref[...]
    r = _row_scale(ssq_ref, x.shape[1])
    hg = r * _dot(x, wg_ref[...])
    hu = r * _dot(x, wu_ref[...])
    o_ref[...] = (hg * _sigmoid(hg) * hu).astype(o_ref.dtype)


def ffn_up(xb, ssq, w_gate_up, layer, tm, tn):
    m, kdim = xb.shape
    hidden = w_gate_up.shape[2] // 2
    nj = hidden // tn
    return pl.pallas_call(
        _ffn_up_body,
        grid=(m // tm, nj),
        in_specs=[
            pl.BlockSpec((tm, kdim), lambda i, j: (i, 0)),
            pl.BlockSpec((tm, ssq.shape[1]), lambda i, j: (i, 0)),
            pl.BlockSpec((None, kdim, tn), lambda i, j: (layer, 0, j)),
            pl.BlockSpec((None, kdim, tn), lambda i, j: (layer, 0, nj + j)),
        ],
        out_specs=pl.BlockSpec((tm, tn), lambda i, j: (i, j)),
        out_shape=jax.ShapeDtypeStruct((m, hidden), BF16),
        compiler_params=_params("parallel", "parallel"),
        name="ffn_up",
    )(xb, ssq, w_gate_up, w_gate_up)


def _pad_cols(w, width):
    return jnp.pad(w, [(0, 0)] * (w.ndim - 1) + [(0, width - w.shape[-1])])


def _w_in_pieces():
    widths = [3 * RW_WIDTH, RW_DECAY_LORA, RW_A_LORA, RW_GATE_LORA, 4 * RET_WIDTH, MLA_Q_RANK, MLA_KV_RANK, MLA_ROPE,
              3 * D_MODEL]
    dsts = [OFF_RW, OFF_RW_WD, OFF_RW_AD, OFF_RW_GD, OFF_RET, OFF_MLA_Q, OFF_MLA_KV, OFF_MLA_KR, OFF_GATE]
    pieces, src = [], 0
    for dst, width in zip(dsts, widths):
        pieces.append((dst, src, width))
        src += width
    return pieces


def _pack_w_in_body(w_ref, g_ref, lo_ref, hi_ref):
    g = g_ref[...]
    for o_ref, base in ((lo_ref, 0), (hi_ref, P_SPLIT)):
        covered = 0
        for dst, src, width in sorted(_w_in_pieces()):
            dst -= base
            if not 0 <= dst < o_ref.shape[1]:
                continue
            if dst > covered:
                o_ref[:, covered:dst] = jnp.zeros((o_ref.shape[0], dst - covered), o_ref.dtype)
            o_ref[:, dst:dst + width] = (w_ref[:, src:src + width] * g).astype(o_ref.dtype)
            covered = dst + width
        if covered < o_ref.shape[1]:
            o_ref[:, covered:] = jnp.zeros((o_ref.shape[0], o_ref.shape[1] - covered), o_ref.dtype)


def _pack_w_in(w_in, gain, tr=128):
    nl, rows, cols = w_in.shape
    widths = (P_SPLIT, P_COLS - P_SPLIT)
    return pl.pallas_call(
        _pack_w_in_body,
        grid=(nl, rows // tr),
        in_specs=[pl.BlockSpec((None, tr, cols), lambda l, i: (l, i, 0)),
                  pl.BlockSpec((None, tr, 1), lambda l, i: (l, i, 0))],
        out_specs=[pl.BlockSpec((None, tr, n), lambda l, i: (l, i, 0)) for n in widths],
        out_shape=[jax.ShapeDtypeStruct((nl, rows, n), BF16) for n in widths],
        compiler_params=_params("parallel", "parallel"),
        name="pack_w_in",
    )(w_in, gain[..., None])


def _pack_mu(mu):
    rkv = mu[..., :3 * RW_WIDTH]
    wd = mu[..., 3 * RW_WIDTH:3 * RW_WIDTH + RW_DECAY_LORA]
    ad = mu[..., 3 * RW_WIDTH + RW_DECAY_LORA:3 * RW_WIDTH + RW_DECAY_LORA + RW_A_LORA]
    gd = mu[..., 3 * RW_WIDTH + RW_DECAY_LORA + RW_A_LORA:]
    return jnp.concatenate([rkv, _pad_cols(wd, LANE), _pad_cols(ad, LANE), gd], axis=-1)


def _pad_rows(w, rows):
    return jnp.pad(w, [(0, 0)] * (w.ndim - 2) + [(0, rows - w.shape[-2]), (0, 0)])


def _pack_w_uq(w):
    nl, rank, _ = w.shape
    w = w.reshape(nl, rank, MLA_HEADS, MLA_NOPE + MLA_ROPE)
    w = jnp.pad(w, ((0, 0), (0, 0), (0, 0), (0, MLA_QK_PAD - MLA_NOPE - MLA_ROPE)))
    return w.reshape(nl, rank, MLA_HEADS * MLA_QK_PAD).astype(BF16)


def _pack_w_ukv(w):
    nl, rank, _ = w.shape
    w = w.reshape(nl, rank, MLA_HEADS, 2, MLA_NOPE)
    w = jnp.swapaxes(w, 2, 3)
    return w.reshape(nl, rank, 2 * MLA_HEADS * MLA_NOPE).astype(BF16)


def _rope_tables(lp):
    pos = jnp.arange(lp, dtype=F32)

    def tables(dim):
        inv = ROPE_BASE ** (-jnp.arange(0, dim, 2, dtype=F32) / dim)
        ang = pos[:, None] * inv[None, :]
        return jnp.cos(ang), jnp.sin(ang)

    c, s = tables(RET_HEAD_DIM)
    ret = (jnp.concatenate([c, c], axis=1), jnp.concatenate([-s, s], axis=1))
    c, s = tables(MLA_ROPE)
    z = jnp.zeros((lp, LANE - MLA_ROPE), F32)
    mla = (jnp.concatenate([c, c, z], axis=1), jnp.concatenate([-s, s, z], axis=1))
    return ret, mla


def kernel(x, meta_tokens, norm_mix, w_in, rw_mu, rw_w0, rw_w_up, rw_a0, rw_a_up, rw_g_up, rw_k_k, rw_k_a, rw_r_k, rw_ln_w, rw_ln_b, mla_norm_q, mla_norm_kv, mla_w_uq, mla_w_ukv, w_br_rwkv, w_br_ret, w_br_mla, w_out, norm_ffn, w_gate_up, w_down, final_norm):
    batch, seq, d = x.shape
    depth = w_in.shape[0]
    lp = -(-(N_META + seq) // SEQ_ALIGN) * SEQ_ALIGN
    tp = batch * lp


    wp_lo, wp_hi = _pack_w_in(w_in, norm_mix)
    mu = _pack_mu(rw_mu)
    w_up = _pad_rows(rw_w_up, LANE).astype(BF16)
    a_up = _pad_rows(rw_a_up, LANE).astype(BF16)
    g_up = rw_g_up.astype(BF16)
    wuq = _pack_w_uq(mla_w_uq)
    wukv = _pack_w_ukv(mla_w_ukv)
    wa = w_br_rwkv.astype(BF16)
    wb = w_br_ret.astype(BF16)
    wc = w_br_mla.astype(BF16)
    wo = w_out.astype(BF16)
    wgu = (w_gate_up * norm_ffn[..., None]).astype(BF16)
    wdn = w_down.astype(BF16)
    (cos_ret, sin_ret), (cos_mla, sin_mla) = _rope_tables(lp)

    def row_tile(pref):
        return next((t for t in pref if tp % t == 0), SEQ_ALIGN)

    tm = row_tile((768,))
    tm_wide = row_tile((1536, 768))
    tm_seq = SEQ_ALIGN

    h, hb, ssq = stream_init(x, meta_tokens, lp, tm_seq)
    for l in range(depth):
        p_lo = norm_matmul(hb, ssq, wp_lo, l, tm_wide, 2 * P_TILE_N, BF16)
        p = norm_matmul(hb, ssq, wp_hi, l, tm, 3 * P_TILE_N, F32)
        ya = rwkv_mix(p, batch, mu[l:l + 1], rw_w0[l], rw_a0[l], rw_k_k[l], rw_k_a[l], rw_r_k[l],
                      rw_ln_w[l], rw_ln_b[l], w_up[l], a_up[l], g_up[l])
        yb = retention_mix(p_lo, batch, cos_ret, sin_ret)
        q, k, v = mla_proj(p, batch, mla_norm_q[l], mla_norm_kv[l], wuq[l], wukv[l], cos_mla, sin_mla, tm_seq)
        yc = mla_attention(q, k, v, batch, tm_seq)
        merged = merge_branches(ya, yb, yc, wa, wb, wc, l, p_lo, tm, 1024)
        h, hb, ssq = resid_matmul(merged, wo, l, h, tm_seq, d)
        act = ffn_up(hb, ssq, wgu, l, tm_wide, 512)
        h, hb, ssq = resid_matmul(act, wdn, l, h, tm, 512)
    tm_out = next(t for t in (512, 256, LANE) if seq % t == 0)
    return final_rmsnorm(h, final_norm, batch, N_META, seq, tm_out).reshape(batch, seq, d)
```

```python
import functools
import math

import jax
import jax.numpy as jnp
from jax import lax
from jax.experimental import pallas as pl
from jax.experimental.pallas import tpu as pltpu

F32 = jnp.float32
BF16 = jnp.bfloat16

D_MODEL = 2048
N_META = 16
NORM_EPS = 1e-6
ROPE_BASE = 10000.0

RW_HEADS = 16
RW_HEAD_DIM = 64
RW_WIDTH = RW_HEADS * RW_HEAD_DIM
RW_DECAY_LORA = 96
RW_A_LORA = 96
RW_GATE_LORA = 256
RW_GN_EPS = RW_HEAD_DIM * 1e-5
RW_CHUNK = 64
RW_SUB = 3
RW_PAIRS = RW_WIDTH // 128

RET_HEADS = 8
RET_HEAD_DIM = 128
RET_WIDTH = RET_HEADS * RET_HEAD_DIM
RET_CHUNK = 128
RET_SUB = 3

MLA_HEADS = 8
MLA_NOPE = 128
MLA_ROPE = 64
MLA_V = 128
MLA_Q_RANK = 512
MLA_KV_RANK = 256
MLA_WIDTH = MLA_HEADS * MLA_V
MLA_QK_PAD = 256
ATTN_HEADS_PER_STEP = 2

FFN_HIDDEN = -(-8 * D_MODEL // (3 * 256)) * 256

LANE = 128
SEQ_ALIGN = 384
assert SEQ_ALIGN % (RW_CHUNK * RW_SUB) == 0 and SEQ_ALIGN % (RET_CHUNK * RET_SUB) == 0 and SEQ_ALIGN % LANE == 0

OFF_RET = 0
OFF_GATE = OFF_RET + 4 * RET_WIDTH
OFF_RW = OFF_GATE + 3 * D_MODEL
OFF_RW_WD = OFF_RW + 3 * RW_WIDTH
OFF_RW_AD = OFF_RW_WD + LANE
OFF_RW_GD = OFF_RW_AD + LANE
OFF_MLA_Q = OFF_RW_GD + RW_GATE_LORA
OFF_MLA_KV = OFF_MLA_Q + MLA_Q_RANK
OFF_MLA_KR = OFF_MLA_KV + MLA_KV_RANK
P_COLS_USED = OFF_MLA_KR + LANE
P_TILE_N = 512
P_COLS = -(-P_COLS_USED // P_TILE_N) * P_TILE_N
P_SPLIT = OFF_RW
assert P_SPLIT % (4 * P_TILE_N) == 0
assert (P_COLS - P_SPLIT) % (3 * P_TILE_N) == 0

VMEM_LIMIT = 56 * 1024 * 1024


def _params(*sem):
    return pltpu.CompilerParams(dimension_semantics=sem, vmem_limit_bytes=VMEM_LIMIT)


def _sigmoid(x):
    return 1.0 / (1.0 + jnp.exp(-x))


def _dot(a, b):
    return jnp.dot(a, b, preferred_element_type=F32)


def _dot_nt(a, b):
    return lax.dot_general(a, b, (((1,), (1,)), ((), ())), preferred_element_type=F32)


def _dot_tn(a, b):
    return lax.dot_general(a, b, (((0,), (0,)), ((), ())), preferred_element_type=F32)


def _rmsnorm_body(x_ref, g_ref, o_ref):
    x = x_ref[...]
    y = x * lax.rsqrt(jnp.mean(x * x, axis=-1, keepdims=True) + NORM_EPS)
    o_ref[...] = (y * g_ref[...]).astype(o_ref.dtype)


def final_rmsnorm(x, g, batch, first, count, tm):
    m, d = x.shape
    rows_per_batch = m // batch
    assert rows_per_batch % 8 == 0 and first % 8 == 0 and tm % 8 == 0
    tiles = count // tm
    return pl.pallas_call(
        _rmsnorm_body,
        grid=(batch, tiles),
        in_specs=[pl.BlockSpec((pl.Element(tm), pl.Element(d)),
                               lambda b, t: (pl.multiple_of(b * rows_per_batch + first + t * tm, 8), 0)),
                  pl.BlockSpec((1, d), lambda b, t: (0, 0))],
        out_specs=pl.BlockSpec((tm, d), lambda b, t: (b * tiles + t, 0)),
        out_shape=jax.ShapeDtypeStruct((batch * count, d), x.dtype),
        compiler_params=_params("parallel", "parallel"),
        name="final_rmsnorm",
    )(x, g.reshape(1, d))


def _row_scale(ssq_ref, d):
    ssq = ssq_ref[...]
    total = ssq[:, 0:1]
    for j in range(1, ssq.shape[1] // LANE):
        total = total + ssq[:, LANE * j:LANE * j + 1]
    return lax.rsqrt(total * (1.0 / d) + NORM_EPS)


def _ssq_block(x):
    return jnp.broadcast_to(jnp.sum(x * x, axis=-1, keepdims=True), (x.shape[0], LANE))


def _stream_init_body(x_ref, meta_ref, h_ref, hb_ref, ssq_ref, *, tiles, pad_rows):
    j = pl.program_id(1)
    tm, d = h_ref.shape

    def emit(rows):
        h_ref[...] = rows
        hb_ref[...] = rows.astype(hb_ref.dtype)
        ssq_ref[...] = _ssq_block(rows)

    @pl.when(j == 0)
    def _():
        emit(jnp.concatenate([meta_ref[...], x_ref[0:tm - N_META, :]], axis=0))

    @pl.when((j > 0) & (j < tiles - 1))
    def _():
        emit(x_ref[...])

    @pl.when(j == tiles - 1)
    def _():
        emit(jnp.concatenate([x_ref[pad_rows:, :], jnp.zeros((pad_rows, d), F32)], axis=0))


def stream_init(x, meta_tokens, lp, tm):
    batch, seq, d = x.shape
    tiles = lp // tm
    pad_rows = lp - N_META - seq
    assert tiles >= 2 and seq >= tm and 0 <= pad_rows < tm and pad_rows % 8 == 0 and N_META % 8 == 0 and seq % 8 == 0
    tile = pl.BlockSpec((tm, d), lambda b, j: (b * tiles + j, 0))
    return pl.pallas_call(
        functools.partial(_stream_init_body, tiles=tiles, pad_rows=pad_rows),
        grid=(batch, tiles),
        in_specs=[pl.BlockSpec((pl.Element(tm), pl.Element(d)),
                               lambda b, j: (pl.multiple_of(b * seq + jnp.clip(j * tm - N_META, 0, seq - tm), 8), 0)),
                  pl.BlockSpec((N_META, d), lambda b, j: (0, 0))],
        out_specs=[tile, tile, pl.BlockSpec((tm, LANE), lambda b, j: (b * tiles + j, 0))],
        out_shape=[jax.ShapeDtypeStruct((batch * lp, d), F32), jax.ShapeDtypeStruct((batch * lp, d), BF16),
                   jax.ShapeDtypeStruct((batch * lp, LANE), F32)],
        compiler_params=_params("parallel", "parallel"),
        name="stream_init",
    )(x.reshape(batch * seq, d), meta_tokens.astype(x.dtype))


def _norm_matmul_body(x_ref, ssq_ref, w_ref, o_ref):
    o_ref[...] = (_row_scale(ssq_ref, x_ref.shape[1]) * _dot(x_ref[...], w_ref[...])).astype(o_ref.dtype)


def norm_matmul(xb, ssq, w, layer, tm, tn, out_dtype):
    m, k = xb.shape
    n = w.shape[2]
    assert n % tn == 0 and m % tm == 0
    return pl.pallas_call(
        _norm_matmul_body,
        grid=(m // tm, n // tn),
        in_specs=[pl.BlockSpec((tm, k), lambda i, j: (i, 0)), pl.BlockSpec((tm, ssq.shape[1]), lambda i, j: (i, 0)),
                  pl.BlockSpec((None, k, tn), lambda i, j: (layer, 0, j))],
        out_specs=pl.BlockSpec((tm, tn), lambda i, j: (i, j)),
        out_shape=jax.ShapeDtypeStruct((m, n), out_dtype),
        compiler_params=_params("parallel", "parallel"),
        name="in_proj",
    )(xb, ssq, w)


def _rwkv_body(r_ref, k_ref, v_ref, wd_ref, ad_ref, gd_ref, mu_ref,
               w0_ref, a0_ref, kk_ref, ka_ref, rk_ref, lnw_ref, lnb_ref,
               wup_ref, aup_ref, gup_ref,
               o_ref,
               s_ref, pr_ref, pk_ref, pv_ref, pwd_ref, pad_ref, pgd_ref):
    c = pl.program_id(1)
    C = RW_CHUNK
    RS = RW_SUB * C
    HD = RW_HEAD_DIM

    @pl.when(c == 0)
    def _():
        s_ref[...] = jnp.zeros_like(s_ref)
        pr_ref[...] = jnp.zeros_like(pr_ref)
        pk_ref[...] = jnp.zeros_like(pk_ref)
        pv_ref[...] = jnp.zeros_like(pv_ref)
        pwd_ref[...] = jnp.zeros_like(pwd_ref)
        pad_ref[...] = jnp.zeros_like(pad_ref)
        pgd_ref[...] = jnp.zeros_like(pgd_ref)

    def shift(x_ref, prev_ref, mu, sl):
        z = x_ref[:, sl]
        first = lax.broadcasted_iota(jnp.int32, z.shape, 0) == 0
        zs = jnp.where(first, prev_ref[0:1, sl], pltpu.roll(z, 1, 0))
        prev_ref[0:1, sl] = z[RS - 1:RS, :]
        return z + (zs - z) * mu

    mu_lora = 3 * RW_WIDTH
    full = slice(None)
    wd = shift(wd_ref, pwd_ref, mu_ref[:, mu_lora:mu_lora + LANE], full)
    ad = shift(ad_ref, pad_ref, mu_ref[:, mu_lora + LANE:mu_lora + 2 * LANE], full)
    gd = shift(gd_ref, pgd_ref, mu_ref[:, mu_lora + 2 * LANE:], full)
    subs = range(RW_SUB)

    def rows(t, s):
        return t[C * s:C * (s + 1)]

    tanh_wd = [rows(jnp.tanh(wd), s).astype(BF16) for s in subs]
    ad_b = [rows(ad, s).astype(BF16) for s in subs]
    sig_gd = [rows(_sigmoid(gd), s).astype(BF16) for s in subs]

    lane_sq = lax.broadcasted_iota(jnp.int32, (LANE, LANE), 1)
    row_sq = lax.broadcasted_iota(jnp.int32, (LANE, LANE), 0)
    same_head = (lane_sq < HD) == (row_sq < HD)
    head_ones = same_head.astype(BF16)
    eye = (lane_sq == row_sq).astype(F32)
    rc = lax.broadcasted_iota(jnp.int32, (C, C), 0)
    cc = lax.broadcasted_iota(jnp.int32, (C, C), 1)
    tril_incl = (cc <= rc).astype(BF16)
    lane_tall = lax.broadcasted_iota(jnp.int32, (2 * C, LANE), 1)
    lane_c = lax.broadcasted_iota(jnp.int32, (C, LANE), 1)
    row_c = lax.broadcasted_iota(jnp.int32, (C, LANE), 0)
    head0_c = lane_c < HD
    strict_lo = lane_c < row_c
    strict_hi = (lane_c >= C) & (lane_c - C < row_c)
    incl_lo = lane_c <= row_c
    incl_hi = (lane_c >= C) & (lane_c - C <= row_c)
    inv_n = 1.0 / HD

    P = range(RW_PAIRS)

    def head_sums(ts):
        out = _dot(jnp.concatenate([t.astype(BF16) for t in ts], axis=0), head_ones)
        return [out[C * i:C * (i + 1)] for i in range(len(ts))]

    pair_sls = [slice(LANE * i, LANE * (i + 1)) for i in P]

    def each(fn, *lists):
        return [fn(*args) for args in zip(*lists)]

    def items(per_pair):
        return [rows(t, s) for s in subs for t in per_pair]

    r = items([shift(r_ref, pr_ref, mu_ref[:, sl], sl) for sl in pair_sls])
    k = items([shift(k_ref, pk_ref, mu_ref[:, RW_WIDTH + sl.start:RW_WIDTH + sl.stop], sl) for sl in pair_sls])
    v = items([shift(v_ref, pv_ref, mu_ref[:, 2 * RW_WIDTH + sl.start:2 * RW_WIDTH + sl.stop], sl)
               for sl in pair_sls])
    sls = pair_sls * RW_SUB
    sub_of = [s for s in subs for _ in P]

    def log_decay(s, sl):
        x = -(w0_ref[:, sl] + _dot(tanh_wd[s], wup_ref[:, sl]))
        softplus = jnp.maximum(x, 0.0) + jnp.log1p(jnp.exp(-jnp.abs(x)))
        return -jnp.exp(-softplus - 0.5)

    lw = each(log_decay, sub_of, sls)
    a = each(lambda s, sl: _sigmoid(a0_ref[:, sl] + _dot(ad_b[s], aup_ref[:, sl])), sub_of, sls)
    g = each(lambda s, sl: _dot(sig_gd[s], gup_ref[:, sl]), sub_of, sls)

    kkr = each(lambda ki, sl: ki * kk_ref[:, sl], k, sls)
    ksq = head_sums(each(lambda t: t * t, kkr))
    kkn = each(lambda t, ss: t / jnp.maximum(jnp.sqrt(ss), 1e-12), kkr, ksq)
    kmod = each(lambda ki, ai, sl: ki * (1.0 + (ai - 1.0) * ka_ref[:, sl]), k, a, sls)
    beta = each(lambda ai, t: ai * t, a, kkn)

    def running_sum(lwi):
        hi = lwi.astype(BF16)
        both = _dot(tril_incl, jnp.concatenate([hi, (lwi - hi.astype(F32)).astype(BF16)], axis=1))
        return both[:, :LANE] + both[:, LANE:]

    lcum = each(running_sum, lw)
    lend = each(lambda t: t[C - 1:C, :], lcum)
    rh = each(lambda ri, lc: ri * jnp.exp(lc), r, lcum)
    kh = each(lambda t, lc, lwi: t * jnp.exp(lc - lwi), kkn, lcum, lw)
    e_neg = each(lambda lc: jnp.exp(-lc), lcum)
    e_end = each(lambda le, lc: jnp.exp(le - lc), lend, lcum)
    kb = each(lambda t, e: t * e, kmod, e_neg)
    bb = each(lambda t, e: t * e, beta, e_neg)
    kbe = each(lambda t, e: t * e, kmod, e_end)
    bbe = each(lambda t, e: t * e, beta, e_end)

    kr_f = each(lambda x1, x2: jnp.concatenate([x1, x2], axis=0), kh, rh)
    bk = each(lambda x1, x2: jnp.concatenate([x1, x2], axis=0).astype(BF16), bb, kb)
    vb = each(lambda t: t.astype(BF16), v)

    def gram(krf, bki):
        kr2 = jnp.concatenate([jnp.where(lane_tall < HD, krf, 0.0), jnp.where(lane_tall >= HD, krf, 0.0)], axis=0)
        return _dot_nt(kr2.astype(BF16), bki)

    g_all = each(gram, kr_f, bk)

    n_bd = each(lambda ga: jnp.concatenate([jnp.where(strict_lo, -ga[0:C], 0.0),
                                            jnp.where(strict_hi, -pltpu.roll(ga[2 * C:3 * C], C, 1), 0.0)], axis=0),
                g_all)
    t = each(lambda n: eye + n, n_bd)
    pw = each(lambda n: _dot(n.astype(BF16), n.astype(BF16)), n_bd)
    for _ in range(4):
        both = each(lambda ti, pi: _dot(jnp.concatenate([ti, pi], axis=0).astype(BF16), pi.astype(BF16)), t, pw)
        t = each(lambda ti, bi: ti + bi[:LANE], t, both)
        pw = each(lambda bi: bi[LANE:], both)
    t = each(lambda ti, pi: ti + _dot(ti.astype(BF16), pi.astype(BF16)), t, pw)

    def intra_rhs(ga, vbi):
        m1s = jnp.concatenate([jnp.where(strict_hi, ga[0:C], 0.0), jnp.where(strict_hi, ga[2 * C:3 * C], 0.0)], axis=0)
        return _dot(m1s.astype(BF16), jnp.concatenate([vbi, vbi], axis=0))

    q_intra = each(intra_rhs, g_all, vb)

    def m2(gb):
        return jnp.where(incl_lo, -gb, jnp.where(incl_hi, gb, 0.0))

    m2s = each(lambda ga: jnp.concatenate([m2(ga[C:2 * C]), m2(ga[3 * C:4 * C])], axis=0).astype(BF16), g_all)
    kbe_all = each(lambda kbei, bbei: jnp.concatenate([kbei, -bbei], axis=0).astype(BF16), kbe, bbe)
    s_decay = each(jnp.exp, lend)

    state = [s_ref[i] for i in P]
    y = []
    for s in subs:
        of = lambda lst: lst[RW_PAIRS * s:RW_PAIRS * (s + 1)]
        p_all = each(lambda x1, si: _dot_nt(x1.astype(BF16), si.astype(BF16)), of(kr_f), state)
        q_s = each(lambda qi, pa: jnp.where(same_head, qi + jnp.concatenate([pa[:C], pa[:C]], axis=0), 0.0),
                   of(q_intra), p_all)
        u_s = each(lambda ti, qi: _dot(ti.astype(BF16), qi.astype(BF16)), of(t), q_s)
        u = each(lambda us: us[:C] + us[C:], u_s)
        y_s = each(lambda mi, ui, vi: _dot(mi, jnp.concatenate([ui, vi], axis=0).astype(BF16)), of(m2s), u, of(v))
        y += each(lambda pa, ys: pa[C:] + jnp.where(head0_c, ys[:C], ys[C:]), p_all, y_s)
        ds = each(lambda vi, ui, kb_all: _dot_tn(jnp.concatenate([vi, ui], axis=0).astype(BF16), kb_all),
                  of(v), u, of(kbe_all))
        state = each(lambda si, di, dec: si * dec + jnp.where(same_head, di, 0.0), state, ds, of(s_decay))
    for i in P:
        s_ref[i] = state[i]

    d = each(lambda yi, si: yi - si * inv_n, y, head_sums(y))
    var = each(lambda si: si * inv_n, head_sums(each(lambda di: di * di, d)))
    bsum = head_sums(each(lambda ri, ki, sl: ri * ki * rk_ref[:, sl], r, kmod, sls))
    for i, (s, sl) in enumerate(zip(sub_of, sls)):
        yn = d[i] * lax.rsqrt(var[i] + RW_GN_EPS) * lnw_ref[:, sl] + lnb_ref[:, sl]
        o_ref[C * s:C * (s + 1), sl] = ((yn + bsum[i] * v[i]) * g[i]).astype(o_ref.dtype)


def rwkv_mix(p, batch, mu, w0, a0, k_k, k_a, r_k, ln_w, ln_b, w_up, a_up, g_up):
    tp = p.shape[0]
    lp = tp // batch
    C = RW_CHUNK * RW_SUB
    nchunk = lp // C

    def pspec(width, base):
        return pl.BlockSpec((C, width), lambda b, c: (b * nchunk + c, base // width))

    def const(shape):
        return pl.BlockSpec(shape, lambda b, c: (0, 0))

    in_specs = [
        pspec(RW_WIDTH, OFF_RW - P_SPLIT), pspec(RW_WIDTH, OFF_RW - P_SPLIT + RW_WIDTH),
        pspec(RW_WIDTH, OFF_RW - P_SPLIT + 2 * RW_WIDTH),
        pspec(LANE, OFF_RW_WD - P_SPLIT), pspec(LANE, OFF_RW_AD - P_SPLIT), pspec(RW_GATE_LORA, OFF_RW_GD - P_SPLIT),
        const(mu.shape),
    ] + [const((1, RW_WIDTH))] * 7 + [const(w_up.shape), const(a_up.shape), const(g_up.shape)]
    row = lambda t: t.reshape(1, -1)
    return pl.pallas_call(
        _rwkv_body,
        grid=(batch, nchunk),
        in_specs=in_specs,
        out_specs=pl.BlockSpec((C, RW_WIDTH), lambda b, c: (b * nchunk + c, 0)),
        out_shape=jax.ShapeDtypeStruct((tp, RW_WIDTH), BF16),
        scratch_shapes=[pltpu.VMEM((RW_PAIRS, LANE, LANE), F32)] + [pltpu.VMEM((8, RW_WIDTH), F32)] * 3
        + [pltpu.VMEM((8, LANE), F32)] * 2 + [pltpu.VMEM((8, RW_GATE_LORA), F32)],
        compiler_params=_params("parallel", "arbitrary"),
        name="rwkv7_mix",
    )(p, p, p, p, p, p, mu, row(w0), row(a0), row(k_k), row(k_a), row(r_k), row(ln_w), row(ln_b),
      w_up, a_up, g_up)


def _ret_body(q_ref, k_ref, v_ref, g_ref, cos_ref, sin_ref, o_ref, state_ref):
    c = pl.program_id(1)
    C = RET_CHUNK
    d = RET_HEAD_DIM

    @pl.when(c == 0)
    def _():
        state_ref[...] = jnp.zeros_like(state_ref)

    row = lax.broadcasted_iota(jnp.int32, (C, C), 0).astype(F32)
    col = lax.broadcasted_iota(jnp.int32, (C, C), 1).astype(F32)
    diff = row - col
    causal = diff >= 0
    heads = range(RET_HEADS)
    subs = range(RET_SUB)
    rws = [slice(C * s, C * (s + 1)) for s in subs for _ in heads]
    sls = [slice(d * h, d * (h + 1)) for _ in subs for h in heads]
    lgs = [math.log1p(-(2.0 ** (-5.0 - h))) for _ in subs for h in heads]

    def each(fn, *lists):
        return [fn(*args) for args in zip(*lists)]

    def rope(x_ref, rw, sl):
        x = x_ref[rw, sl].astype(F32)
        return x * cos_ref[rw, :] + pltpu.roll(x, d // 2, 1) * sin_ref[rw, :]

    qb = each(lambda rw, sl: rope(q_ref, rw, sl).astype(BF16), rws, sls)
    k = each(lambda rw, sl: rope(k_ref, rw, sl) * (d ** -0.5), rws, sls)
    kb = each(lambda t: t.astype(BF16), k)
    vb = each(lambda rw, sl: v_ref[rw, sl], rws, sls)
    s = each(lambda qi, ki, lg: _dot_nt(qi, ki) * jnp.where(causal, jnp.exp(lg * jnp.maximum(diff, 0.0)), 0.0),
             qb, kb, lgs)
    o_intra = each(lambda si, vi: _dot(si.astype(BF16), vi), s, vb)
    kd = each(lambda ki, lg: (ki * jnp.exp(lg * (C - 1.0 - row))).astype(BF16), k, lgs)
    kv = each(_dot_tn, kd, vb)

    state = [state_ref[h] for h in heads]
    o = []
    for sub in subs:
        of = lambda lst: lst[RET_HEADS * sub:RET_HEADS * (sub + 1)]
        o += each(lambda oi, qi, st, lg: oi + _dot(qi, st.astype(BF16)) * jnp.exp(lg * (row + 1.0)),
                  of(o_intra), of(qb), state, of(lgs))
        state = each(lambda st, kvi, lg: st * math.exp(lg * C) + kvi, state, of(kv), of(lgs))
    for h in heads:
        state_ref[h] = state[h]
    o = each(lambda oi: oi * lax.rsqrt(jnp.mean(oi * oi, axis=-1, keepdims=True) + NORM_EPS), o)
    for oi, rw, sl in zip(o, rws, sls):
        g = g_ref[rw, sl].astype(F32)
        o_ref[rw, sl] = (g * _sigmoid(g) * oi).astype(o_ref.dtype)


def retention_mix(p, batch, cos, sin):
    tp = p.shape[0]
    lp = tp // batch
    C = RET_CHUNK * RET_SUB
    nchunk = lp // C
    base = OFF_RET // RET_WIDTH

    def pspec(j):
        return pl.BlockSpec((C, RET_WIDTH), lambda b, c: (b * nchunk + c, base + j))

    tab = pl.BlockSpec((C, RET_HEAD_DIM), lambda b, c: (c, 0))
    return pl.pallas_call(
        _ret_body,
        grid=(batch, nchunk),
        in_specs=[pspec(0), pspec(1), pspec(2), pspec(3), tab, tab],
        out_specs=pl.BlockSpec((C, RET_WIDTH), lambda b, c: (b * nchunk + c, 0)),
        out_shape=jax.ShapeDtypeStruct((tp, RET_WIDTH), BF16),
        scratch_shapes=[pltpu.VMEM((RET_HEADS, RET_HEAD_DIM, RET_HEAD_DIM), F32)],
        compiler_params=_params("parallel", "arbitrary"),
        name="retention_mix",
    )(p, p, p, p, cos, sin)


def _mla_proj_body(qd_ref, kvd_ref, krd_ref, nq_ref, nkv_ref, wuq_ref, wukv_ref, cos_ref, sin_ref,
                   q_out, k_out, v_out):
    cos = cos_ref[...]
    sin = sin_ref[...]

    def rope(x):
        return x * cos + (pltpu.roll(x, MLA_ROPE // 2, 1) + pltpu.roll(x, LANE - MLA_ROPE // 2, 1)) * sin

    def norm(x, g):
        return x * lax.rsqrt(jnp.mean(x * x, axis=-1, keepdims=True) + NORM_EPS) * g

    scale = (MLA_NOPE + MLA_ROPE) ** -0.5 * math.log2(math.e)
    q = _dot(norm(qd_ref[...], nq_ref[...]).astype(BF16), wuq_ref[...]) * scale
    kv = _dot(norm(kvd_ref[...], nkv_ref[...]).astype(BF16), wukv_ref[...])
    kr = rope(krd_ref[...]).astype(k_out.dtype)
    for h in range(MLA_HEADS):
        lo = MLA_QK_PAD * h
        q_out[:, lo:lo + LANE] = q[:, lo:lo + LANE].astype(q_out.dtype)
        q_out[:, lo + LANE:lo + 2 * LANE] = rope(q[:, lo + LANE:lo + 2 * LANE]).astype(q_out.dtype)
        k_out[:, lo:lo + LANE] = kv[:, MLA_NOPE * h:MLA_NOPE * (h + 1)].astype(k_out.dtype)
        k_out[:, lo + LANE:lo + 2 * LANE] = kr
    v_out[...] = kv[:, MLA_HEADS * MLA_NOPE:].astype(v_out.dtype)


def mla_proj(p, batch, norm_q, norm_kv, w_uq, w_ukv, cos, sin, tm):
    tp = p.shape[0]
    lp = tp // batch
    per_seq = lp // tm
    qk_w = MLA_HEADS * MLA_QK_PAD
    const = lambda i: (0, 0)
    return pl.pallas_call(
        _mla_proj_body,
        grid=(tp // tm,),
        in_specs=[
            pl.BlockSpec((tm, MLA_Q_RANK), lambda i: (i, (OFF_MLA_Q - P_SPLIT) // MLA_Q_RANK)),
            pl.BlockSpec((tm, MLA_KV_RANK), lambda i: (i, (OFF_MLA_KV - P_SPLIT) // MLA_KV_RANK)),
            pl.BlockSpec((tm, LANE), lambda i: (i, (OFF_MLA_KR - P_SPLIT) // LANE)),
            pl.BlockSpec((1, MLA_Q_RANK), const),
            pl.BlockSpec((1, MLA_KV_RANK), const),
            pl.BlockSpec((MLA_Q_RANK, qk_w), const),
            pl.BlockSpec((MLA_KV_RANK, MLA_HEADS * (MLA_NOPE + MLA_V)), const),
            pl.BlockSpec((tm, LANE), lambda i: (i % per_seq, 0)),
            pl.BlockSpec((tm, LANE), lambda i: (i % per_seq, 0)),
        ],
        out_specs=[
            pl.BlockSpec((tm, qk_w), lambda i: (i, 0)),
            pl.BlockSpec((tm, qk_w), lambda i: (i, 0)),
            pl.BlockSpec((tm, MLA_WIDTH), lambda i: (i, 0)),
        ],
        out_shape=[
            jax.ShapeDtypeStruct((tp, qk_w), BF16),
            jax.ShapeDtypeStruct((tp, qk_w), BF16),
            jax.ShapeDtypeStruct((tp, MLA_WIDTH), BF16),
        ],
        compiler_params=_params("parallel"),
        name="mla_proj",
    )(p, p, p, norm_q.reshape(1, -1), norm_kv.reshape(1, -1), w_uq, w_ukv, cos, sin)


def _attn_body(q_ref, k_ref, v_ref, o_ref, *, tq):
    i = pl.program_id(2)
    heads = range(ATTN_HEADS_PER_STEP)
    qs = [q_ref[:, MLA_QK_PAD * h:MLA_QK_PAD * (h + 1)] for h in heads]

    def step(off, width, masked, carry):
        ms, ls, accs = carry
        off = pl.multiple_of(off, tq)
        ss = [_dot_nt(qs[h], k_ref[pl.ds(off, width), MLA_QK_PAD * h:MLA_QK_PAD * (h + 1)]) for h in heads]
        if masked:
            row = i * tq + lax.broadcasted_iota(jnp.int32, (tq, width), 0)
            col = off + lax.broadcasted_iota(jnp.int32, (tq, width), 1)
            ss = [jnp.where(col <= row, s, -jnp.inf) for s in ss]
        m_new = [jnp.maximum(ms[h], jnp.max(ss[h], axis=-1, keepdims=True)) for h in heads]
        alpha = [jnp.exp2(ms[h] - m_new[h]) for h in heads]
        ps = [jnp.exp2(ss[h] - m_new[h]) for h in heads]
        ls = [alpha[h] * ls[h] + jnp.sum(ps[h], axis=-1, keepdims=True) for h in heads]
        pv = [_dot(ps[h].astype(BF16), v_ref[pl.ds(off, width), MLA_V * h:MLA_V * (h + 1)]) for h in heads]
        accs = [alpha[h] * accs[h] + pv[h] for h in heads]
        return tuple(m_new), tuple(ls), tuple(accs)

    carry = (tuple(jnp.full((tq, 1), -1e30, F32) for _ in heads),
             tuple(jnp.zeros((tq, 1), F32) for _ in heads),
             tuple(jnp.zeros((tq, MLA_V), F32) for _ in heads))
    n_single = (i + 1) % 2
    n_pairs = (i + 1) // 2
    carry = lax.cond(i == 0, lambda c: step(0, tq, True, c), lambda c: c, carry)
    carry = lax.cond((n_single == 1) & (i > 0), lambda c: step(0, tq, False, c), lambda c: c, carry)
    pair_off = lambda p: (n_single + 2 * p) * tq
    carry = lax.fori_loop(0, n_pairs - 1, lambda p, c: step(pair_off(p), 2 * tq, False, c), carry)
    carry = lax.cond(n_pairs > 0, lambda c: step(pair_off(n_pairs - 1), 2 * tq, True, c), lambda c: c, carry)
    _, ls, accs = carry
    for h in heads:
        o_ref[:, MLA_V * h:MLA_V * (h + 1)] = (accs[h] / ls[h]).astype(o_ref.dtype)


def mla_attention(q, k, v, batch, tq):
    tp = q.shape[0]
    lp = tp // batch
    nq = lp // tq
    hs = ATTN_HEADS_PER_STEP
    return pl.pallas_call(
        functools.partial(_attn_body, tq=tq),
        grid=(batch, MLA_HEADS // hs, nq),
        in_specs=[
            pl.BlockSpec((tq, hs * MLA_QK_PAD), lambda b, h, i: (b * nq + i, h)),
            pl.BlockSpec((lp, hs * MLA_QK_PAD), lambda b, h, i: (b, h)),
            pl.BlockSpec((lp, hs * MLA_V), lambda b, h, i: (b, h)),
        ],
        out_specs=pl.BlockSpec((tq, hs * MLA_V), lambda b, h, i: (b * nq + i, h)),
        out_shape=jax.ShapeDtypeStruct((tp, MLA_WIDTH), BF16),
        compiler_params=_params("parallel", "parallel", "arbitrary"),
        name="mla_attention",
    )(q, k, v)


def _merge_body(ya_ref, yb_ref, yc_ref, wa_ref, wb_ref, wc_ref, ga_ref, gb_ref, gc_ref, o_ref):
    def branch(y_ref, w_ref, g_ref):
        return _sigmoid(g_ref[...].astype(F32)) * _dot(y_ref[...], w_ref[...])

    o_ref[...] = (branch(ya_ref, wa_ref, ga_ref) + branch(yb_ref, wb_ref, gb_ref)
                  + branch(yc_ref, wc_ref, gc_ref)).astype(o_ref.dtype)


def merge_branches(ya, yb, yc, wa, wb, wc, layer, p, tm, tn):
    tp = ya.shape[0]

    def yspec(width):
        return pl.BlockSpec((tm, width), lambda i, j: (i, 0))

    def wspec(width):
        return pl.BlockSpec((None, width, tn), lambda i, j: (layer, 0, j))

    def gspec(branch):
        base = (OFF_GATE + branch * D_MODEL) // tn
        return pl.BlockSpec((tm, tn), lambda i, j: (i, base + j))

    return pl.pallas_call(
        _merge_body,
        grid=(tp // tm, D_MODEL // tn),
        in_specs=[yspec(RW_WIDTH), yspec(RET_WIDTH), yspec(MLA_WIDTH),
                  wspec(RW_WIDTH), wspec(RET_WIDTH), wspec(MLA_WIDTH),
                  gspec(0), gspec(1), gspec(2)],
        out_specs=pl.BlockSpec((tm, tn), lambda i, j: (i, j)),
        out_shape=jax.ShapeDtypeStruct((tp, D_MODEL), BF16),
        compiler_params=_params("parallel", "parallel"),
        name="merge_branches",
    )(ya, yb, yc, wa, wb, wc, p, p, p)


def _resid_body(x_ref, w_ref, h_ref, o_ref, ob_ref, ssq_ref):
    hn = h_ref[...] + _dot(x_ref[...], w_ref[...])
    o_ref[...] = hn
    ob_ref[...] = hn.astype(ob_ref.dtype)
    ssq_ref[...] = _ssq_block(hn)


def resid_matmul(x, w, layer, h, tm, tn):
    m, kdim = x.shape
    n = w.shape[2]
    tile = pl.BlockSpec((tm, tn), lambda i, j: (i, j))
    return pl.pallas_call(
        _resid_body,
        grid=(m // tm, n // tn),
        in_specs=[
            pl.BlockSpec((tm, kdim), lambda i, j: (i, 0)),
            pl.BlockSpec((None, kdim, tn), lambda i, j: (layer, 0, j)),
            tile,
        ],
        out_specs=[tile, tile, pl.BlockSpec((tm, LANE), lambda i, j: (i, j))],
        out_shape=[jax.ShapeDtypeStruct((m, n), F32), jax.ShapeDtypeStruct((m, n), BF16),
                   jax.ShapeDtypeStruct((m, LANE * (n // tn)), F32)],
        compiler_params=_params("parallel", "parallel"),
        name="resid_matmul",
    )(x, w, h)


def _ffn_up_body(x_ref, ssq_ref, wg_ref, wu_ref, o_ref):
    x = x_ref[...]
    r = _row_scale(ssq_ref, x.shape[1])
    hg = r * _dot(x, wg_ref[...])
    hu = r * _dot(x, wu_ref[...])
    o_ref[...] = (hg * _sigmoid(hg) * hu).astype(o_ref.dtype)


def ffn_up(xb, ssq, w_gate_up, layer, tm, tn):
    m, kdim = xb.shape
    hidden = w_gate_up.shape[2] // 2
    nj = hidden // tn
    return pl.pallas_call(
        _ffn_up_body,
        grid=(m // tm, nj),
        in_specs=[
            pl.BlockSpec((tm, kdim), lambda i, j: (i, 0)),
            pl.BlockSpec((tm, ssq.shape[1]), lambda i, j: (i, 0)),
            pl.BlockSpec((None, kdim, tn), lambda i, j: (layer, 0, j)),
            pl.BlockSpec((None, kdim, tn), lambda i, j: (layer, 0, nj + j)),
        ],
        out_specs=pl.BlockSpec((tm, tn), lambda i, j: (i, j)),
        out_shape=jax.ShapeDtypeStruct((m, hidden), BF16),
        compiler_params=_params("parallel", "parallel"),
        name="ffn_up",
    )(xb, ssq, w_gate_up, w_gate_up)


def _pad_cols(w, width):
    return jnp.pad(w, [(0, 0)] * (w.ndim - 1) + [(0, width - w.shape[-1])])


def _w_in_pieces():
    widths = [3 * RW_WIDTH, RW_DECAY_LORA, RW_A_LORA, RW_GATE_LORA, 4 * RET_WIDTH, MLA_Q_RANK, MLA_KV_RANK, MLA_ROPE,
              3 * D_MODEL]
    dsts = [OFF_RW, OFF_RW_WD, OFF_RW_AD, OFF_RW_GD, OFF_RET, OFF_MLA_Q, OFF_MLA_KV, OFF_MLA_KR, OFF_GATE]
    pieces, src = [], 0
    for dst, width in zip(dsts, widths):
        pieces.append((dst, src, width))
        src += width
    return pieces


def _pack_w_in_body(w_ref, g_ref, lo_ref, hi_ref):
    g = g_ref[...]
    for o_ref, base in ((lo_ref, 0), (hi_ref, P_SPLIT)):
        covered = 0
        for dst, src, width in sorted(_w_in_pieces()):
            dst -= base
            if not 0 <= dst < o_ref.shape[1]:
                continue
            if dst > covered:
                o_ref[:, covered:dst] = jnp.zeros((o_ref.shape[0], dst - covered), o_ref.dtype)
            o_ref[:, dst:dst + width] = (w_ref[:, src:src + width] * g).astype(o_ref.dtype)
            covered = dst + width
        if covered < o_ref.shape[1]:
            o_ref[:, covered:] = jnp.zeros((o_ref.shape[0], o_ref.shape[1] - covered), o_ref.dtype)


def _pack_w_in(w_in, gain, tr=128):
    nl, rows, cols = w_in.shape
    widths = (P_SPLIT, P_COLS - P_SPLIT)
    return pl.pallas_call(
        _pack_w_in_body,
        grid=(nl, rows // tr),
        in_specs=[pl.BlockSpec((None, tr, cols), lambda l, i: (l, i, 0)),
                  pl.BlockSpec((None, tr, 1), lambda l, i: (l, i, 0))],
        out_specs=[pl.BlockSpec((None, tr, n), lambda l, i: (l, i, 0)) for n in widths],
        out_shape=[jax.ShapeDtypeStruct((nl, rows, n), BF16) for n in widths],
        compiler_params=_params("parallel", "parallel"),
        name="pack_w_in",
    )(w_in, gain[..., None])


def _pack_mu(mu):
    rkv = mu[..., :3 * RW_WIDTH]
    wd = mu[..., 3 * RW_WIDTH:3 * RW_WIDTH + RW_DECAY_LORA]
    ad = mu[..., 3 * RW_WIDTH + RW_DECAY_LORA:3 * RW_WIDTH + RW_DECAY_LORA + RW_A_LORA]
    gd = mu[..., 3 * RW_WIDTH + RW_DECAY_LORA + RW_A_LORA:]
    return jnp.concatenate([rkv, _pad_cols(wd, LANE), _pad_cols(ad, LANE), gd], axis=-1)


def _pad_rows(w, rows):
    return jnp.pad(w, [(0, 0)] * (w.ndim - 2) + [(0, rows - w.shape[-2]), (0, 0)])


def _pack_w_uq(w):
    nl, rank, _ = w.shape
    w = w.reshape(nl, rank, MLA_HEADS, MLA_NOPE + MLA_ROPE)
    w = jnp.pad(w, ((0, 0), (0, 0), (0, 0), (0, MLA_QK_PAD - MLA_NOPE - MLA_ROPE)))
    return w.reshape(nl, rank, MLA_HEADS * MLA_QK_PAD).astype(BF16)


def _pack_w_ukv(w):
    nl, rank, _ = w.shape
    w = w.reshape(nl, rank, MLA_HEADS, 2, MLA_NOPE)
    w = jnp.swapaxes(w, 2, 3)
    return w.reshape(nl, rank, 2 * MLA_HEADS * MLA_NOPE).astype(BF16)


def _rope_tables(lp):
    pos = jnp.arange(lp, dtype=F32)

    def tables(dim):
        inv = ROPE_BASE ** (-jnp.arange(0, dim, 2, dtype=F32) / dim)
        ang = pos[:, None] * inv[None, :]
        return jnp.cos(ang), jnp.sin(ang)

    c, s = tables(RET_HEAD_DIM)
    ret = (jnp.concatenate([c, c], axis=1), jnp.concatenate([-s, s], axis=1))
    c, s = tables(MLA_ROPE)
    z = jnp.zeros((lp, LANE - MLA_ROPE), F32)
    mla = (jnp.concatenate([c, c, z], axis=1), jnp.concatenate([-s, s, z], axis=1))
    return ret, mla


def kernel(x, meta_tokens, norm_mix, w_in, rw_mu, rw_w0, rw_w_up, rw_a0, rw_a_up, rw_g_up, rw_k_k, rw_k_a, rw_r_k, rw_ln_w, rw_ln_b, mla_norm_q, mla_norm_kv, mla_w_uq, mla_w_ukv, w_br_rwkv, w_br_ret, w_br_mla, w_out, norm_ffn, w_gate_up, w_down, final_norm):
    batch, seq, d = x.shape
    depth = w_in.shape[0]
    lp = -(-(N_META + seq) // SEQ_ALIGN) * SEQ_ALIGN
    tp = batch * lp


    wp_lo, wp_hi = _pack_w_in(w_in, norm_mix)
    mu = _pack_mu(rw_mu)
    w_up = _pad_rows(rw_w_up, LANE).astype(BF16)
    a_up = _pad_rows(rw_a_up, LANE).astype(BF16)
    g_up = rw_g_up.astype(BF16)
    wuq = _pack_w_uq(mla_w_uq)
    wukv = _pack_w_ukv(mla_w_ukv)
    wa = w_br_rwkv.astype(BF16)
    wb = w_br_ret.astype(BF16)
    wc = w_br_mla.astype(BF16)
    wo = w_out.astype(BF16)
    wgu = (w_gate_up * norm_ffn[..., None]).astype(BF16)
    wdn = w_down.astype(BF16)
    (cos_ret, sin_ret), (cos_mla, sin_mla) = _rope_tables(lp)

    def row_tile(pref):
        return next((t for t in pref if tp % t == 0), SEQ_ALIGN)

    tm = row_tile((768,))
    tm_wide = row_tile((1536, 768))
    tm_seq = SEQ_ALIGN

    h, hb, ssq = stream_init(x, meta_tokens, lp, tm_seq)
    for l in range(depth):
        p_lo = norm_matmul(hb, ssq, wp_lo, l, tm_wide, 4 * P_TILE_N, BF16)
        p = norm_matmul(hb, ssq, wp_hi, l, tm, 3 * P_TILE_N, F32)
        ya = rwkv_mix(p, batch, mu[l:l + 1], rw_w0[l], rw_a0[l], rw_k_k[l], rw_k_a[l], rw_r_k[l],
                      rw_ln_w[l], rw_ln_b[l], w_up[l], a_up[l], g_up[l])
        yb = retention_mix(p_lo, batch, cos_ret, sin_ret)
        q, k, v = mla_proj(p, batch, mla_norm_q[l], mla_norm_kv[l], wuq[l], wukv[l], cos_mla, sin_mla, tm_seq)
        yc = mla_attention(q, k, v, batch, tm_seq)
        merged = merge_branches(ya, yb, yc, wa, wb, wc, l, p_lo, tm, 1024)
        h, hb, ssq = resid_matmul(merged, wo, l, h, tm_seq, d)
        act = ffn_up(hb, ssq, wgu, l, tm_wide, 512)
        h, hb, ssq = resid_matmul(act, wdn, l, h, tm, 512)
    tm_out = next(t for t in (512, 256, LANE) if seq % t == 0)
    return final_rmsnorm(h, final_norm, batch, N_META, seq, tm_out).reshape(batch, seq, d)
```
